```python
import jax, jax.numpy as jnp
from jax import lax
import numpy as np

D_MODEL = 1024
BATCH = 8
SEQ = 4096
DEPTH = 4

N_META = 16
N_HEADS = 8
N_KV_HEADS = 2
HEAD_DIM = 64
GQA_GROUP = N_HEADS // N_KV_HEADS
ATTN_WIDTH = N_HEADS * HEAD_DIM
KV_WIDTH = N_KV_HEADS * HEAD_DIM
WINDOW = 128
BLOCK = 128
N_FOURIER_GROUPS = 4
FOURIER_GROUP = 128
FOURIER_WIDTH = N_FOURIER_GROUPS * FOURIER_GROUP
N_BRANCHES = 2
IN_WIDTH = ATTN_WIDTH + 2 * KV_WIDTH + FOURIER_WIDTH + N_BRANCHES * D_MODEL
N_EXPERTS = 16
CAPACITY_FACTOR = 2
D_FF_EXPERT = 1536
DEEPNORM_ALPHA = (2 * DEPTH) ** 0.25
DEEPNORM_BETA = (8 * DEPTH) ** -0.25
LN_EPS = 1e-5
NEG_INF = -1e30
Q_END = ATTN_WIDTH
K_END = Q_END + KV_WIDTH
V_END = K_END + KV_WIDTH
F_END = V_END + FOURIER_WIDTH

kernel_name = 'hybrid_swa_fnet_ec_moe_encoder'


def alibi_slopes():
    return np.array([2.0 ** (-8.0 * (h + 1) / N_HEADS) for h in range(N_HEADS)], dtype=np.float32)


def layer_norm(x, g, b):
    xf = x.astype(jnp.float32)
    mu = jnp.mean(xf, axis=-1, keepdims=True)
    var = jnp.mean(jnp.square(xf - mu), axis=-1, keepdims=True)
    y = (xf - mu) * lax.rsqrt(var + LN_EPS)
    return (y * g.astype(jnp.float32) + b.astype(jnp.float32)).astype(x.dtype)


def windowed_gqa_alibi_sink(q, k, v, sink):
    B, L = q.shape[0], q.shape[1]
    nb = -(-L // BLOCK)
    lp = nb * BLOCK
    lead = lp - L
    qb = jnp.pad(q, ((0, 0), (lead, 0), (0, 0), (0, 0))).reshape(B, nb, BLOCK, N_KV_HEADS, GQA_GROUP, HEAD_DIM)
    kp = jnp.pad(k, ((0, 0), (lead + BLOCK, BLOCK), (0, 0), (0, 0))).reshape(B, nb + 2, BLOCK, N_KV_HEADS, HEAD_DIM)
    vp = jnp.pad(v, ((0, 0), (lead + BLOCK, BLOCK), (0, 0), (0, 0))).reshape(B, nb + 2, BLOCK, N_KV_HEADS, HEAD_DIM)
    kwin = jnp.concatenate([kp[:, :-2], kp[:, 1:-1], kp[:, 2:]], axis=2)
    vwin = jnp.concatenate([vp[:, :-2], vp[:, 1:-1], vp[:, 2:]], axis=2)
    scores = jnp.einsum('bnqkgd,bnskd->bnkgqs', qb, kwin).astype(jnp.float32) * (HEAD_DIM ** -0.5)
    s_idx = jnp.arange(3 * BLOCK)
    rel = s_idx[None, :] - BLOCK - jnp.arange(BLOCK)[:, None]
    kpos = (jnp.arange(nb)[:, None] - 1) * BLOCK + s_idx[None, :]
    kvalid = (kpos >= lead) & (kpos < lp)
    mask = (jnp.abs(rel) <= WINDOW)[None, :, :] & kvalid[:, None, :]
    slopes = jnp.asarray(alibi_slopes()).reshape(N_KV_HEADS, GQA_GROUP)
    bias = -slopes[:, :, None, None] * jnp.abs(rel).astype(jnp.float32)
    logits = jnp.where(mask[None, :, None, None], scores + bias, NEG_INF)
    sink_l = sink.astype(jnp.float32).reshape(N_KV_HEADS, GQA_GROUP)[:, :, None]
    m = jnp.maximum(jnp.max(logits, axis=-1), sink_l)
    p = jnp.exp(logits - m[..., None])
    denom = jnp.sum(p, axis=-1) + jnp.exp(sink_l - m)
    o = jnp.einsum('bnkgqs,bnskd->bnqkgd', p, vwin.astype(jnp.float32))
    o = o / jnp.transpose(denom, (0, 1, 4, 2, 3))[..., None]
    return o.reshape(B, lp, ATTN_WIDTH)[:, lead:].astype(q.dtype)


def fourier_mix(u):
    B, L = u.shape[0], u.shape[1]
    uf = u.astype(jnp.float32).reshape(B, L, N_FOURIER_GROUPS, FOURIER_GROUP)
    z = jnp.fft.fft2(uf, axes=(1, 3), norm='ortho').real
    return z.reshape(B, L, FOURIER_WIDTH).astype(u.dtype)


def hybrid_mixer(h, w_in, b_gate, sink, w_attn_o, w_four_o, w_out):
    B, L, _ = h.shape
    u = jnp.einsum('bld,dn->bln', h, w_in)
    q = u[..., :Q_END].reshape(B, L, N_HEADS, HEAD_DIM)
    k = u[..., Q_END:K_END].reshape(B, L, N_KV_HEADS, HEAD_DIM)
    v = u[..., K_END:V_END].reshape(B, L, N_KV_HEADS, HEAD_DIM)
    uf = u[..., V_END:F_END]
    ug = u[..., F_END:].reshape(B, L, N_BRANCHES, D_MODEL)
    attn = windowed_gqa_alibi_sink(q, k, v, sink)
    four = fourier_mix(uf)
    gates = jax.nn.sigmoid(ug + b_gate)
    merged = (gates[:, :, 0] * jnp.einsum('blc,cd->bld', attn, w_attn_o)
              + gates[:, :, 1] * jnp.einsum('blc,cd->bld', four, w_four_o))
    return jnp.einsum('bld,de->ble', merged, w_out)


def expert_choice_moe(h, w_router, w_gate, w_up, w_down):
    B, L, _ = h.shape
    cap = CAPACITY_FACTOR * L // N_EXPERTS
    aff = jax.nn.softmax(jnp.einsum('bld,de->ble', h, w_router).astype(jnp.float32), axis=-1)
    g, idx = lax.top_k(jnp.swapaxes(aff, 1, 2), cap)
    bidx = jnp.arange(B)[:, None, None]
    xg = h[bidx, idx]
    a = jnp.einsum('becd,edf->becf', xg, w_gate)
    b = jnp.einsum('becd,edf->becf', xg, w_up)
    y = jnp.einsum('becf,efd->becd', jax.nn.silu(a) * b, w_down)
    y = y * g[..., None].astype(y.dtype)
    return jnp.zeros_like(h).at[bidx, idx].add(y)


def setup_inputs(seed: int = 0) -> dict:
    key = jax.random.key(seed)
    ks = jax.random.split(key, 24)
    nrm = lambda kk, shape, s: jax.random.normal(kk, shape, jnp.float32) * s
    w_in = nrm(ks[3], (DEPTH, D_MODEL, IN_WIDTH), D_MODEL ** -0.5)
    w_in = w_in.at[:, :, K_END:V_END].multiply(DEEPNORM_BETA)
    return {
        'x': nrm(ks[0], (BATCH, SEQ, D_MODEL), 1.0),
        'meta': nrm(ks[1], (N_META, D_MODEL), 1.0),
        'ln0_g': 1.0 + nrm(ks[2], (D_MODEL,), 0.02),
        'ln0_b': nrm(ks[4], (D_MODEL,), 0.02),
        'w_in': w_in,
        'b_gate': nrm(ks[5], (DEPTH, N_BRANCHES, D_MODEL), 0.1),
        'sink': nrm(ks[6], (DEPTH, N_HEADS), 0.5),
        'w_attn_o': nrm(ks[7], (DEPTH, ATTN_WIDTH, D_MODEL), ATTN_WIDTH ** -0.5 * DEEPNORM_BETA),
        'w_four_o': nrm(ks[8], (DEPTH, FOURIER_WIDTH, D_MODEL), FOURIER_WIDTH ** -0.5 * DEEPNORM_BETA),
        'w_out': nrm(ks[9], (DEPTH, D_MODEL, D_MODEL), D_MODEL ** -0.5 * DEEPNORM_BETA),
        'ln1_g': 1.0 + nrm(ks[10], (DEPTH, D_MODEL), 0.02),
        'ln1_b': nrm(ks[11], (DEPTH, D_MODEL), 0.02),
        'w_router': nrm(ks[12], (DEPTH, D_MODEL, N_EXPERTS), D_MODEL ** -0.5),
        'w_e_gate': nrm(ks[13], (DEPTH, N_EXPERTS, D_MODEL, D_FF_EXPERT), D_MODEL ** -0.5),
        'w_e_up': nrm(ks[14], (DEPTH, N_EXPERTS, D_MODEL, D_FF_EXPERT), D_MODEL ** -0.5),
        'w_e_down': nrm(ks[15], (DEPTH, N_EXPERTS, D_FF_EXPERT, D_MODEL), D_FF_EXPERT ** -0.5 * DEEPNORM_BETA),
        'ln2_g': 1.0 + nrm(ks[16], (DEPTH, D_MODEL), 0.02),
        'ln2_b': nrm(ks[17], (DEPTH, D_MODEL), 0.02),
    }


def reference(x, meta, ln0_g, ln0_b, w_in, b_gate, sink, w_attn_o, w_four_o, w_out,
              ln1_g, ln1_b, w_router, w_e_gate, w_e_up, w_e_down, ln2_g, ln2_b):
    B = x.shape[0]
    m = jnp.broadcast_to(meta[None].astype(x.dtype), (B, N_META, D_MODEL))
    h = layer_norm(jnp.concatenate([m, x], axis=1), ln0_g, ln0_b)
    for l in range(DEPTH):
        mix = hybrid_mixer(h, w_in[l], b_gate[l], sink[l], w_attn_o[l], w_four_o[l], w_out[l])
        h = layer_norm(DEEPNORM_ALPHA * h + mix, ln1_g[l], ln1_b[l])
        ffn = expert_choice_moe(h, w_router[l], w_e_gate[l], w_e_up[l], w_e_down[l])
        h = layer_norm(DEEPNORM_ALPHA * h + ffn, ln2_g[l], ln2_b[l])
    return h[:, N_META:]
```

```python
import numpy as np
import jax
import jax.numpy as jnp
from jax import lax
from jax.experimental import pallas as pl

D_MODEL = 1024
DEPTH = 4
N_META = 16
N_HEADS = 8
N_KV_HEADS = 2
HEAD_DIM = 64
GQA_GROUP = N_HEADS // N_KV_HEADS
ATTN_WIDTH = N_HEADS * HEAD_DIM
KV_WIDTH = N_KV_HEADS * HEAD_DIM
WINDOW = 128
BLOCK = 128
N_FOURIER_GROUPS = 4
FOURIER_GROUP = 128
FOURIER_WIDTH = 512
N_BRANCHES = 2
N_EXPERTS = 16
CAPACITY_FACTOR = 2
DEEPNORM_ALPHA = (2 * DEPTH) ** 0.25
LN_EPS = 1e-5
NEG_INF = -1e30
Q_END = ATTN_WIDTH
K_END = Q_END + KV_WIDTH
V_END = K_END + KV_WIDTH
F_END = V_END + FOURIER_WIDTH


def _ln_kernel(x_ref, g_ref, b_ref, o_ref):
    x = x_ref[...]
    mu = jnp.mean(x, axis=-1, keepdims=True)
    xc = x - mu
    var = jnp.mean(xc * xc, axis=-1, keepdims=True)
    o_ref[...] = xc * lax.rsqrt(var + LN_EPS) * g_ref[...] + b_ref[...]


def _ln_pallas(x2d, g, b, tm=512):
    n, d = x2d.shape
    return pl.pallas_call(
        _ln_kernel,
        grid=(n // tm,),
        in_specs=[pl.BlockSpec((tm, d), lambda i: (i, 0)),
                  pl.BlockSpec((1, d), lambda i: (0, 0)),
                  pl.BlockSpec((1, d), lambda i: (0, 0))],
        out_specs=pl.BlockSpec((tm, d), lambda i: (i, 0)),
        out_shape=jax.ShapeDtypeStruct((n, d), jnp.float32),
    )(x2d, g.reshape(1, d), b.reshape(1, d))


def _layer_norm(x, g, b):
    mu = jnp.mean(x, axis=-1, keepdims=True)
    var = jnp.mean(jnp.square(x - mu), axis=-1, keepdims=True)
    return (x - mu) * lax.rsqrt(var + LN_EPS) * g + b


def _attn(q, k, v, sink):
    B, L = q.shape[0], q.shape[1]
    nb = -(-L // BLOCK)
    lp = nb * BLOCK
    lead = lp - L
    qb = jnp.pad(q, ((0, 0), (lead, 0), (0, 0), (0, 0))).reshape(B, nb, BLOCK, N_KV_HEADS, GQA_GROUP, HEAD_DIM)
    kp = jnp.pad(k, ((0, 0), (lead + BLOCK, BLOCK), (0, 0), (0, 0))).reshape(B, nb + 2, BLOCK, N_KV_HEADS, HEAD_DIM)
    vp = jnp.pad(v, ((0, 0), (lead + BLOCK, BLOCK), (0, 0), (0, 0))).reshape(B, nb + 2, BLOCK, N_KV_HEADS, HEAD_DIM)
    kwin = jnp.concatenate([kp[:, :-2], kp[:, 1:-1], kp[:, 2:]], axis=2)
    vwin = jnp.concatenate([vp[:, :-2], vp[:, 1:-1], vp[:, 2:]], axis=2)
    scores = jnp.einsum('bnqkgd,bnskd->bnkgqs', qb, kwin) * (HEAD_DIM ** -0.5)
    s_idx = jnp.arange(3 * BLOCK)
    rel = s_idx[None, :] - BLOCK - jnp.arange(BLOCK)[:, None]
    kpos = (jnp.arange(nb)[:, None] - 1) * BLOCK + s_idx[None, :]
    kvalid = (kpos >= lead) & (kpos < lp)
    mask = (jnp.abs(rel) <= WINDOW)[None, :, :] & kvalid[:, None, :]
    slopes = jnp.asarray(np.array([2.0 ** (-8.0 * (h + 1) / N_HEADS) for h in range(N_HEADS)], np.float32)).reshape(N_KV_HEADS, GQA_GROUP)
    bias = -slopes[:, :, None, None] * jnp.abs(rel).astype(jnp.float32)
    logits = jnp.where(mask[None, :, None, None], scores + bias, NEG_INF)
    sink_l = sink.reshape(N_KV_HEADS, GQA_GROUP)[:, :, None]
    m = jnp.maximum(jnp.max(logits, axis=-1), sink_l)
    p = jnp.exp(logits - m[..., None])
    denom = jnp.sum(p, axis=-1) + jnp.exp(sink_l - m)
    o = jnp.einsum('bnkgqs,bnskd->bnqkgd', p, vwin)
    o = o / jnp.transpose(denom, (0, 1, 4, 2, 3))[..., None]
    return o.reshape(B, lp, ATTN_WIDTH)[:, lead:]


def _fourier(u):
    B, L = u.shape[0], u.shape[1]
    uf = u.reshape(B, L, N_FOURIER_GROUPS, FOURIER_GROUP)
    z = jnp.fft.fft2(uf, axes=(1, 3), norm='ortho').real
    return z.reshape(B, L, FOURIER_WIDTH)


def _mixer(h, w_in, b_gate, sink, w_attn_o, w_four_o, w_out):
    B, L, _ = h.shape
    u = jnp.einsum('bld,dn->bln', h, w_in)
    q = u[..., :Q_END].reshape(B, L, N_HEADS, HEAD_DIM)
    k = u[..., Q_END:K_END].reshape(B, L, N_KV_HEADS, HEAD_DIM)
    v = u[..., K_END:V_END].reshape(B, L, N_KV_HEADS, HEAD_DIM)
    uf = u[..., V_END:F_END]
    ug = u[..., F_END:].reshape(B, L, N_BRANCHES, D_MODEL)
    attn = _attn(q, k, v, sink)
    four = _fourier(uf)
    gates = jax.nn.sigmoid(ug + b_gate)
    merged = (gates[:, :, 0] * jnp.einsum('blc,cd->bld', attn, w_attn_o)
              + gates[:, :, 1] * jnp.einsum('blc,cd->bld', four, w_four_o))
    return jnp.einsum('bld,de->ble', merged, w_out)


def _moe(h, w_router, w_gate, w_up, w_down):
    B, L, _ = h.shape
    cap = CAPACITY_FACTOR * L // N_EXPERTS
    aff = jax.nn.softmax(jnp.einsum('bld,de->ble', h, w_router), axis=-1)
    g, idx = lax.top_k(jnp.swapaxes(aff, 1, 2), cap)
    bidx = jnp.arange(B)[:, None, None]
    xg = h[bidx, idx]
    a = jnp.einsum('becd,edf->becf', xg, w_gate)
    b = jnp.einsum('becd,edf->becf', xg, w_up)
    y = jnp.einsum('becf,efd->becd', jax.nn.silu(a) * b, w_down)
    y = y * g[..., None]
    return jnp.zeros_like(h).at[bidx, idx].add(y)


def kernel(x, meta, ln0_g, ln0_b, w_in, b_gate, sink, w_attn_o, w_four_o, w_out, ln1_g, ln1_b, w_router, w_e_gate, w_e_up, w_e_down, ln2_g, ln2_b):
    B = x.shape[0]
    m = jnp.broadcast_to(meta[None], (B, N_META, D_MODEL))
    h = _layer_norm(jnp.concatenate([m, x], axis=1), ln0_g, ln0_b)
    for l in range(DEPTH):
        mix = _mixer(h, w_in[l], b_gate[l], sink[l], w_attn_o[l], w_four_o[l], w_out[l])
        h = _layer_norm(DEEPNORM_ALPHA * h + mix, ln1_g[l], ln1_b[l])
        ffn = _moe(h, w_router[l], w_e_gate[l], w_e_up[l], w_e_down[l])
        pre = DEEPNORM_ALPHA * h + ffn
        if l == DEPTH - 1:
            pre = pre[:, N_META:]
            h = _ln_pallas(pre.reshape(-1, D_MODEL), ln2_g[l], ln2_b[l]).reshape(B, -1, D_MODEL)
        else:
            h = _layer_norm(pre, ln2_g[l], ln2_b[l])
    return h
```

```python
import functools
import math

import numpy as np
import jax
import jax.numpy as jnp
from jax import lax
from jax.experimental import pallas as pl
from jax.experimental.pallas import tpu as pltpu

D_MODEL = 1024
N_META = 16
N_HEADS = 8
N_KV_HEADS = 2
HEAD_DIM = 64
GQA_GROUP = N_HEADS // N_KV_HEADS
ATTN_WIDTH = N_HEADS * HEAD_DIM
KV_WIDTH = N_KV_HEADS * HEAD_DIM
WINDOW = 128
BLOCK = 128
N_FOURIER_GROUPS = 4
FOURIER_GROUP = 128
FOURIER_WIDTH = N_FOURIER_GROUPS * FOURIER_GROUP
N_BRANCHES = 2
GATE_WIDTH = N_BRANCHES * D_MODEL
N_EXPERTS = 16
CAPACITY_FACTOR = 2
D_FF_EXPERT = 1536
LN_EPS = 1e-5
NEG_INF = -1e30
Q_END = ATTN_WIDTH
K_END = Q_END + KV_WIDTH
V_END = K_END + KV_WIDTH
F_END = V_END + FOURIER_WIDTH
IN_WIDTH = F_END + GATE_WIDTH

LANES = 128
SUBLANES = 8
VMEM_LIMIT_BYTES = 56 * 1024 * 1024

F32 = jnp.float32
BF16 = jnp.bfloat16
NT_DIMS = (((1,), (1,)), ((), ()))


def _pick_tile(n, candidates):
    for c in candidates:
        if n % c == 0:
            return c
    raise ValueError(f"no tile in {candidates} divides {n}")


def _params(*sem):
    return pltpu.CompilerParams(dimension_semantics=sem, vmem_limit_bytes=VMEM_LIMIT_BYTES)


def _ln(x, g, b):
    mu = jnp.mean(x, axis=-1, keepdims=True)
    xc = x - mu
    var = jnp.mean(xc * xc, axis=-1, keepdims=True)
    return xc * lax.rsqrt(var + LN_EPS) * g + b


def _embed_kernel(x_ref, meta_ref, g_ref, b_ref, o_ref, *, lead):
    j = pl.program_id(1)

    @pl.when(j == 0)
    def _():
        o_ref[0:lead, :] = jnp.zeros((lead, D_MODEL), F32)
        o_ref[lead:BLOCK, :] = _ln(meta_ref[...], g_ref[...], b_ref[...])

    @pl.when(j > 0)
    def _():
        o_ref[...] = _ln(x_ref[...], g_ref[...], b_ref[...])


def _embed(x, meta, g, b, nb, lead):
    B = x.shape[0]
    return pl.pallas_call(
        functools.partial(_embed_kernel, lead=lead),
        grid=(B, nb),
        in_specs=[
            pl.BlockSpec((None, BLOCK, D_MODEL), lambda bi, j: (bi, jnp.maximum(j - 1, 0), 0)),
            pl.BlockSpec((N_META, D_MODEL), lambda bi, j: (0, 0)),
            pl.BlockSpec((1, D_MODEL), lambda bi, j: (0, 0)),
            pl.BlockSpec((1, D_MODEL), lambda bi, j: (0, 0)),
        ],
        out_specs=pl.BlockSpec((None, BLOCK, D_MODEL), lambda bi, j: (bi, j, 0)),
        out_shape=jax.ShapeDtypeStruct((B, nb * BLOCK, D_MODEL), F32),
        compiler_params=_params("parallel", "arbitrary"),
        name="embed_ln",
    )(x, meta, g.reshape(1, -1), b.reshape(1, -1))


def _inproj_kernel(h_ref, w_ref, bg_ref, cs_ref, q_ref, kv_ref, pq_ref, gate_ref):
    hb = h_ref[...].astype(BF16)
    q_ref[...] = jnp.dot(hb, w_ref[:, 0:Q_END], preferred_element_type=F32).astype(BF16)
    kv_ref[...] = jnp.dot(hb, w_ref[:, Q_END:V_END], preferred_element_type=F32).astype(BF16)
    uf = jnp.dot(hb, w_ref[:, V_END:F_END], preferred_element_type=F32).astype(BF16)
    for g in range(N_FOURIER_GROUPS):
        lo = g * FOURIER_GROUP
        pq = jnp.dot(uf[:, lo:lo + FOURIER_GROUP], cs_ref[...], preferred_element_type=F32)
        pq_ref[:, lo:lo + FOURIER_GROUP] = pq[:, 0:FOURIER_GROUP].astype(BF16)
        pq_ref[:, FOURIER_WIDTH + lo:FOURIER_WIDTH + lo + FOURIER_GROUP] = (
            pq[:, FOURIER_GROUP:2 * FOURIER_GROUP].astype(BF16))
    chunk = 512
    for c in range(GATE_WIDTH // chunk):
        lo = c * chunk
        ug = jnp.dot(hb, w_ref[:, F_END + lo:F_END + lo + chunk], preferred_element_type=F32)
        gate_ref[:, lo:lo + chunk] = jax.nn.sigmoid(ug + bg_ref[:, lo:lo + chunk])


def _inproj(h, w_bf16, b_gate, cs, tm):
    n = h.shape[0]
    row = lambda i: (i, 0)
    const = lambda i: (0, 0)
    return pl.pallas_call(
        _inproj_kernel,
        grid=(n // tm,),
        in_specs=[
            pl.BlockSpec((tm, D_MODEL), row),
            pl.BlockSpec((D_MODEL, IN_WIDTH), const),
            pl.BlockSpec((1, GATE_WIDTH), const),
            pl.BlockSpec((FOURIER_GROUP, 2 * FOURIER_GROUP), const),
        ],
        out_specs=[
            pl.BlockSpec((tm, ATTN_WIDTH), row),
            pl.BlockSpec((tm, 2 * KV_WIDTH), row),
            pl.BlockSpec((tm, 2 * FOURIER_WIDTH), row),
            pl.BlockSpec((tm, GATE_WIDTH), row),
        ],
        out_shape=[
            jax.ShapeDtypeStruct((n, ATTN_WIDTH), BF16),
            jax.ShapeDtypeStruct((n, 2 * KV_WIDTH), BF16),
            jax.ShapeDtypeStruct((n, 2 * FOURIER_WIDTH), BF16),
            jax.ShapeDtypeStruct((n, GATE_WIDTH), F32),
        ],
        compiler_params=_params("parallel"),
        name="inproj",
    )(h, w_bf16, b_gate.reshape(1, GATE_WIDTH), cs)


def _attn_kernel(sink_ref, q_ref, kvp_ref, kvc_ref, kvn_ref, bias_ref, o_ref, *, lead, lp):
    i = pl.program_id(1)
    kpos = (i - 1) * BLOCK + lax.broadcasted_iota(jnp.int32, (1, 3 * BLOCK), 1)
    colpen = jnp.where((kpos >= lead) & (kpos < lp), 0.0, NEG_INF).astype(F32)
    q = q_ref[...]
    kv = jnp.concatenate([kvp_ref[...], kvc_ref[...], kvn_ref[...]], axis=0)
    outs = []
    for kvh in range(N_KV_HEADS):
        k = kv[:, kvh * HEAD_DIM:(kvh + 1) * HEAD_DIM]
        v = kv[:, KV_WIDTH + kvh * HEAD_DIM:KV_WIDTH + (kvh + 1) * HEAD_DIM]
        heads = [kvh * GQA_GROUP + g for g in range(GQA_GROUP)]
        qg = jnp.concatenate([q[:, h * HEAD_DIM:(h + 1) * HEAD_DIM] for h in heads], axis=0)
        sink = jnp.concatenate(
            [jnp.full((BLOCK, 1), sink_ref[h], F32) for h in heads], axis=0)
        s = lax.dot_general(qg, k, NT_DIMS, preferred_element_type=F32)
        logits = s * (HEAD_DIM ** -0.5) + bias_ref[kvh] + colpen
        m = jnp.maximum(jnp.max(logits, axis=-1, keepdims=True), sink)
        p = jnp.exp(logits - m)
        denom = jnp.sum(p, axis=-1, keepdims=True) + jnp.exp(sink - m)
        o = jnp.dot(p.astype(BF16), v, preferred_element_type=F32) / denom
        outs.extend(o[g * BLOCK:(g + 1) * BLOCK, :] for g in range(GQA_GROUP))
    o_ref[...] = jnp.concatenate(outs, axis=1).astype(BF16)


def _attn_bias_table():
    qi = np.arange(BLOCK)[:, None]
    si = np.arange(3 * BLOCK)[None, :]
    rel = np.abs(si - BLOCK - qi).astype(np.float32)
    slopes = np.array([2.0 ** (-8.0 * (h + 1) / N_HEADS) for h in range(N_HEADS)], np.float32)
    tab = np.where(rel[None] <= WINDOW, -slopes[:, None, None] * rel[None], np.float32(NEG_INF))
    return jnp.asarray(tab.reshape(N_KV_HEADS, GQA_GROUP * BLOCK, 3 * BLOCK).astype(np.float32))


def _attention(q, kv, sink, bias, nb, lead):
    B, lp, _ = q.shape
    kvspec = lambda f: pl.BlockSpec((None, BLOCK, 2 * KV_WIDTH), f)
    return pl.pallas_call(
        functools.partial(_attn_kernel, lead=lead, lp=lp),
        grid=(B, nb),
        in_specs=[
            pl.BlockSpec(memory_space=pltpu.SMEM),
            pl.BlockSpec((None, BLOCK, ATTN_WIDTH), lambda b, i: (b, i, 0)),
            kvspec(lambda b, i: (b, jnp.maximum(i - 1, 0), 0)),
            kvspec(lambda b, i: (b, i, 0)),
            kvspec(lambda b, i: (b, jnp.minimum(i + 1, nb - 1), 0)),
            pl.BlockSpec((N_KV_HEADS, GQA_GROUP * BLOCK, 3 * BLOCK), lambda b, i: (0, 0, 0)),
        ],
        out_specs=pl.BlockSpec((None, BLOCK, ATTN_WIDTH), lambda b, i: (b, i, 0)),
        out_shape=jax.ShapeDtypeStruct((B, lp, ATTN_WIDTH), BF16),
        compiler_params=_params("parallel", "arbitrary"),
        name="attention",
    )(sink, q, kv, kv, kv, bias)


def _fourier_kernel(w_ref, p_ref, q_ref, o_ref, *, lp):
    z = jnp.dot(w_ref[:, 0:lp], p_ref[...], preferred_element_type=F32)
    z = z + jnp.dot(w_ref[:, lp:2 * lp], q_ref[...], preferred_element_type=F32)
    o_ref[...] = z.astype(BF16)


def _dft_matrix(lp, lead, seq_len):
    pos = jnp.arange(lp, dtype=jnp.int32) - lead
    m = (pos[:, None] * pos[None, :]) % seq_len
    ang = m.astype(F32) * np.float32(2.0 * math.pi / seq_len)
    valid = (pos[:, None] >= 0) & (pos[None, :] >= 0)
    scale = np.float32(seq_len ** -0.5)
    wc = jnp.where(valid, jnp.cos(ang) * scale, 0.0)
    ws = jnp.where(valid, -jnp.sin(ang) * scale, 0.0)
    return jnp.concatenate([wc, ws], axis=1).astype(BF16)


def _fourier(w, pq, tm):
    B, lp, _ = pq.shape
    return pl.pallas_call(
        functools.partial(_fourier_kernel, lp=lp),
        grid=(lp // tm, B),
        in_specs=[
            pl.BlockSpec((tm, 2 * lp), lambda i, b: (i, 0)),
            pl.BlockSpec((None, lp, FOURIER_WIDTH), lambda i, b: (b, 0, 0)),
            pl.BlockSpec((None, lp, FOURIER_WIDTH), lambda i, b: (b, 0, 1)),
        ],
        out_specs=pl.BlockSpec((None, tm, FOURIER_WIDTH), lambda i, b: (b, i, 0)),
        out_shape=jax.ShapeDtypeStruct((B, lp, FOURIER_WIDTH), BF16),
        compiler_params=_params("parallel", "arbitrary"),
        name="fourier",
    )(w, pq, pq)


def _split_bf16(x):
    hi = x.astype(BF16)
    lo = (x - hi.astype(F32)).astype(BF16)
    return hi, lo


def _outproj_kernel(attn_ref, four_ref, gate_ref, h_ref, wa_ref, wf_ref, wo_ref, g_ref, b_ref,
                    wrh_ref, wrl_ref, wrth_ref, wrtl_ref,
                    h1_ref, h1b_ref, affr_ref, affc_ref, *, alpha):
    ya = jnp.dot(attn_ref[...], wa_ref[...], preferred_element_type=F32)
    yf = jnp.dot(four_ref[...], wf_ref[...], preferred_element_type=F32)
    merged = gate_ref[:, 0:D_MODEL] * ya + gate_ref[:, D_MODEL:GATE_WIDTH] * yf
    mix = jnp.dot(merged.astype(BF16), wo_ref[...], preferred_element_type=F32)
    h1 = _ln(alpha * h_ref[...] + mix, g_ref[...], b_ref[...])
    h1_ref[...] = h1
    hi, lo = _split_bf16(h1)
    h1b_ref[...] = hi
    lc = (jnp.dot(hi, wrh_ref[...], preferred_element_type=F32)
          + jnp.dot(hi, wrl_ref[...], preferred_element_type=F32)
          + jnp.dot(lo, wrh_ref[...], preferred_element_type=F32))
    lane = lax.broadcasted_iota(jnp.int32, lc.shape, 1)
    lc = jnp.where(lane < N_EXPERTS, lc, NEG_INF)
    ec = jnp.exp(lc - jnp.max(lc, axis=-1, keepdims=True))
    affc_ref[...] = (ec / jnp.sum(ec, axis=-1, keepdims=True))[:, 0:N_EXPERTS]
    lr = (lax.dot_general(wrth_ref[...], hi, NT_DIMS, preferred_element_type=F32)
          + lax.dot_general(wrtl_ref[...], hi, NT_DIMS, preferred_element_type=F32)
          + lax.dot_general(wrth_ref[...], lo, NT_DIMS, preferred_element_type=F32))
    er = jnp.exp(lr - jnp.max(lr, axis=0, keepdims=True))
    affr_ref[...] = er / jnp.sum(er, axis=0, keepdims=True)


def _outproj(attn, four, gates, h, wa, wf, wo, g, b, w_router, alpha, tm):
    n = h.shape[0]
    row = lambda i: (i, 0)
    const = lambda i: (0, 0)
    wr_pad = jnp.pad(w_router, ((0, 0), (0, LANES - N_EXPERTS)))
    wrh, wrl = _split_bf16(wr_pad)
    wrth, wrtl = _split_bf16(w_router.T)
    return pl.pallas_call(
        functools.partial(_outproj_kernel, alpha=alpha),
        grid=(n // tm,),
        in_specs=[
            pl.BlockSpec((tm, ATTN_WIDTH), row),
            pl.BlockSpec((tm, FOURIER_WIDTH), row),
            pl.BlockSpec((tm, GATE_WIDTH), row),
            pl.BlockSpec((tm, D_MODEL), row),
            pl.BlockSpec((ATTN_WIDTH, D_MODEL), const),
            pl.BlockSpec((FOURIER_WIDTH, D_MODEL), const),
            pl.BlockSpec((D_MODEL, D_MODEL), const),
            pl.BlockSpec((1, D_MODEL), const),
            pl.BlockSpec((1, D_MODEL), const),
            pl.BlockSpec((D_MODEL, LANES), const),
            pl.BlockSpec((D_MODEL, LANES), const),
            pl.BlockSpec((N_EXPERTS, D_MODEL), const),
            pl.BlockSpec((N_EXPERTS, D_MODEL), const),
        ],
        out_specs=[
            pl.BlockSpec((tm, D_MODEL), row),
            pl.BlockSpec((tm, D_MODEL), row),
            pl.BlockSpec((N_EXPERTS, tm), lambda i: (0, i)),
            pl.BlockSpec((tm, N_EXPERTS), row),
        ],
        out_shape=[
            jax.ShapeDtypeStruct((n, D_MODEL), F32),
            jax.ShapeDtypeStruct((n, D_MODEL), BF16),
            jax.ShapeDtypeStruct((N_EXPERTS, n), F32),
            jax.ShapeDtypeStruct((n, N_EXPERTS), F32),
        ],
        compiler_params=_params("parallel"),
        name="outproj_ln_router",
    )(attn, four, gates, h, wa, wf, wo, g.reshape(1, -1), b.reshape(1, -1), wrh, wrl, wrth, wrtl)


def _topk_kernel(aff_ref, rr_ref, rc_ref, *, lead, lp, cap):
    nchunk = lp // LANES
    lane = lax.broadcasted_iota(jnp.int32, (N_EXPERTS, lp), 1)
    bits = jnp.where(lane >= lead, pltpu.bitcast(aff_ref[...], jnp.int32), -1)

    def search(i, t):
        cand = t | (jnp.int32(1) << (30 - i))
        cnt = jnp.sum((bits >= cand).astype(jnp.int32), axis=-1, keepdims=True)
        return jnp.where(cnt >= cap, cand, t)

    thr = lax.fori_loop(0, 31, search, jnp.zeros((N_EXPERTS, 1), jnp.int32))
    gt = bits > thr
    eq = bits == thr
    need = cap - jnp.sum(gt.astype(jnp.int32), axis=-1, keepdims=True)

    r = lax.broadcasted_iota(jnp.int32, (LANES, LANES), 0)
    c = lax.broadcasted_iota(jnp.int32, (LANES, LANES), 1)
    upper = (r <= c).astype(BF16)
    lower = (c <= r).astype(BF16)
    ident = (c == r).astype(BF16)

    eqb = eq.astype(BF16)
    off = jnp.zeros((N_EXPERTS, 1), F32)
    needf = need.astype(F32)
    sel_chunks = []
    for k in range(nchunk):
        sl = slice(k * LANES, (k + 1) * LANES)
        pre = jnp.dot(eqb[:, sl], upper, preferred_element_type=F32) + off
        off = pre[:, LANES - 1:LANES]
        sel_chunks.append(gt[:, sl] | (eq[:, sl] & (pre <= needf)))

    off_r = jnp.zeros((N_EXPERTS, 1), F32)
    off_c = jnp.zeros((1, N_EXPERTS), F32)
    for k in range(nchunk):
        sl = slice(k * LANES, (k + 1) * LANES)
        sel = sel_chunks[k]
        selb = sel.astype(BF16)
        pre_r = jnp.dot(selb, upper, preferred_element_type=F32) + off_r
        rr_ref[:, sl] = jnp.where(sel, pre_r - 1.0, -1.0).astype(jnp.int32)
        off_r = pre_r[:, LANES - 1:LANES]
        pre_c = lax.dot_general(lower, selb, NT_DIMS, preferred_element_type=F32) + off_c
        sel_c = lax.dot_general(ident, selb, NT_DIMS, preferred_element_type=F32)
        rc_ref[sl, :] = jnp.where(sel_c > 0.5, pre_c - 1.0, -1.0).astype(jnp.int32)
        off_c = pre_c[LANES - 1:LANES, :]


def _topk(aff_r, B, lp, lead, cap):
    n = B * lp
    return pl.pallas_call(
        functools.partial(_topk_kernel, lead=lead, lp=lp, cap=cap),
        grid=(B,),
        in_specs=[pl.BlockSpec((N_EXPERTS, lp), lambda b: (0, b))],
        out_specs=[
            pl.BlockSpec((N_EXPERTS, lp), lambda b: (0, b)),
            pl.BlockSpec((lp, N_EXPERTS), lambda b: (b, 0)),
        ],
        out_shape=[
            jax.ShapeDtypeStruct((N_EXPERTS, n), jnp.int32),
            jax.ShapeDtypeStruct((n, N_EXPERTS), jnp.int32),
        ],
        compiler_params=_params("parallel"),
        name="topk_select",
    )(aff_r)


def _gather_kernel(rank_ref, h_ref, o_ref, *, slots, lp):
    slot = lax.broadcasted_iota(jnp.int32, (slots, lp), 0)
    onehot = (slot == rank_ref[...]).astype(BF16)
    o_ref[...] = jnp.dot(onehot, h_ref[...], preferred_element_type=F32).astype(BF16)


def _gather(rank_r, h1b, B, lp, slots):
    rank4 = rank_r.reshape(N_EXPERTS, B, 1, lp)
    return pl.pallas_call(
        functools.partial(_gather_kernel, slots=slots, lp=lp),
        grid=(B, N_EXPERTS),
        in_specs=[
            pl.BlockSpec((None, None, 1, lp), lambda b, e: (e, b, 0, 0)),
            pl.BlockSpec((lp, D_MODEL), lambda b, e: (b, 0)),
        ],
        out_specs=pl.BlockSpec((None, None, slots, D_MODEL), lambda b, e: (b, e, 0, 0)),
        out_shape=jax.ShapeDtypeStruct((B, N_EXPERTS, slots, D_MODEL), BF16),
        compiler_params=_params("parallel", "arbitrary"),
        name="moe_gather",
    )(rank4, h1b)


def _ffn_kernel(x_ref, wg_ref, wu_ref, wd_ref, o_ref):
    x = x_ref[...]
    a = jnp.dot(x, wg_ref[...], preferred_element_type=F32)
    u = jnp.dot(x, wu_ref[...], preferred_element_type=F32)
    hm = (a * jax.nn.sigmoid(a) * u).astype(BF16)
    o_ref[...] = jnp.dot(hm, wd_ref[...], preferred_element_type=F32).astype(BF16)


def _ffn(xg, wg, wu, wd):
    B, _, slots, _ = xg.shape
    act = pl.BlockSpec((None, None, slots, D_MODEL), lambda e, b: (b, e, 0, 0))
    return pl.pallas_call(
        _ffn_kernel,
        grid=(N_EXPERTS, B),
        in_specs=[
            act,
            pl.BlockSpec((None, D_MODEL, D_FF_EXPERT), lambda e, b: (e, 0, 0)),
            pl.BlockSpec((None, D_MODEL, D_FF_EXPERT), lambda e, b: (e, 0, 0)),
            pl.BlockSpec((None, D_FF_EXPERT, D_MODEL), lambda e, b: (e, 0, 0)),
        ],
        out_specs=act,
        out_shape=jax.ShapeDtypeStruct(xg.shape, BF16),
        compiler_params=_params("parallel", "arbitrary"),
        name="moe_ffn",
    )(xg, wg, wu, wd)


def _combine_kernel(rank_ref, aff_ref, y_ref, h_ref, g_ref, b_ref, o_ref, acc_ref, *, slots, alpha):
    e = pl.program_id(2)

    @pl.when(e == 0)
    def _():
        acc_ref[...] = jnp.zeros_like(acc_ref)

    lane = lax.broadcasted_iota(jnp.int32, rank_ref.shape, 1)
    mine = lane == e
    rank = jnp.sum(jnp.where(mine, rank_ref[...], 0), axis=-1, keepdims=True)
    gate = jnp.sum(jnp.where(mine, aff_ref[...], 0.0), axis=-1, keepdims=True)
    slot = lax.broadcasted_iota(jnp.int32, (rank.shape[0], slots), 1)
    onehot = (slot == rank).astype(BF16)
    acc_ref[...] += gate * jnp.dot(onehot, y_ref[...], preferred_element_type=F32)

    @pl.when(e == N_EXPERTS - 1)
    def _():
        o_ref[...] = _ln(alpha * h_ref[...] + acc_ref[...], g_ref[...], b_ref[...])


def _combine(rank_c, aff_c, y, h1, g, b, B, lp, alpha, tt):
    n = B * lp
    slots = y.shape[2]
    nt = lp // tt
    tok = lambda bi, t, e: (bi * nt + t, 0)
    const = lambda bi, t, e: (0, 0)
    return pl.pallas_call(
        functools.partial(_combine_kernel, slots=slots, alpha=alpha),
        grid=(B, nt, N_EXPERTS),
        in_specs=[
            pl.BlockSpec((tt, N_EXPERTS), tok),
            pl.BlockSpec((tt, N_EXPERTS), tok),
            pl.BlockSpec((None, None, slots, D_MODEL), lambda bi, t, e: (bi, e, 0, 0)),
            pl.BlockSpec((tt, D_MODEL), tok),
            pl.BlockSpec((1, D_MODEL), const),
            pl.BlockSpec((1, D_MODEL), const),
        ],
        out_specs=pl.BlockSpec((tt, D_MODEL), tok),
        out_shape=jax.ShapeDtypeStruct((n, D_MODEL), F32),
        scratch_shapes=[pltpu.VMEM((tt, D_MODEL), F32)],
        compiler_params=_params("parallel", "parallel", "arbitrary"),
        name="moe_combine_ln",
    )(rank_c, aff_c, y, h1, g.reshape(1, -1), b.reshape(1, -1))


def kernel(x, meta, ln0_g, ln0_b, w_in, b_gate, sink, w_attn_o, w_four_o, w_out, ln1_g, ln1_b,
           w_router, w_e_gate, w_e_up, w_e_down, ln2_g, ln2_b):
    B, seq, d = x.shape
    depth = w_in.shape[0]
    assert d == D_MODEL and meta.shape == (N_META, D_MODEL)
    assert seq % BLOCK == 0 and N_META % SUBLANES == 0 and N_META <= BLOCK
    L = seq + N_META
    nb = -(-L // BLOCK)
    lp = nb * BLOCK
    lead = lp - L
    cap = CAPACITY_FACTOR * L // N_EXPERTS
    slots = -(-cap // SUBLANES) * SUBLANES
    alpha = float((2 * depth) ** 0.25)
    n = B * lp
    tm = _pick_tile(n, (512, 256, 128))
    tf = _pick_tile(lp, (384, 640, 128))
    tt = _pick_tile(lp, (1408, 640, 128))

    gi = np.arange(FOURIER_GROUP)
    ang = 2.0 * np.pi * ((gi[:, None] * gi[None, :]) % FOURIER_GROUP) / FOURIER_GROUP
    cs = jnp.asarray(np.concatenate([np.cos(ang), np.sin(ang)], axis=1) * FOURIER_GROUP ** -0.5, dtype=BF16)
    wdft = _dft_matrix(lp, lead, L)
    bias = _attn_bias_table()

    h = _embed(x, meta, ln0_g, ln0_b, nb, lead).reshape(n, D_MODEL)
    for l in range(depth):
        q, kv, pq, gates = _inproj(h, w_in[l].astype(BF16), b_gate[l], cs, tm)
        attn = _attention(q.reshape(B, lp, -1), kv.reshape(B, lp, -1), sink[l], bias, nb, lead)
        four = _fourier(wdft, pq.reshape(B, lp, -1), tf)
        h1, h1b, aff_r, aff_c = _outproj(
            attn.reshape(n, -1), four.reshape(n, -1), gates, h,
            w_attn_o[l].astype(BF16), w_four_o[l].astype(BF16), w_out[l].astype(BF16),
            ln1_g[l], ln1_b[l], w_router[l], alpha, tm)
        rank_r, rank_c = _topk(aff_r, B, lp, lead, cap)
        xg = _gather(rank_r, h1b, B, lp, slots)
        y = _ffn(xg, w_e_gate[l].astype(BF16), w_e_up[l].astype(BF16), w_e_down[l].astype(BF16))
        h = _combine(rank_c, aff_c, y, h1, ln2_g[l], ln2_b[l], B, lp, alpha, tt)
    return h.reshape(B, lp, D_MODEL)[:, lead + N_META:]
```

```python
import functools
import math

import numpy as np
import jax
import jax.numpy as jnp
from jax import lax
from jax.experimental import pallas as pl
from jax.experimental.pallas import tpu as pltpu

D_MODEL = 1024
N_META = 16
N_HEADS = 8
N_KV_HEADS = 2
HEAD_DIM = 64
GQA_GROUP = N_HEADS // N_KV_HEADS
ATTN_WIDTH = N_HEADS * HEAD_DIM
KV_WIDTH = N_KV_HEADS * HEAD_DIM
WINDOW = 128
BLOCK = 128
N_FOURIER_GROUPS = 4
FOURIER_GROUP = 128
FOURIER_WIDTH = N_FOURIER_GROUPS * FOURIER_GROUP
N_BRANCHES = 2
GATE_WIDTH = N_BRANCHES * D_MODEL
N_EXPERTS = 16
CAPACITY_FACTOR = 2
D_FF_EXPERT = 1536
LN_EPS = 1e-5
NEG_INF = -1e30
Q_END = ATTN_WIDTH
K_END = Q_END + KV_WIDTH
V_END = K_END + KV_WIDTH
F_END = V_END + FOURIER_WIDTH
IN_WIDTH = F_END + GATE_WIDTH

LANES = 128
SUBLANES = 8
VMEM_LIMIT_BYTES = 56 * 1024 * 1024

F32 = jnp.float32
BF16 = jnp.bfloat16
NT_DIMS = (((1,), (1,)), ((), ()))


def _pick_tile(n, candidates):
    for c in candidates:
        if n % c == 0:
            return c
    raise ValueError(f"no tile in {candidates} divides {n}")


def _params(*sem):
    return pltpu.CompilerParams(dimension_semantics=sem, vmem_limit_bytes=VMEM_LIMIT_BYTES)


def _ln(x, g, b):
    mu = jnp.mean(x, axis=-1, keepdims=True)
    xc = x - mu
    var = jnp.mean(xc * xc, axis=-1, keepdims=True)
    return xc * lax.rsqrt(var + LN_EPS) * g + b


def _embed_kernel(x_ref, meta_ref, g_ref, b_ref, o_ref, *, lead):
    j = pl.program_id(1)

    @pl.when(j == 0)
    def _():
        o_ref[0:lead, :] = jnp.zeros((lead, D_MODEL), F32)
        o_ref[lead:BLOCK, :] = _ln(meta_ref[...], g_ref[...], b_ref[...])

    @pl.when(j > 0)
    def _():
        o_ref[...] = _ln(x_ref[...], g_ref[...], b_ref[...])


def _embed(x, meta, g, b, nb, lead):
    B = x.shape[0]
    return pl.pallas_call(
        functools.partial(_embed_kernel, lead=lead),
        grid=(B, nb),
        in_specs=[
            pl.BlockSpec((None, BLOCK, D_MODEL), lambda bi, j: (bi, jnp.maximum(j - 1, 0), 0)),
            pl.BlockSpec((N_META, D_MODEL), lambda bi, j: (0, 0)),
            pl.BlockSpec((1, D_MODEL), lambda bi, j: (0, 0)),
            pl.BlockSpec((1, D_MODEL), lambda bi, j: (0, 0)),
        ],
        out_specs=pl.BlockSpec((None, BLOCK, D_MODEL), lambda bi, j: (bi, j, 0)),
        out_shape=jax.ShapeDtypeStruct((B, nb * BLOCK, D_MODEL), F32),
        compiler_params=_params("parallel", "arbitrary"),
        name="embed_ln",
    )(x, meta, g.reshape(1, -1), b.reshape(1, -1))


def _inproj_kernel(h_ref, w_ref, bg_ref, cs_ref, q_ref, kv_ref, pq_ref, gate_ref):
    hb = h_ref[...].astype(BF16)
    q_ref[...] = jnp.dot(hb, w_ref[:, 0:Q_END], preferred_element_type=F32).astype(BF16)
    kv_ref[...] = jnp.dot(hb, w_ref[:, Q_END:V_END], preferred_element_type=F32).astype(BF16)
    uf = jnp.dot(hb, w_ref[:, V_END:F_END], preferred_element_type=F32).astype(BF16)
    for g in range(N_FOURIER_GROUPS):
        lo = g * FOURIER_GROUP
        pq = jnp.dot(uf[:, lo:lo + FOURIER_GROUP], cs_ref[...], preferred_element_type=F32)
        pq_ref[:, lo:lo + FOURIER_GROUP] = pq[:, 0:FOURIER_GROUP].astype(BF16)
        pq_ref[:, FOURIER_WIDTH + lo:FOURIER_WIDTH + lo + FOURIER_GROUP] = (
            pq[:, FOURIER_GROUP:2 * FOURIER_GROUP].astype(BF16))
    chunk = 512
    for c in range(GATE_WIDTH // chunk):
        lo = c * chunk
        ug = jnp.dot(hb, w_ref[:, F_END + lo:F_END + lo + chunk], preferred_element_type=F32)
        gate_ref[:, lo:lo + chunk] = jax.nn.sigmoid(ug + bg_ref[:, lo:lo + chunk])


def _inproj(h, w_bf16, b_gate, cs, tm):
    n = h.shape[0]
    row = lambda i: (i, 0)
    const = lambda i: (0, 0)
    return pl.pallas_call(
        _inproj_kernel,
        grid=(n // tm,),
        in_specs=[
            pl.BlockSpec((tm, D_MODEL), row),
            pl.BlockSpec((D_MODEL, IN_WIDTH), const),
            pl.BlockSpec((1, GATE_WIDTH), const),
            pl.BlockSpec((FOURIER_GROUP, 2 * FOURIER_GROUP), const),
        ],
        out_specs=[
            pl.BlockSpec((tm, ATTN_WIDTH), row),
            pl.BlockSpec((tm, 2 * KV_WIDTH), row),
            pl.BlockSpec((tm, 2 * FOURIER_WIDTH), row),
            pl.BlockSpec((tm, GATE_WIDTH), row),
        ],
        out_shape=[
            jax.ShapeDtypeStruct((n, ATTN_WIDTH), BF16),
            jax.ShapeDtypeStruct((n, 2 * KV_WIDTH), BF16),
            jax.ShapeDtypeStruct((n, 2 * FOURIER_WIDTH), BF16),
            jax.ShapeDtypeStruct((n, GATE_WIDTH), F32),
        ],
        compiler_params=_params("parallel"),
        name="inproj",
    )(h, w_bf16, b_gate.reshape(1, GATE_WIDTH), cs)


def _attn_kernel(sink_ref, q_ref, kvp_ref, kvc_ref, kvn_ref, bias_ref, o_ref, *, lead, lp):
    i = pl.program_id(1)
    kpos = (i - 1) * BLOCK + lax.broadcasted_iota(jnp.int32, (1, 3 * BLOCK), 1)
    colpen = jnp.where((kpos >= lead) & (kpos < lp), 0.0, NEG_INF).astype(F32)
    kv = jnp.concatenate([kvp_ref[...], kvc_ref[...], kvn_ref[...]], axis=0)
    low_half = lax.broadcasted_iota(jnp.int32, (3 * BLOCK, LANES), 1) < HEAD_DIM

    def lane_half_operands(x):
        swapped = jnp.concatenate([x[:, HEAD_DIM:], x[:, :HEAD_DIM]], axis=1)
        zero = jnp.zeros_like(x)
        return {(kvh, half): jnp.where(low_half if half == 0 else ~low_half,
                                       x if kvh == half else swapped, zero)
                for kvh in range(N_KV_HEADS) for half in range(2)}

    k_ops = lane_half_operands(kv[:, 0:KV_WIDTH])
    v_ops = lane_half_operands(kv[:, KV_WIDTH:2 * KV_WIDTH])

    def scores(h):
        pair = h // 2
        qp = q_ref[:, pair * LANES:(pair + 1) * LANES]
        return lax.dot_general(qp, k_ops[(h // GQA_GROUP, h % 2)], NT_DIMS, preferred_element_type=F32)

    def head_out(h, s):
        sink = sink_ref[h]
        logits = s * (HEAD_DIM ** -0.5) + bias_ref[h] + colpen
        m = jnp.maximum(jnp.max(logits, axis=-1, keepdims=True), sink)
        p = jnp.exp(logits - m)
        denom = jnp.sum(p, axis=-1, keepdims=True) + jnp.exp(sink - m)
        o = jnp.dot(p.astype(BF16), v_ops[(h // GQA_GROUP, h % 2)], preferred_element_type=F32)
        return o / denom

    s_next = scores(0)
    for h in range(N_HEADS):
        s_cur = s_next
        if h + 1 < N_HEADS:
            s_next = scores(h + 1)
        o = head_out(h, s_cur)
        if h % 2 == 0:
            o_even = o
        else:
            pair = h // 2
            o_ref[:, pair * LANES:(pair + 1) * LANES] = (o_even + o).astype(BF16)


def _attn_bias_table():
    qi = np.arange(BLOCK)[:, None]
    si = np.arange(3 * BLOCK)[None, :]
    rel = np.abs(si - BLOCK - qi).astype(np.float32)
    slopes = np.array([2.0 ** (-8.0 * (h + 1) / N_HEADS) for h in range(N_HEADS)], np.float32)
    tab = np.where(rel[None] <= WINDOW, -slopes[:, None, None] * rel[None], np.float32(NEG_INF))
    return jnp.asarray(tab.astype(np.float32))


def _attention(q, kv, sink, bias, nb, lead):
    B, lp, _ = q.shape
    kvspec = lambda f: pl.BlockSpec((None, BLOCK, 2 * KV_WIDTH), f)
    return pl.pallas_call(
        functools.partial(_attn_kernel, lead=lead, lp=lp),
        grid=(B, nb),
        in_specs=[
            pl.BlockSpec(memory_space=pltpu.SMEM),
            pl.BlockSpec((None, BLOCK, ATTN_WIDTH), lambda b, i: (b, i, 0)),
            kvspec(lambda b, i: (b, jnp.maximum(i - 1, 0), 0)),
            kvspec(lambda b, i: (b, i, 0)),
            kvspec(lambda b, i: (b, jnp.minimum(i + 1, nb - 1), 0)),
            pl.BlockSpec((N_HEADS, BLOCK, 3 * BLOCK), lambda b, i: (0, 0, 0)),
        ],
        out_specs=pl.BlockSpec((None, BLOCK, ATTN_WIDTH), lambda b, i: (b, i, 0)),
        out_shape=jax.ShapeDtypeStruct((B, lp, ATTN_WIDTH), BF16),
        compiler_params=_params("parallel", "arbitrary"),
        name="attention",
    )(sink, q, kv, kv, kv, bias)


def _fourier_kernel(w_ref, p_ref, q_ref, o_ref, *, lp):
    z = jnp.dot(w_ref[:, 0:lp], p_ref[...], preferred_element_type=F32)
    z = z + jnp.dot(w_ref[:, lp:2 * lp], q_ref[...], preferred_element_type=F32)
    o_ref[...] = z.astype(BF16)


def _dft_matrix(lp, lead, seq_len):
    pos = jnp.arange(lp, dtype=jnp.int32) - lead
    m = (pos[:, None] * pos[None, :]) % seq_len
    ang = m.astype(F32) * np.float32(2.0 * math.pi / seq_len)
    valid = (pos[:, None] >= 0) & (pos[None, :] >= 0)
    scale = np.float32(seq_len ** -0.5)
    wc = jnp.where(valid, jnp.cos(ang) * scale, 0.0)
    ws = jnp.where(valid, -jnp.sin(ang) * scale, 0.0)
    return jnp.concatenate([wc, ws], axis=1).astype(BF16)


def _fourier(w, pq, tm):
    B, lp, _ = pq.shape
    return pl.pallas_call(
        functools.partial(_fourier_kernel, lp=lp),
        grid=(lp // tm, B),
        in_specs=[
            pl.BlockSpec((tm, 2 * lp), lambda i, b: (i, 0)),
            pl.BlockSpec((None, lp, FOURIER_WIDTH), lambda i, b: (b, 0, 0)),
            pl.BlockSpec((None, lp, FOURIER_WIDTH), lambda i, b: (b, 0, 1)),
        ],
        out_specs=pl.BlockSpec((None, tm, FOURIER_WIDTH), lambda i, b: (b, i, 0)),
        out_shape=jax.ShapeDtypeStruct((B, lp, FOURIER_WIDTH), BF16),
        compiler_params=_params("parallel", "arbitrary"),
        name="fourier",
    )(w, pq, pq)


def _split_bf16(x):
    hi = x.astype(BF16)
    lo = (x - hi.astype(F32)).astype(BF16)
    return hi, lo


def _outproj_kernel(attn_ref, four_ref, gate_ref, h_ref, wa_ref, wf_ref, wo_ref, g_ref, b_ref,
                    wrh_ref, wrl_ref, wrth_ref, wrtl_ref,
                    h1_ref, h1b_ref, affr_ref, affc_ref, *, alpha):
    ya = jnp.dot(attn_ref[...], wa_ref[...], preferred_element_type=F32)
    yf = jnp.dot(four_ref[...], wf_ref[...], preferred_element_type=F32)
    merged = gate_ref[:, 0:D_MODEL] * ya + gate_ref[:, D_MODEL:GATE_WIDTH] * yf
    mix = jnp.dot(merged.astype(BF16), wo_ref[...], preferred_element_type=F32)
    h1 = _ln(alpha * h_ref[...] + mix, g_ref[...], b_ref[...])
    h1_ref[...] = h1
    hi, lo = _split_bf16(h1)
    h1b_ref[...] = hi
    lc = (jnp.dot(hi, wrh_ref[...], preferred_element_type=F32)
          + jnp.dot(hi, wrl_ref[...], preferred_element_type=F32)
          + jnp.dot(lo, wrh_ref[...], preferred_element_type=F32))
    lane = lax.broadcasted_iota(jnp.int32, lc.shape, 1)
    lc = jnp.where(lane < N_EXPERTS, lc, NEG_INF)
    ec = jnp.exp(lc - jnp.max(lc, axis=-1, keepdims=True))
    affc_ref[...] = (ec / jnp.sum(ec, axis=-1, keepdims=True))[:, 0:N_EXPERTS]
    lr = (lax.dot_general(wrth_ref[...], hi, NT_DIMS, preferred_element_type=F32)
          + lax.dot_general(wrtl_ref[...], hi, NT_DIMS, preferred_element_type=F32)
          + lax.dot_general(wrth_ref[...], lo, NT_DIMS, preferred_element_type=F32))
    er = jnp.exp(lr - jnp.max(lr, axis=0, keepdims=True))
    affr_ref[...] = er / jnp.sum(er, axis=0, keepdims=True)


def _outproj(attn, four, gates, h, wa, wf, wo, g, b, w_router, alpha, tm):
    n = h.shape[0]
    row = lambda i: (i, 0)
    const = lambda i: (0, 0)
    wr_pad = jnp.pad(w_router, ((0, 0), (0, LANES - N_EXPERTS)))
    wrh, wrl = _split_bf16(wr_pad)
    wrth, wrtl = _split_bf16(w_router.T)
    return pl.pallas_call(
        functools.partial(_outproj_kernel, alpha=alpha),
        grid=(n // tm,),
        in_specs=[
            pl.BlockSpec((tm, ATTN_WIDTH), row),
            pl.BlockSpec((tm, FOURIER_WIDTH), row),
            pl.BlockSpec((tm, GATE_WIDTH), row),
            pl.BlockSpec((tm, D_MODEL), row),
            pl.BlockSpec((ATTN_WIDTH, D_MODEL), const),
            pl.BlockSpec((FOURIER_WIDTH, D_MODEL), const),
            pl.BlockSpec((D_MODEL, D_MODEL), const),
            pl.BlockSpec((1, D_MODEL), const),
            pl.BlockSpec((1, D_MODEL), const),
            pl.BlockSpec((D_MODEL, LANES), const),
            pl.BlockSpec((D_MODEL, LANES), const),
            pl.BlockSpec((N_EXPERTS, D_MODEL), const),
            pl.BlockSpec((N_EXPERTS, D_MODEL), const),
        ],
        out_specs=[
            pl.BlockSpec((tm, D_MODEL), row),
            pl.BlockSpec((tm, D_MODEL), row),
            pl.BlockSpec((N_EXPERTS, tm), lambda i: (0, i)),
            pl.BlockSpec((tm, N_EXPERTS), row),
        ],
        out_shape=[
            jax.ShapeDtypeStruct((n, D_MODEL), F32),
            jax.ShapeDtypeStruct((n, D_MODEL), BF16),
            jax.ShapeDtypeStruct((N_EXPERTS, n), F32),
            jax.ShapeDtypeStruct((n, N_EXPERTS), F32),
        ],
        compiler_params=_params("parallel"),
        name="outproj_ln_router",
    )(attn, four, gates, h, wa, wf, wo, g.reshape(1, -1), b.reshape(1, -1), wrh, wrl, wrth, wrtl)


def _topk_kernel(aff_ref, rr_ref, rc_ref, cnt_ref, *, lead, lp, cap):
    nchunk = lp // LANES
    lane = lax.broadcasted_iota(jnp.int32, (N_EXPERTS, lp), 1)
    bits = jnp.where(lane >= lead, pltpu.bitcast(aff_ref[...], jnp.int32), -1)

    def search(i, t):
        cand = t | (jnp.int32(1) << (30 - i))
        cnt = jnp.sum((bits >= cand).astype(jnp.int32), axis=-1, keepdims=True)
        return jnp.where(cnt >= cap, cand, t)

    thr = lax.fori_loop(0, 31, search, jnp.zeros((N_EXPERTS, 1), jnp.int32))
    gt = bits > thr
    eq = bits == thr
    need = cap - jnp.sum(gt.astype(jnp.int32), axis=-1, keepdims=True)

    r = lax.broadcasted_iota(jnp.int32, (LANES, LANES), 0)
    c = lax.broadcasted_iota(jnp.int32, (LANES, LANES), 1)
    upper = (r <= c).astype(BF16)
    lower = (c <= r).astype(BF16)
    ident = (c == r).astype(BF16)

    eqb = eq.astype(BF16)
    off = jnp.zeros((N_EXPERTS, 1), F32)
    needf = need.astype(F32)
    sel_chunks = []
    for k in range(nchunk):
        sl = slice(k * LANES, (k + 1) * LANES)
        pre = jnp.dot(eqb[:, sl], upper, preferred_element_type=F32) + off
        off = pre[:, LANES - 1:LANES]
        sel_chunks.append(gt[:, sl] | (eq[:, sl] & (pre <= needf)))

    off_r = jnp.zeros((N_EXPERTS, 1), F32)
    off_c = jnp.zeros((1, N_EXPERTS), F32)
    cnt_lane = lax.broadcasted_iota(jnp.int32, (N_EXPERTS, LANES), 1)
    cnt = jnp.zeros((N_EXPERTS, LANES), F32)
    for k in range(nchunk):
        sl = slice(k * LANES, (k + 1) * LANES)
        sel = sel_chunks[k]
        selb = sel.astype(BF16)
        pre_r = jnp.dot(selb, upper, preferred_element_type=F32) + off_r
        rr_ref[k] = jnp.where(sel, pre_r - 1.0, -1.0).astype(jnp.int32)
        off_r = pre_r[:, LANES - 1:LANES]
        cnt = jnp.where(cnt_lane == k + 1, off_r, cnt)
        pre_c = lax.dot_general(lower, selb, NT_DIMS, preferred_element_type=F32) + off_c
        sel_c = lax.dot_general(ident, selb, NT_DIMS, preferred_element_type=F32)
        rc_ref[sl, :] = jnp.where(sel_c > 0.5, pre_c - 1.0, -1.0).astype(jnp.int32)
        off_c = pre_c[LANES - 1:LANES, :]
    cnt_ref[...] = cnt.astype(jnp.int32)


def _topk(aff_r, B, lp, lead, cap):
    n = B * lp
    nchunk = lp // LANES
    assert nchunk + 1 <= LANES
    return pl.pallas_call(
        functools.partial(_topk_kernel, lead=lead, lp=lp, cap=cap),
        grid=(B,),
        in_specs=[pl.BlockSpec((N_EXPERTS, lp), lambda b: (0, b))],
        out_specs=[
            pl.BlockSpec((None, nchunk, N_EXPERTS, LANES), lambda b: (b, 0, 0, 0)),
            pl.BlockSpec((lp, N_EXPERTS), lambda b: (b, 0)),
            pl.BlockSpec((None, N_EXPERTS, LANES), lambda b: (b, 0, 0)),
        ],
        out_shape=[
            jax.ShapeDtypeStruct((B, nchunk, N_EXPERTS, LANES), jnp.int32),
            jax.ShapeDtypeStruct((n, N_EXPERTS), jnp.int32),
            jax.ShapeDtypeStruct((B, N_EXPERTS, LANES), jnp.int32),
        ],
        compiler_params=_params("parallel"),
        name="topk_select",
    )(aff_r)


def _slot_tile(slots):
    for n_tiles in range(1, slots // SUBLANES + 1):
        ts = slots // n_tiles
        if slots % n_tiles == 0 and ts % SUBLANES == 0 and ts <= LANES:
            return ts
    raise ValueError(slots)


def _gather_kernel(cnt_ref, rank_ref, h_ref, o_ref, acc_ref, *, slots, ts, nchunk):
    b, e = pl.program_id(0), pl.program_id(1)
    base = (b * N_EXPERTS + e) * (nchunk + 1)
    npair = -(-nchunk // 2)
    slot0 = lax.broadcasted_iota(jnp.int32, (ts, LANES), 0)
    for j in range(slots // ts):
        acc_ref[...] = jnp.zeros_like(acc_ref)

        def pair(c, carry, j=j):
            first = 2 * c
            row0 = jnp.minimum(first, nchunk - 2)
            lo = cnt_ref[base + first]
            hi = cnt_ref[base + jnp.minimum(first + 2, nchunk)]

            @pl.when((hi > j * ts) & (lo < (j + 1) * ts))
            def _():
                r = rank_ref[pl.ds(row0, 2), :]
                slot = slot0 + j * ts
                o_a = (slot == r[0:1, :]) & (row0 == first)
                o_b = slot == r[1:2, :]
                onehot = jnp.concatenate([o_a, o_b], axis=1).astype(BF16)
                rows = h_ref[pl.ds(pl.multiple_of(row0 * LANES, LANES), 2 * LANES), :]
                acc_ref[...] += jnp.dot(onehot, rows, preferred_element_type=F32)
            return carry

        lax.fori_loop(0, npair, pair, 0)
        o_ref[j * ts:(j + 1) * ts, :] = acc_ref[...].astype(BF16)


def _gather(cnt, rank_r, h1b, B, lp, slots):
    nchunk = lp // LANES
    assert nchunk >= 2
    ts = _slot_tile(slots)
    return pl.pallas_call(
        functools.partial(_gather_kernel, slots=slots, ts=ts, nchunk=nchunk),
        grid_spec=pltpu.PrefetchScalarGridSpec(
            num_scalar_prefetch=1,
            grid=(B, N_EXPERTS),
            in_specs=[
                pl.BlockSpec((None, None, nchunk, LANES), lambda b, e, cnt: (b, e, 0, 0)),
                pl.BlockSpec((lp, D_MODEL), lambda b, e, cnt: (b, 0)),
            ],
            out_specs=pl.BlockSpec((None, None, slots, D_MODEL), lambda b, e, cnt: (b, e, 0, 0)),
            scratch_shapes=[pltpu.VMEM((ts, D_MODEL), F32)],
        ),
        out_shape=jax.ShapeDtypeStruct((B, N_EXPERTS, slots, D_MODEL), BF16),
        compiler_params=_params("parallel", "arbitrary"),
        name="moe_gather",
    )(cnt, rank_r, h1b)


def _ffn_kernel(x_ref, wg_ref, wu_ref, wd_ref, o_ref):
    x = x_ref[...]
    a = jnp.dot(x, wg_ref[...], preferred_element_type=F32)
    u = jnp.dot(x, wu_ref[...], preferred_element_type=F32)
    hm = (a * jax.nn.sigmoid(a) * u).astype(BF16)
    o_ref[...] = jnp.dot(hm, wd_ref[...], preferred_element_type=F32).astype(BF16)


def _ffn(xg, wg, wu, wd):
    B, _, slots, _ = xg.shape
    act = pl.BlockSpec((None, None, slots, D_MODEL), lambda e, b: (b, e, 0, 0))
    return pl.pallas_call(
        _ffn_kernel,
        grid=(N_EXPERTS, B),
        in_specs=[
            act,
            pl.BlockSpec((None, D_MODEL, D_FF_EXPERT), lambda e, b: (e, 0, 0)),
            pl.BlockSpec((None, D_MODEL, D_FF_EXPERT), lambda e, b: (e, 0, 0)),
            pl.BlockSpec((None, D_FF_EXPERT, D_MODEL), lambda e, b: (e, 0, 0)),
        ],
        out_specs=act,
        out_shape=jax.ShapeDtypeStruct(xg.shape, BF16),
        compiler_params=_params("parallel", "arbitrary"),
        name="moe_ffn",
    )(xg, wg, wu, wd)


def _combine_kernel(cnt_ref, rank_ref, aff_ref, y_ref, h_ref, g_ref, b_ref, o_ref,
                    acc_ref, rank_s, gate_s, *, slots, ts, nchunk, alpha):
    bi, t, e = pl.program_id(0), pl.program_id(1), pl.program_id(2)
    tile_chunks = acc_ref.shape[0] // LANES
    base = (bi * N_EXPERTS + e) * (nchunk + 1) + t * tile_chunks

    @pl.when(e == 0)
    def _():
        acc_ref[...] = jnp.zeros_like(acc_ref)

    lane = lax.broadcasted_iota(jnp.int32, rank_ref.shape, 1)
    mine = lane == e
    rank_s[...] = jnp.sum(jnp.where(mine, rank_ref[...], 0), axis=-1, keepdims=True)
    gate_s[...] = jnp.sum(jnp.where(mine, aff_ref[...], 0.0), axis=-1, keepdims=True)
    slot0 = lax.broadcasted_iota(jnp.int32, (LANES, ts), 1)

    def chunk(k, carry):
        lo = cnt_ref[base + k]
        hi = cnt_ref[base + k + 1]
        rows = pl.ds(pl.multiple_of(k * LANES, LANES), LANES)
        for j in range(slots // ts):
            @pl.when((hi > j * ts) & (lo < (j + 1) * ts))
            def _(j=j):
                onehot = (rank_s[rows, :] == slot0 + j * ts).astype(BF16)
                part = jnp.dot(onehot, y_ref[j * ts:(j + 1) * ts, :], preferred_element_type=F32)
                acc_ref[rows, :] += gate_s[rows, :] * part
        return carry

    lax.fori_loop(0, tile_chunks, chunk, 0)

    @pl.when(e == N_EXPERTS - 1)
    def _():
        o_ref[...] = _ln(alpha * h_ref[...] + acc_ref[...], g_ref[...], b_ref[...])


def _combine(cnt, rank_c, aff_c, y, h1, g, b, B, lp, alpha, tt):
    n = B * lp
    slots = y.shape[2]
    ts = _slot_tile(slots)
    nt = lp // tt
    tok = lambda bi, t, e, cnt: (bi * nt + t, 0)
    const = lambda bi, t, e, cnt: (0, 0)
    return pl.pallas_call(
        functools.partial(_combine_kernel, slots=slots, ts=ts, nchunk=lp // LANES, alpha=alpha),
        grid_spec=pltpu.PrefetchScalarGridSpec(
            num_scalar_prefetch=1,
            grid=(B, nt, N_EXPERTS),
            in_specs=[
                pl.BlockSpec((tt, N_EXPERTS), tok),
                pl.BlockSpec((tt, N_EXPERTS), tok),
                pl.BlockSpec((None, None, slots, D_MODEL), lambda bi, t, e, cnt: (bi, e, 0, 0)),
                pl.BlockSpec((tt, D_MODEL), tok),
                pl.BlockSpec((1, D_MODEL), const),
                pl.BlockSpec((1, D_MODEL), const),
            ],
            out_specs=pl.BlockSpec((tt, D_MODEL), tok),
            scratch_shapes=[pltpu.VMEM((tt, D_MODEL), F32),
                            pltpu.VMEM((tt, 1), jnp.int32),
                            pltpu.VMEM((tt, 1), F32)],
        ),
        out_shape=jax.ShapeDtypeStruct((n, D_MODEL), F32),
        compiler_params=_params("parallel", "parallel", "arbitrary"),
        name="moe_combine_ln",
    )(cnt, rank_c, aff_c, y, h1, g.reshape(1, -1), b.reshape(1, -1))


def kernel(x, meta, ln0_g, ln0_b, w_in, b_gate, sink, w_attn_o, w_four_o, w_out, ln1_g, ln1_b,
           w_router, w_e_gate, w_e_up, w_e_down, ln2_g, ln2_b):
    B, seq, d = x.shape
    depth = w_in.shape[0]
    assert d == D_MODEL and meta.shape == (N_META, D_MODEL)
    assert seq % BLOCK == 0 and N_META % SUBLANES == 0 and N_META <= BLOCK
    L = seq + N_META
    nb = -(-L // BLOCK)
    lp = nb * BLOCK
    lead = lp - L
    cap = CAPACITY_FACTOR * L // N_EXPERTS
    slots = -(-cap // SUBLANES) * SUBLANES
    alpha = float((2 * depth) ** 0.25)
    n = B * lp
    tm = _pick_tile(n, (512, 256, 128))
    tf = _pick_tile(lp, (384, 640, 128))
    tt = _pick_tile(lp, (1408, 640, 128))

    gi = np.arange(FOURIER_GROUP)
    ang = 2.0 * np.pi * ((gi[:, None] * gi[None, :]) % FOURIER_GROUP) / FOURIER_GROUP
    cs = jnp.asarray(np.concatenate([np.cos(ang), np.sin(ang)], axis=1) * FOURIER_GROUP ** -0.5, dtype=BF16)
    wdft = _dft_matrix(lp, lead, L)
    bias = _attn_bias_table()

    h = _embed(x, meta, ln0_g, ln0_b, nb, lead).reshape(n, D_MODEL)
    for l in range(depth):
        q, kv, pq, gates = _inproj(h, w_in[l].astype(BF16), b_gate[l], cs, tm)
        attn = _attention(q.reshape(B, lp, -1), kv.reshape(B, lp, -1), sink[l], bias, nb, lead)
        four = _fourier(wdft, pq.reshape(B, lp, -1), tf)
        h1, h1b, aff_r, aff_c = _outproj(
            attn.reshape(n, -1), four.reshape(n, -1), gates, h,
            w_attn_o[l].astype(BF16), w_four_o[l].astype(BF16), w_out[l].astype(BF16),
            ln1_g[l], ln1_b[l], w_router[l], alpha, tm)
        rank_r, rank_c, cnt = _topk(aff_r, B, lp, lead, cap)
        cnt = cnt[:, :, :lp // LANES + 1].reshape(-1)
        xg = _gather(cnt, rank_r.transpose(0, 2, 1, 3), h1b, B, lp, slots)
        y = _ffn(xg, w_e_gate[l].astype(BF16), w_e_up[l].astype(BF16), w_e_down[l].astype(BF16))
        h = _combine(cnt, rank_c, aff_c, y, h1, ln2_g[l], ln2_b[l], B, lp, alpha, tt)
    return h.reshape(B, lp, D_MODEL)[:, lead + N_META:]
```

```python
import functools
import math

import numpy as np
import jax
import jax.numpy as jnp
from jax import lax
from jax.experimental import pallas as pl
from jax.experimental.pallas import tpu as pltpu

D_MODEL = 1024
N_META = 16
N_HEADS = 8
N_KV_HEADS = 2
HEAD_DIM = 64
GQA_GROUP = N_HEADS // N_KV_HEADS
ATTN_WIDTH = N_HEADS * HEAD_DIM
KV_WIDTH = N_KV_HEADS * HEAD_DIM
WINDOW = 128
BLOCK = 128
N_FOURIER_GROUPS = 4
FOURIER_GROUP = 128
FOURIER_WIDTH = N_FOURIER_GROUPS * FOURIER_GROUP
N_BRANCHES = 2
GATE_WIDTH = N_BRANCHES * D_MODEL
N_EXPERTS = 16
CAPACITY_FACTOR = 2
D_FF_EXPERT = 1536
LN_EPS = 1e-5
NEG_INF = -1e30
Q_END = ATTN_WIDTH
K_END = Q_END + KV_WIDTH
V_END = K_END + KV_WIDTH
F_END = V_END + FOURIER_WIDTH
IN_WIDTH = F_END + GATE_WIDTH

LANES = 128
SUBLANES = 8
VMEM_LIMIT_BYTES = 56 * 1024 * 1024

F32 = jnp.float32
BF16 = jnp.bfloat16
NT_DIMS = (((1,), (1,)), ((), ()))


def _pick_tile(n, candidates):
    for c in candidates:
        if n % c == 0:
            return c
    raise ValueError(f"no tile in {candidates} divides {n}")


def _params(*sem):
    return pltpu.CompilerParams(dimension_semantics=sem, vmem_limit_bytes=VMEM_LIMIT_BYTES)


def _ln(x, g, b):
    mu = jnp.mean(x, axis=-1, keepdims=True)
    xc = x - mu
    var = jnp.mean(xc * xc, axis=-1, keepdims=True)
    return xc * lax.rsqrt(var + LN_EPS) * g + b


def _embed_kernel(x_ref, meta_ref, g_ref, b_ref, o_ref, *, lead):
    j = pl.program_id(1)

    @pl.when(j == 0)
    def _():
        o_ref[0:lead, :] = jnp.zeros((lead, D_MODEL), F32)
        o_ref[lead:BLOCK, :] = _ln(meta_ref[...], g_ref[...], b_ref[...])

    @pl.when(j > 0)
    def _():
        o_ref[...] = _ln(x_ref[...], g_ref[...], b_ref[...])


def _embed(x, meta, g, b, nb, lead):
    B = x.shape[0]
    return pl.pallas_call(
        functools.partial(_embed_kernel, lead=lead),
        grid=(B, nb),
        in_specs=[
            pl.BlockSpec((None, BLOCK, D_MODEL), lambda bi, j: (bi, jnp.maximum(j - 1, 0), 0)),
            pl.BlockSpec((N_META, D_MODEL), lambda bi, j: (0, 0)),
            pl.BlockSpec((1, D_MODEL), lambda bi, j: (0, 0)),
            pl.BlockSpec((1, D_MODEL), lambda bi, j: (0, 0)),
        ],
        out_specs=pl.BlockSpec((None, BLOCK, D_MODEL), lambda bi, j: (bi, j, 0)),
        out_shape=jax.ShapeDtypeStruct((B, nb * BLOCK, D_MODEL), F32),
        compiler_params=_params("parallel", "arbitrary"),
        name="embed_ln",
    )(x, meta, g.reshape(1, -1), b.reshape(1, -1))


def _inproj_kernel(h_ref, w_ref, bg_ref, cs_ref, q_ref, kv_ref, pq_ref, gate_ref):
    hb = h_ref[...].astype(BF16)
    q_ref[...] = jnp.dot(hb, w_ref[:, 0:Q_END], preferred_element_type=F32).astype(BF16)
    kv_ref[...] = jnp.dot(hb, w_ref[:, Q_END:V_END], preferred_element_type=F32).astype(BF16)
    uf = jnp.dot(hb, w_ref[:, V_END:F_END], preferred_element_type=F32).astype(BF16)
    for g in range(N_FOURIER_GROUPS):
        lo = g * FOURIER_GROUP
        pq = jnp.dot(uf[:, lo:lo + FOURIER_GROUP], cs_ref[...], preferred_element_type=F32)
        pq_ref[:, lo:lo + FOURIER_GROUP] = pq[:, 0:FOURIER_GROUP].astype(BF16)
        pq_ref[:, FOURIER_WIDTH + lo:FOURIER_WIDTH + lo + FOURIER_GROUP] = (
            pq[:, FOURIER_GROUP:2 * FOURIER_GROUP].astype(BF16))
    chunk = 512
    for c in range(GATE_WIDTH // chunk):
        lo = c * chunk
        ug = jnp.dot(hb, w_ref[:, F_END + lo:F_END + lo + chunk], preferred_element_type=F32)
        gate_ref[:, lo:lo + chunk] = jax.nn.sigmoid(ug + bg_ref[:, lo:lo + chunk])


def _inproj(h, w_bf16, b_gate, cs, tm):
    n = h.shape[0]
    row = lambda i: (i, 0)
    const = lambda i: (0, 0)
    return pl.pallas_call(
        _inproj_kernel,
        grid=(n // tm,),
        in_specs=[
            pl.BlockSpec((tm, D_MODEL), row),
            pl.BlockSpec((D_MODEL, IN_WIDTH), const),
            pl.BlockSpec((1, GATE_WIDTH), const),
            pl.BlockSpec((FOURIER_GROUP, 2 * FOURIER_GROUP), const),
        ],
        out_specs=[
            pl.BlockSpec((tm, ATTN_WIDTH), row),
            pl.BlockSpec((tm, 2 * KV_WIDTH), row),
            pl.BlockSpec((tm, 2 * FOURIER_WIDTH), row),
            pl.BlockSpec((tm, GATE_WIDTH), row),
        ],
        out_shape=[
            jax.ShapeDtypeStruct((n, ATTN_WIDTH), BF16),
            jax.ShapeDtypeStruct((n, 2 * KV_WIDTH), BF16),
            jax.ShapeDtypeStruct((n, 2 * FOURIER_WIDTH), BF16),
            jax.ShapeDtypeStruct((n, GATE_WIDTH), F32),
        ],
        compiler_params=_params("parallel"),
        name="inproj",
    )(h, w_bf16, b_gate.reshape(1, GATE_WIDTH), cs)


def _attn_kernel(sink_ref, q_ref, kvp_ref, kvc_ref, kvn_ref, bias_ref, o_ref, *, lead, lp):
    i = pl.program_id(1)
    kpos = (i - 1) * BLOCK + lax.broadcasted_iota(jnp.int32, (1, 3 * BLOCK), 1)
    colpen = jnp.where((kpos >= lead) & (kpos < lp), 0.0, NEG_INF).astype(F32)
    kv = jnp.concatenate([kvp_ref[...], kvc_ref[...], kvn_ref[...]], axis=0)
    low_half = lax.broadcasted_iota(jnp.int32, (3 * BLOCK, LANES), 1) < HEAD_DIM

    def lane_half_operands(x):
        swapped = jnp.concatenate([x[:, HEAD_DIM:], x[:, :HEAD_DIM]], axis=1)
        zero = jnp.zeros_like(x)
        return {(kvh, half): jnp.where(low_half if half == 0 else ~low_half,
                                       x if kvh == half else swapped, zero)
                for kvh in range(N_KV_HEADS) for half in range(2)}

    k_ops = lane_half_operands(kv[:, 0:KV_WIDTH])
    v_ops = lane_half_operands(kv[:, KV_WIDTH:2 * KV_WIDTH])

    def scores(h):
        pair = h // 2
        qp = q_ref[:, pair * LANES:(pair + 1) * LANES]
        return lax.dot_general(qp, k_ops[(h // GQA_GROUP, h % 2)], NT_DIMS, preferred_element_type=F32)

    def head_out(h, s):
        sink = sink_ref[h]
        logits = s * (HEAD_DIM ** -0.5) + bias_ref[h] + colpen
        m = jnp.maximum(jnp.max(logits, axis=-1, keepdims=True), sink)
        p = jnp.exp(logits - m)
        denom = jnp.sum(p, axis=-1, keepdims=True) + jnp.exp(sink - m)
        o = jnp.dot(p.astype(BF16), v_ops[(h // GQA_GROUP, h % 2)], preferred_element_type=F32)
        return o / denom

    s_next = scores(0)
    for h in range(N_HEADS):
        s_cur = s_next
        if h + 1 < N_HEADS:
            s_next = scores(h + 1)
        o = head_out(h, s_cur)
        if h % 2 == 0:
            o_even = o
        else:
            pair = h // 2
            o_ref[:, pair * LANES:(pair + 1) * LANES] = (o_even + o).astype(BF16)


def _attn_bias_table():
    qi = np.arange(BLOCK)[:, None]
    si = np.arange(3 * BLOCK)[None, :]
    rel = np.abs(si - BLOCK - qi).astype(np.float32)
    slopes = np.array([2.0 ** (-8.0 * (h + 1) / N_HEADS) for h in range(N_HEADS)], np.float32)
    tab = np.where(rel[None] <= WINDOW, -slopes[:, None, None] * rel[None], np.float32(NEG_INF))
    return jnp.asarray(tab.astype(np.float32))


def _attention(q, kv, sink, bias, nb, lead):
    B, lp, _ = q.shape
    kvspec = lambda f: pl.BlockSpec((None, BLOCK, 2 * KV_WIDTH), f)
    return pl.pallas_call(
        functools.partial(_attn_kernel, lead=lead, lp=lp),
        grid=(B, nb),
        in_specs=[
            pl.BlockSpec(memory_space=pltpu.SMEM),
            pl.BlockSpec((None, BLOCK, ATTN_WIDTH), lambda b, i: (b, i, 0)),
            kvspec(lambda b, i: (b, jnp.maximum(i - 1, 0), 0)),
            kvspec(lambda b, i: (b, i, 0)),
            kvspec(lambda b, i: (b, jnp.minimum(i + 1, nb - 1), 0)),
            pl.BlockSpec((N_HEADS, BLOCK, 3 * BLOCK), lambda b, i: (0, 0, 0)),
        ],
        out_specs=pl.BlockSpec((None, BLOCK, ATTN_WIDTH), lambda b, i: (b, i, 0)),
        out_shape=jax.ShapeDtypeStruct((B, lp, ATTN_WIDTH), BF16),
        compiler_params=_params("parallel", "arbitrary"),
        name="attention",
    )(sink, q, kv, kv, kv, bias)


def _fourier_kernel(w_ref, p_ref, q_ref, o_ref, *, lp):
    z = jnp.dot(w_ref[:, 0:lp], p_ref[...], preferred_element_type=F32)
    z = z + jnp.dot(w_ref[:, lp:2 * lp], q_ref[...], preferred_element_type=F32)
    o_ref[...] = z.astype(BF16)


def _dft_matrix(lp, lead, seq_len):
    pos = jnp.arange(lp, dtype=jnp.int32) - lead
    m = (pos[:, None] * pos[None, :]) % seq_len
    ang = m.astype(F32) * np.float32(2.0 * math.pi / seq_len)
    valid = (pos[:, None] >= 0) & (pos[None, :] >= 0)
    scale = np.float32(seq_len ** -0.5)
    wc = jnp.where(valid, jnp.cos(ang) * scale, 0.0)
    ws = jnp.where(valid, -jnp.sin(ang) * scale, 0.0)
    return jnp.concatenate([wc, ws], axis=1).astype(BF16)


def _fourier(w, pq, tm):
    B, lp, _ = pq.shape
    return pl.pallas_call(
        functools.partial(_fourier_kernel, lp=lp),
        grid=(lp // tm, B),
        in_specs=[
            pl.BlockSpec((tm, 2 * lp), lambda i, b: (i, 0)),
            pl.BlockSpec((None, lp, FOURIER_WIDTH), lambda i, b: (b, 0, 0)),
            pl.BlockSpec((None, lp, FOURIER_WIDTH), lambda i, b: (b, 0, 1)),
        ],
        out_specs=pl.BlockSpec((None, tm, FOURIER_WIDTH), lambda i, b: (b, i, 0)),
        out_shape=jax.ShapeDtypeStruct((B, lp, FOURIER_WIDTH), BF16),
        compiler_params=_params("parallel", "arbitrary"),
        name="fourier",
    )(w, pq, pq)


def _split_bf16(x):
    hi = x.astype(BF16)
    lo = (x - hi.astype(F32)).astype(BF16)
    return hi, lo


def _outproj_kernel(attn_ref, four_ref, gate_ref, h_ref, wa_ref, wf_ref, wo_ref, g_ref, b_ref,
                    wrh_ref, wrl_ref, wrth_ref, wrtl_ref,
                    h1_ref, h1b_ref, affr_ref, affc_ref, *, alpha):
    ya = jnp.dot(attn_ref[...], wa_ref[...], preferred_element_type=F32)
    yf = jnp.dot(four_ref[...], wf_ref[...], preferred_element_type=F32)
    merged = gate_ref[:, 0:D_MODEL] * ya + gate_ref[:, D_MODEL:GATE_WIDTH] * yf
    mix = jnp.dot(merged.astype(BF16), wo_ref[...], preferred_element_type=F32)
    h1 = _ln(alpha * h_ref[...] + mix, g_ref[...], b_ref[...])
    h1_ref[...] = h1
    hi, lo = _split_bf16(h1)
    h1b_ref[...] = hi
    lc = (jnp.dot(hi, wrh_ref[...], preferred_element_type=F32)
          + jnp.dot(hi, wrl_ref[...], preferred_element_type=F32)
          + jnp.dot(lo, wrh_ref[...], preferred_element_type=F32))
    lane = lax.broadcasted_iota(jnp.int32, lc.shape, 1)
    lc = jnp.where(lane < N_EXPERTS, lc, NEG_INF)
    ec = jnp.exp(lc - jnp.max(lc, axis=-1, keepdims=True))
    affc_ref[...] = (ec / jnp.sum(ec, axis=-1, keepdims=True))[:, 0:N_EXPERTS]
    lr = (lax.dot_general(wrth_ref[...], hi, NT_DIMS, preferred_element_type=F32)
          + lax.dot_general(wrtl_ref[...], hi, NT_DIMS, preferred_element_type=F32)
          + lax.dot_general(wrth_ref[...], lo, NT_DIMS, preferred_element_type=F32))
    er = jnp.exp(lr - jnp.max(lr, axis=0, keepdims=True))
    affr_ref[...] = er / jnp.sum(er, axis=0, keepdims=True)


def _outproj(attn, four, gates, h, wa, wf, wo, g, b, w_router, alpha, tm):
    n = h.shape[0]
    row = lambda i: (i, 0)
    const = lambda i: (0, 0)
    wr_pad = jnp.pad(w_router, ((0, 0), (0, LANES - N_EXPERTS)))
    wrh, wrl = _split_bf16(wr_pad)
    wrth, wrtl = _split_bf16(w_router.T)
    return pl.pallas_call(
        functools.partial(_outproj_kernel, alpha=alpha),
        grid=(n // tm,),
        in_specs=[
            pl.BlockSpec((tm, ATTN_WIDTH), row),
            pl.BlockSpec((tm, FOURIER_WIDTH), row),
            pl.BlockSpec((tm, GATE_WIDTH), row),
            pl.BlockSpec((tm, D_MODEL), row),
            pl.BlockSpec((ATTN_WIDTH, D_MODEL), const),
            pl.BlockSpec((FOURIER_WIDTH, D_MODEL), const),
            pl.BlockSpec((D_MODEL, D_MODEL), const),
            pl.BlockSpec((1, D_MODEL), const),
            pl.BlockSpec((1, D_MODEL), const),
            pl.BlockSpec((D_MODEL, LANES), const),
            pl.BlockSpec((D_MODEL, LANES), const),
            pl.BlockSpec((N_EXPERTS, D_MODEL), const),
            pl.BlockSpec((N_EXPERTS, D_MODEL), const),
        ],
        out_specs=[
            pl.BlockSpec((tm, D_MODEL), row),
            pl.BlockSpec((tm, D_MODEL), row),
            pl.BlockSpec((N_EXPERTS, tm), lambda i: (0, i)),
            pl.BlockSpec((tm, N_EXPERTS), row),
        ],
        out_shape=[
            jax.ShapeDtypeStruct((n, D_MODEL), F32),
            jax.ShapeDtypeStruct((n, D_MODEL), BF16),
            jax.ShapeDtypeStruct((N_EXPERTS, n), F32),
            jax.ShapeDtypeStruct((n, N_EXPERTS), F32),
        ],
        compiler_params=_params("parallel"),
        name="outproj_ln_router",
    )(attn, four, gates, h, wa, wf, wo, g.reshape(1, -1), b.reshape(1, -1), wrh, wrl, wrth, wrtl)


def _topk_kernel(aff_ref, rr_ref, rc_ref, cnt_ref, cntc_ref, *, lead, lp, cap):
    nchunk = lp // LANES
    lane = lax.broadcasted_iota(jnp.int32, (N_EXPERTS, lp), 1)
    bits = jnp.where(lane >= lead, pltpu.bitcast(aff_ref[...], jnp.int32), -1)

    def search(i, t):
        cand = t | (jnp.int32(1) << (30 - i))
        cnt = jnp.sum((bits >= cand).astype(jnp.int32), axis=-1, keepdims=True)
        return jnp.where(cnt >= cap, cand, t)

    thr = lax.fori_loop(0, 31, search, jnp.zeros((N_EXPERTS, 1), jnp.int32))
    gt = bits > thr
    eq = bits == thr
    need = cap - jnp.sum(gt.astype(jnp.int32), axis=-1, keepdims=True)

    r = lax.broadcasted_iota(jnp.int32, (LANES, LANES), 0)
    c = lax.broadcasted_iota(jnp.int32, (LANES, LANES), 1)
    upper = (r <= c).astype(BF16)
    lower = (c <= r).astype(BF16)
    ident = (c == r).astype(BF16)

    eqb = eq.astype(BF16)
    off = jnp.zeros((N_EXPERTS, 1), F32)
    needf = need.astype(F32)
    sel_chunks = []
    for k in range(nchunk):
        sl = slice(k * LANES, (k + 1) * LANES)
        pre = jnp.dot(eqb[:, sl], upper, preferred_element_type=F32) + off
        off = pre[:, LANES - 1:LANES]
        sel_chunks.append(gt[:, sl] | (eq[:, sl] & (pre <= needf)))

    off_r = jnp.zeros((N_EXPERTS, 1), F32)
    off_c = jnp.zeros((1, N_EXPERTS), F32)
    cnt_lane = lax.broadcasted_iota(jnp.int32, (N_EXPERTS, LANES), 1)
    cnt = jnp.zeros((N_EXPERTS, LANES), F32)
    for k in range(nchunk):
        sl = slice(k * LANES, (k + 1) * LANES)
        sel = sel_chunks[k]
        selb = sel.astype(BF16)
        pre_r = jnp.dot(selb, upper, preferred_element_type=F32) + off_r
        rr_ref[k] = jnp.where(sel, pre_r - 1.0, -1.0).astype(jnp.int32)
        off_r = pre_r[:, LANES - 1:LANES]
        cnt = jnp.where(cnt_lane == k + 1, off_r, cnt)
        pre_c = lax.dot_general(lower, selb, NT_DIMS, preferred_element_type=F32) + off_c
        sel_c = lax.dot_general(ident, selb, NT_DIMS, preferred_element_type=F32)
        rc_ref[sl, :] = jnp.where(sel_c > 0.5, pre_c - 1.0, -1.0).astype(jnp.int32)
        off_c = pre_c[LANES - 1:LANES, :]
        cntc_ref[k + 1:k + 2, :] = off_c.astype(jnp.int32)
    cnt_ref[...] = cnt.astype(jnp.int32)
    cntc_ref[0:1, :] = jnp.zeros((1, N_EXPERTS), jnp.int32)
    pad_rows = cntc_ref.shape[0] - nchunk - 1
    if pad_rows:
        cntc_ref[nchunk + 1:, :] = jnp.zeros((pad_rows, N_EXPERTS), jnp.int32)


def _topk(aff_r, B, lp, lead, cap):
    n = B * lp
    nchunk = lp // LANES
    assert nchunk + 1 <= LANES
    nb1 = -(-(nchunk + 1) // SUBLANES) * SUBLANES
    return pl.pallas_call(
        functools.partial(_topk_kernel, lead=lead, lp=lp, cap=cap),
        grid=(B,),
        in_specs=[pl.BlockSpec((N_EXPERTS, lp), lambda b: (0, b))],
        out_specs=[
            pl.BlockSpec((None, nchunk, N_EXPERTS, LANES), lambda b: (b, 0, 0, 0)),
            pl.BlockSpec((lp, N_EXPERTS), lambda b: (b, 0)),
            pl.BlockSpec((None, N_EXPERTS, LANES), lambda b: (b, 0, 0)),
            pl.BlockSpec((None, nb1, N_EXPERTS), lambda b: (b, 0, 0)),
        ],
        out_shape=[
            jax.ShapeDtypeStruct((B, nchunk, N_EXPERTS, LANES), jnp.int32),
            jax.ShapeDtypeStruct((n, N_EXPERTS), jnp.int32),
            jax.ShapeDtypeStruct((B, N_EXPERTS, LANES), jnp.int32),
            jax.ShapeDtypeStruct((B, nb1, N_EXPERTS), jnp.int32),
        ],
        compiler_params=_params("parallel"),
        name="topk_select",
    )(aff_r)


def _gather_kernel(rank_ref, aff_ref, h_ref, o_ref, g_ref, *, slots, lp):
    slot = lax.broadcasted_iota(jnp.int32, (slots, lp), 0)
    hit = slot == rank_ref[...]
    o_ref[...] = jnp.dot(hit.astype(BF16), h_ref[...], preferred_element_type=F32).astype(BF16)
    g_ref[...] = jnp.sum(jnp.where(hit, aff_ref[...], 0.0), axis=-1, keepdims=True)


def _gather(rank_r, aff_r, h1b, B, lp, slots):
    row = pl.BlockSpec((None, None, 1, lp), lambda b, e: (b, e, 0, 0))
    return pl.pallas_call(
        functools.partial(_gather_kernel, slots=slots, lp=lp),
        grid=(B, N_EXPERTS),
        in_specs=[row, row, pl.BlockSpec((lp, D_MODEL), lambda b, e: (b, 0))],
        out_specs=[
            pl.BlockSpec((None, None, slots, D_MODEL), lambda b, e: (b, e, 0, 0)),
            pl.BlockSpec((None, None, slots, 1), lambda b, e: (b, e, 0, 0)),
        ],
        out_shape=[
            jax.ShapeDtypeStruct((B, N_EXPERTS, slots, D_MODEL), BF16),
            jax.ShapeDtypeStruct((B, N_EXPERTS, slots, 1), F32),
        ],
        compiler_params=_params("parallel", "arbitrary"),
        name="moe_gather",
    )(rank_r, aff_r, h1b)


def _ffn_kernel(x_ref, g_ref, wg_ref, wu_ref, wd_ref, o_ref):
    x = x_ref[...]
    a = jnp.dot(x, wg_ref[...], preferred_element_type=F32)
    u = jnp.dot(x, wu_ref[...], preferred_element_type=F32)
    hm = (a * jax.nn.sigmoid(a) * u).astype(BF16)
    y = jnp.dot(hm, wd_ref[...], preferred_element_type=F32)
    o_ref[...] = (y * g_ref[...]).astype(BF16)


def _ffn(xg, gates, wg, wu, wd):
    B, _, slots, _ = xg.shape
    act = pl.BlockSpec((None, None, slots, D_MODEL), lambda e, b: (b, e, 0, 0))
    return pl.pallas_call(
        _ffn_kernel,
        grid=(N_EXPERTS, B),
        in_specs=[
            act,
            pl.BlockSpec((None, None, slots, 1), lambda e, b: (b, e, 0, 0)),
            pl.BlockSpec((None, D_MODEL, D_FF_EXPERT), lambda e, b: (e, 0, 0)),
            pl.BlockSpec((None, D_MODEL, D_FF_EXPERT), lambda e, b: (e, 0, 0)),
            pl.BlockSpec((None, D_FF_EXPERT, D_MODEL), lambda e, b: (e, 0, 0)),
        ],
        out_specs=act,
        out_shape=jax.ShapeDtypeStruct(xg.shape, BF16),
        compiler_params=_params("parallel", "arbitrary"),
        name="moe_ffn",
    )(xg, gates, wg, wu, wd)


SLOT_ALIGN = 16


def _combine_kernel(cnt_ref, cntc_ref, rank_ref, y_ref, h_ref, g_ref, b_ref, expand_ref, rpat_ref,
                    o_ref, acc_ref, *, slots, ch, win, nchunk, alpha):
    bi, c = pl.program_id(0), pl.program_id(1)
    k0 = c * ch
    tt = ch * LANES
    max_start = slots - win
    rank = rank_ref[...]
    lo_row = cntc_ref[pl.ds(k0, 1), :]
    w_row = jnp.minimum(lo_row & -SLOT_ALIGN, max_start)
    rel = rank - w_row
    in_win = (rank >= 0) & (rel >= 0) & (rel < win)
    relb = jnp.where(in_win, rel, -1).astype(F32).astype(BF16)
    rel_wide = jnp.dot(relb, expand_ref[...], preferred_element_type=F32)
    onehot = (rel_wide == rpat_ref[...]).astype(BF16)

    starts, windows = [], []
    for e in range(N_EXPERTS):
        lo = cnt_ref[(bi * N_EXPERTS + e) * (nchunk + 1) + k0]
        w = jnp.minimum(lo & -SLOT_ALIGN, max_start)
        starts.append(w)
        windows.append(y_ref[e, pl.ds(pl.multiple_of(w, SLOT_ALIGN), win), :])
    acc_ref[...] = jnp.dot(onehot, jnp.concatenate(windows, axis=0), preferred_element_type=F32)

    lane = lax.broadcasted_iota(jnp.int32, rank.shape, 1)
    col = lax.broadcasted_iota(jnp.int32, (tt, win), 1)
    for e in range(N_EXPERTS):
        hi = cnt_ref[(bi * N_EXPERTS + e) * (nchunk + 1) + k0 + ch]

        @pl.when(hi > starts[e] + win)
        def _(e=e, hi=hi):
            rk = jnp.sum(jnp.where(lane == e, rank, 0), axis=-1, keepdims=True)

            def extra(i, carry):
                w2 = starts[e] + (i + 1) * win
                w2c = jnp.minimum(w2, max_start)
                rows = y_ref[e, pl.ds(pl.multiple_of(w2c, SLOT_ALIGN), win), :]
                hit = ((rk - w2c) == col) & (rk >= w2)
                acc_ref[...] += jnp.dot(hit.astype(BF16), rows, preferred_element_type=F32)
                return carry

            lax.fori_loop(0, (hi - starts[e] - 1) // win, extra, 0)

    o_ref[...] = _ln(alpha * h_ref[...] + acc_ref[...], g_ref[...], b_ref[...])


def _combine(cnt, cntc, rank_c, y, h1, g, b, B, lp, alpha, ch, win):
    n = B * lp
    slots = y.shape[2]
    nchunk = lp // LANES
    tt = ch * LANES
    nt = nchunk // ch
    assert slots % SLOT_ALIGN == 0 and win % SLOT_ALIGN == 0 and win <= slots
    wide = N_EXPERTS * win
    expand = np.zeros((N_EXPERTS, wide), np.float32)
    for e in range(N_EXPERTS):
        expand[e, e * win:(e + 1) * win] = 1.0
    rpat = (np.arange(wide) % win).astype(np.float32).reshape(1, wide)
    tok = lambda bi, c, cnt: (bi * nt + c, 0)
    const = lambda bi, c, cnt: (0, 0)
    return pl.pallas_call(
        functools.partial(_combine_kernel, slots=slots, ch=ch, win=win, nchunk=nchunk, alpha=alpha),
        grid_spec=pltpu.PrefetchScalarGridSpec(
            num_scalar_prefetch=1,
            grid=(B, nt),
            in_specs=[
                pl.BlockSpec((None, cntc.shape[1], N_EXPERTS), lambda bi, c, cnt: (bi, 0, 0)),
                pl.BlockSpec((tt, N_EXPERTS), tok),
                pl.BlockSpec((None, N_EXPERTS, slots, D_MODEL), lambda bi, c, cnt: (bi, 0, 0, 0)),
                pl.BlockSpec((tt, D_MODEL), tok),
                pl.BlockSpec((1, D_MODEL), const),
                pl.BlockSpec((1, D_MODEL), const),
                pl.BlockSpec((N_EXPERTS, wide), const),
                pl.BlockSpec((1, wide), const),
            ],
            out_specs=pl.BlockSpec((tt, D_MODEL), tok),
            scratch_shapes=[pltpu.VMEM((tt, D_MODEL), F32)],
        ),
        out_shape=jax.ShapeDtypeStruct((n, D_MODEL), F32),
        compiler_params=_params("parallel", "arbitrary"),
        name="moe_combine_ln",
    )(cnt, cntc, rank_c, y, h1, g.reshape(1, -1), b.reshape(1, -1),
      jnp.asarray(expand, dtype=BF16), jnp.asarray(rpat))


def kernel(x, meta, ln0_g, ln0_b, w_in, b_gate, sink, w_attn_o, w_four_o, w_out, ln1_g, ln1_b,
           w_router, w_e_gate, w_e_up, w_e_down, ln2_g, ln2_b):
    B, seq, d = x.shape
    depth = w_in.shape[0]
    assert d == D_MODEL and meta.shape == (N_META, D_MODEL)
    assert seq % BLOCK == 0 and N_META % SUBLANES == 0 and N_META <= BLOCK
    L = seq + N_META
    nb = -(-L // BLOCK)
    lp = nb * BLOCK
    lead = lp - L
    cap = CAPACITY_FACTOR * L // N_EXPERTS
    slots = -(-cap // SLOT_ALIGN) * SLOT_ALIGN
    alpha = float((2 * depth) ** 0.25)
    n = B * lp
    nchunk = lp // LANES
    tm = _pick_tile(n, (512, 256, 128))
    tf = _pick_tile(lp, (384, 640, 128))
    ch = _pick_tile(nchunk, (3, 2, 1))
    win = min(slots, -(-(2 * ch * LANES * cap // L + SLOT_ALIGN) // SLOT_ALIGN) * SLOT_ALIGN)

    gi = np.arange(FOURIER_GROUP)
    ang = 2.0 * np.pi * ((gi[:, None] * gi[None, :]) % FOURIER_GROUP) / FOURIER_GROUP
    cs = jnp.asarray(np.concatenate([np.cos(ang), np.sin(ang)], axis=1) * FOURIER_GROUP ** -0.5, dtype=BF16)
    wdft = _dft_matrix(lp, lead, L)
    bias = _attn_bias_table()

    h = _embed(x, meta, ln0_g, ln0_b, nb, lead).reshape(n, D_MODEL)
    for l in range(depth):
        q, kv, pq, gates = _inproj(h, w_in[l].astype(BF16), b_gate[l], cs, tm)
        attn = _attention(q.reshape(B, lp, -1), kv.reshape(B, lp, -1), sink[l], bias, nb, lead)
        four = _fourier(wdft, pq.reshape(B, lp, -1), tf)
        h1, h1b, aff_r, aff_c = _outproj(
            attn.reshape(n, -1), four.reshape(n, -1), gates, h,
            w_attn_o[l].astype(BF16), w_four_o[l].astype(BF16), w_out[l].astype(BF16),
            ln1_g[l], ln1_b[l], w_router[l], alpha, tm)
        rank_r, rank_c, cnt, cntc = _topk(aff_r, B, lp, lead, cap)
        cnt = cnt[:, :, :nchunk + 1].reshape(-1)
        rank_rows = rank_r.transpose(0, 2, 1, 3).reshape(B, N_EXPERTS, 1, lp)
        aff_rows = aff_r.reshape(N_EXPERTS, B, 1, lp).transpose(1, 0, 2, 3)
        xg, gsel = _gather(rank_rows, aff_rows, h1b, B, lp, slots)
        y = _ffn(xg, gsel, w_e_gate[l].astype(BF16), w_e_up[l].astype(BF16), w_e_down[l].astype(BF16))
        h = _combine(cnt, cntc, rank_c, y, h1, ln2_g[l], ln2_b[l], B, lp, alpha, ch, win)
    return h.reshape(B, lp, D_MODEL)[:, lead + N_META:]
```

```python
import functools
import math

import numpy as np
import jax
import jax.numpy as jnp
from jax import lax
from jax.experimental import pallas as pl
from jax.experimental.pallas import tpu as pltpu

D_MODEL = 1024
N_META = 16
N_HEADS = 8
N_KV_HEADS = 2
HEAD_DIM = 64
GQA_GROUP = N_HEADS // N_KV_HEADS
ATTN_WIDTH = N_HEADS * HEAD_DIM
KV_WIDTH = N_KV_HEADS * HEAD_DIM
WINDOW = 128
BLOCK = 128
N_FOURIER_GROUPS = 4
FOURIER_GROUP = 128
FOURIER_WIDTH = N_FOURIER_GROUPS * FOURIER_GROUP
N_BRANCHES = 2
GATE_WIDTH = N_BRANCHES * D_MODEL
N_EXPERTS = 16
CAPACITY_FACTOR = 2
D_FF_EXPERT = 1536
LN_EPS = 1e-5
NEG_INF = -1e30
Q_END = ATTN_WIDTH
K_END = Q_END + KV_WIDTH
V_END = K_END + KV_WIDTH
F_END = V_END + FOURIER_WIDTH
IN_WIDTH = F_END + GATE_WIDTH

LANES = 128
SUBLANES = 8
VMEM_LIMIT_BYTES = 56 * 1024 * 1024

F32 = jnp.float32
BF16 = jnp.bfloat16
NT_DIMS = (((1,), (1,)), ((), ()))


def _pick_tile(n, candidates):
    for c in candidates:
        if n % c == 0:
            return c
    raise ValueError(f"no tile in {candidates} divides {n}")


def _params(*sem, flags=None):
    return pltpu.CompilerParams(dimension_semantics=sem, vmem_limit_bytes=VMEM_LIMIT_BYTES, flags=flags)


def _ln(x, g, b):
    mu = jnp.mean(x, axis=-1, keepdims=True)
    xc = x - mu
    var = jnp.mean(xc * xc, axis=-1, keepdims=True)
    return xc * lax.rsqrt(var + LN_EPS) * g + b


def _embed_kernel(x_ref, meta_ref, g_ref, b_ref, o_ref, *, lead):
    j = pl.program_id(1)

    @pl.when(j == 0)
    def _():
        o_ref[0:lead, :] = jnp.zeros((lead, D_MODEL), F32)
        o_ref[lead:BLOCK, :] = _ln(meta_ref[...], g_ref[...], b_ref[...])

    @pl.when(j > 0)
    def _():
        o_ref[...] = _ln(x_ref[...], g_ref[...], b_ref[...])


def _embed(x, meta, g, b, nb, lead):
    B = x.shape[0]
    return pl.pallas_call(
        functools.partial(_embed_kernel, lead=lead),
        grid=(B, nb),
        in_specs=[
            pl.BlockSpec((None, BLOCK, D_MODEL), lambda bi, j: (bi, jnp.maximum(j - 1, 0), 0)),
            pl.BlockSpec((N_META, D_MODEL), lambda bi, j: (0, 0)),
            pl.BlockSpec((1, D_MODEL), lambda bi, j: (0, 0)),
            pl.BlockSpec((1, D_MODEL), lambda bi, j: (0, 0)),
        ],
        out_specs=pl.BlockSpec((None, BLOCK, D_MODEL), lambda bi, j: (bi, j, 0)),
        out_shape=jax.ShapeDtypeStruct((B, nb * BLOCK, D_MODEL), F32),
        compiler_params=_params("parallel", "arbitrary"),
        name="embed_ln",
    )(x, meta, g.reshape(1, -1), b.reshape(1, -1))


def _inproj_kernel(h_ref, w_ref, bg_ref, cs_ref, q_ref, kv_ref, pq_ref, gate_ref):
    hb = h_ref[...].astype(BF16)
    q_ref[...] = jnp.dot(hb, w_ref[:, 0:Q_END], preferred_element_type=F32).astype(BF16)
    kv_ref[...] = jnp.dot(hb, w_ref[:, Q_END:V_END], preferred_element_type=F32).astype(BF16)
    uf = jnp.dot(hb, w_ref[:, V_END:F_END], preferred_element_type=F32).astype(BF16)
    for g in range(N_FOURIER_GROUPS):
        lo = g * FOURIER_GROUP
        pq = jnp.dot(uf[:, lo:lo + FOURIER_GROUP], cs_ref[...], preferred_element_type=F32)
        pq_ref[:, lo:lo + FOURIER_GROUP] = pq[:, 0:FOURIER_GROUP].astype(BF16)
        pq_ref[:, FOURIER_WIDTH + lo:FOURIER_WIDTH + lo + FOURIER_GROUP] = (
            pq[:, FOURIER_GROUP:2 * FOURIER_GROUP].astype(BF16))
    chunk = 512
    for c in range(GATE_WIDTH // chunk):
        lo = c * chunk
        ug = jnp.dot(hb, w_ref[:, F_END + lo:F_END + lo + chunk], preferred_element_type=F32)
        gate_ref[:, lo:lo + chunk] = jax.nn.sigmoid(ug + bg_ref[:, lo:lo + chunk])


def _inproj(h, w_bf16, b_gate, cs, tm):
    n = h.shape[0]
    row = lambda i: (i, 0)
    const = lambda i: (0, 0)
    return pl.pallas_call(
        _inproj_kernel,
        grid=(n // tm,),
        in_specs=[
            pl.BlockSpec((tm, D_MODEL), row),
            pl.BlockSpec((D_MODEL, IN_WIDTH), const),
            pl.BlockSpec((1, GATE_WIDTH), const),
            pl.BlockSpec((FOURIER_GROUP, 2 * FOURIER_GROUP), const),
        ],
        out_specs=[
            pl.BlockSpec((tm, ATTN_WIDTH), row),
            pl.BlockSpec((tm, 2 * KV_WIDTH), row),
            pl.BlockSpec((tm, 2 * FOURIER_WIDTH), row),
            pl.BlockSpec((tm, GATE_WIDTH), row),
        ],
        out_shape=[
            jax.ShapeDtypeStruct((n, ATTN_WIDTH), BF16),
            jax.ShapeDtypeStruct((n, 2 * KV_WIDTH), BF16),
            jax.ShapeDtypeStruct((n, 2 * FOURIER_WIDTH), BF16),
            jax.ShapeDtypeStruct((n, GATE_WIDTH), F32),
        ],
        compiler_params=_params("parallel"),
        name="inproj",
    )(h, w_bf16, b_gate.reshape(1, GATE_WIDTH), cs)


def _attn_kernel(sink_ref, q_ref, kvp_ref, kvc_ref, kvn_ref, bias_ref, o_ref):
    kv = jnp.concatenate([kvp_ref[...], kvc_ref[...], kvn_ref[...]], axis=0)
    low_half = lax.broadcasted_iota(jnp.int32, (3 * BLOCK, LANES), 1) < HEAD_DIM

    def lane_half_operands(x):
        swapped = jnp.concatenate([x[:, HEAD_DIM:], x[:, :HEAD_DIM]], axis=1)
        zero = jnp.zeros_like(x)
        return {(kvh, half): jnp.where(low_half if half == 0 else ~low_half,
                                       x if kvh == half else swapped, zero)
                for kvh in range(N_KV_HEADS) for half in range(2)}

    k_ops = lane_half_operands(kv[:, 0:KV_WIDTH])
    v_ops = lane_half_operands(kv[:, KV_WIDTH:2 * KV_WIDTH])

    def scores(h):
        pair = h // 2
        qp = q_ref[:, pair * LANES:(pair + 1) * LANES]
        return lax.dot_general(qp, k_ops[(h // GQA_GROUP, h % 2)], NT_DIMS, preferred_element_type=F32)

    def head_out(h, s):
        sink = sink_ref[h]
        logits = s * (HEAD_DIM ** -0.5) + bias_ref[h]
        m = jnp.maximum(jnp.max(logits, axis=-1, keepdims=True), sink)
        p = jnp.exp(logits - m)
        denom = jnp.sum(p, axis=-1, keepdims=True) + jnp.exp(sink - m)
        o = jnp.dot(p.astype(BF16), v_ops[(h // GQA_GROUP, h % 2)], preferred_element_type=F32)
        return o / denom

    s_next = scores(0)
    for h in range(N_HEADS):
        s_cur = s_next
        if h + 1 < N_HEADS:
            s_next = scores(h + 1)
        o = head_out(h, s_cur)
        if h % 2 == 0:
            o_even = o
        else:
            pair = h // 2
            o_ref[:, pair * LANES:(pair + 1) * LANES] = (o_even + o).astype(BF16)


def _attn_bias_tables(nb, lead):
    qi = np.arange(BLOCK)[:, None]
    si = np.arange(3 * BLOCK)[None, :]
    rel = np.abs(si - BLOCK - qi).astype(np.float32)
    slopes = np.array([2.0 ** (-8.0 * (h + 1) / N_HEADS) for h in range(N_HEADS)], np.float32)
    base = np.where(rel[None] <= WINDOW, -slopes[:, None, None] * rel[None], np.float32(NEG_INF))
    variants, keys, variant_of_block = [], [], []
    for i in range(nb):
        kpos = (i - 1) * BLOCK + np.arange(3 * BLOCK)
        valid = (kpos >= lead) & (kpos < nb * BLOCK)
        key = valid.tobytes()
        if key not in keys:
            keys.append(key)
            variants.append(np.where(valid[None, None, :], base, np.float32(NEG_INF)))
        variant_of_block.append(keys.index(key))
    return jnp.asarray(np.stack(variants).astype(np.float32)), variant_of_block


def _attention(q, kv, sink, bias, variant_of_block, nb):
    B, lp, _ = q.shape
    kvspec = lambda f: pl.BlockSpec((None, BLOCK, 2 * KV_WIDTH), f)
    interior = max(set(variant_of_block), key=variant_of_block.count)

    def variant(i):
        v = jnp.int32(interior)
        for blk, var in enumerate(variant_of_block):
            if var != interior:
                v = jnp.where(i == blk, var, v)
        return v

    return pl.pallas_call(
        _attn_kernel,
        grid=(B, nb),
        in_specs=[
            pl.BlockSpec(memory_space=pltpu.SMEM),
            pl.BlockSpec((None, BLOCK, ATTN_WIDTH), lambda b, i: (b, i, 0)),
            kvspec(lambda b, i: (b, jnp.maximum(i - 1, 0), 0)),
            kvspec(lambda b, i: (b, i, 0)),
            kvspec(lambda b, i: (b, jnp.minimum(i + 1, nb - 1), 0)),
            pl.BlockSpec((None, N_HEADS, BLOCK, 3 * BLOCK), lambda b, i: (variant(i), 0, 0, 0)),
        ],
        out_specs=pl.BlockSpec((None, BLOCK, ATTN_WIDTH), lambda b, i: (b, i, 0)),
        out_shape=jax.ShapeDtypeStruct((B, lp, ATTN_WIDTH), BF16),
        compiler_params=_params("parallel", "arbitrary"),
        name="attention",
    )(sink, q, kv, kv, kv, bias)


def _fourier_kernel(w_ref, p_ref, q_ref, o_ref, *, lp):
    z = jnp.dot(w_ref[:, 0:lp], p_ref[...], preferred_element_type=F32)
    z = z + jnp.dot(w_ref[:, lp:2 * lp], q_ref[...], preferred_element_type=F32)
    o_ref[...] = z.astype(BF16)


def _dft_matrix(lp, lead, seq_len):
    w = np.float32(2.0 * math.pi / seq_len)
    k = jnp.maximum(jnp.arange(lp, dtype=jnp.int32) - lead, 0)[:, None]
    j = jnp.arange(lp // LANES, dtype=jnp.int32)[None, :]
    r = jnp.arange(LANES, dtype=jnp.int32)[None, :]
    a = ((k * (LANES * j - lead)) % seq_len).astype(F32) * w
    b = ((k * r) % seq_len).astype(F32) * w
    ca, sa, cb, sb = jnp.cos(a)[:, :, None], jnp.sin(a)[:, :, None], jnp.cos(b)[:, None, :], jnp.sin(b)[:, None, :]
    pos = jnp.arange(lp, dtype=jnp.int32) - lead
    valid = ((pos[:, None] >= 0) & (pos[None, :] >= 0)).reshape(lp, lp // LANES, LANES)
    scale = np.float32(seq_len ** -0.5)
    wc = jnp.where(valid, (ca * cb - sa * sb) * scale, 0.0).reshape(lp, lp)
    ws = jnp.where(valid, -(sa * cb + ca * sb) * scale, 0.0).reshape(lp, lp)
    return jnp.concatenate([wc, ws], axis=1).astype(BF16)


def _fourier(w, pq, tm):
    B, lp, _ = pq.shape
    return pl.pallas_call(
        functools.partial(_fourier_kernel, lp=lp),
        grid=(lp // tm, B),
        in_specs=[
            pl.BlockSpec((tm, 2 * lp), lambda i, b: (i, 0)),
            pl.BlockSpec((None, lp, FOURIER_WIDTH), lambda i, b: (b, 0, 0)),
            pl.BlockSpec((None, lp, FOURIER_WIDTH), lambda i, b: (b, 0, 1)),
        ],
        out_specs=pl.BlockSpec((None, tm, FOURIER_WIDTH), lambda i, b: (b, i, 0)),
        out_shape=jax.ShapeDtypeStruct((B, lp, FOURIER_WIDTH), BF16),
        compiler_params=_params("parallel", "arbitrary"),
        name="fourier",
    )(w, pq, pq)


def _split_bf16(x):
    hi = x.astype(BF16)
    lo = (x - hi.astype(F32)).astype(BF16)
    return hi, lo


def _outproj_kernel(attn_ref, four_ref, gate_ref, h_ref, wa_ref, wf_ref, wo_ref, g_ref, b_ref,
                    wrh_ref, wrl_ref, wrth_ref, wrtl_ref,
                    h1_ref, h1b_ref, affr_ref, affc_ref, *, alpha):
    ya = jnp.dot(attn_ref[...], wa_ref[...], preferred_element_type=F32)
    yf = jnp.dot(four_ref[...], wf_ref[...], preferred_element_type=F32)
    merged = gate_ref[:, 0:D_MODEL] * ya + gate_ref[:, D_MODEL:GATE_WIDTH] * yf
    mix = jnp.dot(merged.astype(BF16), wo_ref[...], preferred_element_type=F32)
    h1 = _ln(alpha * h_ref[...] + mix, g_ref[...], b_ref[...])
    h1_ref[...] = h1
    hi, lo = _split_bf16(h1)
    h1b_ref[...] = hi
    lc = (jnp.dot(hi, wrh_ref[...], preferred_element_type=F32)
          + jnp.dot(hi, wrl_ref[...], preferred_element_type=F32)
          + jnp.dot(lo, wrh_ref[...], preferred_element_type=F32))
    lane = lax.broadcasted_iota(jnp.int32, lc.shape, 1)
    lc = jnp.where(lane < N_EXPERTS, lc, NEG_INF)
    ec = jnp.exp(lc - jnp.max(lc, axis=-1, keepdims=True))
    affc_ref[...] = (ec / jnp.sum(ec, axis=-1, keepdims=True))[:, 0:N_EXPERTS]
    lr = (lax.dot_general(wrth_ref[...], hi, NT_DIMS, preferred_element_type=F32)
          + lax.dot_general(wrtl_ref[...], hi, NT_DIMS, preferred_element_type=F32)
          + lax.dot_general(wrth_ref[...], lo, NT_DIMS, preferred_element_type=F32))
    er = jnp.exp(lr - jnp.max(lr, axis=0, keepdims=True))
    affr_ref[...] = er / jnp.sum(er, axis=0, keepdims=True)


def _outproj(attn, four, gates, h, wa, wf, wo, g, b, w_router, alpha, tm):
    n = h.shape[0]
    row = lambda i: (i, 0)
    const = lambda i: (0, 0)
    wr_pad = jnp.pad(w_router, ((0, 0), (0, LANES - N_EXPERTS)))
    wrh, wrl = _split_bf16(wr_pad)
    wrth, wrtl = _split_bf16(w_router.T)
    return pl.pallas_call(
        functools.partial(_outproj_kernel, alpha=alpha),
        grid=(n // tm,),
        in_specs=[
            pl.BlockSpec((tm, ATTN_WIDTH), row),
            pl.BlockSpec((tm, FOURIER_WIDTH), row),
            pl.BlockSpec((tm, GATE_WIDTH), row),
            pl.BlockSpec((tm, D_MODEL), row),
            pl.BlockSpec((ATTN_WIDTH, D_MODEL), const),
            pl.BlockSpec((FOURIER_WIDTH, D_MODEL), const),
            pl.BlockSpec((D_MODEL, D_MODEL), const),
            pl.BlockSpec((1, D_MODEL), const),
            pl.BlockSpec((1, D_MODEL), const),
            pl.BlockSpec((D_MODEL, LANES), const),
            pl.BlockSpec((D_MODEL, LANES), const),
            pl.BlockSpec((N_EXPERTS, D_MODEL), const),
            pl.BlockSpec((N_EXPERTS, D_MODEL), const),
        ],
        out_specs=[
            pl.BlockSpec((tm, D_MODEL), row),
            pl.BlockSpec((tm, D_MODEL), row),
            pl.BlockSpec((N_EXPERTS, tm), lambda i: (0, i)),
            pl.BlockSpec((tm, N_EXPERTS), row),
        ],
        out_shape=[
            jax.ShapeDtypeStruct((n, D_MODEL), F32),
            jax.ShapeDtypeStruct((n, D_MODEL), BF16),
            jax.ShapeDtypeStruct((N_EXPERTS, n), F32),
            jax.ShapeDtypeStruct((n, N_EXPERTS), F32),
        ],
        compiler_params=_params("parallel"),
        name="outproj_ln_router",
    )(attn, four, gates, h, wa, wf, wo, g.reshape(1, -1), b.reshape(1, -1), wrh, wrl, wrth, wrtl)


def _topk_kernel(aff_ref, rr_ref, rc_ref, cnt_ref, cntc_ref, *, lead, lp, cap):
    nchunk = lp // LANES
    lane = lax.broadcasted_iota(jnp.int32, (N_EXPERTS, lp), 1)
    bits = jnp.where(lane >= lead, pltpu.bitcast(aff_ref[...], jnp.int32), -1)

    def search(i, t):
        cand = t | (jnp.int32(1) << (30 - i))
        cnt = jnp.sum((bits >= cand).astype(jnp.int32), axis=-1, keepdims=True)
        return jnp.where(cnt >= cap, cand, t)

    thr = lax.fori_loop(0, 31, search, jnp.zeros((N_EXPERTS, 1), jnp.int32))
    gt = bits > thr
    eq = bits == thr
    need = cap - jnp.sum(gt.astype(jnp.int32), axis=-1, keepdims=True)

    r = lax.broadcasted_iota(jnp.int32, (LANES, LANES), 0)
    c = lax.broadcasted_iota(jnp.int32, (LANES, LANES), 1)
    upper = (r <= c).astype(BF16)
    lower = (c <= r).astype(BF16)
    ident = (c == r).astype(BF16)

    eqb = eq.astype(BF16)
    off = jnp.zeros((N_EXPERTS, 1), F32)
    needf = need.astype(F32)
    sel_chunks = []
    for k in range(nchunk):
        sl = slice(k * LANES, (k + 1) * LANES)
        pre = jnp.dot(eqb[:, sl], upper, preferred_element_type=F32) + off
        off = pre[:, LANES - 1:LANES]
        sel_chunks.append(gt[:, sl] | (eq[:, sl] & (pre <= needf)))

    off_r = jnp.zeros((N_EXPERTS, 1), F32)
    off_c = jnp.zeros((1, N_EXPERTS), F32)
    cnt_lane = lax.broadcasted_iota(jnp.int32, (N_EXPERTS, LANES), 1)
    cnt = jnp.zeros((N_EXPERTS, LANES), F32)
    for k in range(nchunk):
        sl = slice(k * LANES, (k + 1) * LANES)
        sel = sel_chunks[k]
        selb = sel.astype(BF16)
        pre_r = jnp.dot(selb, upper, preferred_element_type=F32) + off_r
        rr_ref[k] = jnp.where(sel, pre_r - 1.0, -1.0).astype(jnp.int32)
        off_r = pre_r[:, LANES - 1:LANES]
        cnt = jnp.where(cnt_lane == k + 1, off_r, cnt)
        pre_c = lax.dot_general(lower, selb, NT_DIMS, preferred_element_type=F32) + off_c
        sel_c = lax.dot_general(ident, selb, NT_DIMS, preferred_element_type=F32)
        rc_ref[sl, :] = jnp.where(sel_c > 0.5, pre_c - 1.0, -1.0).astype(jnp.int32)
        off_c = pre_c[LANES - 1:LANES, :]
        cntc_ref[k + 1:k + 2, :] = off_c.astype(jnp.int32)
    cnt_ref[...] = cnt.astype(jnp.int32)
    cntc_ref[0:1, :] = jnp.zeros((1, N_EXPERTS), jnp.int32)
    pad_rows = cntc_ref.shape[0] - nchunk - 1
    if pad_rows:
        cntc_ref[nchunk + 1:, :] = jnp.zeros((pad_rows, N_EXPERTS), jnp.int32)


def _topk(aff_r, B, lp, lead, cap):
    n = B * lp
    nchunk = lp // LANES
    assert nchunk + 1 <= LANES
    nb1 = -(-(nchunk + 1) // SUBLANES) * SUBLANES
    return pl.pallas_call(
        functools.partial(_topk_kernel, lead=lead, lp=lp, cap=cap),
        grid=(B,),
        in_specs=[pl.BlockSpec((N_EXPERTS, lp), lambda b: (0, b))],
        out_specs=[
            pl.BlockSpec((None, nchunk, N_EXPERTS, LANES), lambda b: (b, 0, 0, 0)),
            pl.BlockSpec((lp, N_EXPERTS), lambda b: (b, 0)),
            pl.BlockSpec((None, N_EXPERTS, LANES), lambda b: (b, 0, 0)),
            pl.BlockSpec((None, nb1, N_EXPERTS), lambda b: (b, 0, 0)),
        ],
        out_shape=[
            jax.ShapeDtypeStruct((B, nchunk, N_EXPERTS, LANES), jnp.int32),
            jax.ShapeDtypeStruct((n, N_EXPERTS), jnp.int32),
            jax.ShapeDtypeStruct((B, N_EXPERTS, LANES), jnp.int32),
            jax.ShapeDtypeStruct((B, nb1, N_EXPERTS), jnp.int32),
        ],
        compiler_params=_params("parallel"),
        name="topk_select",
    )(aff_r)


SLOT_ALIGN = 16


def _window_constants(win):
    wide = N_EXPERTS * win
    expand = np.zeros((N_EXPERTS, wide), np.float32)
    for e in range(N_EXPERTS):
        expand[e, e * win:(e + 1) * win] = 1.0
    return expand, (np.arange(wide) % win).astype(np.float32)


def _gather_kernel(cnt_ref, cntv_ref, rank_ref, h_ref, expand_ref, pos_ref, o_ref,
                   *, slots, ch, win, nchunk):
    bi, c = pl.program_id(0), pl.program_id(1)
    k0 = c * ch
    tt = ch * LANES
    max_start = slots - win

    @pl.when(c == 0)
    def _():
        o_ref[...] = jnp.zeros_like(o_ref)

    lane = lax.broadcasted_iota(jnp.int32, cntv_ref.shape, 1)
    lo_col = jnp.sum(jnp.where(lane == k0, cntv_ref[...], 0), axis=-1, keepdims=True)
    w_col = jnp.minimum(lo_col & -SLOT_ALIGN, max_start)
    rank = rank_ref[...]
    rel = rank - w_col
    in_win = (rank >= 0) & (rel >= 0) & (rel < win)
    relb = jnp.where(in_win, rel, -1).astype(F32).astype(BF16)
    rel_tall = jnp.dot(expand_ref[...], relb, preferred_element_type=F32)
    onehot = (rel_tall == pos_ref[...]).astype(BF16)
    rows = h_ref[...]
    xw = jnp.dot(onehot, rows, preferred_element_type=F32)

    def add_rows(e, start, vals):
        dst = pl.ds(pl.multiple_of(start, SLOT_ALIGN), win)
        o_ref[e, dst, :] = (o_ref[e, dst, :].astype(F32) + vals).astype(BF16)

    row = lax.broadcasted_iota(jnp.int32, (win, tt), 0)
    for e in range(N_EXPERTS):
        base = (bi * N_EXPERTS + e) * (nchunk + 1) + k0
        w = jnp.minimum(cnt_ref[base] & -SLOT_ALIGN, max_start)
        add_rows(e, w, xw[e * win:(e + 1) * win, :])
        hi = cnt_ref[base + ch]

        @pl.when(hi > w + win)
        def _(e=e, w=w, hi=hi):
            rk = rank[e:e + 1, :]

            def extra(i, carry):
                w2 = w + (i + 1) * win
                w2c = jnp.minimum(w2, max_start)
                hit = ((rk - w2c) == row) & (rk >= w2)
                add_rows(e, w2c, jnp.dot(hit.astype(BF16), rows, preferred_element_type=F32))
                return carry

            lax.fori_loop(0, (hi - w - 1) // win, extra, 0)


def _gather(cnt, cntv, rank_rows, h1b, B, lp, slots, ch, win):
    nchunk = lp // LANES
    tt = ch * LANES
    nt = nchunk // ch
    assert slots % SLOT_ALIGN == 0 and win % SLOT_ALIGN == 0 and win <= slots
    expand, pos = _window_constants(win)
    wide = N_EXPERTS * win
    const = lambda bi, c, cnt: (0, 0)
    return pl.pallas_call(
        functools.partial(_gather_kernel, slots=slots, ch=ch, win=win, nchunk=nchunk),
        grid_spec=pltpu.PrefetchScalarGridSpec(
            num_scalar_prefetch=1,
            grid=(B, nt),
            in_specs=[
                pl.BlockSpec((None, N_EXPERTS, LANES), lambda bi, c, cnt: (bi, 0, 0)),
                pl.BlockSpec((None, N_EXPERTS, tt), lambda bi, c, cnt: (bi, 0, c)),
                pl.BlockSpec((tt, D_MODEL), lambda bi, c, cnt: (bi * nt + c, 0)),
                pl.BlockSpec((wide, N_EXPERTS), const),
                pl.BlockSpec((wide, 1), const),
            ],
            out_specs=pl.BlockSpec((None, N_EXPERTS, slots, D_MODEL), lambda bi, c, cnt: (bi, 0, 0, 0)),
        ),
        out_shape=jax.ShapeDtypeStruct((B, N_EXPERTS, slots, D_MODEL), BF16),
        compiler_params=_params("parallel", "arbitrary"),
        name="moe_gather",
    )(cnt, cntv, rank_rows, h1b, jnp.asarray(expand.T, dtype=BF16), jnp.asarray(pos.reshape(wide, 1)))


def _ffn_kernel(x_ref, rank_ref, aff_ref, wg_ref, wu_ref, wd_ref, o_ref):
    x = x_ref[...]
    slot = lax.broadcasted_iota(jnp.int32, (x.shape[0], rank_ref.shape[-1]), 0)
    gate = jnp.sum(jnp.where(slot == rank_ref[...], aff_ref[...], 0.0), axis=-1, keepdims=True)
    a = jnp.dot(x, wg_ref[...], preferred_element_type=F32)
    u = jnp.dot(x, wu_ref[...], preferred_element_type=F32)
    hm = (a * jax.nn.sigmoid(a) * u).astype(BF16)
    y = jnp.dot(hm, wd_ref[...], preferred_element_type=F32)
    o_ref[...] = (y * gate).astype(BF16)


def _ffn(xg, rank_rows, aff_rows, wg, wu, wd):
    B, _, slots, _ = xg.shape
    lp = rank_rows.shape[-1]
    act = pl.BlockSpec((None, None, slots, D_MODEL), lambda e, b: (b, e, 0, 0))
    row = pl.BlockSpec((None, None, 1, lp), lambda e, b: (b, e, 0, 0))
    return pl.pallas_call(
        _ffn_kernel,
        grid=(N_EXPERTS, B),
        in_specs=[
            act, row, row,
            pl.BlockSpec((None, D_MODEL, D_FF_EXPERT), lambda e, b: (e, 0, 0)),
            pl.BlockSpec((None, D_MODEL, D_FF_EXPERT), lambda e, b: (e, 0, 0)),
            pl.BlockSpec((None, D_FF_EXPERT, D_MODEL), lambda e, b: (e, 0, 0)),
        ],
        out_specs=act,
        out_shape=jax.ShapeDtypeStruct(xg.shape, BF16),
        compiler_params=_params("parallel", "arbitrary"),
        name="moe_ffn",
    )(xg, rank_rows, aff_rows, wg, wu, wd)


def _combine_kernel(cnt_ref, cntc_ref, rank_ref, y_ref, h_ref, g_ref, b_ref, expand_ref, rpat_ref,
                    o_ref, acc_ref, *, slots, ch, win, nchunk, alpha):
    bi, c = pl.program_id(0), pl.program_id(1)
    k0 = c * ch
    tt = ch * LANES
    max_start = slots - win
    rank = rank_ref[...]
    lo_row = cntc_ref[pl.ds(k0, 1), :]
    w_row = jnp.minimum(lo_row & -SLOT_ALIGN, max_start)
    rel = rank - w_row
    in_win = (rank >= 0) & (rel >= 0) & (rel < win)
    relb = jnp.where(in_win, rel, -1).astype(F32).astype(BF16)
    rel_wide = jnp.dot(relb, expand_ref[...], preferred_element_type=F32)
    onehot = (rel_wide == rpat_ref[...]).astype(BF16)

    starts, windows = [], []
    for e in range(N_EXPERTS):
        lo = cnt_ref[(bi * N_EXPERTS + e) * (nchunk + 1) + k0]
        w = jnp.minimum(lo & -SLOT_ALIGN, max_start)
        starts.append(w)
        windows.append(y_ref[e, pl.ds(pl.multiple_of(w, SLOT_ALIGN), win), :])
    acc_ref[...] = jnp.dot(onehot, jnp.concatenate(windows, axis=0), preferred_element_type=F32)

    lane = lax.broadcasted_iota(jnp.int32, rank.shape, 1)
    col = lax.broadcasted_iota(jnp.int32, (tt, win), 1)
    for e in range(N_EXPERTS):
        hi = cnt_ref[(bi * N_EXPERTS + e) * (nchunk + 1) + k0 + ch]

        @pl.when(hi > starts[e] + win)
        def _(e=e, hi=hi):
            rk = jnp.sum(jnp.where(lane == e, rank, 0), axis=-1, keepdims=True)

            def extra(i, carry):
                w2 = starts[e] + (i + 1) * win
                w2c = jnp.minimum(w2, max_start)
                rows = y_ref[e, pl.ds(pl.multiple_of(w2c, SLOT_ALIGN), win), :]
                hit = ((rk - w2c) == col) & (rk >= w2)
                acc_ref[...] += jnp.dot(hit.astype(BF16), rows, preferred_element_type=F32)
                return carry

            lax.fori_loop(0, (hi - starts[e] - 1) // win, extra, 0)

    o_ref[...] = _ln(alpha * h_ref[...] + acc_ref[...], g_ref[...], b_ref[...])


def _combine(cnt, cntc, rank_c, y, h1, g, b, B, lp, alpha, ch, win):
    n = B * lp
    slots = y.shape[2]
    nchunk = lp // LANES
    tt = ch * LANES
    nt = nchunk // ch
    assert slots % SLOT_ALIGN == 0 and win % SLOT_ALIGN == 0 and win <= slots
    wide = N_EXPERTS * win
    expand, pos = _window_constants(win)
    tok = lambda bi, c, cnt: (bi * nt + c, 0)
    const = lambda bi, c, cnt: (0, 0)
    return pl.pallas_call(
        functools.partial(_combine_kernel, slots=slots, ch=ch, win=win, nchunk=nchunk, alpha=alpha),
        grid_spec=pltpu.PrefetchScalarGridSpec(
            num_scalar_prefetch=1,
            grid=(B, nt),
            in_specs=[
                pl.BlockSpec((None, cntc.shape[1], N_EXPERTS), lambda bi, c, cnt: (bi, 0, 0)),
                pl.BlockSpec((tt, N_EXPERTS), tok),
                pl.BlockSpec((None, N_EXPERTS, slots, D_MODEL), lambda bi, c, cnt: (bi, 0, 0, 0)),
                pl.BlockSpec((tt, D_MODEL), tok),
                pl.BlockSpec((1, D_MODEL), const),
                pl.BlockSpec((1, D_MODEL), const),
                pl.BlockSpec((N_EXPERTS, wide), const),
                pl.BlockSpec((1, wide), const),
            ],
            out_specs=pl.BlockSpec((tt, D_MODEL), tok),
            scratch_shapes=[pltpu.VMEM((tt, D_MODEL), F32)],
        ),
        out_shape=jax.ShapeDtypeStruct((n, D_MODEL), F32),
        compiler_params=_params("parallel", "arbitrary"),
        name="moe_combine_ln",
    )(cnt, cntc, rank_c, y, h1, g.reshape(1, -1), b.reshape(1, -1),
      jnp.asarray(expand, dtype=BF16), jnp.asarray(pos.reshape(1, wide)))


def kernel(x, meta, ln0_g, ln0_b, w_in, b_gate, sink, w_attn_o, w_four_o, w_out, ln1_g, ln1_b,
           w_router, w_e_gate, w_e_up, w_e_down, ln2_g, ln2_b):
    B, seq, d = x.shape
    depth = w_in.shape[0]
    assert d == D_MODEL and meta.shape == (N_META, D_MODEL)
    assert seq % BLOCK == 0 and N_META % SUBLANES == 0 and N_META <= BLOCK
    L = seq + N_META
    nb = -(-L // BLOCK)
    lp = nb * BLOCK
    lead = lp - L
    cap = CAPACITY_FACTOR * L // N_EXPERTS
    slots = -(-cap // SLOT_ALIGN) * SLOT_ALIGN
    alpha = float((2 * depth) ** 0.25)
    n = B * lp
    nchunk = lp // LANES
    tm = _pick_tile(n, (512, 256, 128))
    tf = _pick_tile(lp, (384, 640, 128))
    ch = _pick_tile(nchunk, (3, 2, 1))
    win = min(slots, -(-(2 * ch * LANES * cap // L + SLOT_ALIGN) // SLOT_ALIGN) * SLOT_ALIGN)

    gi = np.arange(FOURIER_GROUP)
    ang = 2.0 * np.pi * ((gi[:, None] * gi[None, :]) % FOURIER_GROUP) / FOURIER_GROUP
    cs = jnp.asarray(np.concatenate([np.cos(ang), np.sin(ang)], axis=1) * FOURIER_GROUP ** -0.5, dtype=BF16)
    wdft = _dft_matrix(lp, lead, L)
    bias, variant_of_block = _attn_bias_tables(nb, lead)

    h = _embed(x, meta, ln0_g, ln0_b, nb, lead).reshape(n, D_MODEL)
    for l in range(depth):
        q, kv, pq, gates = _inproj(h, w_in[l].astype(BF16), b_gate[l], cs, tm)
        attn = _attention(q.reshape(B, lp, -1), kv.reshape(B, lp, -1), sink[l], bias, variant_of_block, nb)
        four = _fourier(wdft, pq.reshape(B, lp, -1), tf)
        h1, h1b, aff_r, aff_c = _outproj(
            attn.reshape(n, -1), four.reshape(n, -1), gates, h,
            w_attn_o[l].astype(BF16), w_four_o[l].astype(BF16), w_out[l].astype(BF16),
            ln1_g[l], ln1_b[l], w_router[l], alpha, tm)
        rank_r, rank_c, cntv, cntc = _topk(aff_r, B, lp, lead, cap)
        cnt = cntv[:, :, :nchunk + 1].reshape(-1)
        rank_rows = rank_r.transpose(0, 2, 1, 3).reshape(B, N_EXPERTS, lp)
        aff_rows = aff_r.reshape(N_EXPERTS, B, 1, lp).transpose(1, 0, 2, 3)
        xg = _gather(cnt, cntv, rank_rows, h1b, B, lp, slots, ch, win)
        y = _ffn(xg, rank_rows.reshape(B, N_EXPERTS, 1, lp), aff_rows,
                 w_e_gate[l].astype(BF16), w_e_up[l].astype(BF16), w_e_down[l].astype(BF16))
        h = _combine(cnt, cntc, rank_c, y, h1, ln2_g[l], ln2_b[l], B, lp, alpha, ch, win)
    return h.reshape(B, lp, D_MODEL)[:, lead + N_META:]
```

```python
import functools
import math

import numpy as np
import jax
import jax.numpy as jnp
from jax import lax
from jax.experimental import pallas as pl
from jax.experimental.pallas import tpu as pltpu

D_MODEL = 1024
N_META = 16
N_HEADS = 8
N_KV_HEADS = 2
HEAD_DIM = 64
GQA_GROUP = N_HEADS // N_KV_HEADS
ATTN_WIDTH = N_HEADS * HEAD_DIM
KV_WIDTH = N_KV_HEADS * HEAD_DIM
WINDOW = 128
BLOCK = 128
N_FOURIER_GROUPS = 4
FOURIER_GROUP = 128
FOURIER_WIDTH = N_FOURIER_GROUPS * FOURIER_GROUP
N_BRANCHES = 2
GATE_WIDTH = N_BRANCHES * D_MODEL
N_EXPERTS = 16
CAPACITY_FACTOR = 2
D_FF_EXPERT = 1536
LN_EPS = 1e-5
NEG_INF = -1e30
Q_END = ATTN_WIDTH
K_END = Q_END + KV_WIDTH
V_END = K_END + KV_WIDTH
F_END = V_END + FOURIER_WIDTH
IN_WIDTH = F_END + GATE_WIDTH

LANES = 128
SUBLANES = 8
VMEM_LIMIT_BYTES = 56 * 1024 * 1024

F32 = jnp.float32
BF16 = jnp.bfloat16
NT_DIMS = (((1,), (1,)), ((), ()))


def _pick_tile(n, candidates):
    for c in candidates:
        if n % c == 0:
            return c
    raise ValueError(f"no tile in {candidates} divides {n}")


def _params(*sem, flags=None):
    return pltpu.CompilerParams(dimension_semantics=sem, vmem_limit_bytes=VMEM_LIMIT_BYTES, flags=flags)


def _ln(x, g, b):
    mu = jnp.mean(x, axis=-1, keepdims=True)
    xc = x - mu
    var = jnp.mean(xc * xc, axis=-1, keepdims=True)
    return xc * lax.rsqrt(var + LN_EPS) * g + b


def _embed_kernel(x_ref, meta_ref, g_ref, b_ref, o_ref, *, lead):
    j = pl.program_id(1)

    @pl.when(j == 0)
    def _():
        o_ref[0:lead, :] = jnp.zeros((lead, D_MODEL), F32)
        o_ref[lead:BLOCK, :] = _ln(meta_ref[...], g_ref[...], b_ref[...])

    @pl.when(j > 0)
    def _():
        o_ref[...] = _ln(x_ref[...], g_ref[...], b_ref[...])


def _embed(x, meta, g, b, nb, lead):
    B = x.shape[0]
    return pl.pallas_call(
        functools.partial(_embed_kernel, lead=lead),
        grid=(B, nb),
        in_specs=[
            pl.BlockSpec((None, BLOCK, D_MODEL), lambda bi, j: (bi, jnp.maximum(j - 1, 0), 0)),
            pl.BlockSpec((N_META, D_MODEL), lambda bi, j: (0, 0)),
            pl.BlockSpec((1, D_MODEL), lambda bi, j: (0, 0)),
            pl.BlockSpec((1, D_MODEL), lambda bi, j: (0, 0)),
        ],
        out_specs=pl.BlockSpec((None, BLOCK, D_MODEL), lambda bi, j: (bi, j, 0)),
        out_shape=jax.ShapeDtypeStruct((B, nb * BLOCK, D_MODEL), F32),
        compiler_params=_params("parallel", "arbitrary"),
        name="embed_ln",
    )(x, meta, g.reshape(1, -1), b.reshape(1, -1))


def _inproj_kernel(h_ref, w_ref, bg_ref, cs_ref, q_ref, kv_ref, pq_ref, gate_ref):
    hb = h_ref[...].astype(BF16)
    q_ref[...] = jnp.dot(hb, w_ref[:, 0:Q_END], preferred_element_type=F32).astype(BF16)
    kv_ref[...] = jnp.dot(hb, w_ref[:, Q_END:V_END], preferred_element_type=F32).astype(BF16)
    uf = jnp.dot(hb, w_ref[:, V_END:F_END], preferred_element_type=F32).astype(BF16)
    for g in range(N_FOURIER_GROUPS):
        lo = g * FOURIER_GROUP
        pq = jnp.dot(uf[:, lo:lo + FOURIER_GROUP], cs_ref[...], preferred_element_type=F32)
        pq_ref[:, lo:lo + FOURIER_GROUP] = pq[:, 0:FOURIER_GROUP].astype(BF16)
        pq_ref[:, FOURIER_WIDTH + lo:FOURIER_WIDTH + lo + FOURIER_GROUP] = (
            pq[:, FOURIER_GROUP:2 * FOURIER_GROUP].astype(BF16))
    chunk = 512
    for c in range(GATE_WIDTH // chunk):
        lo = c * chunk
        ug = jnp.dot(hb, w_ref[:, F_END + lo:F_END + lo + chunk], preferred_element_type=F32)
        gate_ref[:, lo:lo + chunk] = jax.nn.sigmoid(ug + bg_ref[:, lo:lo + chunk])


def _inproj(h, w_bf16, b_gate, cs, tm):
    n = h.shape[0]
    row = lambda i: (i, 0)
    const = lambda i: (0, 0)
    return pl.pallas_call(
        _inproj_kernel,
        grid=(n // tm,),
        in_specs=[
            pl.BlockSpec((tm, D_MODEL), row),
            pl.BlockSpec((D_MODEL, IN_WIDTH), const),
            pl.BlockSpec((1, GATE_WIDTH), const),
            pl.BlockSpec((FOURIER_GROUP, 2 * FOURIER_GROUP), const),
        ],
        out_specs=[
            pl.BlockSpec((tm, ATTN_WIDTH), row),
            pl.BlockSpec((tm, 2 * KV_WIDTH), row),
            pl.BlockSpec((tm, 2 * FOURIER_WIDTH), row),
            pl.BlockSpec((tm, GATE_WIDTH), row),
        ],
        out_shape=[
            jax.ShapeDtypeStruct((n, ATTN_WIDTH), BF16),
            jax.ShapeDtypeStruct((n, 2 * KV_WIDTH), BF16),
            jax.ShapeDtypeStruct((n, 2 * FOURIER_WIDTH), BF16),
            jax.ShapeDtypeStruct((n, GATE_WIDTH), F32),
        ],
        compiler_params=_params("parallel"),
        name="inproj",
    )(h, w_bf16, b_gate.reshape(1, GATE_WIDTH), cs)


def _attn_kernel(sink_ref, q_ref, kvp_ref, kvc_ref, kvn_ref, bias_ref, o_ref):
    kv = jnp.concatenate([kvp_ref[...], kvc_ref[...], kvn_ref[...]], axis=0)
    low_half = lax.broadcasted_iota(jnp.int32, (3 * BLOCK, LANES), 1) < HEAD_DIM

    def lane_half_operands(x):
        swapped = jnp.concatenate([x[:, HEAD_DIM:], x[:, :HEAD_DIM]], axis=1)
        zero = jnp.zeros_like(x)
        return {(kvh, half): jnp.where(low_half if half == 0 else ~low_half,
                                       x if kvh == half else swapped, zero)
                for kvh in range(N_KV_HEADS) for half in range(2)}

    k_ops = lane_half_operands(kv[:, 0:KV_WIDTH])
    v_ops = lane_half_operands(kv[:, KV_WIDTH:2 * KV_WIDTH])

    def scores(h):
        pair = h // 2
        qp = q_ref[:, pair * LANES:(pair + 1) * LANES]
        return lax.dot_general(qp, k_ops[(h // GQA_GROUP, h % 2)], NT_DIMS, preferred_element_type=F32)

    def head_out(h, s):
        sink = sink_ref[h]
        logits = s * (HEAD_DIM ** -0.5) + bias_ref[h]
        m = jnp.maximum(jnp.max(logits, axis=-1, keepdims=True), sink)
        p = jnp.exp(logits - m)
        denom = jnp.sum(p, axis=-1, keepdims=True) + jnp.exp(sink - m)
        o = jnp.dot(p.astype(BF16), v_ops[(h // GQA_GROUP, h % 2)], preferred_element_type=F32)
        return o / denom

    s_next = scores(0)
    for h in range(N_HEADS):
        s_cur = s_next
        if h + 1 < N_HEADS:
            s_next = scores(h + 1)
        o = head_out(h, s_cur)
        if h % 2 == 0:
            o_even = o
        else:
            pair = h // 2
            o_ref[:, pair * LANES:(pair + 1) * LANES] = (o_even + o).astype(BF16)


def _attn_bias_tables(nb, lead):
    qi = np.arange(BLOCK)[:, None]
    si = np.arange(3 * BLOCK)[None, :]
    rel = np.abs(si - BLOCK - qi).astype(np.float32)
    slopes = np.array([2.0 ** (-8.0 * (h + 1) / N_HEADS) for h in range(N_HEADS)], np.float32)
    base = np.where(rel[None] <= WINDOW, -slopes[:, None, None] * rel[None], np.float32(NEG_INF))
    variants, keys, variant_of_block = [], [], []
    for i in range(nb):
        kpos = (i - 1) * BLOCK + np.arange(3 * BLOCK)
        valid = (kpos >= lead) & (kpos < nb * BLOCK)
        key = valid.tobytes()
        if key not in keys:
            keys.append(key)
            variants.append(np.where(valid[None, None, :], base, np.float32(NEG_INF)))
        variant_of_block.append(keys.index(key))
    return jnp.asarray(np.stack(variants).astype(np.float32)), variant_of_block


def _attention(q, kv, sink, bias, variant_of_block, nb):
    B, lp, _ = q.shape
    kvspec = lambda f: pl.BlockSpec((None, BLOCK, 2 * KV_WIDTH), f)
    interior = max(set(variant_of_block), key=variant_of_block.count)

    def variant(i):
        v = jnp.int32(interior)
        for blk, var in enumerate(variant_of_block):
            if var != interior:
                v = jnp.where(i == blk, var, v)
        return v

    return pl.pallas_call(
        _attn_kernel,
        grid=(B, nb),
        in_specs=[
            pl.BlockSpec(memory_space=pltpu.SMEM),
            pl.BlockSpec((None, BLOCK, ATTN_WIDTH), lambda b, i: (b, i, 0)),
            kvspec(lambda b, i: (b, jnp.maximum(i - 1, 0), 0)),
            kvspec(lambda b, i: (b, i, 0)),
            kvspec(lambda b, i: (b, jnp.minimum(i + 1, nb - 1), 0)),
            pl.BlockSpec((None, N_HEADS, BLOCK, 3 * BLOCK), lambda b, i: (variant(i), 0, 0, 0)),
        ],
        out_specs=pl.BlockSpec((None, BLOCK, ATTN_WIDTH), lambda b, i: (b, i, 0)),
        out_shape=jax.ShapeDtypeStruct((B, lp, ATTN_WIDTH), BF16),
        compiler_params=_params("parallel", "arbitrary"),
        name="attention",
    )(sink, q, kv, kv, kv, bias)


def _fourier_kernel(w_ref, p_ref, q_ref, o_ref, *, lp):
    z = jnp.dot(w_ref[:, 0:lp], p_ref[...], preferred_element_type=F32)
    z = z + jnp.dot(w_ref[:, lp:2 * lp], q_ref[...], preferred_element_type=F32)
    o_ref[...] = z.astype(BF16)


def _dft_matrix(lp, lead, seq_len):
    w = np.float32(2.0 * math.pi / seq_len)
    k = jnp.maximum(jnp.arange(lp, dtype=jnp.int32) - lead, 0)[:, None]
    j = jnp.arange(lp // LANES, dtype=jnp.int32)[None, :]
    r = jnp.arange(LANES, dtype=jnp.int32)[None, :]
    a = ((k * (LANES * j - lead)) % seq_len).astype(F32) * w
    b = ((k * r) % seq_len).astype(F32) * w
    ca, sa, cb, sb = jnp.cos(a)[:, :, None], jnp.sin(a)[:, :, None], jnp.cos(b)[:, None, :], jnp.sin(b)[:, None, :]
    pos = jnp.arange(lp, dtype=jnp.int32) - lead
    valid = ((pos[:, None] >= 0) & (pos[None, :] >= 0)).reshape(lp, lp // LANES, LANES)
    scale = np.float32(seq_len ** -0.5)
    wc = jnp.where(valid, (ca * cb - sa * sb) * scale, 0.0).reshape(lp, lp)
    ws = jnp.where(valid, -(sa * cb + ca * sb) * scale, 0.0).reshape(lp, lp)
    return jnp.concatenate([wc, ws], axis=1).astype(BF16)


def _fourier(w, pq, tm):
    B, lp, _ = pq.shape
    return pl.pallas_call(
        functools.partial(_fourier_kernel, lp=lp),
        grid=(lp // tm, B),
        in_specs=[
            pl.BlockSpec((tm, 2 * lp), lambda i, b: (i, 0)),
            pl.BlockSpec((None, lp, FOURIER_WIDTH), lambda i, b: (b, 0, 0)),
            pl.BlockSpec((None, lp, FOURIER_WIDTH), lambda i, b: (b, 0, 1)),
        ],
        out_specs=pl.BlockSpec((None, tm, FOURIER_WIDTH), lambda i, b: (b, i, 0)),
        out_shape=jax.ShapeDtypeStruct((B, lp, FOURIER_WIDTH), BF16),
        compiler_params=_params("parallel", "arbitrary"),
        name="fourier",
    )(w, pq, pq)


def _split_bf16(x):
    hi = x.astype(BF16)
    lo = (x - hi.astype(F32)).astype(BF16)
    return hi, lo


def _outproj_kernel(attn_ref, four_ref, gate_ref, h_ref, wa_ref, wf_ref, wo_ref, g_ref, b_ref,
                    wrh_ref, wrl_ref, wrth_ref, wrtl_ref,
                    h1_ref, h1b_ref, affr_ref, affc_ref, *, alpha):
    ya = jnp.dot(attn_ref[...], wa_ref[...], preferred_element_type=F32)
    yf = jnp.dot(four_ref[...], wf_ref[...], preferred_element_type=F32)
    merged = gate_ref[:, 0:D_MODEL] * ya + gate_ref[:, D_MODEL:GATE_WIDTH] * yf
    mix = jnp.dot(merged.astype(BF16), wo_ref[...], preferred_element_type=F32)
    h1 = _ln(alpha * h_ref[...] + mix, g_ref[...], b_ref[...])
    h1_ref[...] = h1
    hi, lo = _split_bf16(h1)
    h1b_ref[...] = hi
    lc = (jnp.dot(hi, wrh_ref[...], preferred_element_type=F32)
          + jnp.dot(hi, wrl_ref[...], preferred_element_type=F32)
          + jnp.dot(lo, wrh_ref[...], preferred_element_type=F32))
    lane = lax.broadcasted_iota(jnp.int32, lc.shape, 1)
    lc = jnp.where(lane < N_EXPERTS, lc, NEG_INF)
    ec = jnp.exp(lc - jnp.max(lc, axis=-1, keepdims=True))
    affc_ref[...] = (ec / jnp.sum(ec, axis=-1, keepdims=True))[:, 0:N_EXPERTS]
    lr = (lax.dot_general(wrth_ref[...], hi, NT_DIMS, preferred_element_type=F32)
          + lax.dot_general(wrtl_ref[...], hi, NT_DIMS, preferred_element_type=F32)
          + lax.dot_general(wrth_ref[...], lo, NT_DIMS, preferred_element_type=F32))
    er = jnp.exp(lr - jnp.max(lr, axis=0, keepdims=True))
    affr_ref[...] = er / jnp.sum(er, axis=0, keepdims=True)


def _outproj(attn, four, gates, h, wa, wf, wo, g, b, w_router, alpha, tm):
    n = h.shape[0]
    row = lambda i: (i, 0)
    const = lambda i: (0, 0)
    wr_pad = jnp.pad(w_router, ((0, 0), (0, LANES - N_EXPERTS)))
    wrh, wrl = _split_bf16(wr_pad)
    wrth, wrtl = _split_bf16(w_router.T)
    return pl.pallas_call(
        functools.partial(_outproj_kernel, alpha=alpha),
        grid=(n // tm,),
        in_specs=[
            pl.BlockSpec((tm, ATTN_WIDTH), row),
            pl.BlockSpec((tm, FOURIER_WIDTH), row),
            pl.BlockSpec((tm, GATE_WIDTH), row),
            pl.BlockSpec((tm, D_MODEL), row),
            pl.BlockSpec((ATTN_WIDTH, D_MODEL), const),
            pl.BlockSpec((FOURIER_WIDTH, D_MODEL), const),
            pl.BlockSpec((D_MODEL, D_MODEL), const),
            pl.BlockSpec((1, D_MODEL), const),
            pl.BlockSpec((1, D_MODEL), const),
            pl.BlockSpec((D_MODEL, LANES), const),
            pl.BlockSpec((D_MODEL, LANES), const),
            pl.BlockSpec((N_EXPERTS, D_MODEL), const),
            pl.BlockSpec((N_EXPERTS, D_MODEL), const),
        ],
        out_specs=[
            pl.BlockSpec((tm, D_MODEL), row),
            pl.BlockSpec((tm, D_MODEL), row),
            pl.BlockSpec((N_EXPERTS, tm), lambda i: (0, i)),
            pl.BlockSpec((tm, N_EXPERTS), row),
        ],
        out_shape=[
            jax.ShapeDtypeStruct((n, D_MODEL), F32),
            jax.ShapeDtypeStruct((n, D_MODEL), BF16),
            jax.ShapeDtypeStruct((N_EXPERTS, n), F32),
            jax.ShapeDtypeStruct((n, N_EXPERTS), F32),
        ],
        compiler_params=_params("parallel"),
        name="outproj_ln_router",
    )(attn, four, gates, h, wa, wf, wo, g.reshape(1, -1), b.reshape(1, -1), wrh, wrl, wrth, wrtl)


def _topk_kernel(aff_ref, rr_ref, rc_ref, cnt_ref, cntc_ref, *, lead, lp, cap):
    nchunk = lp // LANES
    lane = lax.broadcasted_iota(jnp.int32, (N_EXPERTS, lp), 1)
    bits = jnp.where(lane >= lead, pltpu.bitcast(aff_ref[...], jnp.int32), -1)

    def search(i, t):
        cand = t | (jnp.int32(1) << (30 - i))
        cnt = jnp.sum((bits >= cand).astype(jnp.int32), axis=-1, keepdims=True)
        return jnp.where(cnt >= cap, cand, t)

    thr = lax.fori_loop(0, 31, search, jnp.zeros((N_EXPERTS, 1), jnp.int32))
    gt = bits > thr
    eq = bits == thr
    need = cap - jnp.sum(gt.astype(jnp.int32), axis=-1, keepdims=True)

    r = lax.broadcasted_iota(jnp.int32, (LANES, LANES), 0)
    c = lax.broadcasted_iota(jnp.int32, (LANES, LANES), 1)
    upper = (r <= c).astype(BF16)
    lower = (c <= r).astype(BF16)
    ident = (c == r).astype(BF16)

    eqb = eq.astype(BF16)
    off = jnp.zeros((N_EXPERTS, 1), F32)
    needf = need.astype(F32)
    sel_chunks = []
    for k in range(nchunk):
        sl = slice(k * LANES, (k + 1) * LANES)
        pre = jnp.dot(eqb[:, sl], upper, preferred_element_type=F32) + off
        off = pre[:, LANES - 1:LANES]
        sel_chunks.append(gt[:, sl] | (eq[:, sl] & (pre <= needf)))

    off_r = jnp.zeros((N_EXPERTS, 1), F32)
    off_c = jnp.zeros((1, N_EXPERTS), F32)
    cnt_lane = lax.broadcasted_iota(jnp.int32, (N_EXPERTS, LANES), 1)
    cnt = jnp.zeros((N_EXPERTS, LANES), F32)
    for k in range(nchunk):
        sl = slice(k * LANES, (k + 1) * LANES)
        sel = sel_chunks[k]
        selb = sel.astype(BF16)
        pre_r = jnp.dot(selb, upper, preferred_element_type=F32) + off_r
        rr_ref[k] = jnp.where(sel, pre_r - 1.0, -1.0).astype(jnp.int32)
        off_r = pre_r[:, LANES - 1:LANES]
        cnt = jnp.where(cnt_lane == k + 1, off_r, cnt)
        pre_c = lax.dot_general(lower, selb, NT_DIMS, preferred_element_type=F32) + off_c
        sel_c = lax.dot_general(ident, selb, NT_DIMS, preferred_element_type=F32)
        rc_ref[sl, :] = jnp.where(sel_c > 0.5, pre_c - 1.0, -1.0).astype(jnp.int32)
        off_c = pre_c[LANES - 1:LANES, :]
        cntc_ref[k + 1:k + 2, :] = off_c.astype(jnp.int32)
    cnt_ref[...] = cnt.astype(jnp.int32)
    cntc_ref[0:1, :] = jnp.zeros((1, N_EXPERTS), jnp.int32)
    pad_rows = cntc_ref.shape[0] - nchunk - 1
    if pad_rows:
        cntc_ref[nchunk + 1:, :] = jnp.zeros((pad_rows, N_EXPERTS), jnp.int32)


def _topk(aff_r, B, lp, lead, cap):
    n = B * lp
    nchunk = lp // LANES
    assert nchunk + 1 <= LANES
    nb1 = -(-(nchunk + 1) // SUBLANES) * SUBLANES
    return pl.pallas_call(
        functools.partial(_topk_kernel, lead=lead, lp=lp, cap=cap),
        grid=(B,),
        in_specs=[pl.BlockSpec((N_EXPERTS, lp), lambda b: (0, b))],
        out_specs=[
            pl.BlockSpec((None, nchunk, N_EXPERTS, LANES), lambda b: (b, 0, 0, 0)),
            pl.BlockSpec((lp, N_EXPERTS), lambda b: (b, 0)),
            pl.BlockSpec((None, N_EXPERTS, LANES), lambda b: (b, 0, 0)),
            pl.BlockSpec((None, nb1, N_EXPERTS), lambda b: (b, 0, 0)),
        ],
        out_shape=[
            jax.ShapeDtypeStruct((B, nchunk, N_EXPERTS, LANES), jnp.int32),
            jax.ShapeDtypeStruct((n, N_EXPERTS), jnp.int32),
            jax.ShapeDtypeStruct((B, N_EXPERTS, LANES), jnp.int32),
            jax.ShapeDtypeStruct((B, nb1, N_EXPERTS), jnp.int32),
        ],
        compiler_params=_params("parallel"),
        name="topk_select",
    )(aff_r)


SLOT_ALIGN = 16


def _window_constants(win):
    wide = N_EXPERTS * win
    expand = np.zeros((N_EXPERTS, wide), np.float32)
    for e in range(N_EXPERTS):
        expand[e, e * win:(e + 1) * win] = 1.0
    return expand, (np.arange(wide) % win).astype(np.float32)


def _gather_kernel(cnt_ref, cntv_ref, rank_ref, aff_ref, h_ref, expand_ref, pos_ref, o_ref, g_ref,
                   *, slots, ch, win, nchunk):
    bi, c = pl.program_id(0), pl.program_id(1)
    k0 = c * ch
    tt = ch * LANES
    max_start = slots - win

    @pl.when(c == 0)
    def _():
        o_ref[...] = jnp.zeros_like(o_ref)
        g_ref[...] = jnp.zeros_like(g_ref)

    lane = lax.broadcasted_iota(jnp.int32, cntv_ref.shape, 1)
    lo_col = jnp.sum(jnp.where(lane == k0, cntv_ref[...], 0), axis=-1, keepdims=True)
    w_col = jnp.minimum(lo_col & -SLOT_ALIGN, max_start)
    rank = rank_ref[...]
    aff = aff_ref[...]
    rel = rank - w_col
    in_win = (rank >= 0) & (rel >= 0) & (rel < win)
    relb = jnp.where(in_win, rel, -1).astype(F32).astype(BF16)
    rel_tall = jnp.dot(expand_ref[...], relb, preferred_element_type=F32)
    hit_all = rel_tall == pos_ref[...]
    rows = h_ref[...]
    xw = jnp.dot(hit_all.astype(BF16), rows, preferred_element_type=F32)

    def add_rows(e, start, hit, vals):
        dst = pl.ds(pl.multiple_of(start, SLOT_ALIGN), win)
        o_ref[e, dst, :] = (o_ref[e, dst, :].astype(F32) + vals).astype(BF16)
        g_ref[e, dst, :] += jnp.sum(jnp.where(hit, aff[e:e + 1, :], 0.0), axis=-1, keepdims=True)

    row = lax.broadcasted_iota(jnp.int32, (win, tt), 0)
    for e in range(N_EXPERTS):
        base = (bi * N_EXPERTS + e) * (nchunk + 1) + k0
        w = jnp.minimum(cnt_ref[base] & -SLOT_ALIGN, max_start)
        add_rows(e, w, hit_all[e * win:(e + 1) * win, :], xw[e * win:(e + 1) * win, :])
        hi = cnt_ref[base + ch]

        @pl.when(hi > w + win)
        def _(e=e, w=w, hi=hi):
            rk = rank[e:e + 1, :]

            def extra(i, carry):
                w2 = w + (i + 1) * win
                w2c = jnp.minimum(w2, max_start)
                hit = ((rk - w2c) == row) & (rk >= w2)
                add_rows(e, w2c, hit, jnp.dot(hit.astype(BF16), rows, preferred_element_type=F32))
                return carry

            lax.fori_loop(0, (hi - w - 1) // win, extra, 0)


def _gather(cnt, cntv, rank_rows, aff_rows, h1b, B, lp, slots, ch, win):
    nchunk = lp // LANES
    tt = ch * LANES
    nt = nchunk // ch
    assert slots % SLOT_ALIGN == 0 and win % SLOT_ALIGN == 0 and win <= slots
    expand, pos = _window_constants(win)
    wide = N_EXPERTS * win
    const = lambda bi, c, cnt: (0, 0)
    per_seq = lambda bi, c, cnt: (bi, 0, 0, 0)
    return pl.pallas_call(
        functools.partial(_gather_kernel, slots=slots, ch=ch, win=win, nchunk=nchunk),
        grid_spec=pltpu.PrefetchScalarGridSpec(
            num_scalar_prefetch=1,
            grid=(B, nt),
            in_specs=[
                pl.BlockSpec((None, N_EXPERTS, LANES), lambda bi, c, cnt: (bi, 0, 0)),
                pl.BlockSpec((None, N_EXPERTS, tt), lambda bi, c, cnt: (bi, 0, c)),
                pl.BlockSpec((None, N_EXPERTS, tt), lambda bi, c, cnt: (bi, 0, c)),
                pl.BlockSpec((tt, D_MODEL), lambda bi, c, cnt: (bi * nt + c, 0)),
                pl.BlockSpec((wide, N_EXPERTS), const),
                pl.BlockSpec((wide, 1), const),
            ],
            out_specs=[pl.BlockSpec((None, N_EXPERTS, slots, D_MODEL), per_seq),
                       pl.BlockSpec((None, N_EXPERTS, slots, 1), per_seq)],
        ),
        out_shape=[jax.ShapeDtypeStruct((B, N_EXPERTS, slots, D_MODEL), BF16),
                   jax.ShapeDtypeStruct((B, N_EXPERTS, slots, 1), F32)],
        compiler_params=_params("parallel", "arbitrary"),
        name="moe_gather",
    )(cnt, cntv, rank_rows, aff_rows, h1b, jnp.asarray(expand.T, dtype=BF16), jnp.asarray(pos.reshape(wide, 1)))


def _ffn_up_kernel(x_ref, wg_ref, wu_ref, o_ref, wgb_ref, wub_ref):
    @pl.when(pl.program_id(1) == 0)
    def _():
        wgb_ref[...] = wg_ref[...].astype(BF16)
        wub_ref[...] = wu_ref[...].astype(BF16)

    x = x_ref[...]
    a = jnp.dot(x, wgb_ref[...], preferred_element_type=F32)
    u = jnp.dot(x, wub_ref[...], preferred_element_type=F32)
    o_ref[...] = (a * jax.nn.sigmoid(a) * u).astype(BF16)


def _ffn_down_kernel(x_ref, g_ref, wd_ref, o_ref, wdb_ref):
    @pl.when(pl.program_id(1) == 0)
    def _():
        wdb_ref[...] = wd_ref[...].astype(BF16)

    y = jnp.dot(x_ref[...], wdb_ref[...], preferred_element_type=F32)
    o_ref[...] = (y * g_ref[...]).astype(BF16)


def _ffn(xg, gates, wg, wu, wd, layer):
    B, _, slots, _ = xg.shape
    act = lambda width: pl.BlockSpec((None, None, slots, width), lambda e, b: (b, e, 0, 0))
    weight = lambda rows, cols: pl.BlockSpec((None, None, rows, cols), lambda e, b: (layer, e, 0, 0))
    mid = pl.pallas_call(
        _ffn_up_kernel,
        grid=(N_EXPERTS, B),
        in_specs=[act(D_MODEL), weight(D_MODEL, D_FF_EXPERT), weight(D_MODEL, D_FF_EXPERT)],
        out_specs=act(D_FF_EXPERT),
        out_shape=jax.ShapeDtypeStruct((B, N_EXPERTS, slots, D_FF_EXPERT), BF16),
        scratch_shapes=[pltpu.VMEM((D_MODEL, D_FF_EXPERT), BF16), pltpu.VMEM((D_MODEL, D_FF_EXPERT), BF16)],
        compiler_params=_params("arbitrary", "arbitrary"),
        name="moe_ffn_up",
    )(xg, wg, wu)
    return pl.pallas_call(
        _ffn_down_kernel,
        grid=(N_EXPERTS, B),
        in_specs=[act(D_FF_EXPERT), act(1), weight(D_FF_EXPERT, D_MODEL)],
        out_specs=act(D_MODEL),
        out_shape=jax.ShapeDtypeStruct(xg.shape, BF16),
        scratch_shapes=[pltpu.VMEM((D_FF_EXPERT, D_MODEL), BF16)],
        compiler_params=_params("arbitrary", "arbitrary"),
        name="moe_ffn_down",
    )(mid, gates, wd)


def _combine_kernel(cnt_ref, cntc_ref, rank_ref, y_ref, h_ref, g_ref, b_ref, expand_ref, rpat_ref,
                    o_ref, acc_ref, *, slots, ch, win, nchunk, alpha):
    bi, c = pl.program_id(0), pl.program_id(1)
    k0 = c * ch
    tt = ch * LANES
    max_start = slots - win
    rank = rank_ref[...]
    lo_row = cntc_ref[pl.ds(k0, 1), :]
    w_row = jnp.minimum(lo_row & -SLOT_ALIGN, max_start)
    rel = rank - w_row
    in_win = (rank >= 0) & (rel >= 0) & (rel < win)
    relb = jnp.where(in_win, rel, -1).astype(F32).astype(BF16)
    rel_wide = jnp.dot(relb, expand_ref[...], preferred_element_type=F32)
    onehot = (rel_wide == rpat_ref[...]).astype(BF16)

    starts, windows = [], []
    for e in range(N_EXPERTS):
        lo = cnt_ref[(bi * N_EXPERTS + e) * (nchunk + 1) + k0]
        w = jnp.minimum(lo & -SLOT_ALIGN, max_start)
        starts.append(w)
        windows.append(y_ref[e, pl.ds(pl.multiple_of(w, SLOT_ALIGN), win), :])
    acc_ref[...] = jnp.dot(onehot, jnp.concatenate(windows, axis=0), preferred_element_type=F32)

    lane = lax.broadcasted_iota(jnp.int32, rank.shape, 1)
    col = lax.broadcasted_iota(jnp.int32, (tt, win), 1)
    for e in range(N_EXPERTS):
        hi = cnt_ref[(bi * N_EXPERTS + e) * (nchunk + 1) + k0 + ch]

        @pl.when(hi > starts[e] + win)
        def _(e=e, hi=hi):
            rk = jnp.sum(jnp.where(lane == e, rank, 0), axis=-1, keepdims=True)

            def extra(i, carry):
                w2 = starts[e] + (i + 1) * win
                w2c = jnp.minimum(w2, max_start)
                rows = y_ref[e, pl.ds(pl.multiple_of(w2c, SLOT_ALIGN), win), :]
                hit = ((rk - w2c) == col) & (rk >= w2)
                acc_ref[...] += jnp.dot(hit.astype(BF16), rows, preferred_element_type=F32)
                return carry

            lax.fori_loop(0, (hi - starts[e] - 1) // win, extra, 0)

    o_ref[...] = _ln(alpha * h_ref[...] + acc_ref[...], g_ref[...], b_ref[...])


def _combine(cnt, cntc, rank_c, y, h1, g, b, B, lp, alpha, ch, win):
    n = B * lp
    slots = y.shape[2]
    nchunk = lp // LANES
    tt = ch * LANES
    nt = nchunk // ch
    assert slots % SLOT_ALIGN == 0 and win % SLOT_ALIGN == 0 and win <= slots
    wide = N_EXPERTS * win
    expand, pos = _window_constants(win)
    tok = lambda bi, c, cnt: (bi * nt + c, 0)
    const = lambda bi, c, cnt: (0, 0)
    return pl.pallas_call(
        functools.partial(_combine_kernel, slots=slots, ch=ch, win=win, nchunk=nchunk, alpha=alpha),
        grid_spec=pltpu.PrefetchScalarGridSpec(
            num_scalar_prefetch=1,
            grid=(B, nt),
            in_specs=[
                pl.BlockSpec((None, cntc.shape[1], N_EXPERTS), lambda bi, c, cnt: (bi, 0, 0)),
                pl.BlockSpec((tt, N_EXPERTS), tok),
                pl.BlockSpec((None, N_EXPERTS, slots, D_MODEL), lambda bi, c, cnt: (bi, 0, 0, 0)),
                pl.BlockSpec((tt, D_MODEL), tok),
                pl.BlockSpec((1, D_MODEL), const),
                pl.BlockSpec((1, D_MODEL), const),
                pl.BlockSpec((N_EXPERTS, wide), const),
                pl.BlockSpec((1, wide), const),
            ],
            out_specs=pl.BlockSpec((tt, D_MODEL), tok),
            scratch_shapes=[pltpu.VMEM((tt, D_MODEL), F32)],
        ),
        out_shape=jax.ShapeDtypeStruct((n, D_MODEL), F32),
        compiler_params=_params("parallel", "arbitrary"),
        name="moe_combine_ln",
    )(cnt, cntc, rank_c, y, h1, g.reshape(1, -1), b.reshape(1, -1),
      jnp.asarray(expand, dtype=BF16), jnp.asarray(pos.reshape(1, wide)))


def kernel(x, meta, ln0_g, ln0_b, w_in, b_gate, sink, w_attn_o, w_four_o, w_out, ln1_g, ln1_b,
           w_router, w_e_gate, w_e_up, w_e_down, ln2_g, ln2_b):
    B, seq, d = x.shape
    depth = w_in.shape[0]
    assert d == D_MODEL and meta.shape == (N_META, D_MODEL)
    assert seq % BLOCK == 0 and N_META % SUBLANES == 0 and N_META <= BLOCK
    L = seq + N_META
    nb = -(-L // BLOCK)
    lp = nb * BLOCK
    lead = lp - L
    cap = CAPACITY_FACTOR * L // N_EXPERTS
    slots = -(-cap // SLOT_ALIGN) * SLOT_ALIGN
    alpha = float((2 * depth) ** 0.25)
    n = B * lp
    nchunk = lp // LANES
    tm = _pick_tile(n, (512, 256, 128))
    tf = _pick_tile(lp, (384, 640, 128))
    ch = _pick_tile(nchunk, (3, 2, 1))
    win = min(slots, -(-(2 * ch * LANES * cap // L + SLOT_ALIGN) // SLOT_ALIGN) * SLOT_ALIGN)

    gi = np.arange(FOURIER_GROUP)
    ang = 2.0 * np.pi * ((gi[:, None] * gi[None, :]) % FOURIER_GROUP) / FOURIER_GROUP
    cs = jnp.asarray(np.concatenate([np.cos(ang), np.sin(ang)], axis=1) * FOURIER_GROUP ** -0.5, dtype=BF16)
    wdft = _dft_matrix(lp, lead, L)
    bias, variant_of_block = _attn_bias_tables(nb, lead)

    h = _embed(x, meta, ln0_g, ln0_b, nb, lead).reshape(n, D_MODEL)
    for l in range(depth):
        q, kv, pq, gates = _inproj(h, w_in[l].astype(BF16), b_gate[l], cs, tm)
        attn = _attention(q.reshape(B, lp, -1), kv.reshape(B, lp, -1), sink[l], bias, variant_of_block, nb)
        four = _fourier(wdft, pq.reshape(B, lp, -1), tf)
        h1, h1b, aff_r, aff_c = _outproj(
            attn.reshape(n, -1), four.reshape(n, -1), gates, h,
            w_attn_o[l].astype(BF16), w_four_o[l].astype(BF16), w_out[l].astype(BF16),
            ln1_g[l], ln1_b[l], w_router[l], alpha, tm)
        rank_r, rank_c, cntv, cntc = _topk(aff_r, B, lp, lead, cap)
        cnt = cntv[:, :, :nchunk + 1].reshape(-1)
        rank_rows = rank_r.transpose(0, 2, 1, 3).reshape(B, N_EXPERTS, lp)
        aff_rows = aff_r.reshape(N_EXPERTS, B, lp).transpose(1, 0, 2)
        xg, gsel = _gather(cnt, cntv, rank_rows, aff_rows, h1b, B, lp, slots, ch, win)
        y = _ffn(xg, gsel, w_e_gate, w_e_up, w_e_down, l)
        h = _combine(cnt, cntc, rank_c, y, h1, ln2_g[l], ln2_b[l], B, lp, alpha, ch, win)
    return h.reshape(B, lp, D_MODEL)[:, lead + N_META:]
```

```python
import functools
import math

import numpy as np
import jax
import jax.numpy as jnp
from jax import lax
from jax.experimental import pallas as pl
from jax.experimental.pallas import tpu as pltpu

D_MODEL = 1024
N_META = 16
N_HEADS = 8
N_KV_HEADS = 2
HEAD_DIM = 64
GQA_GROUP = N_HEADS // N_KV_HEADS
ATTN_WIDTH = N_HEADS * HEAD_DIM
KV_WIDTH = N_KV_HEADS * HEAD_DIM
WINDOW = 128
BLOCK = 128
N_FOURIER_GROUPS = 4
FOURIER_GROUP = 128
FOURIER_WIDTH = N_FOURIER_GROUPS * FOURIER_GROUP
N_BRANCHES = 2
GATE_WIDTH = N_BRANCHES * D_MODEL
N_EXPERTS = 16
CAPACITY_FACTOR = 2
D_FF_EXPERT = 1536
LN_EPS = 1e-5
NEG_INF = -1e30
Q_END = ATTN_WIDTH
K_END = Q_END + KV_WIDTH
V_END = K_END + KV_WIDTH
F_END = V_END + FOURIER_WIDTH
IN_WIDTH = F_END + GATE_WIDTH

LANES = 128
SUBLANES = 8
VMEM_LIMIT_BYTES = 56 * 1024 * 1024

F32 = jnp.float32
BF16 = jnp.bfloat16
NT_DIMS = (((1,), (1,)), ((), ()))


def _pick_tile(n, candidates):
    for c in candidates:
        if n % c == 0:
            return c
    raise ValueError(f"no tile in {candidates} divides {n}")


def _params(*sem, flags=None):
    return pltpu.CompilerParams(dimension_semantics=sem, vmem_limit_bytes=VMEM_LIMIT_BYTES, flags=flags)


def _ln(x, g, b):
    mu = jnp.mean(x, axis=-1, keepdims=True)
    xc = x - mu
    var = jnp.mean(xc * xc, axis=-1, keepdims=True)
    return xc * lax.rsqrt(var + LN_EPS) * g + b


def _embed_kernel(x_ref, meta_ref, g_ref, b_ref, o_ref, *, lead):
    j = pl.program_id(1)

    @pl.when(j == 0)
    def _():
        o_ref[0:lead, :] = jnp.zeros((lead, D_MODEL), F32)
        o_ref[lead:BLOCK, :] = _ln(meta_ref[...], g_ref[...], b_ref[...])

    @pl.when(j > 0)
    def _():
        o_ref[...] = _ln(x_ref[...], g_ref[...], b_ref[...])


def _embed(x, meta, g, b, nb, lead):
    B = x.shape[0]
    return pl.pallas_call(
        functools.partial(_embed_kernel, lead=lead),
        grid=(B, nb),
        in_specs=[
            pl.BlockSpec((None, BLOCK, D_MODEL), lambda bi, j: (bi, jnp.maximum(j - 1, 0), 0)),
            pl.BlockSpec((N_META, D_MODEL), lambda bi, j: (0, 0)),
            pl.BlockSpec((1, D_MODEL), lambda bi, j: (0, 0)),
            pl.BlockSpec((1, D_MODEL), lambda bi, j: (0, 0)),
        ],
        out_specs=pl.BlockSpec((None, BLOCK, D_MODEL), lambda bi, j: (bi, j, 0)),
        out_shape=jax.ShapeDtypeStruct((B, nb * BLOCK, D_MODEL), F32),
        compiler_params=_params("parallel", "arbitrary"),
        name="embed_ln",
    )(x, meta, g.reshape(1, -1), b.reshape(1, -1))


def _inproj_kernel(h_ref, w_ref, bg_ref, cs_ref, q_ref, kv_ref, pq_ref, gate_ref):
    hb = h_ref[...].astype(BF16)
    q_ref[...] = jnp.dot(hb, w_ref[:, 0:Q_END], preferred_element_type=F32).astype(BF16)
    kv_ref[...] = jnp.dot(hb, w_ref[:, Q_END:V_END], preferred_element_type=F32).astype(BF16)
    uf = jnp.dot(hb, w_ref[:, V_END:F_END], preferred_element_type=F32).astype(BF16)
    for g in range(N_FOURIER_GROUPS):
        lo = g * FOURIER_GROUP
        pq = jnp.dot(uf[:, lo:lo + FOURIER_GROUP], cs_ref[...], preferred_element_type=F32)
        pq_ref[:, lo:lo + FOURIER_GROUP] = pq[:, 0:FOURIER_GROUP].astype(BF16)
        pq_ref[:, FOURIER_WIDTH + lo:FOURIER_WIDTH + lo + FOURIER_GROUP] = (
            pq[:, FOURIER_GROUP:2 * FOURIER_GROUP].astype(BF16))
    chunk = 512
    for c in range(GATE_WIDTH // chunk):
        lo = c * chunk
        ug = jnp.dot(hb, w_ref[:, F_END + lo:F_END + lo + chunk], preferred_element_type=F32)
        gate_ref[:, lo:lo + chunk] = jax.nn.sigmoid(ug + bg_ref[:, lo:lo + chunk])


def _inproj(h, w_bf16, b_gate, cs, tm):
    n = h.shape[0]
    row = lambda i: (i, 0)
    const = lambda i: (0, 0)
    return pl.pallas_call(
        _inproj_kernel,
        grid=(n // tm,),
        in_specs=[
            pl.BlockSpec((tm, D_MODEL), row),
            pl.BlockSpec((D_MODEL, IN_WIDTH), const),
            pl.BlockSpec((1, GATE_WIDTH), const),
            pl.BlockSpec((FOURIER_GROUP, 2 * FOURIER_GROUP), const),
        ],
        out_specs=[
            pl.BlockSpec((tm, ATTN_WIDTH), row),
            pl.BlockSpec((tm, 2 * KV_WIDTH), row),
            pl.BlockSpec((tm, 2 * FOURIER_WIDTH), row),
            pl.BlockSpec((tm, GATE_WIDTH), row),
        ],
        out_shape=[
            jax.ShapeDtypeStruct((n, ATTN_WIDTH), BF16),
            jax.ShapeDtypeStruct((n, 2 * KV_WIDTH), BF16),
            jax.ShapeDtypeStruct((n, 2 * FOURIER_WIDTH), BF16),
            jax.ShapeDtypeStruct((n, GATE_WIDTH), F32),
        ],
        compiler_params=_params("parallel"),
        name="inproj",
    )(h, w_bf16, b_gate.reshape(1, GATE_WIDTH), cs)


def _attn_kernel(sink_ref, q_ref, kvp_ref, kvc_ref, kvn_ref, bias_ref, o_ref):
    kv = jnp.concatenate([kvp_ref[...], kvc_ref[...], kvn_ref[...]], axis=0)
    low_half = lax.broadcasted_iota(jnp.int32, (3 * BLOCK, LANES), 1) < HEAD_DIM

    def lane_half_operands(x):
        swapped = jnp.concatenate([x[:, HEAD_DIM:], x[:, :HEAD_DIM]], axis=1)
        zero = jnp.zeros_like(x)
        return {(kvh, half): jnp.where(low_half if half == 0 else ~low_half,
                                       x if kvh == half else swapped, zero)
                for kvh in range(N_KV_HEADS) for half in range(2)}

    k_ops = lane_half_operands(kv[:, 0:KV_WIDTH])
    v_ops = lane_half_operands(kv[:, KV_WIDTH:2 * KV_WIDTH])

    def scores(h):
        pair = h // 2
        qp = q_ref[:, pair * LANES:(pair + 1) * LANES]
        return lax.dot_general(qp, k_ops[(h // GQA_GROUP, h % 2)], NT_DIMS, preferred_element_type=F32)

    def head_out(h, s):
        sink = sink_ref[h]
        logits = s * (HEAD_DIM ** -0.5) + bias_ref[h]
        m = jnp.maximum(jnp.max(logits, axis=-1, keepdims=True), sink)
        p = jnp.exp(logits - m)
        denom = jnp.sum(p, axis=-1, keepdims=True) + jnp.exp(sink - m)
        o = jnp.dot(p.astype(BF16), v_ops[(h // GQA_GROUP, h % 2)], preferred_element_type=F32)
        return o / denom

    ahead = 5
    pending = [scores(h) for h in range(ahead)]
    for h in range(N_HEADS):
        s_cur = pending.pop(0)
        if h + ahead < N_HEADS:
            pending.append(scores(h + ahead))
        o = head_out(h, s_cur)
        if h % 2 == 0:
            o_even = o
        else:
            pair = h // 2
            o_ref[:, pair * LANES:(pair + 1) * LANES] = (o_even + o).astype(BF16)


def _attn_bias_tables(nb, lead):
    qi = np.arange(BLOCK)[:, None]
    si = np.arange(3 * BLOCK)[None, :]
    rel = np.abs(si - BLOCK - qi).astype(np.float32)
    slopes = np.array([2.0 ** (-8.0 * (h + 1) / N_HEADS) for h in range(N_HEADS)], np.float32)
    base = np.where(rel[None] <= WINDOW, -slopes[:, None, None] * rel[None], np.float32(NEG_INF))
    variants, keys, variant_of_block = [], [], []
    for i in range(nb):
        kpos = (i - 1) * BLOCK + np.arange(3 * BLOCK)
        valid = (kpos >= lead) & (kpos < nb * BLOCK)
        key = valid.tobytes()
        if key not in keys:
            keys.append(key)
            variants.append(np.where(valid[None, None, :], base, np.float32(NEG_INF)))
        variant_of_block.append(keys.index(key))
    return jnp.asarray(np.stack(variants).astype(np.float32)), variant_of_block


def _attention(q, kv, sink, bias, variant_of_block, nb):
    B, lp, _ = q.shape
    kvspec = lambda f: pl.BlockSpec((None, BLOCK, 2 * KV_WIDTH), f)
    interior = max(set(variant_of_block), key=variant_of_block.count)

    def variant(i):
        v = jnp.int32(interior)
        for blk, var in enumerate(variant_of_block):
            if var != interior:
                v = jnp.where(i == blk, var, v)
        return v

    return pl.pallas_call(
        _attn_kernel,
        grid=(B, nb),
        in_specs=[
            pl.BlockSpec(memory_space=pltpu.SMEM),
            pl.BlockSpec((None, BLOCK, ATTN_WIDTH), lambda b, i: (b, i, 0)),
            kvspec(lambda b, i: (b, jnp.maximum(i - 1, 0), 0)),
            kvspec(lambda b, i: (b, i, 0)),
            kvspec(lambda b, i: (b, jnp.minimum(i + 1, nb - 1), 0)),
            pl.BlockSpec((None, N_HEADS, BLOCK, 3 * BLOCK), lambda b, i: (variant(i), 0, 0, 0)),
        ],
        out_specs=pl.BlockSpec((None, BLOCK, ATTN_WIDTH), lambda b, i: (b, i, 0)),
        out_shape=jax.ShapeDtypeStruct((B, lp, ATTN_WIDTH), BF16),
        compiler_params=_params("parallel", "arbitrary"),
        name="attention",
    )(sink, q, kv, kv, kv, bias)


def _fourier_kernel(w_ref, p_ref, q_ref, o_ref, prev_ref, *, lp, lead, seq_len, tf):
    m = pl.program_id(1)
    last = pl.num_programs(1) - 1
    nblk = lp // tf
    shift = lead + 1

    @pl.when(m == 0)
    def _():
        row = lax.broadcasted_iota(jnp.int32, (lp, 1), 0)
        dc = jnp.sum(jnp.where(row >= lead, p_ref[...].astype(F32), 0.0), axis=0, keepdims=True)
        prev_ref[...] = jnp.zeros_like(prev_ref)
        prev_ref[tf - 1:tf, :] = dc * np.float32(seq_len ** -0.5)

    t1 = jnp.dot(w_ref[:, 0:lp], p_ref[...], preferred_element_type=F32)
    t2 = jnp.dot(w_ref[:, lp:2 * lp], q_ref[...], preferred_element_type=F32)
    direct = t1 + t2
    r = lax.broadcasted_iota(jnp.int32, (tf, tf), 0)
    c = lax.broadcasted_iota(jnp.int32, (tf, tf), 1)
    flip = (r + c == tf - 1).astype(BF16)
    mirrored = jnp.dot(flip, (t1 - t2).astype(BF16), preferred_element_type=F32)
    shifted = jnp.concatenate([prev_ref[tf - shift:, :], direct[:tf - shift, :]], axis=0)
    prev_ref[...] = direct

    @pl.when(m < last)
    def _():
        o_ref[pl.ds(pl.multiple_of(m * tf, tf), tf), :] = shifted.astype(BF16)
        o_ref[pl.ds(pl.multiple_of((nblk - 1 - m) * tf, tf), tf), :] = mirrored.astype(BF16)

    @pl.when(m == last)
    def _():
        row = lax.broadcasted_iota(jnp.int32, (tf, 1), 0) + (nblk // 2) * tf
        mid = jnp.where(row <= seq_len // 2 + lead, shifted, mirrored)
        o_ref[(nblk // 2) * tf:(nblk // 2 + 1) * tf, :] = mid.astype(BF16)


def _dft_matrix(nrows, lp, lead, seq_len):
    w = np.float32(2.0 * math.pi / seq_len)
    k = (jnp.arange(nrows, dtype=jnp.int32) + 1)[:, None]
    j = jnp.arange(lp // LANES, dtype=jnp.int32)[None, :]
    r = jnp.arange(LANES, dtype=jnp.int32)[None, :]
    a = ((k * (LANES * j - lead)) % seq_len).astype(F32) * w
    b = ((k * r) % seq_len).astype(F32) * w
    ca, sa, cb, sb = jnp.cos(a)[:, :, None], jnp.sin(a)[:, :, None], jnp.cos(b)[:, None, :], jnp.sin(b)[:, None, :]
    valid = (jnp.arange(lp, dtype=jnp.int32) >= lead).reshape(1, lp // LANES, LANES)
    scale = np.float32(seq_len ** -0.5)
    wc = jnp.where(valid, (ca * cb - sa * sb) * scale, 0.0).reshape(nrows, lp)
    ws = jnp.where(valid, -(sa * cb + ca * sb) * scale, 0.0).reshape(nrows, lp)
    return jnp.concatenate([wc, ws], axis=1).astype(BF16)


def _fourier_steps(lp, lead, seq_len, tf):
    nblk = lp // tf
    assert seq_len % 2 == 0 and lp % tf == 0 and nblk % 2 == 1 and lead + 1 < tf
    steps = nblk // 2 + 1
    assert steps * tf >= seq_len // 2
    return steps


def _fourier(w, pq, lead, seq_len, tf):
    B, lp, _ = pq.shape
    steps = _fourier_steps(lp, lead, seq_len, tf)
    return pl.pallas_call(
        functools.partial(_fourier_kernel, lp=lp, lead=lead, seq_len=seq_len, tf=tf),
        grid=(B, steps),
        in_specs=[
            pl.BlockSpec((tf, 2 * lp), lambda b, m: (m, 0)),
            pl.BlockSpec((None, lp, FOURIER_WIDTH), lambda b, m: (b, 0, 0)),
            pl.BlockSpec((None, lp, FOURIER_WIDTH), lambda b, m: (b, 0, 1)),
        ],
        out_specs=pl.BlockSpec((None, lp, FOURIER_WIDTH), lambda b, m: (b, 0, 0)),
        out_shape=jax.ShapeDtypeStruct((B, lp, FOURIER_WIDTH), BF16),
        scratch_shapes=[pltpu.VMEM((tf, FOURIER_WIDTH), F32)],
        compiler_params=_params("parallel", "arbitrary"),
        name="fourier",
    )(w, pq, pq)


def _split_bf16(x):
    hi = x.astype(BF16)
    lo = (x - hi.astype(F32)).astype(BF16)
    return hi, lo


def _outproj_kernel(attn_ref, four_ref, gate_ref, h_ref, wa_ref, wf_ref, wo_ref, g_ref, b_ref,
                    wrh_ref, wrl_ref, wrth_ref, wrtl_ref,
                    h1_ref, h1b_ref, affr_ref, affc_ref, *, alpha):
    ya = jnp.dot(attn_ref[...], wa_ref[...], preferred_element_type=F32)
    yf = jnp.dot(four_ref[...], wf_ref[...], preferred_element_type=F32)
    merged = gate_ref[:, 0:D_MODEL] * ya + gate_ref[:, D_MODEL:GATE_WIDTH] * yf
    mix = jnp.dot(merged.astype(BF16), wo_ref[...], preferred_element_type=F32)
    h1 = _ln(alpha * h_ref[...] + mix, g_ref[...], b_ref[...])
    h1_ref[...] = h1
    hi, lo = _split_bf16(h1)
    h1b_ref[...] = hi
    lc = (jnp.dot(hi, wrh_ref[...], preferred_element_type=F32)
          + jnp.dot(hi, wrl_ref[...], preferred_element_type=F32)
          + jnp.dot(lo, wrh_ref[...], preferred_element_type=F32))
    lane = lax.broadcasted_iota(jnp.int32, lc.shape, 1)
    lc = jnp.where(lane < N_EXPERTS, lc, NEG_INF)
    ec = jnp.exp(lc - jnp.max(lc, axis=-1, keepdims=True))
    affc_ref[...] = (ec / jnp.sum(ec, axis=-1, keepdims=True))[:, 0:N_EXPERTS]
    lr = (lax.dot_general(wrth_ref[...], hi, NT_DIMS, preferred_element_type=F32)
          + lax.dot_general(wrtl_ref[...], hi, NT_DIMS, preferred_element_type=F32)
          + lax.dot_general(wrth_ref[...], lo, NT_DIMS, preferred_element_type=F32))
    er = jnp.exp(lr - jnp.max(lr, axis=0, keepdims=True))
    affr_ref[...] = er / jnp.sum(er, axis=0, keepdims=True)


def _outproj(attn, four, gates, h, wa, wf, wo, g, b, w_router, alpha, tm):
    n = h.shape[0]
    row = lambda i: (i, 0)
    const = lambda i: (0, 0)
    wr_pad = jnp.pad(w_router, ((0, 0), (0, LANES - N_EXPERTS)))
    wrh, wrl = _split_bf16(wr_pad)
    wrth, wrtl = _split_bf16(w_router.T)
    return pl.pallas_call(
        functools.partial(_outproj_kernel, alpha=alpha),
        grid=(n // tm,),
        in_specs=[
            pl.BlockSpec((tm, ATTN_WIDTH), row),
            pl.BlockSpec((tm, FOURIER_WIDTH), row),
            pl.BlockSpec((tm, GATE_WIDTH), row),
            pl.BlockSpec((tm, D_MODEL), row),
            pl.BlockSpec((ATTN_WIDTH, D_MODEL), const),
            pl.BlockSpec((FOURIER_WIDTH, D_MODEL), const),
            pl.BlockSpec((D_MODEL, D_MODEL), const),
            pl.BlockSpec((1, D_MODEL), const),
            pl.BlockSpec((1, D_MODEL), const),
            pl.BlockSpec((D_MODEL, LANES), const),
            pl.BlockSpec((D_MODEL, LANES), const),
            pl.BlockSpec((N_EXPERTS, D_MODEL), const),
            pl.BlockSpec((N_EXPERTS, D_MODEL), const),
        ],
        out_specs=[
            pl.BlockSpec((tm, D_MODEL), row),
            pl.BlockSpec((tm, D_MODEL), row),
            pl.BlockSpec((N_EXPERTS, tm), lambda i: (0, i)),
            pl.BlockSpec((tm, N_EXPERTS), row),
        ],
        out_shape=[
            jax.ShapeDtypeStruct((n, D_MODEL), F32),
            jax.ShapeDtypeStruct((n, D_MODEL), BF16),
            jax.ShapeDtypeStruct((N_EXPERTS, n), F32),
            jax.ShapeDtypeStruct((n, N_EXPERTS), F32),
        ],
        compiler_params=_params("parallel"),
        name="outproj_ln_router",
    )(attn, four, gates, h, wa, wf, wo, g.reshape(1, -1), b.reshape(1, -1), wrh, wrl, wrth, wrtl)


def _topk_kernel(aff_ref, rr_ref, rc_ref, cnt_ref, cntc_ref, *, lead, lp, cap):
    nchunk = lp // LANES
    lane = lax.broadcasted_iota(jnp.int32, (N_EXPERTS, lp), 1)
    bits = jnp.where(lane >= lead, pltpu.bitcast(aff_ref[...], jnp.int32), -1)

    def search(i, t):
        cand = t | (jnp.int32(1) << (30 - i))
        cnt = jnp.sum((bits >= cand).astype(jnp.int32), axis=-1, keepdims=True)
        return jnp.where(cnt >= cap, cand, t)

    thr = lax.fori_loop(0, 31, search, jnp.zeros((N_EXPERTS, 1), jnp.int32))
    gt = bits > thr
    eq = bits == thr
    need = cap - jnp.sum(gt.astype(jnp.int32), axis=-1, keepdims=True)

    r = lax.broadcasted_iota(jnp.int32, (LANES, LANES), 0)
    c = lax.broadcasted_iota(jnp.int32, (LANES, LANES), 1)
    upper = (r <= c).astype(BF16)
    lower = (c <= r).astype(BF16)
    ident = (c == r).astype(BF16)

    eqb = eq.astype(BF16)
    off = jnp.zeros((N_EXPERTS, 1), F32)
    needf = need.astype(F32)
    sel_chunks = []
    for k in range(nchunk):
        sl = slice(k * LANES, (k + 1) * LANES)
        pre = jnp.dot(eqb[:, sl], upper, preferred_element_type=F32) + off
        off = pre[:, LANES - 1:LANES]
        sel_chunks.append(gt[:, sl] | (eq[:, sl] & (pre <= needf)))

    off_r = jnp.zeros((N_EXPERTS, 1), F32)
    off_c = jnp.zeros((1, N_EXPERTS), F32)
    cnt_lane = lax.broadcasted_iota(jnp.int32, (N_EXPERTS, LANES), 1)
    cnt = jnp.zeros((N_EXPERTS, LANES), F32)
    for k in range(nchunk):
        sl = slice(k * LANES, (k + 1) * LANES)
        sel = sel_chunks[k]
        selb = sel.astype(BF16)
        pre_r = jnp.dot(selb, upper, preferred_element_type=F32) + off_r
        rr_ref[k] = jnp.where(sel, pre_r - 1.0, -1.0).astype(jnp.int32)
        off_r = pre_r[:, LANES - 1:LANES]
        cnt = jnp.where(cnt_lane == k + 1, off_r, cnt)
        pre_c = lax.dot_general(lower, selb, NT_DIMS, preferred_element_type=F32) + off_c
        sel_c = lax.dot_general(ident, selb, NT_DIMS, preferred_element_type=F32)
        rc_ref[sl, :] = jnp.where(sel_c > 0.5, pre_c - 1.0, -1.0).astype(jnp.int32)
        off_c = pre_c[LANES - 1:LANES, :]
        cntc_ref[k + 1:k + 2, :] = off_c.astype(jnp.int32)
    cnt_ref[...] = cnt.astype(jnp.int32)
    cntc_ref[0:1, :] = jnp.zeros((1, N_EXPERTS), jnp.int32)
    pad_rows = cntc_ref.shape[0] - nchunk - 1
    if pad_rows:
        cntc_ref[nchunk + 1:, :] = jnp.zeros((pad_rows, N_EXPERTS), jnp.int32)


def _topk(aff_r, B, lp, lead, cap):
    n = B * lp
    nchunk = lp // LANES
    assert nchunk + 1 <= LANES
    nb1 = -(-(nchunk + 1) // SUBLANES) * SUBLANES
    return pl.pallas_call(
        functools.partial(_topk_kernel, lead=lead, lp=lp, cap=cap),
        grid=(B,),
        in_specs=[pl.BlockSpec((N_EXPERTS, lp), lambda b: (0, b))],
        out_specs=[
            pl.BlockSpec((None, nchunk, N_EXPERTS, LANES), lambda b: (b, 0, 0, 0)),
            pl.BlockSpec((lp, N_EXPERTS), lambda b: (b, 0)),
            pl.BlockSpec((None, N_EXPERTS, LANES), lambda b: (b, 0, 0)),
            pl.BlockSpec((None, nb1, N_EXPERTS), lambda b: (b, 0, 0)),
        ],
        out_shape=[
            jax.ShapeDtypeStruct((B, nchunk, N_EXPERTS, LANES), jnp.int32),
            jax.ShapeDtypeStruct((n, N_EXPERTS), jnp.int32),
            jax.ShapeDtypeStruct((B, N_EXPERTS, LANES), jnp.int32),
            jax.ShapeDtypeStruct((B, nb1, N_EXPERTS), jnp.int32),
        ],
        compiler_params=_params("parallel"),
        name="topk_select",
    )(aff_r)


SLOT_ALIGN = 16


def _window_constants(win):
    wide = N_EXPERTS * win
    expand = np.zeros((N_EXPERTS, wide), np.float32)
    for e in range(N_EXPERTS):
        expand[e, e * win:(e + 1) * win] = 1.0
    return expand, (np.arange(wide) % win).astype(np.float32)


def _gather_kernel(cnt_ref, cntv_ref, rank_ref, aff_ref, h_ref, expand_ref, pos_ref, o_ref, g_ref,
                   *, slots, ch, win, nchunk):
    bi, c = pl.program_id(0), pl.program_id(1)
    k0 = c * ch
    tt = ch * LANES
    max_start = slots - win

    @pl.when(c == 0)
    def _():
        o_ref[...] = jnp.zeros_like(o_ref)
        g_ref[...] = jnp.zeros_like(g_ref)

    lane = lax.broadcasted_iota(jnp.int32, cntv_ref.shape, 1)
    lo_col = jnp.sum(jnp.where(lane == k0, cntv_ref[...], 0), axis=-1, keepdims=True)
    w_col = jnp.minimum(lo_col & -SLOT_ALIGN, max_start)
    rank = rank_ref[...]
    aff = aff_ref[...]
    rel = rank - w_col
    in_win = (rank >= 0) & (rel >= 0) & (rel < win)
    relb = jnp.where(in_win, rel, -1).astype(F32).astype(BF16)
    rel_tall = jnp.dot(expand_ref[...], relb, preferred_element_type=F32)
    hit_all = rel_tall == pos_ref[...]
    rows = h_ref[...]
    xw = jnp.dot(hit_all.astype(BF16), rows, preferred_element_type=F32)

    def add_rows(e, start, hit, vals):
        dst = pl.ds(pl.multiple_of(start, SLOT_ALIGN), win)
        o_ref[e, dst, :] = (o_ref[e, dst, :].astype(F32) + vals).astype(BF16)
        g_ref[e, dst, :] += jnp.sum(jnp.where(hit, aff[e:e + 1, :], 0.0), axis=-1, keepdims=True)

    row = lax.broadcasted_iota(jnp.int32, (win, tt), 0)
    for e in range(N_EXPERTS):
        base = (bi * N_EXPERTS + e) * (nchunk + 1) + k0
        w = jnp.minimum(cnt_ref[base] & -SLOT_ALIGN, max_start)
        add_rows(e, w, hit_all[e * win:(e + 1) * win, :], xw[e * win:(e + 1) * win, :])
        hi = cnt_ref[base + ch]

        @pl.when(hi > w + win)
        def _(e=e, w=w, hi=hi):
            rk = rank[e:e + 1, :]

            def extra(i, carry):
                w2 = w + (i + 1) * win
                w2c = jnp.minimum(w2, max_start)
                hit = ((rk - w2c) == row) & (rk >= w2)
                add_rows(e, w2c, hit, jnp.dot(hit.astype(BF16), rows, preferred_element_type=F32))
                return carry

            lax.fori_loop(0, (hi - w - 1) // win, extra, 0)


def _gather(cnt, cntv, rank_rows, aff_rows, h1b, B, lp, slots, ch, win):
    nchunk = lp // LANES
    tt = ch * LANES
    nt = nchunk // ch
    assert slots % SLOT_ALIGN == 0 and win % SLOT_ALIGN == 0 and win <= slots
    expand, pos = _window_constants(win)
    wide = N_EXPERTS * win
    const = lambda bi, c, cnt: (0, 0)
    per_seq = lambda bi, c, cnt: (bi, 0, 0, 0)
    return pl.pallas_call(
        functools.partial(_gather_kernel, slots=slots, ch=ch, win=win, nchunk=nchunk),
        grid_spec=pltpu.PrefetchScalarGridSpec(
            num_scalar_prefetch=1,
            grid=(B, nt),
            in_specs=[
                pl.BlockSpec((None, N_EXPERTS, LANES), lambda bi, c, cnt: (bi, 0, 0)),
                pl.BlockSpec((None, N_EXPERTS, tt), lambda bi, c, cnt: (bi, 0, c)),
                pl.BlockSpec((None, N_EXPERTS, tt), lambda bi, c, cnt: (bi, 0, c)),
                pl.BlockSpec((tt, D_MODEL), lambda bi, c, cnt: (bi * nt + c, 0)),
                pl.BlockSpec((wide, N_EXPERTS), const),
                pl.BlockSpec((wide, 1), const),
            ],
            out_specs=[pl.BlockSpec((None, N_EXPERTS, slots, D_MODEL), per_seq),
                       pl.BlockSpec((None, N_EXPERTS, slots, 1), per_seq)],
        ),
        out_shape=[jax.ShapeDtypeStruct((B, N_EXPERTS, slots, D_MODEL), BF16),
                   jax.ShapeDtypeStruct((B, N_EXPERTS, slots, 1), F32)],
        compiler_params=_params("parallel", "arbitrary"),
        name="moe_gather",
    )(cnt, cntv, rank_rows, aff_rows, h1b, jnp.asarray(expand.T, dtype=BF16), jnp.asarray(pos.reshape(wide, 1)))


def _ffn_up_kernel(x_ref, wg_ref, wu_ref, o_ref, wgb_ref, wub_ref):
    @pl.when(pl.program_id(1) == 0)
    def _():
        wgb_ref[...] = wg_ref[...].astype(BF16)
        wub_ref[...] = wu_ref[...].astype(BF16)

    x = x_ref[...]
    a = jnp.dot(x, wgb_ref[...], preferred_element_type=F32)
    u = jnp.dot(x, wub_ref[...], preferred_element_type=F32)
    o_ref[...] = (a * jax.nn.sigmoid(a) * u).astype(BF16)


def _ffn_down_kernel(x_ref, g_ref, wd_ref, o_ref, wdb_ref):
    @pl.when(pl.program_id(1) == 0)
    def _():
        wdb_ref[...] = wd_ref[...].astype(BF16)

    y = jnp.dot(x_ref[...], wdb_ref[...], preferred_element_type=F32)
    o_ref[...] = (y * g_ref[...]).astype(BF16)


def _ffn(xg, gates, wg, wu, wd, layer):
    B, _, slots, _ = xg.shape
    act = lambda width: pl.BlockSpec((None, None, slots, width), lambda e, b: (b, e, 0, 0))
    weight = lambda rows, cols: pl.BlockSpec((None, None, rows, cols), lambda e, b: (layer, e, 0, 0))
    mid = pl.pallas_call(
        _ffn_up_kernel,
        grid=(N_EXPERTS, B),
        in_specs=[act(D_MODEL), weight(D_MODEL, D_FF_EXPERT), weight(D_MODEL, D_FF_EXPERT)],
        out_specs=act(D_FF_EXPERT),
        out_shape=jax.ShapeDtypeStruct((B, N_EXPERTS, slots, D_FF_EXPERT), BF16),
        scratch_shapes=[pltpu.VMEM((D_MODEL, D_FF_EXPERT), BF16), pltpu.VMEM((D_MODEL, D_FF_EXPERT), BF16)],
        compiler_params=_params("arbitrary", "arbitrary"),
        name="moe_ffn_up",
    )(xg, wg, wu)
    return pl.pallas_call(
        _ffn_down_kernel,
        grid=(N_EXPERTS, B),
        in_specs=[act(D_FF_EXPERT), act(1), weight(D_FF_EXPERT, D_MODEL)],
        out_specs=act(D_MODEL),
        out_shape=jax.ShapeDtypeStruct(xg.shape, BF16),
        scratch_shapes=[pltpu.VMEM((D_FF_EXPERT, D_MODEL), BF16)],
        compiler_params=_params("arbitrary", "arbitrary"),
        name="moe_ffn_down",
    )(mid, gates, wd)


def _combine_kernel(cnt_ref, cntc_ref, rank_ref, y_ref, h_ref, g_ref, b_ref, expand_ref, rpat_ref,
                    o_ref, acc_ref, *, slots, ch, win, nchunk, alpha):
    bi, c = pl.program_id(0), pl.program_id(1)
    k0 = c * ch
    tt = ch * LANES
    max_start = slots - win
    rank = rank_ref[...]
    lo_row = cntc_ref[pl.ds(k0, 1), :]
    w_row = jnp.minimum(lo_row & -SLOT_ALIGN, max_start)
    rel = rank - w_row
    in_win = (rank >= 0) & (rel >= 0) & (rel < win)
    relb = jnp.where(in_win, rel, -1).astype(F32).astype(BF16)
    rel_wide = jnp.dot(relb, expand_ref[...], preferred_element_type=F32)
    onehot = (rel_wide == rpat_ref[...]).astype(BF16)

    starts, windows = [], []
    for e in range(N_EXPERTS):
        lo = cnt_ref[(bi * N_EXPERTS + e) * (nchunk + 1) + k0]
        w = jnp.minimum(lo & -SLOT_ALIGN, max_start)
        starts.append(w)
        windows.append(y_ref[e, pl.ds(pl.multiple_of(w, SLOT_ALIGN), win), :])
    acc_ref[...] = jnp.dot(onehot, jnp.concatenate(windows, axis=0), preferred_element_type=F32)

    lane = lax.broadcasted_iota(jnp.int32, rank.shape, 1)
    col = lax.broadcasted_iota(jnp.int32, (tt, win), 1)
    for e in range(N_EXPERTS):
        hi = cnt_ref[(bi * N_EXPERTS + e) * (nchunk + 1) + k0 + ch]

        @pl.when(hi > starts[e] + win)
        def _(e=e, hi=hi):
            rk = jnp.sum(jnp.where(lane == e, rank, 0), axis=-1, keepdims=True)

            def extra(i, carry):
                w2 = starts[e] + (i + 1) * win
                w2c = jnp.minimum(w2, max_start)
                rows = y_ref[e, pl.ds(pl.multiple_of(w2c, SLOT_ALIGN), win), :]
                hit = ((rk - w2c) == col) & (rk >= w2)
                acc_ref[...] += jnp.dot(hit.astype(BF16), rows, preferred_element_type=F32)
                return carry

            lax.fori_loop(0, (hi - starts[e] - 1) // win, extra, 0)

    o_ref[...] = _ln(alpha * h_ref[...] + acc_ref[...], g_ref[...], b_ref[...])


def _combine(cnt, cntc, rank_c, y, h1, g, b, B, lp, alpha, ch, win):
    n = B * lp
    slots = y.shape[2]
    nchunk = lp // LANES
    tt = ch * LANES
    nt = nchunk // ch
    assert slots % SLOT_ALIGN == 0 and win % SLOT_ALIGN == 0 and win <= slots
    wide = N_EXPERTS * win
    expand, pos = _window_constants(win)
    tok = lambda bi, c, cnt: (bi * nt + c, 0)
    const = lambda bi, c, cnt: (0, 0)
    return pl.pallas_call(
        functools.partial(_combine_kernel, slots=slots, ch=ch, win=win, nchunk=nchunk, alpha=alpha),
        grid_spec=pltpu.PrefetchScalarGridSpec(
            num_scalar_prefetch=1,
            grid=(B, nt),
            in_specs=[
                pl.BlockSpec((None, cntc.shape[1], N_EXPERTS), lambda bi, c, cnt: (bi, 0, 0)),
                pl.BlockSpec((tt, N_EXPERTS), tok),
                pl.BlockSpec((None, N_EXPERTS, slots, D_MODEL), lambda bi, c, cnt: (bi, 0, 0, 0)),
                pl.BlockSpec((tt, D_MODEL), tok),
                pl.BlockSpec((1, D_MODEL), const),
                pl.BlockSpec((1, D_MODEL), const),
                pl.BlockSpec((N_EXPERTS, wide), const),
                pl.BlockSpec((1, wide), const),
            ],
            out_specs=pl.BlockSpec((tt, D_MODEL), tok),
            scratch_shapes=[pltpu.VMEM((tt, D_MODEL), F32)],
        ),
        out_shape=jax.ShapeDtypeStruct((n, D_MODEL), F32),
        compiler_params=_params("parallel", "arbitrary"),
        name="moe_combine_ln",
    )(cnt, cntc, rank_c, y, h1, g.reshape(1, -1), b.reshape(1, -1),
      jnp.asarray(expand, dtype=BF16), jnp.asarray(pos.reshape(1, wide)))


def kernel(x, meta, ln0_g, ln0_b, w_in, b_gate, sink, w_attn_o, w_four_o, w_out, ln1_g, ln1_b,
           w_router, w_e_gate, w_e_up, w_e_down, ln2_g, ln2_b):
    B, seq, d = x.shape
    depth = w_in.shape[0]
    assert d == D_MODEL and meta.shape == (N_META, D_MODEL)
    assert seq % BLOCK == 0 and N_META % SUBLANES == 0 and N_META <= BLOCK
    L = seq + N_META
    nb = -(-L // BLOCK)
    lp = nb * BLOCK
    lead = lp - L
    cap = CAPACITY_FACTOR * L // N_EXPERTS
    slots = -(-cap // SLOT_ALIGN) * SLOT_ALIGN
    alpha = float((2 * depth) ** 0.25)
    n = B * lp
    nchunk = lp // LANES
    tm = _pick_tile(n, (512, 256, 128))
    tf = _pick_tile(lp, (384, 128))
    ch = _pick_tile(nchunk, (3, 2, 1))
    win = min(slots, -(-(2 * ch * LANES * cap // L + SLOT_ALIGN) // SLOT_ALIGN) * SLOT_ALIGN)

    gi = np.arange(FOURIER_GROUP)
    ang = 2.0 * np.pi * ((gi[:, None] * gi[None, :]) % FOURIER_GROUP) / FOURIER_GROUP
    cs = jnp.asarray(np.concatenate([np.cos(ang), np.sin(ang)], axis=1) * FOURIER_GROUP ** -0.5, dtype=BF16)
    wdft = _dft_matrix(_fourier_steps(lp, lead, L, tf) * tf, lp, lead, L)
    bias, variant_of_block = _attn_bias_tables(nb, lead)

    h = _embed(x, meta, ln0_g, ln0_b, nb, lead).reshape(n, D_MODEL)
    for l in range(depth):
        q, kv, pq, gates = _inproj(h, w_in[l].astype(BF16), b_gate[l], cs, tm)
        attn = _attention(q.reshape(B, lp, -1), kv.reshape(B, lp, -1), sink[l], bias, variant_of_block, nb)
        four = _fourier(wdft, pq.reshape(B, lp, -1), lead, L, tf)
        h1, h1b, aff_r, aff_c = _outproj(
            attn.reshape(n, -1), four.reshape(n, -1), gates, h,
            w_attn_o[l].astype(BF16), w_four_o[l].astype(BF16), w_out[l].astype(BF16),
            ln1_g[l], ln1_b[l], w_router[l], alpha, tm)
        rank_r, rank_c, cntv, cntc = _topk(aff_r, B, lp, lead, cap)
        cnt = cntv[:, :, :nchunk + 1].reshape(-1)
        rank_rows = rank_r.transpose(0, 2, 1, 3).reshape(B, N_EXPERTS, lp)
        aff_rows = aff_r.reshape(N_EXPERTS, B, lp).transpose(1, 0, 2)
        xg, gsel = _gather(cnt, cntv, rank_rows, aff_rows, h1b, B, lp, slots, ch, win)
        y = _ffn(xg, gsel, w_e_gate, w_e_up, w_e_down, l)
        h = _combine(cnt, cntc, rank_c, y, h1, ln2_g[l], ln2_b[l], B, lp, alpha, ch, win)
    return h.reshape(B, lp, D_MODEL)[:, lead + N_META:]
```

```python
import functools
import math

import numpy as np
import jax
import jax.numpy as jnp
from jax import lax
from jax.experimental import pallas as pl
from jax.experimental.pallas import tpu as pltpu

D_MODEL = 1024
N_META = 16
N_HEADS = 8
N_KV_HEADS = 2
HEAD_DIM = 64
GQA_GROUP = N_HEADS // N_KV_HEADS
ATTN_WIDTH = N_HEADS * HEAD_DIM
KV_WIDTH = N_KV_HEADS * HEAD_DIM
WINDOW = 128
BLOCK = 128
N_FOURIER_GROUPS = 4
FOURIER_GROUP = 128
FOURIER_WIDTH = N_FOURIER_GROUPS * FOURIER_GROUP
N_BRANCHES = 2
GATE_WIDTH = N_BRANCHES * D_MODEL
N_EXPERTS = 16
CAPACITY_FACTOR = 2
D_FF_EXPERT = 1536
LN_EPS = 1e-5
NEG_INF = -1e30
Q_END = ATTN_WIDTH
K_END = Q_END + KV_WIDTH
V_END = K_END + KV_WIDTH
F_END = V_END + FOURIER_WIDTH
IN_WIDTH = F_END + GATE_WIDTH

LANES = 128
SUBLANES = 8
VMEM_LIMIT_BYTES = 56 * 1024 * 1024

F32 = jnp.float32
BF16 = jnp.bfloat16
NT_DIMS = (((1,), (1,)), ((), ()))


def _pick_tile(n, candidates):
    for c in candidates:
        if n % c == 0:
            return c
    raise ValueError(f"no tile in {candidates} divides {n}")


def _params(*sem, flags=None):
    return pltpu.CompilerParams(dimension_semantics=sem, vmem_limit_bytes=VMEM_LIMIT_BYTES, flags=flags)


def _ln(x, g, b):
    mu = jnp.mean(x, axis=-1, keepdims=True)
    xc = x - mu
    var = jnp.mean(xc * xc, axis=-1, keepdims=True)
    return xc * lax.rsqrt(var + LN_EPS) * g + b


def _embed_kernel(*refs, lead, group):
    x_refs, (meta_ref, g_ref, b_ref, o_ref) = refs[:group], refs[group:]
    for k in range(1, group):
        o_ref[k * BLOCK:(k + 1) * BLOCK, :] = _ln(x_refs[k][...], g_ref[...], b_ref[...])

    @pl.when(pl.program_id(1) == 0)
    def _():
        o_ref[0:lead, :] = jnp.zeros((lead, D_MODEL), F32)
        o_ref[lead:BLOCK, :] = _ln(meta_ref[...], g_ref[...], b_ref[...])

    @pl.when(pl.program_id(1) > 0)
    def _():
        o_ref[0:BLOCK, :] = _ln(x_refs[0][...], g_ref[...], b_ref[...])


def _embed(x, meta, g, b, nb, lead):
    B = x.shape[0]
    group = _pick_tile(nb, (3, 1))
    x_spec = lambda k: pl.BlockSpec((None, BLOCK, D_MODEL),
                                    lambda bi, j: (bi, jnp.maximum(group * j + k - 1, 0), 0))
    const = lambda bi, j: (0, 0)
    return pl.pallas_call(
        functools.partial(_embed_kernel, lead=lead, group=group),
        grid=(B, nb // group),
        in_specs=[x_spec(k) for k in range(group)] + [
            pl.BlockSpec((N_META, D_MODEL), const),
            pl.BlockSpec((1, D_MODEL), const),
            pl.BlockSpec((1, D_MODEL), const),
        ],
        out_specs=pl.BlockSpec((None, group * BLOCK, D_MODEL), lambda bi, j: (bi, j, 0)),
        out_shape=jax.ShapeDtypeStruct((B, nb * BLOCK, D_MODEL), F32),
        compiler_params=_params("parallel", "arbitrary"),
        name="embed_ln",
    )(*([x] * group), meta, g.reshape(1, -1), b.reshape(1, -1))


def _inproj_kernel(h_ref, w_ref, bg_ref, cs_ref, q_ref, kv_ref, pq_ref, gate_ref):
    hb = h_ref[...].astype(BF16)
    q_ref[...] = jnp.dot(hb, w_ref[:, 0:Q_END], preferred_element_type=F32).astype(BF16)
    kv_ref[...] = jnp.dot(hb, w_ref[:, Q_END:V_END], preferred_element_type=F32).astype(BF16)
    uf = jnp.dot(hb, w_ref[:, V_END:F_END], preferred_element_type=F32).astype(BF16)
    for g in range(N_FOURIER_GROUPS):
        lo = g * FOURIER_GROUP
        pq = jnp.dot(uf[:, lo:lo + FOURIER_GROUP], cs_ref[...], preferred_element_type=F32)
        pq_ref[:, lo:lo + FOURIER_GROUP] = pq[:, 0:FOURIER_GROUP].astype(BF16)
        pq_ref[:, FOURIER_WIDTH + lo:FOURIER_WIDTH + lo + FOURIER_GROUP] = (
            pq[:, FOURIER_GROUP:2 * FOURIER_GROUP].astype(BF16))
    chunk = 512
    for c in range(GATE_WIDTH // chunk):
        lo = c * chunk
        ug = jnp.dot(hb, w_ref[:, F_END + lo:F_END + lo + chunk], preferred_element_type=F32)
        gate_ref[:, lo:lo + chunk] = jax.nn.sigmoid(ug + bg_ref[:, lo:lo + chunk])


def _inproj(h, w_bf16, b_gate, cs, tm):
    n = h.shape[0]
    row = lambda i: (i, 0)
    const = lambda i: (0, 0)
    return pl.pallas_call(
        _inproj_kernel,
        grid=(n // tm,),
        in_specs=[
            pl.BlockSpec((tm, D_MODEL), row),
            pl.BlockSpec((D_MODEL, IN_WIDTH), const),
            pl.BlockSpec((1, GATE_WIDTH), const),
            pl.BlockSpec((FOURIER_GROUP, 2 * FOURIER_GROUP), const),
        ],
        out_specs=[
            pl.BlockSpec((tm, ATTN_WIDTH), row),
            pl.BlockSpec((tm, 2 * KV_WIDTH), row),
            pl.BlockSpec((tm, 2 * FOURIER_WIDTH), row),
            pl.BlockSpec((tm, GATE_WIDTH), row),
        ],
        out_shape=[
            jax.ShapeDtypeStruct((n, ATTN_WIDTH), BF16),
            jax.ShapeDtypeStruct((n, 2 * KV_WIDTH), BF16),
            jax.ShapeDtypeStruct((n, 2 * FOURIER_WIDTH), BF16),
            jax.ShapeDtypeStruct((n, GATE_WIDTH), F32),
        ],
        compiler_params=_params("parallel"),
        name="inproj",
    )(h, w_bf16, b_gate.reshape(1, GATE_WIDTH), cs)


def _attn_kernel(sink_ref, q_ref, kvp_ref, kvc_ref, kvn_ref, bias_ref, o_ref):
    kv = jnp.concatenate([kvp_ref[...], kvc_ref[...], kvn_ref[...]], axis=0)
    low_half = lax.broadcasted_iota(jnp.int32, (3 * BLOCK, LANES), 1) < HEAD_DIM

    def lane_half_operands(x):
        swapped = jnp.concatenate([x[:, HEAD_DIM:], x[:, :HEAD_DIM]], axis=1)
        zero = jnp.zeros_like(x)
        return {(kvh, half): jnp.where(low_half if half == 0 else ~low_half,
                                       x if kvh == half else swapped, zero)
                for kvh in range(N_KV_HEADS) for half in range(2)}

    k_ops = lane_half_operands(kv[:, 0:KV_WIDTH])
    v_ops = lane_half_operands(kv[:, KV_WIDTH:2 * KV_WIDTH])

    def scores(h):
        pair = h // 2
        qp = q_ref[:, pair * LANES:(pair + 1) * LANES]
        return lax.dot_general(qp, k_ops[(h // GQA_GROUP, h % 2)], NT_DIMS, preferred_element_type=F32)

    def head_out(h, s):
        sink = sink_ref[h]
        logits = s * (HEAD_DIM ** -0.5) + bias_ref[h]
        m = jnp.maximum(jnp.max(logits, axis=-1, keepdims=True), sink)
        p = jnp.exp(logits - m)
        denom = jnp.sum(p, axis=-1, keepdims=True) + jnp.exp(sink - m)
        o = jnp.dot(p.astype(BF16), v_ops[(h // GQA_GROUP, h % 2)], preferred_element_type=F32)
        return o / denom

    ahead = 5
    pending = [scores(h) for h in range(ahead)]
    for h in range(N_HEADS):
        s_cur = pending.pop(0)
        if h + ahead < N_HEADS:
            pending.append(scores(h + ahead))
        o = head_out(h, s_cur)
        if h % 2 == 0:
            o_even = o
        else:
            pair = h // 2
            o_ref[:, pair * LANES:(pair + 1) * LANES] = (o_even + o).astype(BF16)


def _attn_bias_tables(nb, lead):
    qi = np.arange(BLOCK)[:, None]
    si = np.arange(3 * BLOCK)[None, :]
    rel = np.abs(si - BLOCK - qi).astype(np.float32)
    slopes = np.array([2.0 ** (-8.0 * (h + 1) / N_HEADS) for h in range(N_HEADS)], np.float32)
    base = np.where(rel[None] <= WINDOW, -slopes[:, None, None] * rel[None], np.float32(NEG_INF))
    variants, keys, variant_of_block = [], [], []
    for i in range(nb):
        kpos = (i - 1) * BLOCK + np.arange(3 * BLOCK)
        valid = (kpos >= lead) & (kpos < nb * BLOCK)
        key = valid.tobytes()
        if key not in keys:
            keys.append(key)
            variants.append(np.where(valid[None, None, :], base, np.float32(NEG_INF)))
        variant_of_block.append(keys.index(key))
    return jnp.asarray(np.stack(variants).astype(np.float32)), variant_of_block


def _attention(q, kv, sink, bias, variant_of_block, nb):
    B, lp, _ = q.shape
    kvspec = lambda f: pl.BlockSpec((None, BLOCK, 2 * KV_WIDTH), f)
    interior = max(set(variant_of_block), key=variant_of_block.count)

    def variant(i):
        v = jnp.int32(interior)
        for blk, var in enumerate(variant_of_block):
            if var != interior:
                v = jnp.where(i == blk, var, v)
        return v

    return pl.pallas_call(
        _attn_kernel,
        grid=(B, nb),
        in_specs=[
            pl.BlockSpec(memory_space=pltpu.SMEM),
            pl.BlockSpec((None, BLOCK, ATTN_WIDTH), lambda b, i: (b, i, 0)),
            kvspec(lambda b, i: (b, jnp.maximum(i - 1, 0), 0)),
            kvspec(lambda b, i: (b, i, 0)),
            kvspec(lambda b, i: (b, jnp.minimum(i + 1, nb - 1), 0)),
            pl.BlockSpec((None, N_HEADS, BLOCK, 3 * BLOCK), lambda b, i: (variant(i), 0, 0, 0)),
        ],
        out_specs=pl.BlockSpec((None, BLOCK, ATTN_WIDTH), lambda b, i: (b, i, 0)),
        out_shape=jax.ShapeDtypeStruct((B, lp, ATTN_WIDTH), BF16),
        compiler_params=_params("parallel", "arbitrary"),
        name="attention",
    )(sink, q, kv, kv, kv, bias)


def _fourier_kernel(w_ref, p_ref, q_ref, o_ref, prev_ref, *, lp, lead, seq_len, tf):
    m = pl.program_id(1)
    last = pl.num_programs(1) - 1
    nblk = lp // tf
    shift = lead + 1

    @pl.when(m == 0)
    def _():
        row = lax.broadcasted_iota(jnp.int32, (lp, 1), 0)
        dc = jnp.sum(jnp.where(row >= lead, p_ref[...].astype(F32), 0.0), axis=0, keepdims=True)
        prev_ref[...] = jnp.zeros_like(prev_ref)
        prev_ref[tf - 1:tf, :] = dc * np.float32(seq_len ** -0.5)

    t1 = jnp.dot(w_ref[:, 0:lp], p_ref[...], preferred_element_type=F32)
    t2 = jnp.dot(w_ref[:, lp:2 * lp], q_ref[...], preferred_element_type=F32)
    direct = t1 + t2
    r = lax.broadcasted_iota(jnp.int32, (tf, tf), 0)
    c = lax.broadcasted_iota(jnp.int32, (tf, tf), 1)
    flip = (r + c == tf - 1).astype(BF16)
    mirrored = jnp.dot(flip, (t1 - t2).astype(BF16), preferred_element_type=F32)
    shifted = jnp.concatenate([prev_ref[tf - shift:, :], direct[:tf - shift, :]], axis=0)
    prev_ref[...] = direct

    @pl.when(m < last)
    def _():
        o_ref[pl.ds(pl.multiple_of(m * tf, tf), tf), :] = shifted.astype(BF16)
        o_ref[pl.ds(pl.multiple_of((nblk - 1 - m) * tf, tf), tf), :] = mirrored.astype(BF16)

    @pl.when(m == last)
    def _():
        row = lax.broadcasted_iota(jnp.int32, (tf, 1), 0) + (nblk // 2) * tf
        mid = jnp.where(row <= seq_len // 2 + lead, shifted, mirrored)
        o_ref[(nblk // 2) * tf:(nblk // 2 + 1) * tf, :] = mid.astype(BF16)


def _dft_matrix(nrows, lp, lead, seq_len):
    w = np.float32(2.0 * math.pi / seq_len)
    k = (jnp.arange(nrows, dtype=jnp.int32) + 1)[:, None]
    j = jnp.arange(lp // LANES, dtype=jnp.int32)[None, :]
    r = jnp.arange(LANES, dtype=jnp.int32)[None, :]
    a = ((k * (LANES * j - lead)) % seq_len).astype(F32) * w
    b = ((k * r) % seq_len).astype(F32) * w
    ca, sa, cb, sb = jnp.cos(a)[:, :, None], jnp.sin(a)[:, :, None], jnp.cos(b)[:, None, :], jnp.sin(b)[:, None, :]
    valid = (jnp.arange(lp, dtype=jnp.int32) >= lead).reshape(1, lp // LANES, LANES)
    scale = np.float32(seq_len ** -0.5)
    wc = jnp.where(valid, (ca * cb - sa * sb) * scale, 0.0).reshape(nrows, lp)
    ws = jnp.where(valid, -(sa * cb + ca * sb) * scale, 0.0).reshape(nrows, lp)
    return jnp.concatenate([wc, ws], axis=1).astype(BF16)


def _fourier_steps(lp, lead, seq_len, tf):
    nblk = lp // tf
    assert seq_len % 2 == 0 and lp % tf == 0 and nblk % 2 == 1 and lead + 1 < tf
    steps = nblk // 2 + 1
    assert steps * tf >= seq_len // 2
    return steps


def _fourier(w, pq, lead, seq_len, tf):
    B, lp, _ = pq.shape
    steps = _fourier_steps(lp, lead, seq_len, tf)
    return pl.pallas_call(
        functools.partial(_fourier_kernel, lp=lp, lead=lead, seq_len=seq_len, tf=tf),
        grid=(B, steps),
        in_specs=[
            pl.BlockSpec((tf, 2 * lp), lambda b, m: (m, 0)),
            pl.BlockSpec((None, lp, FOURIER_WIDTH), lambda b, m: (b, 0, 0)),
            pl.BlockSpec((None, lp, FOURIER_WIDTH), lambda b, m: (b, 0, 1)),
        ],
        out_specs=pl.BlockSpec((None, lp, FOURIER_WIDTH), lambda b, m: (b, 0, 0)),
        out_shape=jax.ShapeDtypeStruct((B, lp, FOURIER_WIDTH), BF16),
        scratch_shapes=[pltpu.VMEM((tf, FOURIER_WIDTH), F32)],
        compiler_params=_params("parallel", "arbitrary"),
        name="fourier",
    )(w, pq, pq)


def _split_bf16(x):
    hi = x.astype(BF16)
    lo = (x - hi.astype(F32)).astype(BF16)
    return hi, lo


def _outproj_kernel(attn_ref, four_ref, gate_ref, h_ref, wa_ref, wf_ref, wo_ref, g_ref, b_ref,
                    wrth_ref, wrtl_ref, h1_ref, h1b_ref, affr_ref, *, alpha):
    ya = jnp.dot(attn_ref[...], wa_ref[...], preferred_element_type=F32)
    yf = jnp.dot(four_ref[...], wf_ref[...], preferred_element_type=F32)
    merged = gate_ref[:, 0:D_MODEL] * ya + gate_ref[:, D_MODEL:GATE_WIDTH] * yf
    mix = jnp.dot(merged.astype(BF16), wo_ref[...], preferred_element_type=F32)
    h1 = _ln(alpha * h_ref[...] + mix, g_ref[...], b_ref[...])
    h1_ref[...] = h1
    hi, lo = _split_bf16(h1)
    h1b_ref[...] = hi
    lr =(lax.dot_general(wrth_ref[...], hi, NT_DIMS, preferred_element_type=F32)
          + lax.dot_general(wrtl_ref[...], hi, NT_DIMS, preferred_element_type=F32)
          + lax.dot_general(wrth_ref[...], lo, NT_DIMS, preferred_element_type=F32))
    er = jnp.exp(lr - jnp.max(lr, axis=0, keepdims=True))
    affr_ref[...] = er / jnp.sum(er, axis=0, keepdims=True)


def _outproj(attn, four, gates, h, wa, wf, wo, g, b, w_router, alpha, tm):
    n = h.shape[0]
    row = lambda i: (i, 0)
    const = lambda i: (0, 0)
    wrth, wrtl = _split_bf16(w_router.T)
    return pl.pallas_call(
        functools.partial(_outproj_kernel, alpha=alpha),
        grid=(n // tm,),
        in_specs=[
            pl.BlockSpec((tm, ATTN_WIDTH), row),
            pl.BlockSpec((tm, FOURIER_WIDTH), row),
            pl.BlockSpec((tm, GATE_WIDTH), row),
            pl.BlockSpec((tm, D_MODEL), row),
            pl.BlockSpec((ATTN_WIDTH, D_MODEL), const),
            pl.BlockSpec((FOURIER_WIDTH, D_MODEL), const),
            pl.BlockSpec((D_MODEL, D_MODEL), const),
            pl.BlockSpec((1, D_MODEL), const),
            pl.BlockSpec((1, D_MODEL), const),
            pl.BlockSpec((N_EXPERTS, D_MODEL), const),
            pl.BlockSpec((N_EXPERTS, D_MODEL), const),
        ],
        out_specs=[
            pl.BlockSpec((tm, D_MODEL), row),
            pl.BlockSpec((tm, D_MODEL), row),
            pl.BlockSpec((N_EXPERTS, tm), lambda i: (0, i)),
        ],
        out_shape=[
            jax.ShapeDtypeStruct((n, D_MODEL), F32),
            jax.ShapeDtypeStruct((n, D_MODEL), BF16),
            jax.ShapeDtypeStruct((N_EXPERTS, n), F32),
        ],
        compiler_params=_params("parallel"),
        name="outproj_ln_router",
    )(attn, four, gates, h, wa, wf, wo, g.reshape(1, -1), b.reshape(1, -1), wrth, wrtl)


def _topk_kernel(aff_ref, rr_ref, rc_ref, cnt_ref, cntc_ref, *, lead, lp, cap):
    nchunk = lp // LANES
    lane = lax.broadcasted_iota(jnp.int32, (N_EXPERTS, lp), 1)
    bits = jnp.where(lane >= lead, pltpu.bitcast(aff_ref[...], jnp.int32), -1)

    def search(i, t):
        cand = t | (jnp.int32(1) << (30 - i))
        cnt = jnp.sum((bits >= cand).astype(jnp.int32), axis=-1, keepdims=True)
        return jnp.where(cnt >= cap, cand, t)

    thr = lax.fori_loop(0, 31, search, jnp.zeros((N_EXPERTS, 1), jnp.int32))
    gt = bits > thr
    eq = bits == thr
    need = cap - jnp.sum(gt.astype(jnp.int32), axis=-1, keepdims=True)

    r = lax.broadcasted_iota(jnp.int32, (LANES, LANES), 0)
    c = lax.broadcasted_iota(jnp.int32, (LANES, LANES), 1)
    upper = (r <= c).astype(BF16)
    lower = (c <= r).astype(BF16)
    ident = (c == r).astype(BF16)

    eqb = eq.astype(BF16)
    off = jnp.zeros((N_EXPERTS, 1), F32)
    needf = need.astype(F32)
    sel_chunks = []
    for k in range(nchunk):
        sl = slice(k * LANES, (k + 1) * LANES)
        pre = jnp.dot(eqb[:, sl], upper, preferred_element_type=F32) + off
        off = pre[:, LANES - 1:LANES]
        sel_chunks.append(gt[:, sl] | (eq[:, sl] & (pre <= needf)))

    off_r = jnp.zeros((N_EXPERTS, 1), F32)
    off_c = jnp.zeros((1, N_EXPERTS), F32)
    cnt_lane = lax.broadcasted_iota(jnp.int32, (N_EXPERTS, LANES), 1)
    cnt = jnp.zeros((N_EXPERTS, LANES), F32)
    for k in range(nchunk):
        sl = slice(k * LANES, (k + 1) * LANES)
        sel = sel_chunks[k]
        selb = sel.astype(BF16)
        pre_r = jnp.dot(selb, upper, preferred_element_type=F32) + off_r
        rr_ref[k] = jnp.where(sel, pre_r - 1.0, -1.0).astype(jnp.int32)
        off_r = pre_r[:, LANES - 1:LANES]
        cnt = jnp.where(cnt_lane == k + 1, off_r, cnt)
        pre_c = lax.dot_general(lower, selb, NT_DIMS, preferred_element_type=F32) + off_c
        sel_c = lax.dot_general(ident, selb, NT_DIMS, preferred_element_type=F32)
        rc_ref[sl, :] = jnp.where(sel_c > 0.5, pre_c - 1.0, -1.0).astype(jnp.int32)
        off_c = pre_c[LANES - 1:LANES, :]
        cntc_ref[k + 1:k + 2, :] = off_c.astype(jnp.int32)
    cnt_ref[...] = cnt.astype(jnp.int32)
    cntc_ref[0:1, :] = jnp.zeros((1, N_EXPERTS), jnp.int32)
    pad_rows = cntc_ref.shape[0] - nchunk - 1
    if pad_rows:
        cntc_ref[nchunk + 1:, :] = jnp.zeros((pad_rows, N_EXPERTS), jnp.int32)


def _topk(aff_r, B, lp, lead, cap):
    n = B * lp
    nchunk = lp // LANES
    assert nchunk + 1 <= LANES
    nb1 = -(-(nchunk + 1) // SUBLANES) * SUBLANES
    return pl.pallas_call(
        functools.partial(_topk_kernel, lead=lead, lp=lp, cap=cap),
        grid=(B,),
        in_specs=[pl.BlockSpec((N_EXPERTS, lp), lambda b: (0, b))],
        out_specs=[
            pl.BlockSpec((None, nchunk, N_EXPERTS, LANES), lambda b: (b, 0, 0, 0)),
            pl.BlockSpec((lp, N_EXPERTS), lambda b: (b, 0)),
            pl.BlockSpec((None, N_EXPERTS, LANES), lambda b: (b, 0, 0)),
            pl.BlockSpec((None, nb1, N_EXPERTS), lambda b: (b, 0, 0)),
        ],
        out_shape=[
            jax.ShapeDtypeStruct((B, nchunk, N_EXPERTS, LANES), jnp.int32),
            jax.ShapeDtypeStruct((n, N_EXPERTS), jnp.int32),
            jax.ShapeDtypeStruct((B, N_EXPERTS, LANES), jnp.int32),
            jax.ShapeDtypeStruct((B, nb1, N_EXPERTS), jnp.int32),
        ],
        compiler_params=_params("parallel"),
        name="topk_select",
    )(aff_r)


SLOT_ALIGN = 16


def _window_constants(win):
    wide = N_EXPERTS * win
    expand = np.zeros((N_EXPERTS, wide), np.float32)
    for e in range(N_EXPERTS):
        expand[e, e * win:(e + 1) * win] = 1.0
    return expand, (np.arange(wide) % win).astype(np.float32)


def _gather_kernel(cnt_ref, cntv_ref, rank_ref, aff_ref, h_ref, expand_ref, pos_ref, o_ref, g_ref,
                   *, slots, ch, win, nchunk):
    bi, c = pl.program_id(0), pl.program_id(1)
    k0 = c * ch
    tt = ch * LANES
    max_start = slots - win

    @pl.when(c == 0)
    def _():
        o_ref[...] = jnp.zeros_like(o_ref)
        g_ref[...] = jnp.zeros_like(g_ref)

    lane = lax.broadcasted_iota(jnp.int32, cntv_ref.shape, 1)
    lo_col = jnp.sum(jnp.where(lane == k0, cntv_ref[...], 0), axis=-1, keepdims=True)
    w_col = jnp.minimum(lo_col & -SLOT_ALIGN, max_start)
    rank = rank_ref[...]
    aff = aff_ref[...]
    rel = rank - w_col
    in_win = (rank >= 0) & (rel >= 0) & (rel < win)
    relb = jnp.where(in_win, rel, -1).astype(F32).astype(BF16)
    rel_tall = jnp.dot(expand_ref[...], relb, preferred_element_type=F32)
    hit_all = rel_tall == pos_ref[...]
    rows = h_ref[...]
    xw = jnp.dot(hit_all.astype(BF16), rows, preferred_element_type=F32)

    def add_rows(e, start, hit, vals):
        dst = pl.ds(pl.multiple_of(start, SLOT_ALIGN), win)
        o_ref[e, dst, :] = o_ref[e, dst, :] + vals.astype(BF16)
        g_ref[e, dst, :] += jnp.sum(jnp.where(hit, aff[e:e + 1, :], 0.0), axis=-1, keepdims=True)

    row = lax.broadcasted_iota(jnp.int32, (win, tt), 0)
    for e in range(N_EXPERTS):
        base = (bi * N_EXPERTS + e) * (nchunk + 1) + k0
        w = jnp.minimum(cnt_ref[base] & -SLOT_ALIGN, max_start)
        add_rows(e, w, hit_all[e * win:(e + 1) * win, :], xw[e * win:(e + 1) * win, :])
        hi = cnt_ref[base + ch]

        @pl.when(hi > w + win)
        def _(e=e, w=w, hi=hi):
            rk = rank[e:e + 1, :]

            def extra(i, carry):
                w2 = w + (i + 1) * win
                w2c = jnp.minimum(w2, max_start)
                hit = ((rk - w2c) == row) & (rk >= w2)
                add_rows(e, w2c, hit, jnp.dot(hit.astype(BF16), rows, preferred_element_type=F32))
                return carry

            lax.fori_loop(0, (hi - w - 1) // win, extra, 0)


def _gather(cnt, cntv, rank_rows, aff_rows, h1b, B, lp, slots, ch, win):
    nchunk = lp // LANES
    tt = ch * LANES
    nt = nchunk // ch
    assert slots % SLOT_ALIGN == 0 and win % SLOT_ALIGN == 0 and win <= slots
    expand, pos = _window_constants(win)
    wide = N_EXPERTS * win
    const = lambda bi, c, cnt: (0, 0)
    per_seq = lambda bi, c, cnt: (bi, 0, 0, 0)
    return pl.pallas_call(
        functools.partial(_gather_kernel, slots=slots, ch=ch, win=win, nchunk=nchunk),
        grid_spec=pltpu.PrefetchScalarGridSpec(
            num_scalar_prefetch=1,
            grid=(B, nt),
            in_specs=[
                pl.BlockSpec((None, N_EXPERTS, LANES), lambda bi, c, cnt: (bi, 0, 0)),
                pl.BlockSpec((None, N_EXPERTS, tt), lambda bi, c, cnt: (bi, 0, c)),
                pl.BlockSpec((None, N_EXPERTS, tt), lambda bi, c, cnt: (bi, 0, c)),
                pl.BlockSpec((tt, D_MODEL), lambda bi, c, cnt: (bi * nt + c, 0)),
                pl.BlockSpec((wide, N_EXPERTS), const),
                pl.BlockSpec((wide, 1), const),
            ],
            out_specs=[pl.BlockSpec((None, N_EXPERTS, slots, D_MODEL), per_seq),
                       pl.BlockSpec((None, N_EXPERTS, slots, 1), per_seq)],
        ),
        out_shape=[jax.ShapeDtypeStruct((B, N_EXPERTS, slots, D_MODEL), BF16),
                   jax.ShapeDtypeStruct((B, N_EXPERTS, slots, 1), F32)],
        compiler_params=_params("parallel", "arbitrary"),
        name="moe_gather",
    )(cnt, cntv, rank_rows, aff_rows, h1b, jnp.asarray(expand.T, dtype=BF16), jnp.asarray(pos.reshape(wide, 1)))


def _ffn_up_kernel(x_ref, wg_ref, wu_ref, o_ref, wgb_ref, wub_ref):
    @pl.when(pl.program_id(1) == 0)
    def _():
        wgb_ref[...] = wg_ref[...].astype(BF16)
        wub_ref[...] = wu_ref[...].astype(BF16)

    x = x_ref[...]
    a = jnp.dot(x, wgb_ref[...], preferred_element_type=F32)
    u = jnp.dot(x, wub_ref[...], preferred_element_type=F32)
    o_ref[...] = (a * jax.nn.sigmoid(a) * u).astype(BF16)


def _ffn_down_kernel(x_ref, g_ref, wd_ref, o_ref, wdb_ref):
    @pl.when(pl.program_id(1) == 0)
    def _():
        wdb_ref[...] = wd_ref[...].astype(BF16)

    y = jnp.dot(x_ref[...], wdb_ref[...], preferred_element_type=F32)
    o_ref[...] = (y * g_ref[...]).astype(BF16)


def _ffn(xg, gates, wg, wu, wd, layer):
    B, _, slots, _ = xg.shape
    act = lambda width: pl.BlockSpec((None, None, slots, width), lambda e, b: (b, e, 0, 0))
    weight = lambda rows, cols: pl.BlockSpec((None, None, rows, cols), lambda e, b: (layer, e, 0, 0))
    mid = pl.pallas_call(
        _ffn_up_kernel,
        grid=(N_EXPERTS, B),
        in_specs=[act(D_MODEL), weight(D_MODEL, D_FF_EXPERT), weight(D_MODEL, D_FF_EXPERT)],
        out_specs=act(D_FF_EXPERT),
        out_shape=jax.ShapeDtypeStruct((B, N_EXPERTS, slots, D_FF_EXPERT), BF16),
        scratch_shapes=[pltpu.VMEM((D_MODEL, D_FF_EXPERT), BF16), pltpu.VMEM((D_MODEL, D_FF_EXPERT), BF16)],
        compiler_params=_params("arbitrary", "arbitrary"),
        name="moe_ffn_up",
    )(xg, wg, wu)
    return pl.pallas_call(
        _ffn_down_kernel,
        grid=(N_EXPERTS, B),
        in_specs=[act(D_FF_EXPERT), act(1), weight(D_FF_EXPERT, D_MODEL)],
        out_specs=act(D_MODEL),
        out_shape=jax.ShapeDtypeStruct(xg.shape, BF16),
        scratch_shapes=[pltpu.VMEM((D_FF_EXPERT, D_MODEL), BF16)],
        compiler_params=_params("arbitrary", "arbitrary"),
        name="moe_ffn_down",
    )(mid, gates, wd)


def _combine_kernel(cnt_ref, cntc_ref, rank_ref, y_ref, h_ref, g_ref, b_ref, expand_ref, rpat_ref,
                    o_ref, acc_ref, *, slots, ch, win, nchunk, alpha):
    bi, c = pl.program_id(0), pl.program_id(1)
    k0 = c * ch
    tt = ch * LANES
    max_start = slots - win
    rank = rank_ref[...]
    lo_row = cntc_ref[pl.ds(k0, 1), :]
    w_row = jnp.minimum(lo_row & -SLOT_ALIGN, max_start)
    rel = rank - w_row
    in_win = (rank >= 0) & (rel >= 0) & (rel < win)
    relb = jnp.where(in_win, rel, -1).astype(F32).astype(BF16)
    rel_wide = jnp.dot(relb, expand_ref[...], preferred_element_type=F32)
    onehot = (rel_wide == rpat_ref[...]).astype(BF16)

    starts, windows = [], []
    for e in range(N_EXPERTS):
        lo = cnt_ref[(bi * N_EXPERTS + e) * (nchunk + 1) + k0]
        w = jnp.minimum(lo & -SLOT_ALIGN, max_start)
        starts.append(w)
        windows.append(y_ref[e, pl.ds(pl.multiple_of(w, SLOT_ALIGN), win), :])
    acc_ref[...] = jnp.dot(onehot, jnp.concatenate(windows, axis=0), preferred_element_type=F32)

    lane = lax.broadcasted_iota(jnp.int32, rank.shape, 1)
    col = lax.broadcasted_iota(jnp.int32, (tt, win), 1)
    for e in range(N_EXPERTS):
        hi = cnt_ref[(bi * N_EXPERTS + e) * (nchunk + 1) + k0 + ch]

        @pl.when(hi > starts[e] + win)
        def _(e=e, hi=hi):
            rk = jnp.sum(jnp.where(lane == e, rank, 0), axis=-1, keepdims=True)

            def extra(i, carry):
                w2 = starts[e] + (i + 1) * win
                w2c = jnp.minimum(w2, max_start)
                rows = y_ref[e, pl.ds(pl.multiple_of(w2c, SLOT_ALIGN), win), :]
                hit = ((rk - w2c) == col) & (rk >= w2)
                acc_ref[...] += jnp.dot(hit.astype(BF16), rows, preferred_element_type=F32)
                return carry

            lax.fori_loop(0, (hi - starts[e] - 1) // win, extra, 0)

    o_ref[...] = _ln(alpha * h_ref[...] + acc_ref[...], g_ref[...], b_ref[...])


def _combine(cnt, cntc, rank_c, y, h1, g, b, B, lp, alpha, ch, win):
    n = B * lp
    slots = y.shape[2]
    nchunk = lp // LANES
    tt = ch * LANES
    nt = nchunk // ch
    assert slots % SLOT_ALIGN == 0 and win % SLOT_ALIGN == 0 and win <= slots
    wide = N_EXPERTS * win
    expand, pos = _window_constants(win)
    tok = lambda bi, c, cnt: (bi * nt + c, 0)
    const = lambda bi, c, cnt: (0, 0)
    return pl.pallas_call(
        functools.partial(_combine_kernel, slots=slots, ch=ch, win=win, nchunk=nchunk, alpha=alpha),
        grid_spec=pltpu.PrefetchScalarGridSpec(
            num_scalar_prefetch=1,
            grid=(B, nt),
            in_specs=[
                pl.BlockSpec((None, cntc.shape[1], N_EXPERTS), lambda bi, c, cnt: (bi, 0, 0)),
                pl.BlockSpec((tt, N_EXPERTS), tok),
                pl.BlockSpec((None, N_EXPERTS, slots, D_MODEL), lambda bi, c, cnt: (bi, 0, 0, 0)),
                pl.BlockSpec((tt, D_MODEL), tok),
                pl.BlockSpec((1, D_MODEL), const),
                pl.BlockSpec((1, D_MODEL), const),
                pl.BlockSpec((N_EXPERTS, wide), const),
                pl.BlockSpec((1, wide), const),
            ],
            out_specs=pl.BlockSpec((tt, D_MODEL), tok),
            scratch_shapes=[pltpu.VMEM((tt, D_MODEL), F32)],
        ),
        out_shape=jax.ShapeDtypeStruct((n, D_MODEL), F32),
        compiler_params=_params("parallel", "arbitrary"),
        name="moe_combine_ln",
    )(cnt, cntc, rank_c, y, h1, g.reshape(1, -1), b.reshape(1, -1),
      jnp.asarray(expand, dtype=BF16), jnp.asarray(pos.reshape(1, wide)))


def kernel(x, meta, ln0_g, ln0_b, w_in, b_gate, sink, w_attn_o, w_four_o, w_out, ln1_g, ln1_b,
           w_router, w_e_gate, w_e_up, w_e_down, ln2_g, ln2_b):
    B, seq, d = x.shape
    depth = w_in.shape[0]
    assert d == D_MODEL and meta.shape == (N_META, D_MODEL)
    assert seq % BLOCK == 0 and N_META % SUBLANES == 0 and N_META <= BLOCK
    L = seq + N_META
    nb = -(-L // BLOCK)
    lp = nb * BLOCK
    lead = lp - L
    cap = CAPACITY_FACTOR * L // N_EXPERTS
    slots = -(-cap // SLOT_ALIGN) * SLOT_ALIGN
    alpha = float((2 * depth) ** 0.25)
    n = B * lp
    nchunk = lp // LANES
    tm = _pick_tile(n, (512, 256, 128))
    tf = _pick_tile(lp, (384, 128))
    ch = _pick_tile(nchunk, (3, 2, 1))
    win = min(slots, -(-(2 * ch * LANES * cap // L + SLOT_ALIGN) // SLOT_ALIGN) * SLOT_ALIGN)

    gi = np.arange(FOURIER_GROUP)
    ang = 2.0 * np.pi * ((gi[:, None] * gi[None, :]) % FOURIER_GROUP) / FOURIER_GROUP
    cs = jnp.asarray(np.concatenate([np.cos(ang), np.sin(ang)], axis=1) * FOURIER_GROUP ** -0.5, dtype=BF16)
    wdft = _dft_matrix(_fourier_steps(lp, lead, L, tf) * tf, lp, lead, L)
    bias, variant_of_block = _attn_bias_tables(nb, lead)

    h = _embed(x, meta, ln0_g, ln0_b, nb, lead).reshape(n, D_MODEL)
    for l in range(depth):
        q, kv, pq, gates = _inproj(h, w_in[l].astype(BF16), b_gate[l], cs, tm)
        attn = _attention(q.reshape(B, lp, -1), kv.reshape(B, lp, -1), sink[l], bias, variant_of_block, nb)
        four = _fourier(wdft, pq.reshape(B, lp, -1), lead, L, tf)
        h1, h1b, aff_r = _outproj(
            attn.reshape(n, -1), four.reshape(n, -1), gates, h,
            w_attn_o[l].astype(BF16), w_four_o[l].astype(BF16), w_out[l].astype(BF16),
            ln1_g[l], ln1_b[l], w_router[l], alpha, tm)
        rank_r, rank_c, cntv, cntc = _topk(aff_r, B, lp, lead, cap)
        cnt = cntv[:, :, :nchunk + 1].reshape(-1)
        rank_rows = rank_r.transpose(0, 2, 1, 3).reshape(B, N_EXPERTS, lp)
        aff_rows = aff_r.reshape(N_EXPERTS, B, lp).transpose(1, 0, 2)
        xg, gsel = _gather(cnt, cntv, rank_rows, aff_rows, h1b, B, lp, slots, ch, win)
        y = _ffn(xg, gsel, w_e_gate, w_e_up, w_e_down, l)
        h = _combine(cnt, cntc, rank_c, y, h1, ln2_g[l], ln2_b[l], B, lp, alpha, ch, win)
    return h.reshape(B, lp, D_MODEL)[:, lead + N_META:]
```

```python
import functools
import math

import numpy as np
import jax
import jax.numpy as jnp
from jax import lax
from jax.experimental import pallas as pl
from jax.experimental.pallas import tpu as pltpu

D_MODEL = 1024
N_META = 16
N_HEADS = 8
N_KV_HEADS = 2
HEAD_DIM = 64
GQA_GROUP = N_HEADS // N_KV_HEADS
ATTN_WIDTH = N_HEADS * HEAD_DIM
KV_WIDTH = N_KV_HEADS * HEAD_DIM
WINDOW = 128
BLOCK = 128
N_FOURIER_GROUPS = 4
FOURIER_GROUP = 128
FOURIER_WIDTH = N_FOURIER_GROUPS * FOURIER_GROUP
N_BRANCHES = 2
GATE_WIDTH = N_BRANCHES * D_MODEL
N_EXPERTS = 16
CAPACITY_FACTOR = 2
D_FF_EXPERT = 1536
LN_EPS = 1e-5
NEG_INF = -1e30
Q_END = ATTN_WIDTH
K_END = Q_END + KV_WIDTH
V_END = K_END + KV_WIDTH
F_END = V_END + FOURIER_WIDTH
IN_WIDTH = F_END + GATE_WIDTH

LANES = 128
SUBLANES = 8
VMEM_LIMIT_BYTES = 56 * 1024 * 1024

F32 = jnp.float32
BF16 = jnp.bfloat16
NT_DIMS = (((1,), (1,)), ((), ()))


def _pick_tile(n, candidates):
    for c in candidates:
        if n % c == 0:
            return c
    raise ValueError(f"no tile in {candidates} divides {n}")


def _params(*sem, flags=None):
    return pltpu.CompilerParams(dimension_semantics=sem, vmem_limit_bytes=VMEM_LIMIT_BYTES, flags=flags)


def _ln(x, g, b):
    mu = jnp.mean(x, axis=-1, keepdims=True)
    xc = x - mu
    var = jnp.mean(xc * xc, axis=-1, keepdims=True)
    return xc * lax.rsqrt(var + LN_EPS) * g + b


def _embed_kernel(*refs, lead, group):
    x_refs, (meta_ref, g_ref, b_ref, o_ref) = refs[:group], refs[group:]
    for k in range(1, group):
        o_ref[k * BLOCK:(k + 1) * BLOCK, :] = _ln(x_refs[k][...], g_ref[...], b_ref[...])

    @pl.when(pl.program_id(1) == 0)
    def _():
        o_ref[0:lead, :] = jnp.zeros((lead, D_MODEL), F32)
        o_ref[lead:BLOCK, :] = _ln(meta_ref[...], g_ref[...], b_ref[...])

    @pl.when(pl.program_id(1) > 0)
    def _():
        o_ref[0:BLOCK, :] = _ln(x_refs[0][...], g_ref[...], b_ref[...])


def _embed(x, meta, g, b, nb, lead):
    B = x.shape[0]
    group = _pick_tile(nb, (3, 1))
    x_spec = lambda k: pl.BlockSpec((None, BLOCK, D_MODEL),
                                    lambda bi, j: (bi, jnp.maximum(group * j + k - 1, 0), 0))
    const = lambda bi, j: (0, 0)
    return pl.pallas_call(
        functools.partial(_embed_kernel, lead=lead, group=group),
        grid=(B, nb // group),
        in_specs=[x_spec(k) for k in range(group)] + [
            pl.BlockSpec((N_META, D_MODEL), const),
            pl.BlockSpec((1, D_MODEL), const),
            pl.BlockSpec((1, D_MODEL), const),
        ],
        out_specs=pl.BlockSpec((None, group * BLOCK, D_MODEL), lambda bi, j: (bi, j, 0)),
        out_shape=jax.ShapeDtypeStruct((B, nb * BLOCK, D_MODEL), F32),
        compiler_params=_params("parallel", "arbitrary"),
        name="embed_ln",
    )(*([x] * group), meta, g.reshape(1, -1), b.reshape(1, -1))


def _inproj_kernel(h_ref, w_ref, bg_ref, cs_ref, q_ref, kv_ref, pq_ref, gate_ref):
    hb = h_ref[...].astype(BF16)
    q_ref[...] = jnp.dot(hb, w_ref[:, 0:Q_END], preferred_element_type=F32).astype(BF16)
    kv_ref[...] = jnp.dot(hb, w_ref[:, Q_END:V_END], preferred_element_type=F32).astype(BF16)
    uf = jnp.dot(hb, w_ref[:, V_END:F_END], preferred_element_type=F32).astype(BF16)
    for g in range(N_FOURIER_GROUPS):
        lo = g * FOURIER_GROUP
        pq = jnp.dot(uf[:, lo:lo + FOURIER_GROUP], cs_ref[...], preferred_element_type=F32)
        pq_ref[:, lo:lo + FOURIER_GROUP] = pq[:, 0:FOURIER_GROUP].astype(BF16)
        pq_ref[:, FOURIER_WIDTH + lo:FOURIER_WIDTH + lo + FOURIER_GROUP] = (
            pq[:, FOURIER_GROUP:2 * FOURIER_GROUP].astype(BF16))
    chunk = 512
    for c in range(GATE_WIDTH // chunk):
        lo = c * chunk
        ug = jnp.dot(hb, w_ref[:, F_END + lo:F_END + lo + chunk], preferred_element_type=F32)
        gate_ref[:, lo:lo + chunk] = jax.nn.sigmoid(ug + bg_ref[:, lo:lo + chunk])


def _inproj(h, w_bf16, b_gate, cs, tm):
    n = h.shape[0]
    row = lambda i: (i, 0)
    const = lambda i: (0, 0)
    return pl.pallas_call(
        _inproj_kernel,
        grid=(n // tm,),
        in_specs=[
            pl.BlockSpec((tm, D_MODEL), row),
            pl.BlockSpec((D_MODEL, IN_WIDTH), const),
            pl.BlockSpec((1, GATE_WIDTH), const),
            pl.BlockSpec((FOURIER_GROUP, 2 * FOURIER_GROUP), const),
        ],
        out_specs=[
            pl.BlockSpec((tm, ATTN_WIDTH), row),
            pl.BlockSpec((tm, 2 * KV_WIDTH), row),
            pl.BlockSpec((tm, 2 * FOURIER_WIDTH), row),
            pl.BlockSpec((tm, GATE_WIDTH), row),
        ],
        out_shape=[
            jax.ShapeDtypeStruct((n, ATTN_WIDTH), BF16),
            jax.ShapeDtypeStruct((n, 2 * KV_WIDTH), BF16),
            jax.ShapeDtypeStruct((n, 2 * FOURIER_WIDTH), BF16),
            jax.ShapeDtypeStruct((n, GATE_WIDTH), F32),
        ],
        compiler_params=_params("parallel"),
        name="inproj",
    )(h, w_bf16, b_gate.reshape(1, GATE_WIDTH), cs)


def _attn_kernel(sink_ref, q_ref, kvp_ref, kvc_ref, kvn_ref, bias_ref, o_ref):
    kv = jnp.concatenate([kvp_ref[...], kvc_ref[...], kvn_ref[...]], axis=0)
    low_half = lax.broadcasted_iota(jnp.int32, (3 * BLOCK, LANES), 1) < HEAD_DIM

    def lane_half_operands(x):
        swapped = jnp.concatenate([x[:, HEAD_DIM:], x[:, :HEAD_DIM]], axis=1)
        zero = jnp.zeros_like(x)
        return {(kvh, half): jnp.where(low_half if half == 0 else ~low_half,
                                       x if kvh == half else swapped, zero)
                for kvh in range(N_KV_HEADS) for half in range(2)}

    k_ops = lane_half_operands(kv[:, 0:KV_WIDTH])
    v_ops = lane_half_operands(kv[:, KV_WIDTH:2 * KV_WIDTH])

    def scores(h):
        pair = h // 2
        qp = q_ref[:, pair * LANES:(pair + 1) * LANES]
        return lax.dot_general(qp, k_ops[(h // GQA_GROUP, h % 2)], NT_DIMS, preferred_element_type=F32)

    def head_out(h, s):
        sink = sink_ref[h]
        logits = s * (HEAD_DIM ** -0.5) + bias_ref[h]
        m = jnp.maximum(jnp.max(logits, axis=-1, keepdims=True), sink)
        p = jnp.exp(logits - m)
        denom = jnp.sum(p, axis=-1, keepdims=True) + jnp.exp(sink - m)
        o = jnp.dot(p.astype(BF16), v_ops[(h // GQA_GROUP, h % 2)], preferred_element_type=F32)
        return o / denom

    ahead = 5
    pending = [scores(h) for h in range(ahead)]
    for h in range(N_HEADS):
        s_cur = pending.pop(0)
        if h + ahead < N_HEADS:
            pending.append(scores(h + ahead))
        o = head_out(h, s_cur)
        if h % 2 == 0:
            o_even = o
        else:
            pair = h // 2
            o_ref[:, pair * LANES:(pair + 1) * LANES] = (o_even + o).astype(BF16)


def _attn_bias_tables(nb, lead):
    qi = np.arange(BLOCK)[:, None]
    si = np.arange(3 * BLOCK)[None, :]
    rel = np.abs(si - BLOCK - qi).astype(np.float32)
    slopes = np.array([2.0 ** (-8.0 * (h + 1) / N_HEADS) for h in range(N_HEADS)], np.float32)
    base = np.where(rel[None] <= WINDOW, -slopes[:, None, None] * rel[None], np.float32(NEG_INF))
    variants, keys, variant_of_block = [], [], []
    for i in range(nb):
        kpos = (i - 1) * BLOCK + np.arange(3 * BLOCK)
        valid = (kpos >= lead) & (kpos < nb * BLOCK)
        key = valid.tobytes()
        if key not in keys:
            keys.append(key)
            variants.append(np.where(valid[None, None, :], base, np.float32(NEG_INF)))
        variant_of_block.append(keys.index(key))
    return jnp.asarray(np.stack(variants).astype(np.float32)), variant_of_block


def _attention(q, kv, sink, bias, variant_of_block, nb):
    B, lp, _ = q.shape
    kvspec = lambda f: pl.BlockSpec((None, BLOCK, 2 * KV_WIDTH), f)
    interior = max(set(variant_of_block), key=variant_of_block.count)

    def variant(i):
        v = jnp.int32(interior)
        for blk, var in enumerate(variant_of_block):
            if var != interior:
                v = jnp.where(i == blk, var, v)
        return v

    return pl.pallas_call(
        _attn_kernel,
        grid=(B, nb),
        in_specs=[
            pl.BlockSpec(memory_space=pltpu.SMEM),
            pl.BlockSpec((None, BLOCK, ATTN_WIDTH), lambda b, i: (b, i, 0)),
            kvspec(lambda b, i: (b, jnp.maximum(i - 1, 0), 0)),
            kvspec(lambda b, i: (b, i, 0)),
            kvspec(lambda b, i: (b, jnp.minimum(i + 1, nb - 1), 0)),
            pl.BlockSpec((None, N_HEADS, BLOCK, 3 * BLOCK), lambda b, i: (variant(i), 0, 0, 0)),
        ],
        out_specs=pl.BlockSpec((None, BLOCK, ATTN_WIDTH), lambda b, i: (b, i, 0)),
        out_shape=jax.ShapeDtypeStruct((B, lp, ATTN_WIDTH), BF16),
        compiler_params=_params("parallel", "arbitrary"),
        name="attention",
    )(sink, q, kv, kv, kv, bias)


def _fourier_kernel(w_ref, p_ref, q_ref, o_ref, prev_ref, *, lp, lead, seq_len, tf):
    m = pl.program_id(1)
    last = pl.num_programs(1) - 1
    nblk = lp // tf
    shift = lead + 1

    @pl.when(m == 0)
    def _():
        row = lax.broadcasted_iota(jnp.int32, (lp, 1), 0)
        dc = jnp.sum(jnp.where(row >= lead, p_ref[...].astype(F32), 0.0), axis=0, keepdims=True)
        prev_ref[...] = jnp.zeros_like(prev_ref)
        prev_ref[tf - 1:tf, :] = dc * np.float32(seq_len ** -0.5)

    t1 = jnp.dot(w_ref[:, 0:lp], p_ref[...], preferred_element_type=F32)
    t2 = jnp.dot(w_ref[:, lp:2 * lp], q_ref[...], preferred_element_type=F32)
    direct = t1 + t2
    r = lax.broadcasted_iota(jnp.int32, (tf, tf), 0)
    c = lax.broadcasted_iota(jnp.int32, (tf, tf), 1)
    flip = (r + c == tf - 1).astype(BF16)
    mirrored = jnp.dot(flip, (t1 - t2).astype(BF16), preferred_element_type=F32)
    shifted = jnp.concatenate([prev_ref[tf - shift:, :], direct[:tf - shift, :]], axis=0)
    prev_ref[...] = direct

    @pl.when(m < last)
    def _():
        o_ref[pl.ds(pl.multiple_of(m * tf, tf), tf), :] = shifted.astype(BF16)
        o_ref[pl.ds(pl.multiple_of((nblk - 1 - m) * tf, tf), tf), :] = mirrored.astype(BF16)

    @pl.when(m == last)
    def _():
        row = lax.broadcasted_iota(jnp.int32, (tf, 1), 0) + (nblk // 2) * tf
        mid = jnp.where(row <= seq_len // 2 + lead, shifted, mirrored)
        o_ref[(nblk // 2) * tf:(nblk // 2 + 1) * tf, :] = mid.astype(BF16)


def _dft_matrix(nrows, lp, lead, seq_len):
    w = np.float32(2.0 * math.pi / seq_len)
    k = (jnp.arange(nrows, dtype=jnp.int32) + 1)[:, None]
    j = jnp.arange(lp // LANES, dtype=jnp.int32)[None, :]
    r = jnp.arange(LANES, dtype=jnp.int32)[None, :]
    a = ((k * (LANES * j - lead)) % seq_len).astype(F32) * w
    b = ((k * r) % seq_len).astype(F32) * w
    ca, sa, cb, sb = jnp.cos(a)[:, :, None], jnp.sin(a)[:, :, None], jnp.cos(b)[:, None, :], jnp.sin(b)[:, None, :]
    valid = (jnp.arange(lp, dtype=jnp.int32) >= lead).reshape(1, lp // LANES, LANES)
    scale = np.float32(seq_len ** -0.5)
    wc = jnp.where(valid, (ca * cb - sa * sb) * scale, 0.0).reshape(nrows, lp)
    ws = jnp.where(valid, -(sa * cb + ca * sb) * scale, 0.0).reshape(nrows, lp)
    return jnp.concatenate([wc, ws], axis=1).astype(BF16)


def _fourier_steps(lp, lead, seq_len, tf):
    nblk = lp // tf
    assert seq_len % 2 == 0 and lp % tf == 0 and nblk % 2 == 1 and lead + 1 < tf
    steps = nblk // 2 + 1
    assert steps * tf >= seq_len // 2
    return steps


def _fourier(w, pq, lead, seq_len, tf):
    B, lp, _ = pq.shape
    steps = _fourier_steps(lp, lead, seq_len, tf)
    return pl.pallas_call(
        functools.partial(_fourier_kernel, lp=lp, lead=lead, seq_len=seq_len, tf=tf),
        grid=(B, steps),
        in_specs=[
            pl.BlockSpec((tf, 2 * lp), lambda b, m: (m, 0)),
            pl.BlockSpec((None, lp, FOURIER_WIDTH), lambda b, m: (b, 0, 0)),
            pl.BlockSpec((None, lp, FOURIER_WIDTH), lambda b, m: (b, 0, 1)),
        ],
        out_specs=pl.BlockSpec((None, lp, FOURIER_WIDTH), lambda b, m: (b, 0, 0)),
        out_shape=jax.ShapeDtypeStruct((B, lp, FOURIER_WIDTH), BF16),
        scratch_shapes=[pltpu.VMEM((tf, FOURIER_WIDTH), F32)],
        compiler_params=_params("parallel", "arbitrary"),
        name="fourier",
    )(w, pq, pq)


def _split_bf16(x):
    hi = x.astype(BF16)
    lo = (x - hi.astype(F32)).astype(BF16)
    return hi, lo


def _outproj_kernel(attn_ref, four_ref, gate_ref, h_ref, wa_ref, wf_ref, wo_ref, g_ref, b_ref,
                    wrth_ref, wrtl_ref, h1_ref, h1b_ref, affr_ref, *, alpha):
    ya = jnp.dot(attn_ref[...], wa_ref[...], preferred_element_type=F32)
    yf = jnp.dot(four_ref[...], wf_ref[...], preferred_element_type=F32)
    merged = gate_ref[:, 0:D_MODEL] * ya + gate_ref[:, D_MODEL:GATE_WIDTH] * yf
    mix = jnp.dot(merged.astype(BF16), wo_ref[...], preferred_element_type=F32)
    h1 = _ln(alpha * h_ref[...] + mix, g_ref[...], b_ref[...])
    h1_ref[...] = h1
    hi, lo = _split_bf16(h1)
    h1b_ref[...] = hi
    lr =(lax.dot_general(wrth_ref[...], hi, NT_DIMS, preferred_element_type=F32)
          + lax.dot_general(wrtl_ref[...], hi, NT_DIMS, preferred_element_type=F32)
          + lax.dot_general(wrth_ref[...], lo, NT_DIMS, preferred_element_type=F32))
    er = jnp.exp(lr - jnp.max(lr, axis=0, keepdims=True))
    affr_ref[...] = er / jnp.sum(er, axis=0, keepdims=True)


def _outproj(attn, four, gates, h, wa, wf, wo, g, b, w_router, alpha, tm):
    n = h.shape[0]
    row = lambda i: (i, 0)
    const = lambda i: (0, 0)
    wrth, wrtl = _split_bf16(w_router.T)
    return pl.pallas_call(
        functools.partial(_outproj_kernel, alpha=alpha),
        grid=(n // tm,),
        in_specs=[
            pl.BlockSpec((tm, ATTN_WIDTH), row),
            pl.BlockSpec((tm, FOURIER_WIDTH), row),
            pl.BlockSpec((tm, GATE_WIDTH), row),
            pl.BlockSpec((tm, D_MODEL), row),
            pl.BlockSpec((ATTN_WIDTH, D_MODEL), const),
            pl.BlockSpec((FOURIER_WIDTH, D_MODEL), const),
            pl.BlockSpec((D_MODEL, D_MODEL), const),
            pl.BlockSpec((1, D_MODEL), const),
            pl.BlockSpec((1, D_MODEL), const),
            pl.BlockSpec((N_EXPERTS, D_MODEL), const),
            pl.BlockSpec((N_EXPERTS, D_MODEL), const),
        ],
        out_specs=[
            pl.BlockSpec((tm, D_MODEL), row),
            pl.BlockSpec((tm, D_MODEL), row),
            pl.BlockSpec((N_EXPERTS, tm), lambda i: (0, i)),
        ],
        out_shape=[
            jax.ShapeDtypeStruct((n, D_MODEL), F32),
            jax.ShapeDtypeStruct((n, D_MODEL), BF16),
            jax.ShapeDtypeStruct((N_EXPERTS, n), F32),
        ],
        compiler_params=_params("parallel"),
        name="outproj_ln_router",
    )(attn, four, gates, h, wa, wf, wo, g.reshape(1, -1), b.reshape(1, -1), wrth, wrtl)


def _topk_kernel(aff_ref, rr_ref, rc_ref, cnt_ref, cntc_ref, *, lead, lp, cap):
    nchunk = lp // LANES
    lane = lax.broadcasted_iota(jnp.int32, (N_EXPERTS, lp), 1)
    bits = jnp.where(lane >= lead, pltpu.bitcast(aff_ref[...], jnp.int32), -1)

    def search(i, t):
        cand = t | (jnp.int32(1) << (30 - i))
        cnt = jnp.sum((bits >= cand).astype(jnp.int32), axis=-1, keepdims=True)
        return jnp.where(cnt >= cap, cand, t)

    thr = lax.fori_loop(0, 31, search, jnp.zeros((N_EXPERTS, 1), jnp.int32))
    gt = bits > thr
    eq = bits == thr
    need = cap - jnp.sum(gt.astype(jnp.int32), axis=-1, keepdims=True)

    r = lax.broadcasted_iota(jnp.int32, (LANES, LANES), 0)
    c = lax.broadcasted_iota(jnp.int32, (LANES, LANES), 1)
    upper = (r <= c).astype(BF16)
    lower = (c <= r).astype(BF16)
    ident = (c == r).astype(BF16)

    eqb = eq.astype(BF16)
    off = jnp.zeros((N_EXPERTS, 1), F32)
    needf = need.astype(F32)
    sel_chunks = []
    for k in range(nchunk):
        sl = slice(k * LANES, (k + 1) * LANES)
        pre = jnp.dot(eqb[:, sl], upper, preferred_element_type=F32) + off
        off = pre[:, LANES - 1:LANES]
        sel_chunks.append(gt[:, sl] | (eq[:, sl] & (pre <= needf)))

    off_r = jnp.zeros((N_EXPERTS, 1), F32)
    off_c = jnp.zeros((1, N_EXPERTS), F32)
    cnt_lane = lax.broadcasted_iota(jnp.int32, (N_EXPERTS, LANES), 1)
    cnt = jnp.zeros((N_EXPERTS, LANES), F32)
    for k in range(nchunk):
        sl = slice(k * LANES, (k + 1) * LANES)
        sel = sel_chunks[k]
        selb = sel.astype(BF16)
        pre_r = jnp.dot(selb, upper, preferred_element_type=F32) + off_r
        rr_ref[k] = jnp.where(sel, pre_r - 1.0, -1.0).astype(jnp.int32)
        off_r = pre_r[:, LANES - 1:LANES]
        cnt = jnp.where(cnt_lane == k + 1, off_r, cnt)
        pre_c = lax.dot_general(lower, selb, NT_DIMS, preferred_element_type=F32) + off_c
        sel_c = lax.dot_general(ident, selb, NT_DIMS, preferred_element_type=F32)
        rc_ref[sl, :] = jnp.where(sel_c > 0.5, pre_c - 1.0, -1.0).astype(jnp.int32)
        off_c = pre_c[LANES - 1:LANES, :]
        cntc_ref[k + 1:k + 2, :] = off_c.astype(jnp.int32)
    cnt_ref[...] = cnt.astype(jnp.int32)
    cntc_ref[0:1, :] = jnp.zeros((1, N_EXPERTS), jnp.int32)
    pad_rows = cntc_ref.shape[0] - nchunk - 1
    if pad_rows:
        cntc_ref[nchunk + 1:, :] = jnp.zeros((pad_rows, N_EXPERTS), jnp.int32)


def _topk(aff_r, B, lp, lead, cap):
    n = B * lp
    nchunk = lp // LANES
    assert nchunk + 1 <= LANES
    nb1 = -(-(nchunk + 1) // SUBLANES) * SUBLANES
    return pl.pallas_call(
        functools.partial(_topk_kernel, lead=lead, lp=lp, cap=cap),
        grid=(B,),
        in_specs=[pl.BlockSpec((N_EXPERTS, lp), lambda b: (0, b))],
        out_specs=[
            pl.BlockSpec((None, nchunk, N_EXPERTS, LANES), lambda b: (b, 0, 0, 0)),
            pl.BlockSpec((lp, N_EXPERTS), lambda b: (b, 0)),
            pl.BlockSpec((None, N_EXPERTS, LANES), lambda b: (b, 0, 0)),
            pl.BlockSpec((None, nb1, N_EXPERTS), lambda b: (b, 0, 0)),
        ],
        out_shape=[
            jax.ShapeDtypeStruct((B, nchunk, N_EXPERTS, LANES), jnp.int32),
            jax.ShapeDtypeStruct((n, N_EXPERTS), jnp.int32),
            jax.ShapeDtypeStruct((B, N_EXPERTS, LANES), jnp.int32),
            jax.ShapeDtypeStruct((B, nb1, N_EXPERTS), jnp.int32),
        ],
        compiler_params=_params("parallel"),
        name="topk_select",
    )(aff_r)


SLOT_ALIGN = 16


def _window_constants(win):
    wide = N_EXPERTS * win
    expand = np.zeros((N_EXPERTS, wide), np.float32)
    for e in range(N_EXPERTS):
        expand[e, e * win:(e + 1) * win] = 1.0
    return expand, (np.arange(wide) % win).astype(np.float32)


def _gather_kernel(cnt_ref, cntv_ref, rank_ref, aff_ref, h_ref, expand_ref, pos_ref, o_ref, g_ref,
                   *, slots, ch, win, nchunk):
    bi, c = pl.program_id(0), pl.program_id(1)
    k0 = c * ch
    tt = ch * LANES
    max_start = slots - win

    @pl.when(c == 0)
    def _():
        o_ref[...] = jnp.zeros_like(o_ref)
        g_ref[...] = jnp.zeros_like(g_ref)

    lane = lax.broadcasted_iota(jnp.int32, cntv_ref.shape, 1)
    lo_col = jnp.sum(jnp.where(lane == k0, cntv_ref[...], 0), axis=-1, keepdims=True)
    w_col = jnp.minimum(lo_col & -SLOT_ALIGN, max_start)
    rank = rank_ref[...]
    aff = aff_ref[...]
    rel = rank - w_col
    in_win = (rank >= 0) & (rel >= 0) & (rel < win)
    relb = jnp.where(in_win, rel, -1).astype(F32).astype(BF16)
    rel_tall = jnp.dot(expand_ref[...], relb, preferred_element_type=F32)
    hit_all = rel_tall == pos_ref[...]
    rows = h_ref[...]
    xw = jnp.dot(hit_all.astype(BF16), rows, preferred_element_type=F32)

    def add_rows(e, start, hit, vals):
        dst = pl.ds(pl.multiple_of(start, SLOT_ALIGN), win)
        o_ref[e, dst, :] = o_ref[e, dst, :] + vals.astype(BF16)
        g_ref[e, dst, :] += jnp.sum(jnp.where(hit, aff[e:e + 1, :], 0.0), axis=-1, keepdims=True)

    row = lax.broadcasted_iota(jnp.int32, (win, tt), 0)
    for e in range(N_EXPERTS):
        base = (bi * N_EXPERTS + e) * (nchunk + 1) + k0
        w = jnp.minimum(cnt_ref[base] & -SLOT_ALIGN, max_start)
        add_rows(e, w, hit_all[e * win:(e + 1) * win, :], xw[e * win:(e + 1) * win, :])
        hi = cnt_ref[base + ch]

        @pl.when(hi > w + win)
        def _(e=e, w=w, hi=hi):
            rk = rank[e:e + 1, :]

            def extra(i, carry):
                w2 = w + (i + 1) * win
                w2c = jnp.minimum(w2, max_start)
                hit = ((rk - w2c) == row) & (rk >= w2)
                add_rows(e, w2c, hit, jnp.dot(hit.astype(BF16), rows, preferred_element_type=F32))
                return carry

            lax.fori_loop(0, (hi - w - 1) // win, extra, 0)


def _gather(cnt, cntv, rank_rows, aff_rows, h1b, B, lp, slots, ch, win):
    nchunk = lp // LANES
    tt = ch * LANES
    nt = nchunk // ch
    assert slots % SLOT_ALIGN == 0 and win % SLOT_ALIGN == 0 and win <= slots
    expand, pos = _window_constants(win)
    wide = N_EXPERTS * win
    const = lambda bi, c, cnt: (0, 0)
    per_seq = lambda bi, c, cnt: (bi, 0, 0, 0)
    return pl.pallas_call(
        functools.partial(_gather_kernel, slots=slots, ch=ch, win=win, nchunk=nchunk),
        grid_spec=pltpu.PrefetchScalarGridSpec(
            num_scalar_prefetch=1,
            grid=(B, nt),
            in_specs=[
                pl.BlockSpec((None, N_EXPERTS, LANES), lambda bi, c, cnt: (bi, 0, 0)),
                pl.BlockSpec((None, N_EXPERTS, tt), lambda bi, c, cnt: (bi, 0, c)),
                pl.BlockSpec((None, N_EXPERTS, tt), lambda bi, c, cnt: (bi, 0, c)),
                pl.BlockSpec((tt, D_MODEL), lambda bi, c, cnt: (bi * nt + c, 0)),
                pl.BlockSpec((wide, N_EXPERTS), const),
                pl.BlockSpec((wide, 1), const),
            ],
            out_specs=[pl.BlockSpec((None, N_EXPERTS, slots, D_MODEL), per_seq),
                       pl.BlockSpec((None, N_EXPERTS, slots, 1), per_seq)],
        ),
        out_shape=[jax.ShapeDtypeStruct((B, N_EXPERTS, slots, D_MODEL), BF16),
                   jax.ShapeDtypeStruct((B, N_EXPERTS, slots, 1), F32)],
        compiler_params=_params("parallel", "arbitrary"),
        name="moe_gather",
    )(cnt, cntv, rank_rows, aff_rows, h1b, jnp.asarray(expand.T, dtype=BF16), jnp.asarray(pos.reshape(wide, 1)))


def _ffn_up_kernel(x_ref, wg_ref, wu_ref, o_ref, wgb_ref, wub_ref):
    @pl.when(pl.program_id(1) == 0)
    def _():
        wgb_ref[...] = wg_ref[...].astype(BF16)
        wub_ref[...] = wu_ref[...].astype(BF16)

    nseq, slots, width = x_ref.shape
    x = x_ref[...].reshape(nseq * slots, width)
    a = jnp.dot(x, wgb_ref[...], preferred_element_type=F32)
    u = jnp.dot(x, wub_ref[...], preferred_element_type=F32)
    o_ref[...] = (a * jax.nn.sigmoid(a) * u).astype(BF16).reshape(o_ref.shape)


def _ffn_down_kernel(x_ref, g_ref, wd_ref, o_ref, wdb_ref):
    @pl.when(pl.program_id(1) == 0)
    def _():
        wdb_ref[...] = wd_ref[...].astype(BF16)

    nseq, slots, width = x_ref.shape
    y = jnp.dot(x_ref[...].reshape(nseq * slots, width), wdb_ref[...], preferred_element_type=F32)
    y = y * g_ref[...].reshape(nseq * slots, 1)
    o_ref[...] = y.astype(BF16).reshape(o_ref.shape)


def _ffn(xg, gates, wg, wu, wd, layer):
    B, _, slots, _ = xg.shape
    act = lambda width: pl.BlockSpec((None, None, slots, width), lambda e, b: (b, e, 0, 0))
    nseq = _pick_tile(B, (2, 1))
    acts = lambda width: pl.BlockSpec((nseq, None, slots, width), lambda e, b: (b, e, 0, 0))
    weight = lambda rows, cols: pl.BlockSpec((None, None, rows, cols), lambda e, b: (layer, e, 0, 0))
    mid = pl.pallas_call(
        _ffn_up_kernel,
        grid=(N_EXPERTS, B // nseq),
        in_specs=[acts(D_MODEL), weight(D_MODEL, D_FF_EXPERT), weight(D_MODEL, D_FF_EXPERT)],
        out_specs=acts(D_FF_EXPERT),
        out_shape=jax.ShapeDtypeStruct((B, N_EXPERTS, slots, D_FF_EXPERT), BF16),
        scratch_shapes=[pltpu.VMEM((D_MODEL, D_FF_EXPERT), BF16), pltpu.VMEM((D_MODEL, D_FF_EXPERT), BF16)],
        compiler_params=_params("arbitrary", "arbitrary"),
        name="moe_ffn_up",
    )(xg, wg, wu)
    return pl.pallas_call(
        _ffn_down_kernel,
        grid=(N_EXPERTS, B // nseq),
        in_specs=[acts(D_FF_EXPERT), acts(1), weight(D_FF_EXPERT, D_MODEL)],
        out_specs=acts(D_MODEL),
        out_shape=jax.ShapeDtypeStruct(xg.shape, BF16),
        scratch_shapes=[pltpu.VMEM((D_FF_EXPERT, D_MODEL), BF16)],
        compiler_params=_params("arbitrary", "arbitrary"),
        name="moe_ffn_down",
    )(mid, gates, wd)


def _combine_kernel(cnt_ref, cntc_ref, rank_ref, y_ref, h_ref, g_ref, b_ref, expand_ref, rpat_ref,
                    o_ref, acc_ref, *, slots, ch, win, nchunk, alpha):
    bi, c = pl.program_id(0), pl.program_id(1)
    k0 = c * ch
    tt = ch * LANES
    max_start = slots - win
    rank = rank_ref[...]
    lo_row = cntc_ref[pl.ds(k0, 1), :]
    w_row = jnp.minimum(lo_row & -SLOT_ALIGN, max_start)
    rel = rank - w_row
    in_win = (rank >= 0) & (rel >= 0) & (rel < win)
    relb = jnp.where(in_win, rel, -1).astype(F32).astype(BF16)
    rel_wide = jnp.dot(relb, expand_ref[...], preferred_element_type=F32)
    onehot = (rel_wide == rpat_ref[...]).astype(BF16)

    starts, windows = [], []
    for e in range(N_EXPERTS):
        lo = cnt_ref[(bi * N_EXPERTS + e) * (nchunk + 1) + k0]
        w = jnp.minimum(lo & -SLOT_ALIGN, max_start)
        starts.append(w)
        windows.append(y_ref[e, pl.ds(pl.multiple_of(w, SLOT_ALIGN), win), :])
    acc_ref[...] = jnp.dot(onehot, jnp.concatenate(windows, axis=0), preferred_element_type=F32)

    lane = lax.broadcasted_iota(jnp.int32, rank.shape, 1)
    col = lax.broadcasted_iota(jnp.int32, (tt, win), 1)
    for e in range(N_EXPERTS):
        hi = cnt_ref[(bi * N_EXPERTS + e) * (nchunk + 1) + k0 + ch]

        @pl.when(hi > starts[e] + win)
        def _(e=e, hi=hi):
            rk = jnp.sum(jnp.where(lane == e, rank, 0), axis=-1, keepdims=True)

            def extra(i, carry):
                w2 = starts[e] + (i + 1) * win
                w2c = jnp.minimum(w2, max_start)
                rows = y_ref[e, pl.ds(pl.multiple_of(w2c, SLOT_ALIGN), win), :]
                hit = ((rk - w2c) == col) & (rk >= w2)
                acc_ref[...] += jnp.dot(hit.astype(BF16), rows, preferred_element_type=F32)
                return carry

            lax.fori_loop(0, (hi - starts[e] - 1) // win, extra, 0)

    o_ref[...] = _ln(alpha * h_ref[...] + acc_ref[...], g_ref[...], b_ref[...])


def _combine(cnt, cntc, rank_c, y, h1, g, b, B, lp, alpha, ch, win):
    n = B * lp
    slots = y.shape[2]
    nchunk = lp // LANES
    tt = ch * LANES
    nt = nchunk // ch
    assert slots % SLOT_ALIGN == 0 and win % SLOT_ALIGN == 0 and win <= slots
    wide = N_EXPERTS * win
    expand, pos = _window_constants(win)
    tok = lambda bi, c, cnt: (bi * nt + c, 0)
    const = lambda bi, c, cnt: (0, 0)
    return pl.pallas_call(
        functools.partial(_combine_kernel, slots=slots, ch=ch, win=win, nchunk=nchunk, alpha=alpha),
        grid_spec=pltpu.PrefetchScalarGridSpec(
            num_scalar_prefetch=1,
            grid=(B, nt),
            in_specs=[
                pl.BlockSpec((None, cntc.shape[1], N_EXPERTS), lambda bi, c, cnt: (bi, 0, 0)),
                pl.BlockSpec((tt, N_EXPERTS), tok),
                pl.BlockSpec((None, N_EXPERTS, slots, D_MODEL), lambda bi, c, cnt: (bi, 0, 0, 0)),
                pl.BlockSpec((tt, D_MODEL), tok),
                pl.BlockSpec((1, D_MODEL), const),
                pl.BlockSpec((1, D_MODEL), const),
                pl.BlockSpec((N_EXPERTS, wide), const),
                pl.BlockSpec((1, wide), const),
            ],
            out_specs=pl.BlockSpec((tt, D_MODEL), tok),
            scratch_shapes=[pltpu.VMEM((tt, D_MODEL), F32)],
        ),
        out_shape=jax.ShapeDtypeStruct((n, D_MODEL), F32),
        compiler_params=_params("parallel", "arbitrary"),
        name="moe_combine_ln",
    )(cnt, cntc, rank_c, y, h1, g.reshape(1, -1), b.reshape(1, -1),
      jnp.asarray(expand, dtype=BF16), jnp.asarray(pos.reshape(1, wide)))


def kernel(x, meta, ln0_g, ln0_b, w_in, b_gate, sink, w_attn_o, w_four_o, w_out, ln1_g, ln1_b,
           w_router, w_e_gate, w_e_up, w_e_down, ln2_g, ln2_b):
    B, seq, d = x.shape
    depth = w_in.shape[0]
    assert d == D_MODEL and meta.shape == (N_META, D_MODEL)
    assert seq % BLOCK == 0 and N_META % SUBLANES == 0 and N_META <= BLOCK
    L = seq + N_META
    nb = -(-L // BLOCK)
    lp = nb * BLOCK
    lead = lp - L
    cap = CAPACITY_FACTOR * L // N_EXPERTS
    slots = -(-cap // SLOT_ALIGN) * SLOT_ALIGN
    alpha = float((2 * depth) ** 0.25)
    n = B * lp
    nchunk = lp // LANES
    tm = _pick_tile(n, (512, 256, 128))
    tf = _pick_tile(lp, (384, 128))
    ch = _pick_tile(nchunk, (3, 2, 1))
    win = min(slots, -(-(ch * LANES * cap // L * 4 // 3 + SLOT_ALIGN) // SLOT_ALIGN) * SLOT_ALIGN)

    gi = np.arange(FOURIER_GROUP)
    ang = 2.0 * np.pi * ((gi[:, None] * gi[None, :]) % FOURIER_GROUP) / FOURIER_GROUP
    cs = jnp.asarray(np.concatenate([np.cos(ang), np.sin(ang)], axis=1) * FOURIER_GROUP ** -0.5, dtype=BF16)
    wdft = _dft_matrix(_fourier_steps(lp, lead, L, tf) * tf, lp, lead, L)
    bias, variant_of_block = _attn_bias_tables(nb, lead)

    h = _embed(x, meta, ln0_g, ln0_b, nb, lead).reshape(n, D_MODEL)
    for l in range(depth):
        q, kv, pq, gates = _inproj(h, w_in[l].astype(BF16), b_gate[l], cs, tm)
        attn = _attention(q.reshape(B, lp, -1), kv.reshape(B, lp, -1), sink[l], bias, variant_of_block, nb)
        four = _fourier(wdft, pq.reshape(B, lp, -1), lead, L, tf)
        h1, h1b, aff_r = _outproj(
            attn.reshape(n, -1), four.reshape(n, -1), gates, h,
            w_attn_o[l].astype(BF16), w_four_o[l].astype(BF16), w_out[l].astype(BF16),
            ln1_g[l], ln1_b[l], w_router[l], alpha, tm)
        rank_r, rank_c, cntv, cntc = _topk(aff_r, B, lp, lead, cap)
        cnt = cntv[:, :, :nchunk + 1].reshape(-1)
        rank_rows = rank_r.transpose(0, 2, 1, 3).reshape(B, N_EXPERTS, lp)
        aff_rows = aff_r.reshape(N_EXPERTS, B, lp).transpose(1, 0, 2)
        xg, gsel = _gather(cnt, cntv, rank_rows, aff_rows, h1b, B, lp, slots, ch, win)
        y = _ffn(xg, gsel, w_e_gate, w_e_up, w_e_down, l)
        h = _combine(cnt, cntc, rank_c, y, h1, ln2_g[l], ln2_b[l], B, lp, alpha, ch, win)
    return h.reshape(B, lp, D_MODEL)[:, lead + N_META:]
```

```python
import functools
import math

import numpy as np
import jax
import jax.numpy as jnp
from jax import lax
from jax.experimental import pallas as pl
from jax.experimental.pallas import tpu as pltpu

D_MODEL = 1024
N_META = 16
N_HEADS = 8
N_KV_HEADS = 2
HEAD_DIM = 64
GQA_GROUP = N_HEADS // N_KV_HEADS
ATTN_WIDTH = N_HEADS * HEAD_DIM
KV_WIDTH = N_KV_HEADS * HEAD_DIM
WINDOW = 128
BLOCK = 128
N_FOURIER_GROUPS = 4
FOURIER_GROUP = 128
FOURIER_WIDTH = N_FOURIER_GROUPS * FOURIER_GROUP
N_BRANCHES = 2
GATE_WIDTH = N_BRANCHES * D_MODEL
N_EXPERTS = 16
CAPACITY_FACTOR = 2
D_FF_EXPERT = 1536
LN_EPS = 1e-5
NEG_INF = -1e30
Q_END = ATTN_WIDTH
K_END = Q_END + KV_WIDTH
V_END = K_END + KV_WIDTH
F_END = V_END + FOURIER_WIDTH
IN_WIDTH = F_END + GATE_WIDTH

LANES = 128
SUBLANES = 8
VMEM_LIMIT_BYTES = 56 * 1024 * 1024

F32 = jnp.float32
BF16 = jnp.bfloat16
NT_DIMS = (((1,), (1,)), ((), ()))


def _pick_tile(n, candidates):
    for c in candidates:
        if n % c == 0:
            return c
    raise ValueError(f"no tile in {candidates} divides {n}")


def _params(*sem, flags=None):
    return pltpu.CompilerParams(dimension_semantics=sem, vmem_limit_bytes=VMEM_LIMIT_BYTES, flags=flags)


def _ln(x, g, b):
    mu = jnp.mean(x, axis=-1, keepdims=True)
    xc = x - mu
    var = jnp.mean(xc * xc, axis=-1, keepdims=True)
    return xc * lax.rsqrt(var + LN_EPS) * g + b


def _embed_kernel(*refs, lead, group):
    x_refs, (meta_ref, g_ref, b_ref, o_ref) = refs[:group], refs[group:]
    for k in range(1, group):
        o_ref[k * BLOCK:(k + 1) * BLOCK, :] = _ln(x_refs[k][...], g_ref[...], b_ref[...])

    @pl.when(pl.program_id(1) == 0)
    def _():
        o_ref[0:lead, :] = jnp.zeros((lead, D_MODEL), F32)
        o_ref[lead:BLOCK, :] = _ln(meta_ref[...], g_ref[...], b_ref[...])

    @pl.when(pl.program_id(1) > 0)
    def _():
        o_ref[0:BLOCK, :] = _ln(x_refs[0][...], g_ref[...], b_ref[...])


def _embed(x, meta, g, b, nb, lead):
    B = x.shape[0]
    group = _pick_tile(nb, (3, 1))
    x_spec = lambda k: pl.BlockSpec((None, BLOCK, D_MODEL),
                                    lambda bi, j: (bi, jnp.maximum(group * j + k - 1, 0), 0))
    const = lambda bi, j: (0, 0)
    return pl.pallas_call(
        functools.partial(_embed_kernel, lead=lead, group=group),
        grid=(B, nb // group),
        in_specs=[x_spec(k) for k in range(group)] + [
            pl.BlockSpec((N_META, D_MODEL), const),
            pl.BlockSpec((1, D_MODEL), const),
            pl.BlockSpec((1, D_MODEL), const),
        ],
        out_specs=pl.BlockSpec((None, group * BLOCK, D_MODEL), lambda bi, j: (bi, j, 0)),
        out_shape=jax.ShapeDtypeStruct((B, nb * BLOCK, D_MODEL), F32),
        compiler_params=_params("parallel", "arbitrary"),
        name="embed_ln",
    )(*([x] * group), meta, g.reshape(1, -1), b.reshape(1, -1))


def _inproj_kernel(h_ref, w_ref, bg_ref, cs_ref, q_ref, kv_ref, pq_ref, gate_ref):
    hb = h_ref[...].astype(BF16)
    q_ref[...] = jnp.dot(hb, w_ref[:, 0:Q_END], preferred_element_type=F32).astype(BF16)
    kv_ref[...] = jnp.dot(hb, w_ref[:, Q_END:V_END], preferred_element_type=F32).astype(BF16)
    uf = jnp.dot(hb, w_ref[:, V_END:F_END], preferred_element_type=F32).astype(BF16)
    for g in range(N_FOURIER_GROUPS):
        lo = g * FOURIER_GROUP
        pq = jnp.dot(uf[:, lo:lo + FOURIER_GROUP], cs_ref[...], preferred_element_type=F32)
        pq_ref[:, lo:lo + FOURIER_GROUP] = pq[:, 0:FOURIER_GROUP].astype(BF16)
        pq_ref[:, FOURIER_WIDTH + lo:FOURIER_WIDTH + lo + FOURIER_GROUP] = (
            pq[:, FOURIER_GROUP:2 * FOURIER_GROUP].astype(BF16))
    chunk = 512
    for c in range(GATE_WIDTH // chunk):
        lo = c * chunk
        ug = jnp.dot(hb, w_ref[:, F_END + lo:F_END + lo + chunk], preferred_element_type=F32)
        gate_ref[:, lo:lo + chunk] = jax.nn.sigmoid(ug + bg_ref[:, lo:lo + chunk])


def _inproj(h, w_bf16, b_gate, cs, tm):
    n = h.shape[0]
    row = lambda i: (i, 0)
    const = lambda i: (0, 0)
    return pl.pallas_call(
        _inproj_kernel,
        grid=(n // tm,),
        in_specs=[
            pl.BlockSpec((tm, D_MODEL), row),
            pl.BlockSpec((D_MODEL, IN_WIDTH), const),
            pl.BlockSpec((1, GATE_WIDTH), const),
            pl.BlockSpec((FOURIER_GROUP, 2 * FOURIER_GROUP), const),
        ],
        out_specs=[
            pl.BlockSpec((tm, ATTN_WIDTH), row),
            pl.BlockSpec((tm, 2 * KV_WIDTH), row),
            pl.BlockSpec((tm, 2 * FOURIER_WIDTH), row),
            pl.BlockSpec((tm, GATE_WIDTH), row),
        ],
        out_shape=[
            jax.ShapeDtypeStruct((n, ATTN_WIDTH), BF16),
            jax.ShapeDtypeStruct((n, 2 * KV_WIDTH), BF16),
            jax.ShapeDtypeStruct((n, 2 * FOURIER_WIDTH), BF16),
            jax.ShapeDtypeStruct((n, GATE_WIDTH), F32),
        ],
        compiler_params=_params("parallel"),
        name="inproj",
    )(h, w_bf16, b_gate.reshape(1, GATE_WIDTH), cs)


def _attn_kernel(sink_ref, q_ref, kvp_ref, kvc_ref, kvn_ref, bias_ref, o_ref):
    kv = jnp.concatenate([kvp_ref[...], kvc_ref[...], kvn_ref[...]], axis=0)
    low_half = lax.broadcasted_iota(jnp.int32, (3 * BLOCK, LANES), 1) < HEAD_DIM

    def lane_half_operands(x):
        swapped = jnp.concatenate([x[:, HEAD_DIM:], x[:, :HEAD_DIM]], axis=1)
        zero = jnp.zeros_like(x)
        return {(kvh, half): jnp.where(low_half if half == 0 else ~low_half,
                                       x if kvh == half else swapped, zero)
                for kvh in range(N_KV_HEADS) for half in range(2)}

    k_ops = lane_half_operands(kv[:, 0:KV_WIDTH])
    v_ops = lane_half_operands(kv[:, KV_WIDTH:2 * KV_WIDTH])

    def scores(h):
        pair = h // 2
        qp = q_ref[:, pair * LANES:(pair + 1) * LANES]
        return lax.dot_general(qp, k_ops[(h // GQA_GROUP, h % 2)], NT_DIMS, preferred_element_type=F32)

    def head_out(h, s):
        sink = sink_ref[h]
        logits = s * (HEAD_DIM ** -0.5) + bias_ref[h]
        m = jnp.maximum(jnp.max(logits, axis=-1, keepdims=True), sink)
        p = jnp.exp(logits - m)
        denom = jnp.sum(p, axis=-1, keepdims=True) + jnp.exp(sink - m)
        o = jnp.dot(p.astype(BF16), v_ops[(h // GQA_GROUP, h % 2)], preferred_element_type=F32)
        return o / denom

    ahead = 5
    pending = [scores(h) for h in range(ahead)]
    for h in range(N_HEADS):
        s_cur = pending.pop(0)
        if h + ahead < N_HEADS:
            pending.append(scores(h + ahead))
        o = head_out(h, s_cur)
        if h % 2 == 0:
            o_even = o
        else:
            pair = h // 2
            o_ref[:, pair * LANES:(pair + 1) * LANES] = (o_even + o).astype(BF16)


def _attn_bias_tables(nb, lead):
    qi = np.arange(BLOCK)[:, None]
    si = np.arange(3 * BLOCK)[None, :]
    rel = np.abs(si - BLOCK - qi).astype(np.float32)
    slopes = np.array([2.0 ** (-8.0 * (h + 1) / N_HEADS) for h in range(N_HEADS)], np.float32)
    base = np.where(rel[None] <= WINDOW, -slopes[:, None, None] * rel[None], np.float32(NEG_INF))
    variants, keys, variant_of_block = [], [], []
    for i in range(nb):
        kpos = (i - 1) * BLOCK + np.arange(3 * BLOCK)
        valid = (kpos >= lead) & (kpos < nb * BLOCK)
        key = valid.tobytes()
        if key not in keys:
            keys.append(key)
            variants.append(np.where(valid[None, None, :], base, np.float32(NEG_INF)))
        variant_of_block.append(keys.index(key))
    return jnp.asarray(np.stack(variants).astype(np.float32)), variant_of_block


def _attention(q, kv, sink, bias, variant_of_block, nb):
    B, lp, _ = q.shape
    kvspec = lambda f: pl.BlockSpec((None, BLOCK, 2 * KV_WIDTH), f)
    interior = max(set(variant_of_block), key=variant_of_block.count)

    def variant(i):
        v = jnp.int32(interior)
        for blk, var in enumerate(variant_of_block):
            if var != interior:
                v = jnp.where(i == blk, var, v)
        return v

    return pl.pallas_call(
        _attn_kernel,
        grid=(B, nb),
        in_specs=[
            pl.BlockSpec(memory_space=pltpu.SMEM),
            pl.BlockSpec((None, BLOCK, ATTN_WIDTH), lambda b, i: (b, i, 0)),
            kvspec(lambda b, i: (b, jnp.maximum(i - 1, 0), 0)),
            kvspec(lambda b, i: (b, i, 0)),
            kvspec(lambda b, i: (b, jnp.minimum(i + 1, nb - 1), 0)),
            pl.BlockSpec((None, N_HEADS, BLOCK, 3 * BLOCK), lambda b, i: (variant(i), 0, 0, 0)),
        ],
        out_specs=pl.BlockSpec((None, BLOCK, ATTN_WIDTH), lambda b, i: (b, i, 0)),
        out_shape=jax.ShapeDtypeStruct((B, lp, ATTN_WIDTH), BF16),
        compiler_params=_params("parallel", "arbitrary"),
        name="attention",
    )(sink, q, kv, kv, kv, bias)


def _fourier_kernel(w_ref, p_ref, q_ref, o_ref, prev_ref, *, lp, lead, seq_len, tf):
    m = pl.program_id(1)
    last = pl.num_programs(1) - 1
    nblk = lp // tf
    shift = lead + 1

    @pl.when(m == 0)
    def _():
        row = lax.broadcasted_iota(jnp.int32, (lp, 1), 0)
        dc = jnp.sum(jnp.where(row >= lead, p_ref[...].astype(F32), 0.0), axis=0, keepdims=True)
        prev_ref[...] = jnp.zeros_like(prev_ref)
        prev_ref[tf - 1:tf, :] = dc * np.float32(seq_len ** -0.5)

    t1 = jnp.dot(w_ref[:, 0:lp], p_ref[...], preferred_element_type=F32)
    t2 = jnp.dot(w_ref[:, lp:2 * lp], q_ref[...], preferred_element_type=F32)
    direct = t1 + t2
    r = lax.broadcasted_iota(jnp.int32, (tf, tf), 0)
    c = lax.broadcasted_iota(jnp.int32, (tf, tf), 1)
    flip = (r + c == tf - 1).astype(BF16)
    mirrored = jnp.dot(flip, (t1 - t2).astype(BF16), preferred_element_type=F32)
    shifted = jnp.concatenate([prev_ref[tf - shift:, :], direct[:tf - shift, :]], axis=0)
    prev_ref[...] = direct

    @pl.when(m < last)
    def _():
        o_ref[pl.ds(pl.multiple_of(m * tf, tf), tf), :] = shifted.astype(BF16)
        o_ref[pl.ds(pl.multiple_of((nblk - 1 - m) * tf, tf), tf), :] = mirrored.astype(BF16)

    @pl.when(m == last)
    def _():
        row = lax.broadcasted_iota(jnp.int32, (tf, 1), 0) + (nblk // 2) * tf
        mid = jnp.where(row <= seq_len // 2 + lead, shifted, mirrored)
        o_ref[(nblk // 2) * tf:(nblk // 2 + 1) * tf, :] = mid.astype(BF16)


def _dft_matrix(nrows, lp, lead, seq_len):
    w = np.float32(2.0 * math.pi / seq_len)
    k = (jnp.arange(nrows, dtype=jnp.int32) + 1)[:, None]
    j = jnp.arange(lp // LANES, dtype=jnp.int32)[None, :]
    r = jnp.arange(LANES, dtype=jnp.int32)[None, :]
    a = ((k * (LANES * j - lead)) % seq_len).astype(F32) * w
    b = ((k * r) % seq_len).astype(F32) * w
    ca, sa, cb, sb = jnp.cos(a)[:, :, None], jnp.sin(a)[:, :, None], jnp.cos(b)[:, None, :], jnp.sin(b)[:, None, :]
    valid = (jnp.arange(lp, dtype=jnp.int32) >= lead).reshape(1, lp // LANES, LANES)
    scale = np.float32(seq_len ** -0.5)
    wc = jnp.where(valid, (ca * cb - sa * sb) * scale, 0.0).reshape(nrows, lp)
    ws = jnp.where(valid, -(sa * cb + ca * sb) * scale, 0.0).reshape(nrows, lp)
    return jnp.concatenate([wc, ws], axis=1).astype(BF16)


def _fourier_steps(lp, lead, seq_len, tf):
    nblk = lp // tf
    assert seq_len % 2 == 0 and lp % tf == 0 and nblk % 2 == 1 and lead + 1 < tf
    steps = nblk // 2 + 1
    assert steps * tf >= seq_len // 2
    return steps


def _fourier(w, pq, lead, seq_len, tf):
    B, lp, _ = pq.shape
    steps = _fourier_steps(lp, lead, seq_len, tf)
    return pl.pallas_call(
        functools.partial(_fourier_kernel, lp=lp, lead=lead, seq_len=seq_len, tf=tf),
        grid=(B, steps),
        in_specs=[
            pl.BlockSpec((tf, 2 * lp), lambda b, m: (m, 0)),
            pl.BlockSpec((None, lp, FOURIER_WIDTH), lambda b, m: (b, 0, 0)),
            pl.BlockSpec((None, lp, FOURIER_WIDTH), lambda b, m: (b, 0, 1)),
        ],
        out_specs=pl.BlockSpec((None, lp, FOURIER_WIDTH), lambda b, m: (b, 0, 0)),
        out_shape=jax.ShapeDtypeStruct((B, lp, FOURIER_WIDTH), BF16),
        scratch_shapes=[pltpu.VMEM((tf, FOURIER_WIDTH), F32)],
        compiler_params=_params("parallel", "arbitrary"),
        name="fourier",
    )(w, pq, pq)


def _split_bf16(x):
    hi = x.astype(BF16)
    lo = (x - hi.astype(F32)).astype(BF16)
    return hi, lo


def _outproj_kernel(attn_ref, four_ref, gate_ref, h_ref, wa_ref, wf_ref, wo_ref, g_ref, b_ref,
                    wrth_ref, wrtl_ref, h1_ref, h1b_ref, affr_ref, *, alpha):
    ya = jnp.dot(attn_ref[...], wa_ref[...], preferred_element_type=F32)
    yf = jnp.dot(four_ref[...], wf_ref[...], preferred_element_type=F32)
    merged = gate_ref[:, 0:D_MODEL] * ya + gate_ref[:, D_MODEL:GATE_WIDTH] * yf
    mix = jnp.dot(merged.astype(BF16), wo_ref[...], preferred_element_type=F32)
    h1 = _ln(alpha * h_ref[...] + mix, g_ref[...], b_ref[...])
    h1_ref[...] = h1
    hi, lo = _split_bf16(h1)
    h1b_ref[...] = hi
    lr =(lax.dot_general(wrth_ref[...], hi, NT_DIMS, preferred_element_type=F32)
          + lax.dot_general(wrtl_ref[...], hi, NT_DIMS, preferred_element_type=F32)
          + lax.dot_general(wrth_ref[...], lo, NT_DIMS, preferred_element_type=F32))
    er = jnp.exp(lr - jnp.max(lr, axis=0, keepdims=True))
    affr_ref[...] = er / jnp.sum(er, axis=0, keepdims=True)


def _outproj(attn, four, gates, h, wa, wf, wo, g, b, w_router, alpha, tm):
    n = h.shape[0]
    row = lambda i: (i, 0)
    const = lambda i: (0, 0)
    wrth, wrtl = _split_bf16(w_router.T)
    return pl.pallas_call(
        functools.partial(_outproj_kernel, alpha=alpha),
        grid=(n // tm,),
        in_specs=[
            pl.BlockSpec((tm, ATTN_WIDTH), row),
            pl.BlockSpec((tm, FOURIER_WIDTH), row),
            pl.BlockSpec((tm, GATE_WIDTH), row),
            pl.BlockSpec((tm, D_MODEL), row),
            pl.BlockSpec((ATTN_WIDTH, D_MODEL), const),
            pl.BlockSpec((FOURIER_WIDTH, D_MODEL), const),
            pl.BlockSpec((D_MODEL, D_MODEL), const),
            pl.BlockSpec((1, D_MODEL), const),
            pl.BlockSpec((1, D_MODEL), const),
            pl.BlockSpec((N_EXPERTS, D_MODEL), const),
            pl.BlockSpec((N_EXPERTS, D_MODEL), const),
        ],
        out_specs=[
            pl.BlockSpec((tm, D_MODEL), row),
            pl.BlockSpec((tm, D_MODEL), row),
            pl.BlockSpec((N_EXPERTS, tm), lambda i: (0, i)),
        ],
        out_shape=[
            jax.ShapeDtypeStruct((n, D_MODEL), F32),
            jax.ShapeDtypeStruct((n, D_MODEL), BF16),
            jax.ShapeDtypeStruct((N_EXPERTS, n), F32),
        ],
        compiler_params=_params("parallel"),
        name="outproj_ln_router",
    )(attn, four, gates, h, wa, wf, wo, g.reshape(1, -1), b.reshape(1, -1), wrth, wrtl)


def _topk_kernel(aff_ref, rr_ref, rc_ref, cnt_ref, cntc_ref, *, lead, lp, cap):
    nchunk = lp // LANES
    lane = lax.broadcasted_iota(jnp.int32, (N_EXPERTS, lp), 1)
    bits = jnp.where(lane >= lead, pltpu.bitcast(aff_ref[...], jnp.int32), -1)

    def search(i, t):
        cand = t | (jnp.int32(1) << (30 - i))
        cnt = jnp.sum((bits >= cand).astype(jnp.int32), axis=-1, keepdims=True)
        return jnp.where(cnt >= cap, cand, t)

    thr = lax.fori_loop(0, 31, search, jnp.zeros((N_EXPERTS, 1), jnp.int32))
    gt = bits > thr
    eq = bits == thr
    need = cap - jnp.sum(gt.astype(jnp.int32), axis=-1, keepdims=True)

    r = lax.broadcasted_iota(jnp.int32, (LANES, LANES), 0)
    c = lax.broadcasted_iota(jnp.int32, (LANES, LANES), 1)
    upper = (r <= c).astype(BF16)
    lower = (c <= r).astype(BF16)
    ident = (c == r).astype(BF16)

    eqb = eq.astype(BF16)
    off = jnp.zeros((N_EXPERTS, 1), F32)
    needf = need.astype(F32)
    sel_chunks = []
    for k in range(nchunk):
        sl = slice(k * LANES, (k + 1) * LANES)
        pre = jnp.dot(eqb[:, sl], upper, preferred_element_type=F32) + off
        off = pre[:, LANES - 1:LANES]
        sel_chunks.append(gt[:, sl] | (eq[:, sl] & (pre <= needf)))

    off_r = jnp.zeros((N_EXPERTS, 1), F32)
    off_c = jnp.zeros((1, N_EXPERTS), F32)
    cnt_lane = lax.broadcasted_iota(jnp.int32, (N_EXPERTS, LANES), 1)
    cnt = jnp.zeros((N_EXPERTS, LANES), F32)
    for k in range(nchunk):
        sl = slice(k * LANES, (k + 1) * LANES)
        sel = sel_chunks[k]
        selb = sel.astype(BF16)
        pre_r = jnp.dot(selb, upper, preferred_element_type=F32) + off_r
        rr_ref[k] = jnp.where(sel, pre_r - 1.0, -1.0).astype(jnp.int32)
        off_r = pre_r[:, LANES - 1:LANES]
        cnt = jnp.where(cnt_lane == k + 1, off_r, cnt)
        pre_c = lax.dot_general(lower, selb, NT_DIMS, preferred_element_type=F32) + off_c
        sel_c = lax.dot_general(ident, selb, NT_DIMS, preferred_element_type=F32)
        rc_ref[sl, :] = jnp.where(sel_c > 0.5, pre_c - 1.0, -1.0).astype(jnp.int32)
        off_c = pre_c[LANES - 1:LANES, :]
        cntc_ref[k + 1:k + 2, :] = off_c.astype(jnp.int32)
    cnt_ref[...] = cnt.astype(jnp.int32)
    cntc_ref[0:1, :] = jnp.zeros((1, N_EXPERTS), jnp.int32)
    pad_rows = cntc_ref.shape[0] - nchunk - 1
    if pad_rows:
        cntc_ref[nchunk + 1:, :] = jnp.zeros((pad_rows, N_EXPERTS), jnp.int32)


def _topk(aff_r, B, lp, lead, cap):
    n = B * lp
    nchunk = lp // LANES
    assert nchunk + 1 <= LANES
    nb1 = -(-(nchunk + 1) // SUBLANES) * SUBLANES
    return pl.pallas_call(
        functools.partial(_topk_kernel, lead=lead, lp=lp, cap=cap),
        grid=(B,),
        in_specs=[pl.BlockSpec((N_EXPERTS, lp), lambda b: (0, b))],
        out_specs=[
            pl.BlockSpec((None, nchunk, N_EXPERTS, LANES), lambda b: (b, 0, 0, 0)),
            pl.BlockSpec((lp, N_EXPERTS), lambda b: (b, 0)),
            pl.BlockSpec((None, N_EXPERTS, LANES), lambda b: (b, 0, 0)),
            pl.BlockSpec((None, nb1, N_EXPERTS), lambda b: (b, 0, 0)),
        ],
        out_shape=[
            jax.ShapeDtypeStruct((B, nchunk, N_EXPERTS, LANES), jnp.int32),
            jax.ShapeDtypeStruct((n, N_EXPERTS), jnp.int32),
            jax.ShapeDtypeStruct((B, N_EXPERTS, LANES), jnp.int32),
            jax.ShapeDtypeStruct((B, nb1, N_EXPERTS), jnp.int32),
        ],
        compiler_params=_params("parallel"),
        name="topk_select",
    )(aff_r)


SLOT_ALIGN = 16


def _window_constants(win):
    wide = N_EXPERTS * win
    expand = np.zeros((N_EXPERTS, wide), np.float32)
    for e in range(N_EXPERTS):
        expand[e, e * win:(e + 1) * win] = 1.0
    return expand, (np.arange(wide) % win).astype(np.float32)


def _gather_kernel(cnt_ref, cntv_ref, rank_ref, aff_ref, h_ref, o_ref, g_ref,
                   *, slots, ch, win, nchunk):
    bi, c = pl.program_id(0), pl.program_id(1)
    k0 = c * ch
    tt = ch * LANES
    max_start = slots - win

    @pl.when(c == 0)
    def _():
        o_ref[...] = jnp.zeros_like(o_ref)
        g_ref[...] = jnp.zeros_like(g_ref)

    lane = lax.broadcasted_iota(jnp.int32, cntv_ref.shape, 1)
    lo_col = jnp.sum(jnp.where(lane == k0, cntv_ref[...], 0), axis=-1, keepdims=True)
    w_col = jnp.minimum(lo_col & -SLOT_ALIGN, max_start)
    rank = rank_ref[...]
    aff = aff_ref[...]
    rel = rank - w_col
    in_win = (rank >= 0) & (rel >= 0) & (rel < win)
    rel = jnp.where(in_win, rel, -1)
    row = lax.broadcasted_iota(jnp.int32, (win, tt), 0)
    hit_all = jnp.concatenate([rel[e:e + 1, :] == row for e in range(N_EXPERTS)], axis=0)
    rows = h_ref[...]
    xw = jnp.dot(hit_all.astype(BF16), rows, preferred_element_type=F32)

    def add_rows(e, start, hit, vals):
        dst = pl.ds(pl.multiple_of(start, SLOT_ALIGN), win)
        o_ref[e, dst, :] = o_ref[e, dst, :] + vals.astype(BF16)
        g_ref[e, dst, :] += jnp.sum(jnp.where(hit, aff[e:e + 1, :], 0.0), axis=-1, keepdims=True)

    for e in range(N_EXPERTS):
        base = (bi * N_EXPERTS + e) * (nchunk + 1) + k0
        w = jnp.minimum(cnt_ref[base] & -SLOT_ALIGN, max_start)
        add_rows(e, w, hit_all[e * win:(e + 1) * win, :], xw[e * win:(e + 1) * win, :])
        hi = cnt_ref[base + ch]

        @pl.when(hi > w + win)
        def _(e=e, w=w, hi=hi):
            rk = rank[e:e + 1, :]

            def extra(i, carry):
                w2 = w + (i + 1) * win
                w2c = jnp.minimum(w2, max_start)
                hit = ((rk - w2c) == row) & (rk >= w2)
                add_rows(e, w2c, hit, jnp.dot(hit.astype(BF16), rows, preferred_element_type=F32))
                return carry

            lax.fori_loop(0, (hi - w - 1) // win, extra, 0)


def _gather(cnt, cntv, rank_rows, aff_rows, h1b, B, lp, slots, ch, win):
    nchunk = lp // LANES
    tt = ch * LANES
    nt = nchunk // ch
    assert slots % SLOT_ALIGN == 0 and win % SLOT_ALIGN == 0 and win <= slots
    per_seq = lambda bi, c, cnt: (bi, 0, 0, 0)
    return pl.pallas_call(
        functools.partial(_gather_kernel, slots=slots, ch=ch, win=win, nchunk=nchunk),
        grid_spec=pltpu.PrefetchScalarGridSpec(
            num_scalar_prefetch=1,
            grid=(B, nt),
            in_specs=[
                pl.BlockSpec((None, N_EXPERTS, LANES), lambda bi, c, cnt: (bi, 0, 0)),
                pl.BlockSpec((None, N_EXPERTS, tt), lambda bi, c, cnt: (bi, 0, c)),
                pl.BlockSpec((None, N_EXPERTS, tt), lambda bi, c, cnt: (bi, 0, c)),
                pl.BlockSpec((tt, D_MODEL), lambda bi, c, cnt: (bi * nt + c, 0)),
            ],
            out_specs=[pl.BlockSpec((None, N_EXPERTS, slots, D_MODEL), per_seq),
                       pl.BlockSpec((None, N_EXPERTS, slots, 1), per_seq)],
        ),
        out_shape=[jax.ShapeDtypeStruct((B, N_EXPERTS, slots, D_MODEL), BF16),
                   jax.ShapeDtypeStruct((B, N_EXPERTS, slots, 1), F32)],
        compiler_params=_params("parallel", "arbitrary"),
        name="moe_gather",
    )(cnt, cntv, rank_rows, aff_rows, h1b)


def _ffn_up_kernel(x_ref, wg_ref, wu_ref, o_ref, wgb_ref, wub_ref):
    @pl.when(pl.program_id(1) == 0)
    def _():
        wgb_ref[...] = wg_ref[...].astype(BF16)
        wub_ref[...] = wu_ref[...].astype(BF16)

    nseq, slots, width = x_ref.shape
    x = x_ref[...].reshape(nseq * slots, width)
    a = jnp.dot(x, wgb_ref[...], preferred_element_type=F32)
    u = jnp.dot(x, wub_ref[...], preferred_element_type=F32)
    o_ref[...] = (a * jax.nn.sigmoid(a) * u).astype(BF16).reshape(o_ref.shape)


def _ffn_down_kernel(x_ref, g_ref, wd_ref, o_ref, wdb_ref):
    @pl.when(pl.program_id(1) == 0)
    def _():
        wdb_ref[...] = wd_ref[...].astype(BF16)

    nseq, slots, width = x_ref.shape
    y = jnp.dot(x_ref[...].reshape(nseq * slots, width), wdb_ref[...], preferred_element_type=F32)
    y = y * g_ref[...].reshape(nseq * slots, 1)
    o_ref[...] = y.astype(BF16).reshape(o_ref.shape)


def _ffn(xg, gates, wg, wu, wd, layer):
    B, _, slots, _ = xg.shape
    n_up = _pick_tile(B, (2, 1))
    n_down = _pick_tile(B, (4, 2, 1))
    acts = lambda nseq, width: pl.BlockSpec((nseq, None, slots, width), lambda e, b: (b, e, 0, 0))
    weight = lambda rows, cols: pl.BlockSpec((None, None, rows, cols), lambda e, b: (layer, e, 0, 0))
    mid = pl.pallas_call(
        _ffn_up_kernel,
        grid=(N_EXPERTS, B // n_up),
        in_specs=[acts(n_up, D_MODEL), weight(D_MODEL, D_FF_EXPERT), weight(D_MODEL, D_FF_EXPERT)],
        out_specs=acts(n_up, D_FF_EXPERT),
        out_shape=jax.ShapeDtypeStruct((B, N_EXPERTS, slots, D_FF_EXPERT), BF16),
        scratch_shapes=[pltpu.VMEM((D_MODEL, D_FF_EXPERT), BF16), pltpu.VMEM((D_MODEL, D_FF_EXPERT), BF16)],
        compiler_params=_params("arbitrary", "arbitrary"),
        name="moe_ffn_up",
    )(xg, wg, wu)
    return pl.pallas_call(
        _ffn_down_kernel,
        grid=(N_EXPERTS, B // n_down),
        in_specs=[acts(n_down, D_FF_EXPERT), acts(n_down, 1), weight(D_FF_EXPERT, D_MODEL)],
        out_specs=acts(n_down, D_MODEL),
        out_shape=jax.ShapeDtypeStruct(xg.shape, BF16),
        scratch_shapes=[pltpu.VMEM((D_FF_EXPERT, D_MODEL), BF16)],
        compiler_params=_params("arbitrary", "arbitrary"),
        name="moe_ffn_down",
    )(mid, gates, wd)


def _combine_kernel(cnt_ref, cntc_ref, rank_ref, y_ref, h_ref, g_ref, b_ref, expand_ref, rpat_ref,
                    o_ref, acc_ref, *, slots, ch, win, nchunk, alpha):
    bi, c = pl.program_id(0), pl.program_id(1)
    k0 = c * ch
    tt = ch * LANES
    max_start = slots - win
    rank = rank_ref[...]
    lo_row = cntc_ref[pl.ds(k0, 1), :]
    w_row = jnp.minimum(lo_row & -SLOT_ALIGN, max_start)
    rel = rank - w_row
    in_win = (rank >= 0) & (rel >= 0) & (rel < win)
    relb = jnp.where(in_win, rel, -1).astype(F32).astype(BF16)
    rel_wide = jnp.dot(relb, expand_ref[...], preferred_element_type=F32)
    onehot = (rel_wide == rpat_ref[...]).astype(BF16)

    starts, windows = [], []
    for e in range(N_EXPERTS):
        lo = cnt_ref[(bi * N_EXPERTS + e) * (nchunk + 1) + k0]
        w = jnp.minimum(lo & -SLOT_ALIGN, max_start)
        starts.append(w)
        windows.append(y_ref[e, pl.ds(pl.multiple_of(w, SLOT_ALIGN), win), :])
    acc_ref[...] = jnp.dot(onehot, jnp.concatenate(windows, axis=0), preferred_element_type=F32)

    lane = lax.broadcasted_iota(jnp.int32, rank.shape, 1)
    col = lax.broadcasted_iota(jnp.int32, (tt, win), 1)
    for e in range(N_EXPERTS):
        hi = cnt_ref[(bi * N_EXPERTS + e) * (nchunk + 1) + k0 + ch]

        @pl.when(hi > starts[e] + win)
        def _(e=e, hi=hi):
            rk = jnp.sum(jnp.where(lane == e, rank, 0), axis=-1, keepdims=True)

            def extra(i, carry):
                w2 = starts[e] + (i + 1) * win
                w2c = jnp.minimum(w2, max_start)
                rows = y_ref[e, pl.ds(pl.multiple_of(w2c, SLOT_ALIGN), win), :]
                hit = ((rk - w2c) == col) & (rk >= w2)
                acc_ref[...] += jnp.dot(hit.astype(BF16), rows, preferred_element_type=F32)
                return carry

            lax.fori_loop(0, (hi - starts[e] - 1) // win, extra, 0)

    o_ref[...] = _ln(alpha * h_ref[...] + acc_ref[...], g_ref[...], b_ref[...])


def _combine(cnt, cntc, rank_c, y, h1, g, b, B, lp, alpha, ch, win):
    n = B * lp
    slots = y.shape[2]
    nchunk = lp // LANES
    tt = ch * LANES
    nt = nchunk // ch
    assert slots % SLOT_ALIGN == 0 and win % SLOT_ALIGN == 0 and win <= slots
    wide = N_EXPERTS * win
    expand, pos = _window_constants(win)
    tok = lambda bi, c, cnt: (bi * nt + c, 0)
    const = lambda bi, c, cnt: (0, 0)
    return pl.pallas_call(
        functools.partial(_combine_kernel, slots=slots, ch=ch, win=win, nchunk=nchunk, alpha=alpha),
        grid_spec=pltpu.PrefetchScalarGridSpec(
            num_scalar_prefetch=1,
            grid=(B, nt),
            in_specs=[
                pl.BlockSpec((None, cntc.shape[1], N_EXPERTS), lambda bi, c, cnt: (bi, 0, 0)),
                pl.BlockSpec((tt, N_EXPERTS), tok),
                pl.BlockSpec((None, N_EXPERTS, slots, D_MODEL), lambda bi, c, cnt: (bi, 0, 0, 0)),
                pl.BlockSpec((tt, D_MODEL), tok),
                pl.BlockSpec((1, D_MODEL), const),
                pl.BlockSpec((1, D_MODEL), const),
                pl.BlockSpec((N_EXPERTS, wide), const),
                pl.BlockSpec((1, wide), const),
            ],
            out_specs=pl.BlockSpec((tt, D_MODEL), tok),
            scratch_shapes=[pltpu.VMEM((tt, D_MODEL), F32)],
        ),
        out_shape=jax.ShapeDtypeStruct((n, D_MODEL), F32),
        compiler_params=_params("parallel", "arbitrary"),
        name="moe_combine_ln",
    )(cnt, cntc, rank_c, y, h1, g.reshape(1, -1), b.reshape(1, -1),
      jnp.asarray(expand, dtype=BF16), jnp.asarray(pos.reshape(1, wide)))


def kernel(x, meta, ln0_g, ln0_b, w_in, b_gate, sink, w_attn_o, w_four_o, w_out, ln1_g, ln1_b,
           w_router, w_e_gate, w_e_up, w_e_down, ln2_g, ln2_b):
    B, seq, d = x.shape
    depth = w_in.shape[0]
    assert d == D_MODEL and meta.shape == (N_META, D_MODEL)
    assert seq % BLOCK == 0 and N_META % SUBLANES == 0 and N_META <= BLOCK
    L = seq + N_META
    nb = -(-L // BLOCK)
    lp = nb * BLOCK
    lead = lp - L
    cap = CAPACITY_FACTOR * L // N_EXPERTS
    slots = -(-cap // SLOT_ALIGN) * SLOT_ALIGN
    alpha = float((2 * depth) ** 0.25)
    n = B * lp
    nchunk = lp // LANES
    tm = _pick_tile(n, (512, 256, 128))
    tf = _pick_tile(lp, (384, 128))
    ch = _pick_tile(nchunk, (3, 2, 1))
    win = min(slots, -(-(ch * LANES * cap // L * 4 // 3 + SLOT_ALIGN) // SLOT_ALIGN) * SLOT_ALIGN)

    gi = np.arange(FOURIER_GROUP)
    ang = 2.0 * np.pi * ((gi[:, None] * gi[None, :]) % FOURIER_GROUP) / FOURIER_GROUP
    cs = jnp.asarray(np.concatenate([np.cos(ang), np.sin(ang)], axis=1) * FOURIER_GROUP ** -0.5, dtype=BF16)
    wdft = _dft_matrix(_fourier_steps(lp, lead, L, tf) * tf, lp, lead, L)
    bias, variant_of_block = _attn_bias_tables(nb, lead)

    h = _embed(x, meta, ln0_g, ln0_b, nb, lead).reshape(n, D_MODEL)
    for l in range(depth):
        q, kv, pq, gates = _inproj(h, w_in[l].astype(BF16), b_gate[l], cs, tm)
        attn = _attention(q.reshape(B, lp, -1), kv.reshape(B, lp, -1), sink[l], bias, variant_of_block, nb)
        four = _fourier(wdft, pq.reshape(B, lp, -1), lead, L, tf)
        h1, h1b, aff_r = _outproj(
            attn.reshape(n, -1), four.reshape(n, -1), gates, h,
            w_attn_o[l].astype(BF16), w_four_o[l].astype(BF16), w_out[l].astype(BF16),
            ln1_g[l], ln1_b[l], w_router[l], alpha, tm)
        rank_r, rank_c, cntv, cntc = _topk(aff_r, B, lp, lead, cap)
        cnt = cntv[:, :, :nchunk + 1].reshape(-1)
        rank_rows = rank_r.transpose(0, 2, 1, 3).reshape(B, N_EXPERTS, lp)
        aff_rows = aff_r.reshape(N_EXPERTS, B, lp).transpose(1, 0, 2)
        xg, gsel = _gather(cnt, cntv, rank_rows, aff_rows, h1b, B, lp, slots, ch, win)
        y = _ffn(xg, gsel, w_e_gate, w_e_up, w_e_down, l)
        h = _combine(cnt, cntc, rank_c, y, h1, ln2_g[l], ln2_b[l], B, lp, alpha, ch, win)
    return h.reshape(B, lp, D_MODEL)[:, lead + N_META:]
```

```python
import functools
import math

import numpy as np
import jax
import jax.numpy as jnp
from jax import lax
from jax.experimental import pallas as pl
from jax.experimental.pallas import tpu as pltpu

D_MODEL = 1024
N_META = 16
N_HEADS = 8
N_KV_HEADS = 2
HEAD_DIM = 64
GQA_GROUP = N_HEADS // N_KV_HEADS
ATTN_WIDTH = N_HEADS * HEAD_DIM
KV_WIDTH = N_KV_HEADS * HEAD_DIM
WINDOW = 128
BLOCK = 128
N_FOURIER_GROUPS = 4
FOURIER_GROUP = 128
FOURIER_WIDTH = N_FOURIER_GROUPS * FOURIER_GROUP
N_BRANCHES = 2
GATE_WIDTH = N_BRANCHES * D_MODEL
N_EXPERTS = 16
CAPACITY_FACTOR = 2
D_FF_EXPERT = 1536
LN_EPS = 1e-5
NEG_INF = -1e30
Q_END = ATTN_WIDTH
K_END = Q_END + KV_WIDTH
V_END = K_END + KV_WIDTH
F_END = V_END + FOURIER_WIDTH
IN_WIDTH = F_END + GATE_WIDTH

LANES = 128
SUBLANES = 8
VMEM_LIMIT_BYTES = 56 * 1024 * 1024

F32 = jnp.float32
BF16 = jnp.bfloat16
NT_DIMS = (((1,), (1,)), ((), ()))


def _pick_tile(n, candidates):
    for c in candidates:
        if n % c == 0:
            return c
    raise ValueError(f"no tile in {candidates} divides {n}")


def _params(*sem, flags=None):
    return pltpu.CompilerParams(dimension_semantics=sem, vmem_limit_bytes=VMEM_LIMIT_BYTES, flags=flags)


def _ln(x, g, b):
    mu = jnp.mean(x, axis=-1, keepdims=True)
    xc = x - mu
    var = jnp.mean(xc * xc, axis=-1, keepdims=True)
    return xc * lax.rsqrt(var + LN_EPS) * g + b


def _embed_kernel(*refs, lead, group):
    x_refs, (meta_ref, g_ref, b_ref, o_ref) = refs[:group], refs[group:]
    for k in range(1, group):
        o_ref[k * BLOCK:(k + 1) * BLOCK, :] = _ln(x_refs[k][...], g_ref[...], b_ref[...])

    @pl.when(pl.program_id(1) == 0)
    def _():
        o_ref[0:lead, :] = jnp.zeros((lead, D_MODEL), F32)
        o_ref[lead:BLOCK, :] = _ln(meta_ref[...], g_ref[...], b_ref[...])

    @pl.when(pl.program_id(1) > 0)
    def _():
        o_ref[0:BLOCK, :] = _ln(x_refs[0][...], g_ref[...], b_ref[...])


def _embed(x, meta, g, b, nb, lead):
    B = x.shape[0]
    group = _pick_tile(nb, (3, 1))
    x_spec = lambda k: pl.BlockSpec((None, BLOCK, D_MODEL),
                                    lambda bi, j: (bi, jnp.maximum(group * j + k - 1, 0), 0))
    const = lambda bi, j: (0, 0)
    return pl.pallas_call(
        functools.partial(_embed_kernel, lead=lead, group=group),
        grid=(B, nb // group),
        in_specs=[x_spec(k) for k in range(group)] + [
            pl.BlockSpec((N_META, D_MODEL), const),
            pl.BlockSpec((1, D_MODEL), const),
            pl.BlockSpec((1, D_MODEL), const),
        ],
        out_specs=pl.BlockSpec((None, group * BLOCK, D_MODEL), lambda bi, j: (bi, j, 0)),
        out_shape=jax.ShapeDtypeStruct((B, nb * BLOCK, D_MODEL), F32),
        compiler_params=_params("parallel", "arbitrary"),
        name="embed_ln",
    )(*([x] * group), meta, g.reshape(1, -1), b.reshape(1, -1))


def _inproj_kernel(h_ref, w_ref, bg_ref, cs_ref, q_ref, kv_ref, pq_ref, gate_ref):
    hb = h_ref[...].astype(BF16)
    q_ref[...] = jnp.dot(hb, w_ref[:, 0:Q_END], preferred_element_type=F32).astype(BF16)
    kv_ref[...] = jnp.dot(hb, w_ref[:, Q_END:V_END], preferred_element_type=F32).astype(BF16)
    uf = jnp.dot(hb, w_ref[:, V_END:F_END], preferred_element_type=F32).astype(BF16)
    for g in range(N_FOURIER_GROUPS):
        lo = g * FOURIER_GROUP
        pq = jnp.dot(uf[:, lo:lo + FOURIER_GROUP], cs_ref[...], preferred_element_type=F32)
        pq_ref[:, lo:lo + FOURIER_GROUP] = pq[:, 0:FOURIER_GROUP].astype(BF16)
        pq_ref[:, FOURIER_WIDTH + lo:FOURIER_WIDTH + lo + FOURIER_GROUP] = (
            pq[:, FOURIER_GROUP:2 * FOURIER_GROUP].astype(BF16))
    chunk = 512
    for c in range(GATE_WIDTH // chunk):
        lo = c * chunk
        ug = jnp.dot(hb, w_ref[:, F_END + lo:F_END + lo + chunk], preferred_element_type=F32)
        gate_ref[:, lo:lo + chunk] = jax.nn.sigmoid(ug + bg_ref[:, lo:lo + chunk]).astype(BF16)


def _inproj(h, w_bf16, b_gate, cs, tm):
    n = h.shape[0]
    row = lambda i: (i, 0)
    const = lambda i: (0, 0)
    return pl.pallas_call(
        _inproj_kernel,
        grid=(n // tm,),
        in_specs=[
            pl.BlockSpec((tm, D_MODEL), row),
            pl.BlockSpec((D_MODEL, IN_WIDTH), const),
            pl.BlockSpec((1, GATE_WIDTH), const),
            pl.BlockSpec((FOURIER_GROUP, 2 * FOURIER_GROUP), const),
        ],
        out_specs=[
            pl.BlockSpec((tm, ATTN_WIDTH), row),
            pl.BlockSpec((tm, 2 * KV_WIDTH), row),
            pl.BlockSpec((tm, 2 * FOURIER_WIDTH), row),
            pl.BlockSpec((tm, GATE_WIDTH), row),
        ],
        out_shape=[
            jax.ShapeDtypeStruct((n, ATTN_WIDTH), BF16),
            jax.ShapeDtypeStruct((n, 2 * KV_WIDTH), BF16),
            jax.ShapeDtypeStruct((n, 2 * FOURIER_WIDTH), BF16),
            jax.ShapeDtypeStruct((n, GATE_WIDTH), BF16),
        ],
        compiler_params=_params("parallel"),
        name="inproj",
    )(h, w_bf16, b_gate.reshape(1, GATE_WIDTH), cs)


def _attn_kernel(sink_ref, q_ref, kvp_ref, kvc_ref, kvn_ref, bias_ref, o_ref):
    kv = jnp.concatenate([kvp_ref[...], kvc_ref[...], kvn_ref[...]], axis=0)
    low_half = lax.broadcasted_iota(jnp.int32, (3 * BLOCK, LANES), 1) < HEAD_DIM

    def lane_half_operands(x):
        swapped = jnp.concatenate([x[:, HEAD_DIM:], x[:, :HEAD_DIM]], axis=1)
        zero = jnp.zeros_like(x)
        return {(kvh, half): jnp.where(low_half if half == 0 else ~low_half,
                                       x if kvh == half else swapped, zero)
                for kvh in range(N_KV_HEADS) for half in range(2)}

    k_ops = lane_half_operands(kv[:, 0:KV_WIDTH])
    v_ops = lane_half_operands(kv[:, KV_WIDTH:2 * KV_WIDTH])

    def scores(h):
        pair = h // 2
        qp = q_ref[:, pair * LANES:(pair + 1) * LANES]
        return lax.dot_general(qp, k_ops[(h // GQA_GROUP, h % 2)], NT_DIMS, preferred_element_type=F32)

    def head_out(h, s):
        sink = sink_ref[h]
        logits = s * (HEAD_DIM ** -0.5) + bias_ref[h]
        m = jnp.maximum(jnp.max(logits, axis=-1, keepdims=True), sink)
        p = jnp.exp(logits - m)
        denom = jnp.sum(p, axis=-1, keepdims=True) + jnp.exp(sink - m)
        o = jnp.dot(p.astype(BF16), v_ops[(h // GQA_GROUP, h % 2)], preferred_element_type=F32)
        return o / denom

    ahead = 5
    pending = [scores(h) for h in range(ahead)]
    for h in range(N_HEADS):
        s_cur = pending.pop(0)
        if h + ahead < N_HEADS:
            pending.append(scores(h + ahead))
        o = head_out(h, s_cur)
        if h % 2 == 0:
            o_even = o
        else:
            pair = h // 2
            o_ref[:, pair * LANES:(pair + 1) * LANES] = (o_even + o).astype(BF16)


def _attn_bias_tables(nb, lead):
    qi = np.arange(BLOCK)[:, None]
    si = np.arange(3 * BLOCK)[None, :]
    rel = np.abs(si - BLOCK - qi).astype(np.float32)
    slopes = np.array([2.0 ** (-8.0 * (h + 1) / N_HEADS) for h in range(N_HEADS)], np.float32)
    base = np.where(rel[None] <= WINDOW, -slopes[:, None, None] * rel[None], np.float32(NEG_INF))
    variants, keys, variant_of_block = [], [], []
    for i in range(nb):
        kpos = (i - 1) * BLOCK + np.arange(3 * BLOCK)
        valid = (kpos >= lead) & (kpos < nb * BLOCK)
        key = valid.tobytes()
        if key not in keys:
            keys.append(key)
            variants.append(np.where(valid[None, None, :], base, np.float32(NEG_INF)))
        variant_of_block.append(keys.index(key))
    return jnp.asarray(np.stack(variants).astype(np.float32)), variant_of_block


def _attention(q, kv, sink, bias, variant_of_block, nb):
    B, lp, _ = q.shape
    kvspec = lambda f: pl.BlockSpec((None, BLOCK, 2 * KV_WIDTH), f)
    interior = max(set(variant_of_block), key=variant_of_block.count)

    def variant(i):
        v = jnp.int32(interior)
        for blk, var in enumerate(variant_of_block):
            if var != interior:
                v = jnp.where(i == blk, var, v)
        return v

    return pl.pallas_call(
        _attn_kernel,
        grid=(B, nb),
        in_specs=[
            pl.BlockSpec(memory_space=pltpu.SMEM),
            pl.BlockSpec((None, BLOCK, ATTN_WIDTH), lambda b, i: (b, i, 0)),
            kvspec(lambda b, i: (b, jnp.maximum(i - 1, 0), 0)),
            kvspec(lambda b, i: (b, i, 0)),
            kvspec(lambda b, i: (b, jnp.minimum(i + 1, nb - 1), 0)),
            pl.BlockSpec((None, N_HEADS, BLOCK, 3 * BLOCK), lambda b, i: (variant(i), 0, 0, 0)),
        ],
        out_specs=pl.BlockSpec((None, BLOCK, ATTN_WIDTH), lambda b, i: (b, i, 0)),
        out_shape=jax.ShapeDtypeStruct((B, lp, ATTN_WIDTH), BF16),
        compiler_params=_params("parallel", "arbitrary"),
        name="attention",
    )(sink, q, kv, kv, kv, bias)


def _fourier_kernel(w_ref, p_ref, q_ref, o_ref, prev_ref, *, lp, lead, seq_len, tf):
    m = pl.program_id(1)
    last = pl.num_programs(1) - 1
    nblk = lp // tf
    shift = lead + 1

    @pl.when(m == 0)
    def _():
        row = lax.broadcasted_iota(jnp.int32, (lp, 1), 0)
        dc = jnp.sum(jnp.where(row >= lead, p_ref[...].astype(F32), 0.0), axis=0, keepdims=True)
        prev_ref[...] = jnp.zeros_like(prev_ref)
        prev_ref[tf - 1:tf, :] = dc * np.float32(seq_len ** -0.5)

    t1 = jnp.dot(w_ref[:, 0:lp], p_ref[...], preferred_element_type=F32)
    t2 = jnp.dot(w_ref[:, lp:2 * lp], q_ref[...], preferred_element_type=F32)
    direct = t1 + t2
    r = lax.broadcasted_iota(jnp.int32, (tf, tf), 0)
    c = lax.broadcasted_iota(jnp.int32, (tf, tf), 1)
    flip = (r + c == tf - 1).astype(BF16)
    mirrored = jnp.dot(flip, (t1 - t2).astype(BF16), preferred_element_type=F32)
    shifted = jnp.concatenate([prev_ref[tf - shift:, :], direct[:tf - shift, :]], axis=0)
    prev_ref[...] = direct

    @pl.when(m < last)
    def _():
        o_ref[pl.ds(pl.multiple_of(m * tf, tf), tf), :] = shifted.astype(BF16)
        o_ref[pl.ds(pl.multiple_of((nblk - 1 - m) * tf, tf), tf), :] = mirrored.astype(BF16)

    @pl.when(m == last)
    def _():
        row = lax.broadcasted_iota(jnp.int32, (tf, 1), 0) + (nblk // 2) * tf
        mid = jnp.where(row <= seq_len // 2 + lead, shifted, mirrored)
        o_ref[(nblk // 2) * tf:(nblk // 2 + 1) * tf, :] = mid.astype(BF16)


def _dft_matrix(nrows, lp, lead, seq_len):
    w = np.float32(2.0 * math.pi / seq_len)
    k = (jnp.arange(nrows, dtype=jnp.int32) + 1)[:, None]
    j = jnp.arange(lp // LANES, dtype=jnp.int32)[None, :]
    r = jnp.arange(LANES, dtype=jnp.int32)[None, :]
    a = ((k * (LANES * j - lead)) % seq_len).astype(F32) * w
    b = ((k * r) % seq_len).astype(F32) * w
    ca, sa, cb, sb = jnp.cos(a)[:, :, None], jnp.sin(a)[:, :, None], jnp.cos(b)[:, None, :], jnp.sin(b)[:, None, :]
    valid = (jnp.arange(lp, dtype=jnp.int32) >= lead).reshape(1, lp // LANES, LANES)
    scale = np.float32(seq_len ** -0.5)
    wc = jnp.where(valid, (ca * cb - sa * sb) * scale, 0.0).reshape(nrows, lp)
    ws = jnp.where(valid, -(sa * cb + ca * sb) * scale, 0.0).reshape(nrows, lp)
    return jnp.concatenate([wc, ws], axis=1).astype(BF16)


def _fourier_steps(lp, lead, seq_len, tf):
    nblk = lp // tf
    assert seq_len % 2 == 0 and lp % tf == 0 and nblk % 2 == 1 and lead + 1 < tf
    steps = nblk // 2 + 1
    assert steps * tf >= seq_len // 2
    return steps


def _fourier(w, pq, lead, seq_len, tf):
    B, lp, _ = pq.shape
    steps = _fourier_steps(lp, lead, seq_len, tf)
    return pl.pallas_call(
        functools.partial(_fourier_kernel, lp=lp, lead=lead, seq_len=seq_len, tf=tf),
        grid=(B, steps),
        in_specs=[
            pl.BlockSpec((tf, 2 * lp), lambda b, m: (m, 0)),
            pl.BlockSpec((None, lp, FOURIER_WIDTH), lambda b, m: (b, 0, 0)),
            pl.BlockSpec((None, lp, FOURIER_WIDTH), lambda b, m: (b, 0, 1)),
        ],
        out_specs=pl.BlockSpec((None, lp, FOURIER_WIDTH), lambda b, m: (b, 0, 0)),
        out_shape=jax.ShapeDtypeStruct((B, lp, FOURIER_WIDTH), BF16),
        scratch_shapes=[pltpu.VMEM((tf, FOURIER_WIDTH), F32)],
        compiler_params=_params("parallel", "arbitrary"),
        name="fourier",
    )(w, pq, pq)


def _split_bf16(x):
    hi = x.astype(BF16)
    lo = (x - hi.astype(F32)).astype(BF16)
    return hi, lo


def _outproj_kernel(attn_ref, four_ref, gate_ref, h_ref, wa_ref, wf_ref, wo_ref, g_ref, b_ref,
                    wrth_ref, wrtl_ref, h1_ref, h1b_ref, affr_ref, *, alpha):
    ya = jnp.dot(attn_ref[...], wa_ref[...], preferred_element_type=F32)
    yf = jnp.dot(four_ref[...], wf_ref[...], preferred_element_type=F32)
    merged = gate_ref[:, 0:D_MODEL].astype(F32) * ya + gate_ref[:, D_MODEL:GATE_WIDTH].astype(F32) * yf
    mix = jnp.dot(merged.astype(BF16), wo_ref[...], preferred_element_type=F32)
    h1 = _ln(alpha * h_ref[...] + mix, g_ref[...], b_ref[...])
    h1_ref[...] = h1
    hi, lo = _split_bf16(h1)
    h1b_ref[...] = hi
    lr =(lax.dot_general(wrth_ref[...], hi, NT_DIMS, preferred_element_type=F32)
          + lax.dot_general(wrtl_ref[...], hi, NT_DIMS, preferred_element_type=F32)
          + lax.dot_general(wrth_ref[...], lo, NT_DIMS, preferred_element_type=F32))
    er = jnp.exp(lr - jnp.max(lr, axis=0, keepdims=True))
    affr_ref[...] = er / jnp.sum(er, axis=0, keepdims=True)


def _outproj(attn, four, gates, h, wa, wf, wo, g, b, w_router, alpha, tm):
    n = h.shape[0]
    row = lambda i: (i, 0)
    const = lambda i: (0, 0)
    wrth, wrtl = _split_bf16(w_router.T)
    return pl.pallas_call(
        functools.partial(_outproj_kernel, alpha=alpha),
        grid=(n // tm,),
        in_specs=[
            pl.BlockSpec((tm, ATTN_WIDTH), row),
            pl.BlockSpec((tm, FOURIER_WIDTH), row),
            pl.BlockSpec((tm, GATE_WIDTH), row),
            pl.BlockSpec((tm, D_MODEL), row),
            pl.BlockSpec((ATTN_WIDTH, D_MODEL), const),
            pl.BlockSpec((FOURIER_WIDTH, D_MODEL), const),
            pl.BlockSpec((D_MODEL, D_MODEL), const),
            pl.BlockSpec((1, D_MODEL), const),
            pl.BlockSpec((1, D_MODEL), const),
            pl.BlockSpec((N_EXPERTS, D_MODEL), const),
            pl.BlockSpec((N_EXPERTS, D_MODEL), const),
        ],
        out_specs=[
            pl.BlockSpec((tm, D_MODEL), row),
            pl.BlockSpec((tm, D_MODEL), row),
            pl.BlockSpec((N_EXPERTS, tm), lambda i: (0, i)),
        ],
        out_shape=[
            jax.ShapeDtypeStruct((n, D_MODEL), F32),
            jax.ShapeDtypeStruct((n, D_MODEL), BF16),
            jax.ShapeDtypeStruct((N_EXPERTS, n), F32),
        ],
        compiler_params=_params("parallel"),
        name="outproj_ln_router",
    )(attn, four, gates, h, wa, wf, wo, g.reshape(1, -1), b.reshape(1, -1), wrth, wrtl)


def _topk_kernel(aff_ref, rr_ref, rc_ref, cnt_ref, cntc_ref, *, lead, lp, cap):
    nchunk = lp // LANES
    lane = lax.broadcasted_iota(jnp.int32, (N_EXPERTS, lp), 1)
    bits = jnp.where(lane >= lead, pltpu.bitcast(aff_ref[...], jnp.int32), -1)

    def search(i, t):
        cand = t | (jnp.int32(1) << (30 - i))
        cnt = jnp.sum((bits >= cand).astype(jnp.int32), axis=-1, keepdims=True)
        return jnp.where(cnt >= cap, cand, t)

    thr = lax.fori_loop(0, 31, search, jnp.zeros((N_EXPERTS, 1), jnp.int32))
    gt = bits > thr
    eq = bits == thr
    need = cap - jnp.sum(gt.astype(jnp.int32), axis=-1, keepdims=True)

    r = lax.broadcasted_iota(jnp.int32, (LANES, LANES), 0)
    c = lax.broadcasted_iota(jnp.int32, (LANES, LANES), 1)
    upper = (r <= c).astype(BF16)
    lower = (c <= r).astype(BF16)
    ident = (c == r).astype(BF16)

    eqb = eq.astype(BF16)
    off = jnp.zeros((N_EXPERTS, 1), F32)
    needf = need.astype(F32)
    sel_chunks = []
    for k in range(nchunk):
        sl = slice(k * LANES, (k + 1) * LANES)
        pre = jnp.dot(eqb[:, sl], upper, preferred_element_type=F32) + off
        off = pre[:, LANES - 1:LANES]
        sel_chunks.append(gt[:, sl] | (eq[:, sl] & (pre <= needf)))

    off_r = jnp.zeros((N_EXPERTS, 1), F32)
    off_c = jnp.zeros((1, N_EXPERTS), F32)
    cnt_lane = lax.broadcasted_iota(jnp.int32, (N_EXPERTS, LANES), 1)
    cnt = jnp.zeros((N_EXPERTS, LANES), F32)
    for k in range(nchunk):
        sl = slice(k * LANES, (k + 1) * LANES)
        sel = sel_chunks[k]
        selb = sel.astype(BF16)
        pre_r = jnp.dot(selb, upper, preferred_element_type=F32) + off_r
        rr_ref[k] = jnp.where(sel, pre_r - 1.0, -1.0).astype(jnp.int32)
        off_r = pre_r[:, LANES - 1:LANES]
        cnt = jnp.where(cnt_lane == k + 1, off_r, cnt)
        pre_c = lax.dot_general(lower, selb, NT_DIMS, preferred_element_type=F32) + off_c
        sel_c = lax.dot_general(ident, selb, NT_DIMS, preferred_element_type=F32)
        rc_ref[sl, :] = jnp.where(sel_c > 0.5, pre_c - 1.0, -1.0).astype(jnp.int32)
        off_c = pre_c[LANES - 1:LANES, :]
        cntc_ref[k + 1:k + 2, :] = off_c.astype(jnp.int32)
    cnt_ref[...] = cnt.astype(jnp.int32)
    cntc_ref[0:1, :] = jnp.zeros((1, N_EXPERTS), jnp.int32)
    pad_rows = cntc_ref.shape[0] - nchunk - 1
    if pad_rows:
        cntc_ref[nchunk + 1:, :] = jnp.zeros((pad_rows, N_EXPERTS), jnp.int32)


def _topk(aff_r, B, lp, lead, cap):
    n = B * lp
    nchunk = lp // LANES
    assert nchunk + 1 <= LANES
    nb1 = -(-(nchunk + 1) // SUBLANES) * SUBLANES
    return pl.pallas_call(
        functools.partial(_topk_kernel, lead=lead, lp=lp, cap=cap),
        grid=(B,),
        in_specs=[pl.BlockSpec((N_EXPERTS, lp), lambda b: (0, b))],
        out_specs=[
            pl.BlockSpec((None, nchunk, N_EXPERTS, LANES), lambda b: (b, 0, 0, 0)),
            pl.BlockSpec((lp, N_EXPERTS), lambda b: (b, 0)),
            pl.BlockSpec((None, N_EXPERTS, LANES), lambda b: (b, 0, 0)),
            pl.BlockSpec((None, nb1, N_EXPERTS), lambda b: (b, 0, 0)),
        ],
        out_shape=[
            jax.ShapeDtypeStruct((B, nchunk, N_EXPERTS, LANES), jnp.int32),
            jax.ShapeDtypeStruct((n, N_EXPERTS), jnp.int32),
            jax.ShapeDtypeStruct((B, N_EXPERTS, LANES), jnp.int32),
            jax.ShapeDtypeStruct((B, nb1, N_EXPERTS), jnp.int32),
        ],
        compiler_params=_params("parallel"),
        name="topk_select",
    )(aff_r)


SLOT_ALIGN = 16


def _window_constants(win):
    wide = N_EXPERTS * win
    expand = np.zeros((N_EXPERTS, wide), np.float32)
    for e in range(N_EXPERTS):
        expand[e, e * win:(e + 1) * win] = 1.0
    return expand, (np.arange(wide) % win).astype(np.float32)


def _gather_kernel(cnt_ref, cntv_ref, rank_ref, aff_ref, h_ref, o_ref, g_ref,
                   *, slots, ch, win, nchunk):
    bi, c = pl.program_id(0), pl.program_id(1)
    k0 = c * ch
    tt = ch * LANES
    max_start = slots - win

    @pl.when(c == 0)
    def _():
        o_ref[...] = jnp.zeros_like(o_ref)
        g_ref[...] = jnp.zeros_like(g_ref)

    lane = lax.broadcasted_iota(jnp.int32, cntv_ref.shape, 1)
    lo_col = jnp.sum(jnp.where(lane == k0, cntv_ref[...], 0), axis=-1, keepdims=True)
    w_col = jnp.minimum(lo_col & -SLOT_ALIGN, max_start)
    rank = rank_ref[...]
    aff = aff_ref[...]
    rel = rank - w_col
    in_win = (rank >= 0) & (rel >= 0) & (rel < win)
    rel = jnp.where(in_win, rel, -1)
    row = lax.broadcasted_iota(jnp.int32, (win, tt), 0)
    hit_all = jnp.concatenate([rel[e:e + 1, :] == row for e in range(N_EXPERTS)], axis=0)
    rows = h_ref[...]
    xw = jnp.dot(hit_all.astype(BF16), rows, preferred_element_type=F32)

    def add_rows(e, start, hit, vals):
        dst = pl.ds(pl.multiple_of(start, SLOT_ALIGN), win)
        o_ref[e, dst, :] = o_ref[e, dst, :] + vals.astype(BF16)
        g_ref[e, dst, :] += jnp.sum(jnp.where(hit, aff[e:e + 1, :], 0.0), axis=-1, keepdims=True)

    for e in range(N_EXPERTS):
        base = (bi * N_EXPERTS + e) * (nchunk + 1) + k0
        w = jnp.minimum(cnt_ref[base] & -SLOT_ALIGN, max_start)
        add_rows(e, w, hit_all[e * win:(e + 1) * win, :], xw[e * win:(e + 1) * win, :])
        hi = cnt_ref[base + ch]

        @pl.when(hi > w + win)
        def _(e=e, w=w, hi=hi):
            rk = rank[e:e + 1, :]

            def extra(i, carry):
                w2 = w + (i + 1) * win
                w2c = jnp.minimum(w2, max_start)
                hit = ((rk - w2c) == row) & (rk >= w2)
                add_rows(e, w2c, hit, jnp.dot(hit.astype(BF16), rows, preferred_element_type=F32))
                return carry

            lax.fori_loop(0, (hi - w - 1) // win, extra, 0)


def _gather(cnt, cntv, rank_rows, aff_rows, h1b, B, lp, slots, ch, win):
    nchunk = lp // LANES
    tt = ch * LANES
    nt = nchunk // ch
    assert slots % SLOT_ALIGN == 0 and win % SLOT_ALIGN == 0 and win <= slots
    per_seq = lambda bi, c, cnt: (bi, 0, 0, 0)
    return pl.pallas_call(
        functools.partial(_gather_kernel, slots=slots, ch=ch, win=win, nchunk=nchunk),
        grid_spec=pltpu.PrefetchScalarGridSpec(
            num_scalar_prefetch=1,
            grid=(B, nt),
            in_specs=[
                pl.BlockSpec((None, N_EXPERTS, LANES), lambda bi, c, cnt: (bi, 0, 0)),
                pl.BlockSpec((None, N_EXPERTS, tt), lambda bi, c, cnt: (bi, 0, c)),
                pl.BlockSpec((None, N_EXPERTS, tt), lambda bi, c, cnt: (bi, 0, c)),
                pl.BlockSpec((tt, D_MODEL), lambda bi, c, cnt: (bi * nt + c, 0)),
            ],
            out_specs=[pl.BlockSpec((None, N_EXPERTS, slots, D_MODEL), per_seq),
                       pl.BlockSpec((None, N_EXPERTS, slots, 1), per_seq)],
        ),
        out_shape=[jax.ShapeDtypeStruct((B, N_EXPERTS, slots, D_MODEL), BF16),
                   jax.ShapeDtypeStruct((B, N_EXPERTS, slots, 1), F32)],
        compiler_params=_params("parallel", "arbitrary"),
        name="moe_gather",
    )(cnt, cntv, rank_rows, aff_rows, h1b)


def _ffn_up_kernel(x_ref, wg_ref, wu_ref, o_ref, wgb_ref, wub_ref):
    @pl.when(pl.program_id(1) == 0)
    def _():
        wgb_ref[...] = wg_ref[...].astype(BF16)
        wub_ref[...] = wu_ref[...].astype(BF16)

    nseq, slots, width = x_ref.shape
    x = x_ref[...].reshape(nseq * slots, width)
    a = jnp.dot(x, wgb_ref[...], preferred_element_type=F32)
    u = jnp.dot(x, wub_ref[...], preferred_element_type=F32)
    o_ref[...] = (a * jax.nn.sigmoid(a) * u).astype(BF16).reshape(o_ref.shape)


def _ffn_down_kernel(x_ref, g_ref, wd_ref, o_ref, wdb_ref):
    @pl.when(pl.program_id(1) == 0)
    def _():
        wdb_ref[...] = wd_ref[...].astype(BF16)

    nseq, slots, width = x_ref.shape
    y = jnp.dot(x_ref[...].reshape(nseq * slots, width), wdb_ref[...], preferred_element_type=F32)
    y = y * g_ref[...].reshape(nseq * slots, 1)
    o_ref[...] = y.astype(BF16).reshape(o_ref.shape)


def _ffn(xg, gates, wg, wu, wd, layer):
    B, _, slots, _ = xg.shape
    n_up = _pick_tile(B, (2, 1))
    n_down = _pick_tile(B, (4, 2, 1))
    acts = lambda nseq, width: pl.BlockSpec((nseq, None, slots, width), lambda e, b: (b, e, 0, 0))
    weight = lambda rows, cols: pl.BlockSpec((None, None, rows, cols), lambda e, b: (layer, e, 0, 0))
    mid = pl.pallas_call(
        _ffn_up_kernel,
        grid=(N_EXPERTS, B // n_up),
        in_specs=[acts(n_up, D_MODEL), weight(D_MODEL, D_FF_EXPERT), weight(D_MODEL, D_FF_EXPERT)],
        out_specs=acts(n_up, D_FF_EXPERT),
        out_shape=jax.ShapeDtypeStruct((B, N_EXPERTS, slots, D_FF_EXPERT), BF16),
        scratch_shapes=[pltpu.VMEM((D_MODEL, D_FF_EXPERT), BF16), pltpu.VMEM((D_MODEL, D_FF_EXPERT), BF16)],
        compiler_params=_params("arbitrary", "arbitrary"),
        name="moe_ffn_up",
    )(xg, wg, wu)
    return pl.pallas_call(
        _ffn_down_kernel,
        grid=(N_EXPERTS, B // n_down),
        in_specs=[acts(n_down, D_FF_EXPERT), acts(n_down, 1), weight(D_FF_EXPERT, D_MODEL)],
        out_specs=acts(n_down, D_MODEL),
        out_shape=jax.ShapeDtypeStruct(xg.shape, BF16),
        scratch_shapes=[pltpu.VMEM((D_FF_EXPERT, D_MODEL), BF16)],
        compiler_params=_params("arbitrary", "arbitrary"),
        name="moe_ffn_down",
    )(mid, gates, wd)


def _combine_kernel(cnt_ref, cntc_ref, rank_ref, y_ref, h_ref, g_ref, b_ref, expand_ref, rpat_ref,
                    o_ref, acc_ref, *, slots, ch, win, nchunk, alpha):
    bi, c = pl.program_id(0), pl.program_id(1)
    k0 = c * ch
    tt = ch * LANES
    max_start = slots - win
    rank = rank_ref[...]
    lo_row = cntc_ref[pl.ds(k0, 1), :]
    w_row = jnp.minimum(lo_row & -SLOT_ALIGN, max_start)
    rel = rank - w_row
    in_win = (rank >= 0) & (rel >= 0) & (rel < win)
    relb = jnp.where(in_win, rel, -1).astype(F32).astype(BF16)
    rel_wide = jnp.dot(relb, expand_ref[...], preferred_element_type=F32)
    onehot = (rel_wide == rpat_ref[...]).astype(BF16)

    starts, windows = [], []
    for e in range(N_EXPERTS):
        lo = cnt_ref[(bi * N_EXPERTS + e) * (nchunk + 1) + k0]
        w = jnp.minimum(lo & -SLOT_ALIGN, max_start)
        starts.append(w)
        windows.append(y_ref[e, pl.ds(pl.multiple_of(w, SLOT_ALIGN), win), :])
    acc_ref[...] = jnp.dot(onehot, jnp.concatenate(windows, axis=0), preferred_element_type=F32)

    lane = lax.broadcasted_iota(jnp.int32, rank.shape, 1)
    col = lax.broadcasted_iota(jnp.int32, (tt, win), 1)
    for e in range(N_EXPERTS):
        hi = cnt_ref[(bi * N_EXPERTS + e) * (nchunk + 1) + k0 + ch]

        @pl.when(hi > starts[e] + win)
        def _(e=e, hi=hi):
            rk = jnp.sum(jnp.where(lane == e, rank, 0), axis=-1, keepdims=True)

            def extra(i, carry):
                w2 = starts[e] + (i + 1) * win
                w2c = jnp.minimum(w2, max_start)
                rows = y_ref[e, pl.ds(pl.multiple_of(w2c, SLOT_ALIGN), win), :]
                hit = ((rk - w2c) == col) & (rk >= w2)
                acc_ref[...] += jnp.dot(hit.astype(BF16), rows, preferred_element_type=F32)
                return carry

            lax.fori_loop(0, (hi - starts[e] - 1) // win, extra, 0)

    o_ref[...] = _ln(alpha * h_ref[...] + acc_ref[...], g_ref[...], b_ref[...])


def _combine(cnt, cntc, rank_c, y, h1, g, b, B, lp, alpha, ch, win):
    n = B * lp
    slots = y.shape[2]
    nchunk = lp // LANES
    tt = ch * LANES
    nt = nchunk // ch
    assert slots % SLOT_ALIGN == 0 and win % SLOT_ALIGN == 0 and win <= slots
    wide = N_EXPERTS * win
    expand, pos = _window_constants(win)
    tok = lambda bi, c, cnt: (bi * nt + c, 0)
    const = lambda bi, c, cnt: (0, 0)
    return pl.pallas_call(
        functools.partial(_combine_kernel, slots=slots, ch=ch, win=win, nchunk=nchunk, alpha=alpha),
        grid_spec=pltpu.PrefetchScalarGridSpec(
            num_scalar_prefetch=1,
            grid=(B, nt),
            in_specs=[
                pl.BlockSpec((None, cntc.shape[1], N_EXPERTS), lambda bi, c, cnt: (bi, 0, 0)),
                pl.BlockSpec((tt, N_EXPERTS), tok),
                pl.BlockSpec((None, N_EXPERTS, slots, D_MODEL), lambda bi, c, cnt: (bi, 0, 0, 0)),
                pl.BlockSpec((tt, D_MODEL), tok),
                pl.BlockSpec((1, D_MODEL), const),
                pl.BlockSpec((1, D_MODEL), const),
                pl.BlockSpec((N_EXPERTS, wide), const),
                pl.BlockSpec((1, wide), const),
            ],
            out_specs=pl.BlockSpec((tt, D_MODEL), tok),
            scratch_shapes=[pltpu.VMEM((tt, D_MODEL), F32)],
        ),
        out_shape=jax.ShapeDtypeStruct((n, D_MODEL), F32),
        compiler_params=_params("parallel", "arbitrary"),
        name="moe_combine_ln",
    )(cnt, cntc, rank_c, y, h1, g.reshape(1, -1), b.reshape(1, -1),
      jnp.asarray(expand, dtype=BF16), jnp.asarray(pos.reshape(1, wide)))


def kernel(x, meta, ln0_g, ln0_b, w_in, b_gate, sink, w_attn_o, w_four_o, w_out, ln1_g, ln1_b,
           w_router, w_e_gate, w_e_up, w_e_down, ln2_g, ln2_b):
    B, seq, d = x.shape
    depth = w_in.shape[0]
    assert d == D_MODEL and meta.shape == (N_META, D_MODEL)
    assert seq % BLOCK == 0 and N_META % SUBLANES == 0 and N_META <= BLOCK
    L = seq + N_META
    nb = -(-L // BLOCK)
    lp = nb * BLOCK
    lead = lp - L
    cap = CAPACITY_FACTOR * L // N_EXPERTS
    slots = -(-cap // SLOT_ALIGN) * SLOT_ALIGN
    alpha = float((2 * depth) ** 0.25)
    n = B * lp
    nchunk = lp // LANES
    tm = _pick_tile(n, (512, 256, 128))
    tm_in = _pick_tile(n, (1024, 512, 256, 128))
    tf = _pick_tile(lp, (384, 128))
    ch = _pick_tile(nchunk, (3, 2, 1))
    win = min(slots, -(-(ch * LANES * cap // L * 4 // 3 + SLOT_ALIGN) // SLOT_ALIGN) * SLOT_ALIGN)

    gi = np.arange(FOURIER_GROUP)
    ang = 2.0 * np.pi * ((gi[:, None] * gi[None, :]) % FOURIER_GROUP) / FOURIER_GROUP
    cs = jnp.asarray(np.concatenate([np.cos(ang), np.sin(ang)], axis=1) * FOURIER_GROUP ** -0.5, dtype=BF16)
    wdft = _dft_matrix(_fourier_steps(lp, lead, L, tf) * tf, lp, lead, L)
    bias, variant_of_block = _attn_bias_tables(nb, lead)

    h = _embed(x, meta, ln0_g, ln0_b, nb, lead).reshape(n, D_MODEL)
    for l in range(depth):
        q, kv, pq, gates = _inproj(h, w_in[l].astype(BF16), b_gate[l], cs, tm_in)
        attn = _attention(q.reshape(B, lp, -1), kv.reshape(B, lp, -1), sink[l], bias, variant_of_block, nb)
        four = _fourier(wdft, pq.reshape(B, lp, -1), lead, L, tf)
        h1, h1b, aff_r = _outproj(
            attn.reshape(n, -1), four.reshape(n, -1), gates, h,
            w_attn_o[l].astype(BF16), w_four_o[l].astype(BF16), w_out[l].astype(BF16),
            ln1_g[l], ln1_b[l], w_router[l], alpha, tm)
        rank_r, rank_c, cntv, cntc = _topk(aff_r, B, lp, lead, cap)
        cnt = cntv[:, :, :nchunk + 1].reshape(-1)
        rank_rows = rank_r.transpose(0, 2, 1, 3).reshape(B, N_EXPERTS, lp)
        aff_rows = aff_r.reshape(N_EXPERTS, B, lp).transpose(1, 0, 2)
        xg, gsel = _gather(cnt, cntv, rank_rows, aff_rows, h1b, B, lp, slots, ch, win)
        y = _ffn(xg, gsel, w_e_gate, w_e_up, w_e_down, l)
        h = _combine(cnt, cntc, rank_c, y, h1, ln2_g[l], ln2_b[l], B, lp, alpha, ch, win)
    return h.reshape(B, lp, D_MODEL)[:, lead + N_META:]
```

```python
import functools
import math

import numpy as np
import jax
import jax.numpy as jnp
from jax import lax
from jax.experimental import pallas as pl
from jax.experimental.pallas import tpu as pltpu

D_MODEL = 1024
N_META = 16
N_HEADS = 8
N_KV_HEADS = 2
HEAD_DIM = 64
GQA_GROUP = N_HEADS // N_KV_HEADS
ATTN_WIDTH = N_HEADS * HEAD_DIM
KV_WIDTH = N_KV_HEADS * HEAD_DIM
WINDOW = 128
BLOCK = 128
N_FOURIER_GROUPS = 4
FOURIER_GROUP = 128
FOURIER_WIDTH = N_FOURIER_GROUPS * FOURIER_GROUP
N_BRANCHES = 2
GATE_WIDTH = N_BRANCHES * D_MODEL
N_EXPERTS = 16
CAPACITY_FACTOR = 2
D_FF_EXPERT = 1536
LN_EPS = 1e-5
NEG_INF = -1e30
Q_END = ATTN_WIDTH
K_END = Q_END + KV_WIDTH
V_END = K_END + KV_WIDTH
F_END = V_END + FOURIER_WIDTH
IN_WIDTH = F_END + GATE_WIDTH

LANES = 128
SUBLANES = 8
VMEM_LIMIT_BYTES = 56 * 1024 * 1024

F32 = jnp.float32
BF16 = jnp.bfloat16
NT_DIMS = (((1,), (1,)), ((), ()))


def _pick_tile(n, candidates):
    for c in candidates:
        if n % c == 0:
            return c
    raise ValueError(f"no tile in {candidates} divides {n}")


def _params(*sem, flags=None):
    return pltpu.CompilerParams(dimension_semantics=sem, vmem_limit_bytes=VMEM_LIMIT_BYTES, flags=flags)


def _ln(x, g, b):
    mu = jnp.mean(x, axis=-1, keepdims=True)
    xc = x - mu
    var = jnp.mean(xc * xc, axis=-1, keepdims=True)
    return xc * lax.rsqrt(var + LN_EPS) * g + b


def _embed_kernel(*refs, lead, group):
    x_refs, (meta_ref, g_ref, b_ref, o_ref) = refs[:group], refs[group:]
    for k in range(1, group):
        o_ref[k * BLOCK:(k + 1) * BLOCK, :] = _ln(x_refs[k][...], g_ref[...], b_ref[...])

    @pl.when(pl.program_id(1) == 0)
    def _():
        o_ref[0:lead, :] = jnp.zeros((lead, D_MODEL), F32)
        o_ref[lead:BLOCK, :] = _ln(meta_ref[...], g_ref[...], b_ref[...])

    @pl.when(pl.program_id(1) > 0)
    def _():
        o_ref[0:BLOCK, :] = _ln(x_refs[0][...], g_ref[...], b_ref[...])


def _embed(x, meta, g, b, nb, lead):
    B = x.shape[0]
    group = _pick_tile(nb, (3, 1))
    x_spec = lambda k: pl.BlockSpec((None, BLOCK, D_MODEL),
                                    lambda bi, j: (bi, jnp.maximum(group * j + k - 1, 0), 0))
    const = lambda bi, j: (0, 0)
    return pl.pallas_call(
        functools.partial(_embed_kernel, lead=lead, group=group),
        grid=(B, nb // group),
        in_specs=[x_spec(k) for k in range(group)] + [
            pl.BlockSpec((N_META, D_MODEL), const),
            pl.BlockSpec((1, D_MODEL), const),
            pl.BlockSpec((1, D_MODEL), const),
        ],
        out_specs=pl.BlockSpec((None, group * BLOCK, D_MODEL), lambda bi, j: (bi, j, 0)),
        out_shape=jax.ShapeDtypeStruct((B, nb * BLOCK, D_MODEL), F32),
        compiler_params=_params("parallel", "arbitrary"),
        name="embed_ln",
    )(*([x] * group), meta, g.reshape(1, -1), b.reshape(1, -1))


def _inproj_kernel(h_ref, w_ref, bg_ref, cs_ref, q_ref, kv_ref, pq_ref, gate_ref):
    hb = h_ref[...].astype(BF16)
    q_ref[...] = jnp.dot(hb, w_ref[:, 0:Q_END], preferred_element_type=F32).astype(BF16)
    kv_ref[...] = jnp.dot(hb, w_ref[:, Q_END:V_END], preferred_element_type=F32).astype(BF16)
    uf = jnp.dot(hb, w_ref[:, V_END:F_END], preferred_element_type=F32).astype(BF16)
    for g in range(N_FOURIER_GROUPS):
        lo = g * FOURIER_GROUP
        pq = jnp.dot(uf[:, lo:lo + FOURIER_GROUP], cs_ref[...], preferred_element_type=F32)
        pq_ref[:, lo:lo + FOURIER_GROUP] = pq[:, 0:FOURIER_GROUP].astype(BF16)
        pq_ref[:, FOURIER_WIDTH + lo:FOURIER_WIDTH + lo + FOURIER_GROUP] = (
            pq[:, FOURIER_GROUP:2 * FOURIER_GROUP].astype(BF16))
    chunk = 512
    for c in range(GATE_WIDTH // chunk):
        lo = c * chunk
        ug = jnp.dot(hb, w_ref[:, F_END + lo:F_END + lo + chunk], preferred_element_type=F32)
        gate_ref[:, lo:lo + chunk] = jax.nn.sigmoid(ug + bg_ref[:, lo:lo + chunk])


def _inproj(h, w_bf16, b_gate, cs, tm):
    n = h.shape[0]
    row = lambda i: (i, 0)
    const = lambda i: (0, 0)
    return pl.pallas_call(
        _inproj_kernel,
        grid=(n // tm,),
        in_specs=[
            pl.BlockSpec((tm, D_MODEL), row),
            pl.BlockSpec((D_MODEL, IN_WIDTH), const),
            pl.BlockSpec((1, GATE_WIDTH), const),
            pl.BlockSpec((FOURIER_GROUP, 2 * FOURIER_GROUP), const),
        ],
        out_specs=[
            pl.BlockSpec((tm, ATTN_WIDTH), row),
            pl.BlockSpec((tm, 2 * KV_WIDTH), row),
            pl.BlockSpec((tm, 2 * FOURIER_WIDTH), row),
            pl.BlockSpec((tm, GATE_WIDTH), row),
        ],
        out_shape=[
            jax.ShapeDtypeStruct((n, ATTN_WIDTH), BF16),
            jax.ShapeDtypeStruct((n, 2 * KV_WIDTH), BF16),
            jax.ShapeDtypeStruct((n, 2 * FOURIER_WIDTH), BF16),
            jax.ShapeDtypeStruct((n, GATE_WIDTH), F32),
        ],
        compiler_params=_params("parallel"),
        name="inproj",
    )(h, w_bf16, b_gate.reshape(1, GATE_WIDTH), cs)


def _attn_kernel(sink_ref, q_ref, *refs, group, variant):
    kv_refs, (bias_ref, o_ref) = refs[:group + 2], refs[group + 2:]
    first_block = pl.program_id(1) * group
    kv = jnp.concatenate([r[...] for r in kv_refs], axis=0)
    low_half = lax.broadcasted_iota(jnp.int32, (kv.shape[0], LANES), 1) < HEAD_DIM

    def lane_half_operands(x):
        swapped = jnp.concatenate([x[:, HEAD_DIM:], x[:, :HEAD_DIM]], axis=1)
        zero = jnp.zeros_like(x)
        return {(kvh, half): jnp.where(low_half if half == 0 else ~low_half,
                                       x if kvh == half else swapped, zero)
                for kvh in range(N_KV_HEADS) for half in range(2)}

    k_ops = lane_half_operands(kv[:, 0:KV_WIDTH])
    v_ops = lane_half_operands(kv[:, KV_WIDTH:2 * KV_WIDTH])
    keys = lambda t: slice(t * BLOCK, (t + 3) * BLOCK)

    def scores(t, h):
        pair = h // 2
        qp = q_ref[t * BLOCK:(t + 1) * BLOCK, pair * LANES:(pair + 1) * LANES]
        return lax.dot_general(qp, k_ops[(h // GQA_GROUP, h % 2)][keys(t)], NT_DIMS,
                               preferred_element_type=F32)

    def head_out(t, h, s):
        sink = sink_ref[h]
        logits = s * (HEAD_DIM ** -0.5) + bias_ref[variant(first_block + t), h]
        m = jnp.maximum(jnp.max(logits, axis=-1, keepdims=True), sink)
        p = jnp.exp(logits - m)
        denom = jnp.sum(p, axis=-1, keepdims=True) + jnp.exp(sink - m)
        o = jnp.dot(p.astype(BF16), v_ops[(h // GQA_GROUP, h % 2)][keys(t)], preferred_element_type=F32)
        return o / denom

    work = [(t, h) for t in range(group) for h in range(N_HEADS)]
    ahead = 5
    pending = [scores(*w) for w in work[:ahead]]
    for n, (t, h) in enumerate(work):
        s_cur = pending.pop(0)
        if n + ahead < len(work):
            pending.append(scores(*work[n + ahead]))
        o = head_out(t, h, s_cur)
        if h % 2 == 0:
            o_even = o
        else:
            pair = h // 2
            o_ref[t * BLOCK:(t + 1) * BLOCK, pair * LANES:(pair + 1) * LANES] = (o_even + o).astype(BF16)


def _attn_bias_tables(nb, lead):
    qi = np.arange(BLOCK)[:, None]
    si = np.arange(3 * BLOCK)[None, :]
    rel = np.abs(si - BLOCK - qi).astype(np.float32)
    slopes = np.array([2.0 ** (-8.0 * (h + 1) / N_HEADS) for h in range(N_HEADS)], np.float32)
    base = np.where(rel[None] <= WINDOW, -slopes[:, None, None] * rel[None], np.float32(NEG_INF))
    variants, keys, variant_of_block = [], [], []
    for i in range(nb):
        kpos = (i - 1) * BLOCK + np.arange(3 * BLOCK)
        valid = (kpos >= lead) & (kpos < nb * BLOCK)
        key = valid.tobytes()
        if key not in keys:
            keys.append(key)
            variants.append(np.where(valid[None, None, :], base, np.float32(NEG_INF)))
        variant_of_block.append(keys.index(key))
    return jnp.asarray(np.stack(variants).astype(np.float32)), variant_of_block


def _attention(q, kv, sink, bias, variant_of_block, nb):
    B, lp, _ = q.shape
    group = _pick_tile(nb, (3, 1))
    interior = max(set(variant_of_block), key=variant_of_block.count)

    def variant(i):
        v = jnp.int32(interior)
        for blk, var in enumerate(variant_of_block):
            if var != interior:
                v = jnp.where(i == blk, var, v)
        return v

    kv_spec = lambda t: pl.BlockSpec((None, BLOCK, 2 * KV_WIDTH),
                                     lambda b, g: (b, jnp.clip(group * g + t - 1, 0, nb - 1), 0))
    rows = pl.BlockSpec((None, group * BLOCK, ATTN_WIDTH), lambda b, g: (b, g, 0))
    return pl.pallas_call(
        functools.partial(_attn_kernel, group=group, variant=variant),
        grid=(B, nb // group),
        in_specs=[pl.BlockSpec(memory_space=pltpu.SMEM), rows]
                 + [kv_spec(t) for t in range(group + 2)]
                 + [pl.BlockSpec(bias.shape, lambda b, g: (0, 0, 0, 0))],
        out_specs=rows,
        out_shape=jax.ShapeDtypeStruct((B, lp, ATTN_WIDTH), BF16),
        compiler_params=_params("parallel", "arbitrary"),
        name="attention",
    )(sink, q, *([kv] * (group + 2)), bias)


def _fourier_kernel(w_ref, p_ref, q_ref, o_ref, prev_ref, *, lp, lead, seq_len, tf):
    m = pl.program_id(1)
    last = pl.num_programs(1) - 1
    nblk = lp // tf
    shift = lead + 1

    @pl.when(m == 0)
    def _():
        row = lax.broadcasted_iota(jnp.int32, (lp, 1), 0)
        dc = jnp.sum(jnp.where(row >= lead, p_ref[...].astype(F32), 0.0), axis=0, keepdims=True)
        prev_ref[...] = jnp.zeros_like(prev_ref)
        prev_ref[tf - 1:tf, :] = dc * np.float32(seq_len ** -0.5)

    t1 = jnp.dot(w_ref[:, 0:lp], p_ref[...], preferred_element_type=F32)
    t2 = jnp.dot(w_ref[:, lp:2 * lp], q_ref[...], preferred_element_type=F32)
    direct = t1 + t2
    r = lax.broadcasted_iota(jnp.int32, (tf, tf), 0)
    c = lax.broadcasted_iota(jnp.int32, (tf, tf), 1)
    flip = (r + c == tf - 1).astype(BF16)
    mirrored = jnp.dot(flip, (t1 - t2).astype(BF16), preferred_element_type=F32)
    shifted = jnp.concatenate([prev_ref[tf - shift:, :], direct[:tf - shift, :]], axis=0)
    prev_ref[...] = direct

    @pl.when(m < last)
    def _():
        o_ref[pl.ds(pl.multiple_of(m * tf, tf), tf), :] = shifted.astype(BF16)
        o_ref[pl.ds(pl.multiple_of((nblk - 1 - m) * tf, tf), tf), :] = mirrored.astype(BF16)

    @pl.when(m == last)
    def _():
        row = lax.broadcasted_iota(jnp.int32, (tf, 1), 0) + (nblk // 2) * tf
        mid = jnp.where(row <= seq_len // 2 + lead, shifted, mirrored)
        o_ref[(nblk // 2) * tf:(nblk // 2 + 1) * tf, :] = mid.astype(BF16)


def _dft_matrix(nrows, lp, lead, seq_len):
    w = np.float32(2.0 * math.pi / seq_len)
    k = (jnp.arange(nrows, dtype=jnp.int32) + 1)[:, None]
    j = jnp.arange(lp // LANES, dtype=jnp.int32)[None, :]
    r = jnp.arange(LANES, dtype=jnp.int32)[None, :]
    a = ((k * (LANES * j - lead)) % seq_len).astype(F32) * w
    b = ((k * r) % seq_len).astype(F32) * w
    ca, sa, cb, sb = jnp.cos(a)[:, :, None], jnp.sin(a)[:, :, None], jnp.cos(b)[:, None, :], jnp.sin(b)[:, None, :]
    valid = (jnp.arange(lp, dtype=jnp.int32) >= lead).reshape(1, lp // LANES, LANES)
    scale = np.float32(seq_len ** -0.5)
    wc = jnp.where(valid, (ca * cb - sa * sb) * scale, 0.0).reshape(nrows, lp)
    ws = jnp.where(valid, -(sa * cb + ca * sb) * scale, 0.0).reshape(nrows, lp)
    return jnp.concatenate([wc, ws], axis=1).astype(BF16)


def _fourier_steps(lp, lead, seq_len, tf):
    nblk = lp // tf
    assert seq_len % 2 == 0 and lp % tf == 0 and nblk % 2 == 1 and lead + 1 < tf
    steps = nblk // 2 + 1
    assert steps * tf >= seq_len // 2
    return steps


def _fourier(w, pq, lead, seq_len, tf):
    B, lp, _ = pq.shape
    steps = _fourier_steps(lp, lead, seq_len, tf)
    return pl.pallas_call(
        functools.partial(_fourier_kernel, lp=lp, lead=lead, seq_len=seq_len, tf=tf),
        grid=(B, steps),
        in_specs=[
            pl.BlockSpec((tf, 2 * lp), lambda b, m: (m, 0)),
            pl.BlockSpec((None, lp, FOURIER_WIDTH), lambda b, m: (b, 0, 0)),
            pl.BlockSpec((None, lp, FOURIER_WIDTH), lambda b, m: (b, 0, 1)),
        ],
        out_specs=pl.BlockSpec((None, lp, FOURIER_WIDTH), lambda b, m: (b, 0, 0)),
        out_shape=jax.ShapeDtypeStruct((B, lp, FOURIER_WIDTH), BF16),
        scratch_shapes=[pltpu.VMEM((tf, FOURIER_WIDTH), F32)],
        compiler_params=_params("parallel", "arbitrary"),
        name="fourier",
    )(w, pq, pq)


def _split_bf16(x):
    hi = x.astype(BF16)
    lo = (x - hi.astype(F32)).astype(BF16)
    return hi, lo


def _outproj_kernel(attn_ref, four_ref, gate_ref, h_ref, wa_ref, wf_ref, wo_ref, g_ref, b_ref,
                    wrth_ref, wrtl_ref, h1_ref, h1b_ref, affr_ref, *, alpha):
    ya = jnp.dot(attn_ref[...], wa_ref[...], preferred_element_type=F32)
    yf = jnp.dot(four_ref[...], wf_ref[...], preferred_element_type=F32)
    merged = gate_ref[:, 0:D_MODEL] * ya + gate_ref[:, D_MODEL:GATE_WIDTH] * yf
    mix = jnp.dot(merged.astype(BF16), wo_ref[...], preferred_element_type=F32)
    h1 = _ln(alpha * h_ref[...] + mix, g_ref[...], b_ref[...])
    h1_ref[...] = h1
    hi, lo = _split_bf16(h1)
    h1b_ref[...] = hi
    lr =(lax.dot_general(wrth_ref[...], hi, NT_DIMS, preferred_element_type=F32)
          + lax.dot_general(wrtl_ref[...], hi, NT_DIMS, preferred_element_type=F32)
          + lax.dot_general(wrth_ref[...], lo, NT_DIMS, preferred_element_type=F32))
    er = jnp.exp(lr - jnp.max(lr, axis=0, keepdims=True))
    affr_ref[...] = er / jnp.sum(er, axis=0, keepdims=True)


def _outproj(attn, four, gates, h, wa, wf, wo, g, b, w_router, alpha, tm):
    n = h.shape[0]
    row = lambda i: (i, 0)
    const = lambda i: (0, 0)
    wrth, wrtl = _split_bf16(w_router.T)
    return pl.pallas_call(
        functools.partial(_outproj_kernel, alpha=alpha),
        grid=(n // tm,),
        in_specs=[
            pl.BlockSpec((tm, ATTN_WIDTH), row),
            pl.BlockSpec((tm, FOURIER_WIDTH), row),
            pl.BlockSpec((tm, GATE_WIDTH), row),
            pl.BlockSpec((tm, D_MODEL), row),
            pl.BlockSpec((ATTN_WIDTH, D_MODEL), const),
            pl.BlockSpec((FOURIER_WIDTH, D_MODEL), const),
            pl.BlockSpec((D_MODEL, D_MODEL), const),
            pl.BlockSpec((1, D_MODEL), const),
            pl.BlockSpec((1, D_MODEL), const),
            pl.BlockSpec((N_EXPERTS, D_MODEL), const),
            pl.BlockSpec((N_EXPERTS, D_MODEL), const),
        ],
        out_specs=[
            pl.BlockSpec((tm, D_MODEL), row),
            pl.BlockSpec((tm, D_MODEL), row),
            pl.BlockSpec((N_EXPERTS, tm), lambda i: (0, i)),
        ],
        out_shape=[
            jax.ShapeDtypeStruct((n, D_MODEL), F32),
            jax.ShapeDtypeStruct((n, D_MODEL), BF16),
            jax.ShapeDtypeStruct((N_EXPERTS, n), F32),
        ],
        compiler_params=_params("parallel"),
        name="outproj_ln_router",
    )(attn, four, gates, h, wa, wf, wo, g.reshape(1, -1), b.reshape(1, -1), wrth, wrtl)


def _topk_kernel(aff_ref, rr_ref, rc_ref, cnt_ref, cntc_ref, *, lead, lp, cap):
    nchunk = lp // LANES
    lane = lax.broadcasted_iota(jnp.int32, (N_EXPERTS, lp), 1)
    bits = jnp.where(lane >= lead, pltpu.bitcast(aff_ref[...], jnp.int32), -1)

    def search(i, t):
        cand = t | (jnp.int32(1) << (30 - i))
        cnt = jnp.sum((bits >= cand).astype(jnp.int32), axis=-1, keepdims=True)
        return jnp.where(cnt >= cap, cand, t)

    thr = lax.fori_loop(0, 31, search, jnp.zeros((N_EXPERTS, 1), jnp.int32))
    gt = bits > thr
    eq = bits == thr
    need = cap - jnp.sum(gt.astype(jnp.int32), axis=-1, keepdims=True)

    r = lax.broadcasted_iota(jnp.int32, (LANES, LANES), 0)
    c = lax.broadcasted_iota(jnp.int32, (LANES, LANES), 1)
    upper = (r <= c).astype(BF16)
    lower = (c <= r).astype(BF16)
    ident = (c == r).astype(BF16)

    eqb = eq.astype(BF16)
    off = jnp.zeros((N_EXPERTS, 1), F32)
    needf = need.astype(F32)
    sel_chunks = []
    for k in range(nchunk):
        sl = slice(k * LANES, (k + 1) * LANES)
        pre = jnp.dot(eqb[:, sl], upper, preferred_element_type=F32) + off
        off = pre[:, LANES - 1:LANES]
        sel_chunks.append(gt[:, sl] | (eq[:, sl] & (pre <= needf)))

    off_r = jnp.zeros((N_EXPERTS, 1), F32)
    off_c = jnp.zeros((1, N_EXPERTS), F32)
    cnt_lane = lax.broadcasted_iota(jnp.int32, (N_EXPERTS, LANES), 1)
    cnt = jnp.zeros((N_EXPERTS, LANES), F32)
    for k in range(nchunk):
        sl = slice(k * LANES, (k + 1) * LANES)
        sel = sel_chunks[k]
        selb = sel.astype(BF16)
        pre_r = jnp.dot(selb, upper, preferred_element_type=F32) + off_r
        rr_ref[k] = jnp.where(sel, pre_r - 1.0, -1.0).astype(jnp.int32)
        off_r = pre_r[:, LANES - 1:LANES]
        cnt = jnp.where(cnt_lane == k + 1, off_r, cnt)
        pre_c = lax.dot_general(lower, selb, NT_DIMS, preferred_element_type=F32) + off_c
        sel_c = lax.dot_general(ident, selb, NT_DIMS, preferred_element_type=F32)
        rc_ref[sl, :] = jnp.where(sel_c > 0.5, pre_c - 1.0, -1.0).astype(jnp.int32)
        off_c = pre_c[LANES - 1:LANES, :]
        cntc_ref[k + 1:k + 2, :] = off_c.astype(jnp.int32)
    cnt_ref[...] = cnt.astype(jnp.int32)
    cntc_ref[0:1, :] = jnp.zeros((1, N_EXPERTS), jnp.int32)
    pad_rows = cntc_ref.shape[0] - nchunk - 1
    if pad_rows:
        cntc_ref[nchunk + 1:, :] = jnp.zeros((pad_rows, N_EXPERTS), jnp.int32)


def _topk(aff_r, B, lp, lead, cap):
    n = B * lp
    nchunk = lp // LANES
    assert nchunk + 1 <= LANES
    nb1 = -(-(nchunk + 1) // SUBLANES) * SUBLANES
    return pl.pallas_call(
        functools.partial(_topk_kernel, lead=lead, lp=lp, cap=cap),
        grid=(B,),
        in_specs=[pl.BlockSpec((N_EXPERTS, lp), lambda b: (0, b))],
        out_specs=[
            pl.BlockSpec((None, nchunk, N_EXPERTS, LANES), lambda b: (b, 0, 0, 0)),
            pl.BlockSpec((lp, N_EXPERTS), lambda b: (b, 0)),
            pl.BlockSpec((None, N_EXPERTS, LANES), lambda b: (b, 0, 0)),
            pl.BlockSpec((None, nb1, N_EXPERTS), lambda b: (b, 0, 0)),
        ],
        out_shape=[
            jax.ShapeDtypeStruct((B, nchunk, N_EXPERTS, LANES), jnp.int32),
            jax.ShapeDtypeStruct((n, N_EXPERTS), jnp.int32),
            jax.ShapeDtypeStruct((B, N_EXPERTS, LANES), jnp.int32),
            jax.ShapeDtypeStruct((B, nb1, N_EXPERTS), jnp.int32),
        ],
        compiler_params=_params("parallel"),
        name="topk_select",
    )(aff_r)


SLOT_ALIGN = 16


def _window_constants(win):
    wide = N_EXPERTS * win
    expand = np.zeros((N_EXPERTS, wide), np.float32)
    for e in range(N_EXPERTS):
        expand[e, e * win:(e + 1) * win] = 1.0
    return expand, (np.arange(wide) % win).astype(np.float32)


def _gather_kernel(cnt_ref, cntv_ref, rank_ref, aff_ref, h_ref, o_ref, g_ref,
                   *, slots, ch, win, nchunk):
    bi, c = pl.program_id(0), pl.program_id(1)
    k0 = c * ch
    tt = ch * LANES
    max_start = slots - win

    @pl.when(c == 0)
    def _():
        o_ref[...] = jnp.zeros_like(o_ref)
        g_ref[...] = jnp.zeros_like(g_ref)

    lane = lax.broadcasted_iota(jnp.int32, cntv_ref.shape, 1)
    lo_col = jnp.sum(jnp.where(lane == k0, cntv_ref[...], 0), axis=-1, keepdims=True)
    w_col = jnp.minimum(lo_col & -SLOT_ALIGN, max_start)
    rank = rank_ref[...]
    aff = aff_ref[...]
    rel = rank - w_col
    in_win = (rank >= 0) & (rel >= 0) & (rel < win)
    rel = jnp.where(in_win, rel, -1)
    row = lax.broadcasted_iota(jnp.int32, (win, tt), 0)
    hit_all = jnp.concatenate([rel[e:e + 1, :] == row for e in range(N_EXPERTS)], axis=0)
    rows = h_ref[...]
    xw = jnp.dot(hit_all.astype(BF16), rows, preferred_element_type=F32)

    def add_rows(e, start, hit, vals):
        dst = pl.ds(pl.multiple_of(start, SLOT_ALIGN), win)
        o_ref[e, dst, :] = o_ref[e, dst, :] + vals.astype(BF16)
        g_ref[e, dst, :] += jnp.sum(jnp.where(hit, aff[e:e + 1, :], 0.0), axis=-1, keepdims=True)

    for e in range(N_EXPERTS):
        base = (bi * N_EXPERTS + e) * (nchunk + 1) + k0
        w = jnp.minimum(cnt_ref[base] & -SLOT_ALIGN, max_start)
        add_rows(e, w, hit_all[e * win:(e + 1) * win, :], xw[e * win:(e + 1) * win, :])
        hi = cnt_ref[base + ch]

        @pl.when(hi > w + win)
        def _(e=e, w=w, hi=hi):
            rk = rank[e:e + 1, :]

            def extra(i, carry):
                w2 = w + (i + 1) * win
                w2c = jnp.minimum(w2, max_start)
                hit = ((rk - w2c) == row) & (rk >= w2)
                add_rows(e, w2c, hit, jnp.dot(hit.astype(BF16), rows, preferred_element_type=F32))
                return carry

            lax.fori_loop(0, (hi - w - 1) // win, extra, 0)


def _gather(cnt, cntv, rank_rows, aff_rows, h1b, B, lp, slots, ch, win):
    nchunk = lp // LANES
    tt = ch * LANES
    nt = nchunk // ch
    assert slots % SLOT_ALIGN == 0 and win % SLOT_ALIGN == 0 and win <= slots
    per_seq = lambda bi, c, cnt: (bi, 0, 0, 0)
    return pl.pallas_call(
        functools.partial(_gather_kernel, slots=slots, ch=ch, win=win, nchunk=nchunk),
        grid_spec=pltpu.PrefetchScalarGridSpec(
            num_scalar_prefetch=1,
            grid=(B, nt),
            in_specs=[
                pl.BlockSpec((None, N_EXPERTS, LANES), lambda bi, c, cnt: (bi, 0, 0)),
                pl.BlockSpec((None, N_EXPERTS, tt), lambda bi, c, cnt: (bi, 0, c)),
                pl.BlockSpec((None, N_EXPERTS, tt), lambda bi, c, cnt: (bi, 0, c)),
                pl.BlockSpec((tt, D_MODEL), lambda bi, c, cnt: (bi * nt + c, 0)),
            ],
            out_specs=[pl.BlockSpec((None, N_EXPERTS, slots, D_MODEL), per_seq),
                       pl.BlockSpec((None, N_EXPERTS, slots, 1), per_seq)],
        ),
        out_shape=[jax.ShapeDtypeStruct((B, N_EXPERTS, slots, D_MODEL), BF16),
                   jax.ShapeDtypeStruct((B, N_EXPERTS, slots, 1), F32)],
        compiler_params=_params("parallel", "arbitrary"),
        name="moe_gather",
    )(cnt, cntv, rank_rows, aff_rows, h1b)


def _ffn_up_kernel(x_ref, wg_ref, wu_ref, o_ref, wgb_ref, wub_ref):
    @pl.when(pl.program_id(1) == 0)
    def _():
        wgb_ref[...] = wg_ref[...].astype(BF16)
        wub_ref[...] = wu_ref[...].astype(BF16)

    nseq, slots, width = x_ref.shape
    x = x_ref[...].reshape(nseq * slots, width)
    a = jnp.dot(x, wgb_ref[...], preferred_element_type=F32)
    u = jnp.dot(x, wub_ref[...], preferred_element_type=F32)
    o_ref[...] = (a * jax.nn.sigmoid(a) * u).astype(BF16).reshape(o_ref.shape)


def _ffn_down_kernel(x_ref, g_ref, wd_ref, o_ref, wdb_ref):
    @pl.when(pl.program_id(1) == 0)
    def _():
        wdb_ref[...] = wd_ref[...].astype(BF16)

    nseq, slots, width = x_ref.shape
    y = jnp.dot(x_ref[...].reshape(nseq * slots, width), wdb_ref[...], preferred_element_type=F32)
    y = y * g_ref[...].reshape(nseq * slots, 1)
    o_ref[...] = y.astype(BF16).reshape(o_ref.shape)


def _ffn(xg, gates, wg, wu, wd, layer):
    B, _, slots, _ = xg.shape
    n_up = _pick_tile(B, (2, 1))
    n_down = _pick_tile(B, (4, 2, 1))
    acts = lambda nseq, width: pl.BlockSpec((nseq, None, slots, width), lambda e, b: (b, e, 0, 0))
    weight = lambda rows, cols: pl.BlockSpec((None, None, rows, cols), lambda e, b: (layer, e, 0, 0))
    mid = pl.pallas_call(
        _ffn_up_kernel,
        grid=(N_EXPERTS, B // n_up),
        in_specs=[acts(n_up, D_MODEL), weight(D_MODEL, D_FF_EXPERT), weight(D_MODEL, D_FF_EXPERT)],
        out_specs=acts(n_up, D_FF_EXPERT),
        out_shape=jax.ShapeDtypeStruct((B, N_EXPERTS, slots, D_FF_EXPERT), BF16),
        scratch_shapes=[pltpu.VMEM((D_MODEL, D_FF_EXPERT), BF16), pltpu.VMEM((D_MODEL, D_FF_EXPERT), BF16)],
        compiler_params=_params("arbitrary", "arbitrary"),
        name="moe_ffn_up",
    )(xg, wg, wu)
    return pl.pallas_call(
        _ffn_down_kernel,
        grid=(N_EXPERTS, B // n_down),
        in_specs=[acts(n_down, D_FF_EXPERT), acts(n_down, 1), weight(D_FF_EXPERT, D_MODEL)],
        out_specs=acts(n_down, D_MODEL),
        out_shape=jax.ShapeDtypeStruct(xg.shape, BF16),
        scratch_shapes=[pltpu.VMEM((D_FF_EXPERT, D_MODEL), BF16)],
        compiler_params=_params("arbitrary", "arbitrary"),
        name="moe_ffn_down",
    )(mid, gates, wd)


def _combine_kernel(cnt_ref, cntc_ref, rank_ref, y_ref, h_ref, g_ref, b_ref, expand_ref, rpat_ref,
                    o_ref, acc_ref, *, slots, ch, win, nchunk, alpha):
    bi, c = pl.program_id(0), pl.program_id(1)
    k0 = c * ch
    tt = ch * LANES
    max_start = slots - win
    rank = rank_ref[...]
    lo_row = cntc_ref[pl.ds(k0, 1), :]
    w_row = jnp.minimum(lo_row & -SLOT_ALIGN, max_start)
    rel = rank - w_row
    in_win = (rank >= 0) & (rel >= 0) & (rel < win)
    relb = jnp.where(in_win, rel, -1).astype(F32).astype(BF16)
    rel_wide = jnp.dot(relb, expand_ref[...], preferred_element_type=F32)
    onehot = (rel_wide == rpat_ref[...]).astype(BF16)

    starts, windows = [], []
    for e in range(N_EXPERTS):
        lo = cnt_ref[(bi * N_EXPERTS + e) * (nchunk + 1) + k0]
        w = jnp.minimum(lo & -SLOT_ALIGN, max_start)
        starts.append(w)
        windows.append(y_ref[e, pl.ds(pl.multiple_of(w, SLOT_ALIGN), win), :])
    acc_ref[...] = jnp.dot(onehot, jnp.concatenate(windows, axis=0), preferred_element_type=F32)

    lane = lax.broadcasted_iota(jnp.int32, rank.shape, 1)
    col = lax.broadcasted_iota(jnp.int32, (tt, win), 1)
    for e in range(N_EXPERTS):
        hi = cnt_ref[(bi * N_EXPERTS + e) * (nchunk + 1) + k0 + ch]

        @pl.when(hi > starts[e] + win)
        def _(e=e, hi=hi):
            rk = jnp.sum(jnp.where(lane == e, rank, 0), axis=-1, keepdims=True)

            def extra(i, carry):
                w2 = starts[e] + (i + 1) * win
                w2c = jnp.minimum(w2, max_start)
                rows = y_ref[e, pl.ds(pl.multiple_of(w2c, SLOT_ALIGN), win), :]
                hit = ((rk - w2c) == col) & (rk >= w2)
                acc_ref[...] += jnp.dot(hit.astype(BF16), rows, preferred_element_type=F32)
                return carry

            lax.fori_loop(0, (hi - starts[e] - 1) // win, extra, 0)

    o_ref[...] = _ln(alpha * h_ref[...] + acc_ref[...], g_ref[...], b_ref[...])


def _combine(cnt, cntc, rank_c, y, h1, g, b, B, lp, alpha, ch, win):
    n = B * lp
    slots = y.shape[2]
    nchunk = lp // LANES
    tt = ch * LANES
    nt = nchunk // ch
    assert slots % SLOT_ALIGN == 0 and win % SLOT_ALIGN == 0 and win <= slots
    wide = N_EXPERTS * win
    expand, pos = _window_constants(win)
    tok = lambda bi, c, cnt: (bi * nt + c, 0)
    const = lambda bi, c, cnt: (0, 0)
    return pl.pallas_call(
        functools.partial(_combine_kernel, slots=slots, ch=ch, win=win, nchunk=nchunk, alpha=alpha),
        grid_spec=pltpu.PrefetchScalarGridSpec(
            num_scalar_prefetch=1,
            grid=(B, nt),
            in_specs=[
                pl.BlockSpec((None, cntc.shape[1], N_EXPERTS), lambda bi, c, cnt: (bi, 0, 0)),
                pl.BlockSpec((tt, N_EXPERTS), tok),
                pl.BlockSpec((None, N_EXPERTS, slots, D_MODEL), lambda bi, c, cnt: (bi, 0, 0, 0)),
                pl.BlockSpec((tt, D_MODEL), tok),
                pl.BlockSpec((1, D_MODEL), const),
                pl.BlockSpec((1, D_MODEL), const),
                pl.BlockSpec((N_EXPERTS, wide), const),
                pl.BlockSpec((1, wide), const),
            ],
            out_specs=pl.BlockSpec((tt, D_MODEL), tok),
            scratch_shapes=[pltpu.VMEM((tt, D_MODEL), F32)],
        ),
        out_shape=jax.ShapeDtypeStruct((n, D_MODEL), F32),
        compiler_params=_params("parallel", "arbitrary"),
        name="moe_combine_ln",
    )(cnt, cntc, rank_c, y, h1, g.reshape(1, -1), b.reshape(1, -1),
      jnp.asarray(expand, dtype=BF16), jnp.asarray(pos.reshape(1, wide)))


def kernel(x, meta, ln0_g, ln0_b, w_in, b_gate, sink, w_attn_o, w_four_o, w_out, ln1_g, ln1_b,
           w_router, w_e_gate, w_e_up, w_e_down, ln2_g, ln2_b):
    B, seq, d = x.shape
    depth = w_in.shape[0]
    assert d == D_MODEL and meta.shape == (N_META, D_MODEL)
    assert seq % BLOCK == 0 and N_META % SUBLANES == 0 and N_META <= BLOCK
    L = seq + N_META
    nb = -(-L // BLOCK)
    lp = nb * BLOCK
    lead = lp - L
    cap = CAPACITY_FACTOR * L // N_EXPERTS
    slots = -(-cap // SLOT_ALIGN) * SLOT_ALIGN
    alpha = float((2 * depth) ** 0.25)
    n = B * lp
    nchunk = lp // LANES
    tm = _pick_tile(n, (512, 256, 128))
    tm_in = _pick_tile(n, (1024, 512, 256, 128))
    tf = _pick_tile(lp, (384, 128))
    ch = _pick_tile(nchunk, (3, 2, 1))
    win = min(slots, -(-(ch * LANES * cap // L * 4 // 3 + SLOT_ALIGN) // SLOT_ALIGN) * SLOT_ALIGN)

    gi = np.arange(FOURIER_GROUP)
    ang = 2.0 * np.pi * ((gi[:, None] * gi[None, :]) % FOURIER_GROUP) / FOURIER_GROUP
    cs = jnp.asarray(np.concatenate([np.cos(ang), np.sin(ang)], axis=1) * FOURIER_GROUP ** -0.5, dtype=BF16)
    wdft = _dft_matrix(_fourier_steps(lp, lead, L, tf) * tf, lp, lead, L)
    bias, variant_of_block = _attn_bias_tables(nb, lead)

    h = _embed(x, meta, ln0_g, ln0_b, nb, lead).reshape(n, D_MODEL)
    for l in range(depth):
        q, kv, pq, gates = _inproj(h, w_in[l].astype(BF16), b_gate[l], cs, tm_in)
        attn = _attention(q.reshape(B, lp, -1), kv.reshape(B, lp, -1), sink[l], bias, variant_of_block, nb)
        four = _fourier(wdft, pq.reshape(B, lp, -1), lead, L, tf)
        h1, h1b, aff_r = _outproj(
            attn.reshape(n, -1), four.reshape(n, -1), gates, h,
            w_attn_o[l].astype(BF16), w_four_o[l].astype(BF16), w_out[l].astype(BF16),
            ln1_g[l], ln1_b[l], w_router[l], alpha, tm)
        rank_r, rank_c, cntv, cntc = _topk(aff_r, B, lp, lead, cap)
        cnt = cntv[:, :, :nchunk + 1].reshape(-1)
        rank_rows = rank_r.transpose(0, 2, 1, 3).reshape(B, N_EXPERTS, lp)
        aff_rows = aff_r.reshape(N_EXPERTS, B, lp).transpose(1, 0, 2)
        xg, gsel = _gather(cnt, cntv, rank_rows, aff_rows, h1b, B, lp, slots, ch, win)
        y = _ffn(xg, gsel, w_e_gate, w_e_up, w_e_down, l)
        h = _combine(cnt, cntc, rank_c, y, h1, ln2_g[l], ln2_b[l], B, lp, alpha, ch, win)
    return h.reshape(B, lp, D_MODEL)[:, lead + N_META:]
```

```python
import functools
import math

import numpy as np
import jax
import jax.numpy as jnp
from jax import lax
from jax.experimental import pallas as pl
from jax.experimental.pallas import tpu as pltpu

D_MODEL = 1024
N_META = 16
N_HEADS = 8
N_KV_HEADS = 2
HEAD_DIM = 64
GQA_GROUP = N_HEADS // N_KV_HEADS
ATTN_WIDTH = N_HEADS * HEAD_DIM
KV_WIDTH = N_KV_HEADS * HEAD_DIM
WINDOW = 128
BLOCK = 128
N_FOURIER_GROUPS = 4
FOURIER_GROUP = 128
FOURIER_WIDTH = N_FOURIER_GROUPS * FOURIER_GROUP
N_BRANCHES = 2
GATE_WIDTH = N_BRANCHES * D_MODEL
N_EXPERTS = 16
CAPACITY_FACTOR = 2
D_FF_EXPERT = 1536
LN_EPS = 1e-5
NEG_INF = -1e30
Q_END = ATTN_WIDTH
K_END = Q_END + KV_WIDTH
V_END = K_END + KV_WIDTH
F_END = V_END + FOURIER_WIDTH
IN_WIDTH = F_END + GATE_WIDTH

LANES = 128
SUBLANES = 8
VMEM_LIMIT_BYTES = 56 * 1024 * 1024

F32 = jnp.float32
BF16 = jnp.bfloat16
NT_DIMS = (((1,), (1,)), ((), ()))


def _pick_tile(n, candidates):
    for c in candidates:
        if n % c == 0:
            return c
    raise ValueError(f"no tile in {candidates} divides {n}")


def _params(*sem, flags=None):
    return pltpu.CompilerParams(dimension_semantics=sem, vmem_limit_bytes=VMEM_LIMIT_BYTES, flags=flags)


def _ln(x, g, b):
    mu = jnp.mean(x, axis=-1, keepdims=True)
    xc = x - mu
    var = jnp.mean(xc * xc, axis=-1, keepdims=True)
    return xc * lax.rsqrt(var + LN_EPS) * g + b


def _embed_kernel(*refs, lead, group):
    x_refs, (meta_ref, g_ref, b_ref, o_ref) = refs[:group], refs[group:]
    for k in range(1, group):
        o_ref[k * BLOCK:(k + 1) * BLOCK, :] = _ln(x_refs[k][...], g_ref[...], b_ref[...])

    @pl.when(pl.program_id(1) == 0)
    def _():
        o_ref[0:lead, :] = jnp.zeros((lead, D_MODEL), F32)
        o_ref[lead:BLOCK, :] = _ln(meta_ref[...], g_ref[...], b_ref[...])

    @pl.when(pl.program_id(1) > 0)
    def _():
        o_ref[0:BLOCK, :] = _ln(x_refs[0][...], g_ref[...], b_ref[...])


def _embed(x, meta, g, b, nb, lead):
    B = x.shape[0]
    group = _pick_tile(nb, (3, 1))
    x_spec = lambda k: pl.BlockSpec((None, BLOCK, D_MODEL),
                                    lambda bi, j: (bi, jnp.maximum(group * j + k - 1, 0), 0))
    const = lambda bi, j: (0, 0)
    return pl.pallas_call(
        functools.partial(_embed_kernel, lead=lead, group=group),
        grid=(B, nb // group),
        in_specs=[x_spec(k) for k in range(group)] + [
            pl.BlockSpec((N_META, D_MODEL), const),
            pl.BlockSpec((1, D_MODEL), const),
            pl.BlockSpec((1, D_MODEL), const),
        ],
        out_specs=pl.BlockSpec((None, group * BLOCK, D_MODEL), lambda bi, j: (bi, j, 0)),
        out_shape=jax.ShapeDtypeStruct((B, nb * BLOCK, D_MODEL), F32),
        compiler_params=_params("parallel", "arbitrary"),
        name="embed_ln",
    )(*([x] * group), meta, g.reshape(1, -1), b.reshape(1, -1))


def _inproj_kernel(h_ref, w_ref, bg_ref, cs_ref, q_ref, kv_ref, pq_ref, gate_ref):
    hb = h_ref[...].astype(BF16)
    q_ref[...] = jnp.dot(hb, w_ref[:, 0:Q_END], preferred_element_type=F32).astype(BF16)
    kv_ref[...] = jnp.dot(hb, w_ref[:, Q_END:V_END], preferred_element_type=F32).astype(BF16)
    uf = jnp.dot(hb, w_ref[:, V_END:F_END], preferred_element_type=F32).astype(BF16)
    for g in range(N_FOURIER_GROUPS):
        lo = g * FOURIER_GROUP
        pq = jnp.dot(uf[:, lo:lo + FOURIER_GROUP], cs_ref[...], preferred_element_type=F32)
        pq_ref[:, lo:lo + FOURIER_GROUP] = pq[:, 0:FOURIER_GROUP].astype(BF16)
        pq_ref[:, FOURIER_WIDTH + lo:FOURIER_WIDTH + lo + FOURIER_GROUP] = (
            pq[:, FOURIER_GROUP:2 * FOURIER_GROUP].astype(BF16))
    chunk = 512
    for c in range(GATE_WIDTH // chunk):
        lo = c * chunk
        ug = jnp.dot(hb, w_ref[:, F_END + lo:F_END + lo + chunk], preferred_element_type=F32)
        gate_ref[:, lo:lo + chunk] = jax.nn.sigmoid(ug + bg_ref[:, lo:lo + chunk])


def _inproj(h, w_bf16, b_gate, cs, tm):
    n = h.shape[0]
    row = lambda i: (i, 0)
    const = lambda i: (0, 0)
    return pl.pallas_call(
        _inproj_kernel,
        grid=(n // tm,),
        in_specs=[
            pl.BlockSpec((tm, D_MODEL), row),
            pl.BlockSpec((D_MODEL, IN_WIDTH), const),
            pl.BlockSpec((1, GATE_WIDTH), const),
            pl.BlockSpec((FOURIER_GROUP, 2 * FOURIER_GROUP), const),
        ],
        out_specs=[
            pl.BlockSpec((tm, ATTN_WIDTH), row),
            pl.BlockSpec((tm, 2 * KV_WIDTH), row),
            pl.BlockSpec((tm, 2 * FOURIER_WIDTH), row),
            pl.BlockSpec((tm, GATE_WIDTH), row),
        ],
        out_shape=[
            jax.ShapeDtypeStruct((n, ATTN_WIDTH), BF16),
            jax.ShapeDtypeStruct((n, 2 * KV_WIDTH), BF16),
            jax.ShapeDtypeStruct((n, 2 * FOURIER_WIDTH), BF16),
            jax.ShapeDtypeStruct((n, GATE_WIDTH), F32),
        ],
        compiler_params=_params("parallel"),
        name="inproj",
    )(h, w_bf16, b_gate.reshape(1, GATE_WIDTH), cs)


def _attn_kernel(sink_ref, q_ref, *refs, group, variant):
    kv_refs, (bias_ref, o_ref) = refs[:group + 2], refs[group + 2:]
    first_block = pl.program_id(1) * group
    kv = jnp.concatenate([r[...] for r in kv_refs], axis=0)
    low_half = lax.broadcasted_iota(jnp.int32, (kv.shape[0], LANES), 1) < HEAD_DIM

    def lane_half_operands(x):
        swapped = jnp.concatenate([x[:, HEAD_DIM:], x[:, :HEAD_DIM]], axis=1)
        zero = jnp.zeros_like(x)
        return {(kvh, half): jnp.where(low_half if half == 0 else ~low_half,
                                       x if kvh == half else swapped, zero)
                for kvh in range(N_KV_HEADS) for half in range(2)}

    k_ops = lane_half_operands(kv[:, 0:KV_WIDTH])
    v_ops = lane_half_operands(kv[:, KV_WIDTH:2 * KV_WIDTH])
    keys = lambda t: slice(t * BLOCK, (t + 3) * BLOCK)

    def scores(t, h):
        pair = h // 2
        qp = q_ref[t * BLOCK:(t + 1) * BLOCK, pair * LANES:(pair + 1) * LANES]
        return lax.dot_general(qp, k_ops[(h // GQA_GROUP, h % 2)][keys(t)], NT_DIMS,
                               preferred_element_type=F32)

    def head_out(t, h, s):
        sink = sink_ref[h]
        logits = s * (HEAD_DIM ** -0.5) + bias_ref[variant(first_block + t), h]
        m = jnp.maximum(jnp.max(logits, axis=-1, keepdims=True), sink)
        p = jnp.exp(logits - m)
        denom = jnp.sum(p, axis=-1, keepdims=True) + jnp.exp(sink - m)
        o = jnp.dot(p.astype(BF16), v_ops[(h // GQA_GROUP, h % 2)][keys(t)], preferred_element_type=F32)
        return o / denom

    work = [(t, h) for t in range(group) for h in range(N_HEADS)]
    ahead = 5
    pending = [scores(*w) for w in work[:ahead]]
    for n, (t, h) in enumerate(work):
        s_cur = pending.pop(0)
        if n + ahead < len(work):
            pending.append(scores(*work[n + ahead]))
        o = head_out(t, h, s_cur)
        if h % 2 == 0:
            o_even = o
        else:
            pair = h // 2
            o_ref[t * BLOCK:(t + 1) * BLOCK, pair * LANES:(pair + 1) * LANES] = (o_even + o).astype(BF16)


def _attn_bias_tables(nb, lead):
    qi = np.arange(BLOCK)[:, None]
    si = np.arange(3 * BLOCK)[None, :]
    rel = np.abs(si - BLOCK - qi).astype(np.float32)
    slopes = np.array([2.0 ** (-8.0 * (h + 1) / N_HEADS) for h in range(N_HEADS)], np.float32)
    base = np.where(rel[None] <= WINDOW, -slopes[:, None, None] * rel[None], np.float32(NEG_INF))
    variants, keys, variant_of_block = [], [], []
    for i in range(nb):
        kpos = (i - 1) * BLOCK + np.arange(3 * BLOCK)
        valid = (kpos >= lead) & (kpos < nb * BLOCK)
        key = valid.tobytes()
        if key not in keys:
            keys.append(key)
            variants.append(np.where(valid[None, None, :], base, np.float32(NEG_INF)))
        variant_of_block.append(keys.index(key))
    return jnp.asarray(np.stack(variants).astype(np.float32)), variant_of_block


def _attention(q, kv, sink, bias, variant_of_block, nb):
    B, lp, _ = q.shape
    group = _pick_tile(nb, (3, 1))
    interior = max(set(variant_of_block), key=variant_of_block.count)

    def variant(i):
        v = jnp.int32(interior)
        for blk, var in enumerate(variant_of_block):
            if var != interior:
                v = jnp.where(i == blk, var, v)
        return v

    kv_spec = lambda t: pl.BlockSpec((None, BLOCK, 2 * KV_WIDTH),
                                     lambda b, g: (b, jnp.clip(group * g + t - 1, 0, nb - 1), 0))
    rows = pl.BlockSpec((None, group * BLOCK, ATTN_WIDTH), lambda b, g: (b, g, 0))
    return pl.pallas_call(
        functools.partial(_attn_kernel, group=group, variant=variant),
        grid=(B, nb // group),
        in_specs=[pl.BlockSpec(memory_space=pltpu.SMEM), rows]
                 + [kv_spec(t) for t in range(group + 2)]
                 + [pl.BlockSpec(bias.shape, lambda b, g: (0, 0, 0, 0))],
        out_specs=rows,
        out_shape=jax.ShapeDtypeStruct((B, lp, ATTN_WIDTH), BF16),
        compiler_params=_params("parallel", "arbitrary"),
        name="attention",
    )(sink, q, *([kv] * (group + 2)), bias)


def _fourier_kernel(w_ref, p_ref, q_ref, o_ref, prev_ref, pe_ref, qo_ref, *, lp, lead, seq_len, tf):
    m = pl.program_id(1)
    last = pl.num_programs(1) - 1
    nblk = lp // tf
    shift = lead + 1
    kp = pe_ref.shape[0]
    half = seq_len // 2
    r = lax.broadcasted_iota(jnp.int32, (tf, tf), 0)
    c = lax.broadcasted_iota(jnp.int32, (tf, tf), 1)
    flip = (r + c == tf - 1).astype(BF16)

    @pl.when(m == 0)
    def _():
        row = lax.broadcasted_iota(jnp.int32, (lp, 1), 0)
        dc = jnp.sum(jnp.where(row >= lead, p_ref[...].astype(F32), 0.0), axis=0, keepdims=True)
        prev_ref[...] = jnp.zeros_like(prev_ref)
        prev_ref[tf - 1:tf, :] = dc * np.float32(seq_len ** -0.5)
        def mirror(x_ref, i):
            above = (jnp.dot(flip, x_ref[(nblk - i) * tf:(nblk - i + 1) * tf, :], preferred_element_type=F32)
                     if i > 0 else jnp.zeros((tf, FOURIER_WIDTH), F32))
            below = jnp.dot(flip, x_ref[(nblk - 1 - i) * tf:(nblk - i) * tf, :], preferred_element_type=F32)
            return jnp.concatenate([above[tf - shift:, :], below[:tf - shift, :]], axis=0)

        for i in range(kp // tf):
            rows = slice(i * tf, (i + 1) * tf)
            n = lax.broadcasted_iota(jnp.int32, (tf, 1), 0) + (i * tf - lead)
            paired = (n >= 1) & (n <= half - 1)
            alone = (n == 0) | (n == half)
            p_blk = p_ref[rows, :].astype(F32)
            pe_ref[rows, :] = jnp.where(paired, p_blk + mirror(p_ref, i),
                                        jnp.where(alone, p_blk, 0.0)).astype(BF16)
            qo_ref[rows, :] = jnp.where(paired, q_ref[rows, :].astype(F32) - mirror(q_ref, i), 0.0).astype(BF16)

    t1 = jnp.dot(w_ref[:, 0:kp], pe_ref[...], preferred_element_type=F32)
    t2 = jnp.dot(w_ref[:, kp:2 * kp], qo_ref[...], preferred_element_type=F32)
    direct = t1 + t2
    mirrored = jnp.dot(flip, (t1 - t2).astype(BF16), preferred_element_type=F32)
    shifted = jnp.concatenate([prev_ref[tf - shift:, :], direct[:tf - shift, :]], axis=0)
    prev_ref[...] = direct

    @pl.when(m < last)
    def _():
        o_ref[pl.ds(pl.multiple_of(m * tf, tf), tf), :] = shifted.astype(BF16)
        o_ref[pl.ds(pl.multiple_of((nblk - 1 - m) * tf, tf), tf), :] = mirrored.astype(BF16)

    @pl.when(m == last)
    def _():
        row = lax.broadcasted_iota(jnp.int32, (tf, 1), 0) + (nblk // 2) * tf
        mid = jnp.where(row <= seq_len // 2 + lead, shifted, mirrored)
        o_ref[(nblk // 2) * tf:(nblk // 2 + 1) * tf, :] = mid.astype(BF16)


def _dft_matrix(nrows, ncols, lead, seq_len):
    w = np.float32(2.0 * math.pi / seq_len)
    k = (jnp.arange(nrows, dtype=jnp.int32) + 1)[:, None]
    j = jnp.arange(ncols // LANES, dtype=jnp.int32)[None, :]
    r = jnp.arange(LANES, dtype=jnp.int32)[None, :]
    a = ((k * (LANES * j - lead)) % seq_len).astype(F32) * w
    b = ((k * r) % seq_len).astype(F32) * w
    ca, sa, cb, sb = jnp.cos(a)[:, :, None], jnp.sin(a)[:, :, None], jnp.cos(b)[:, None, :], jnp.sin(b)[:, None, :]
    n = jnp.arange(ncols, dtype=jnp.int32) - lead
    valid = ((n >= 0) & (n <= seq_len // 2)).reshape(1, ncols // LANES, LANES)
    scale = np.float32(seq_len ** -0.5)
    wc = jnp.where(valid, (ca * cb - sa * sb) * scale, 0.0).reshape(nrows, ncols)
    ws = jnp.where(valid, -(sa * cb + ca * sb) * scale, 0.0).reshape(nrows, ncols)
    return jnp.concatenate([wc, ws], axis=1).astype(BF16)


def _fourier_steps(lp, lead, seq_len, tf):
    nblk = lp // tf
    assert seq_len % 2 == 0 and lp % tf == 0 and nblk % 2 == 1 and lead + 1 < tf
    steps = nblk // 2 + 1
    assert steps * tf > seq_len // 2 + lead
    return steps


def _fourier(w, pq, lead, seq_len, tf):
    B, lp, _ = pq.shape
    steps = _fourier_steps(lp, lead, seq_len, tf)
    kp = steps * tf
    return pl.pallas_call(
        functools.partial(_fourier_kernel, lp=lp, lead=lead, seq_len=seq_len, tf=tf),
        grid=(B, steps),
        in_specs=[
            pl.BlockSpec((tf, 2 * kp), lambda b, m: (m, 0)),
            pl.BlockSpec((None, lp, FOURIER_WIDTH), lambda b, m: (b, 0, 0)),
            pl.BlockSpec((None, lp, FOURIER_WIDTH), lambda b, m: (b, 0, 1)),
        ],
        out_specs=pl.BlockSpec((None, lp, FOURIER_WIDTH), lambda b, m: (b, 0, 0)),
        out_shape=jax.ShapeDtypeStruct((B, lp, FOURIER_WIDTH), BF16),
        scratch_shapes=[pltpu.VMEM((tf, FOURIER_WIDTH), F32),
                        pltpu.VMEM((kp, FOURIER_WIDTH), BF16),
                        pltpu.VMEM((kp, FOURIER_WIDTH), BF16)],
        compiler_params=_params("parallel", "arbitrary"),
        name="fourier",
    )(w, pq, pq)


def _split_bf16(x):
    hi = x.astype(BF16)
    lo = (x - hi.astype(F32)).astype(BF16)
    return hi, lo


def _outproj_kernel(attn_ref, four_ref, gate_ref, h_ref, wa_ref, wf_ref, wo_ref, g_ref, b_ref,
                    wrth_ref, wrtl_ref, h1_ref, h1b_ref, affr_ref, *, alpha):
    ya = jnp.dot(attn_ref[...], wa_ref[...], preferred_element_type=F32)
    yf = jnp.dot(four_ref[...], wf_ref[...], preferred_element_type=F32)
    merged = gate_ref[:, 0:D_MODEL] * ya + gate_ref[:, D_MODEL:GATE_WIDTH] * yf
    mix = jnp.dot(merged.astype(BF16), wo_ref[...], preferred_element_type=F32)
    h1 = _ln(alpha * h_ref[...] + mix, g_ref[...], b_ref[...])
    h1_ref[...] = h1
    hi, lo = _split_bf16(h1)
    h1b_ref[...] = hi
    lr =(lax.dot_general(wrth_ref[...], hi, NT_DIMS, preferred_element_type=F32)
          + lax.dot_general(wrtl_ref[...], hi, NT_DIMS, preferred_element_type=F32)
          + lax.dot_general(wrth_ref[...], lo, NT_DIMS, preferred_element_type=F32))
    er = jnp.exp(lr - jnp.max(lr, axis=0, keepdims=True))
    affr_ref[...] = er / jnp.sum(er, axis=0, keepdims=True)


def _outproj(attn, four, gates, h, wa, wf, wo, g, b, w_router, alpha, tm):
    n = h.shape[0]
    row = lambda i: (i, 0)
    const = lambda i: (0, 0)
    wrth, wrtl = _split_bf16(w_router.T)
    return pl.pallas_call(
        functools.partial(_outproj_kernel, alpha=alpha),
        grid=(n // tm,),
        in_specs=[
            pl.BlockSpec((tm, ATTN_WIDTH), row),
            pl.BlockSpec((tm, FOURIER_WIDTH), row),
            pl.BlockSpec((tm, GATE_WIDTH), row),
            pl.BlockSpec((tm, D_MODEL), row),
            pl.BlockSpec((ATTN_WIDTH, D_MODEL), const),
            pl.BlockSpec((FOURIER_WIDTH, D_MODEL), const),
            pl.BlockSpec((D_MODEL, D_MODEL), const),
            pl.BlockSpec((1, D_MODEL), const),
            pl.BlockSpec((1, D_MODEL), const),
            pl.BlockSpec((N_EXPERTS, D_MODEL), const),
            pl.BlockSpec((N_EXPERTS, D_MODEL), const),
        ],
        out_specs=[
            pl.BlockSpec((tm, D_MODEL), row),
            pl.BlockSpec((tm, D_MODEL), row),
            pl.BlockSpec((N_EXPERTS, tm), lambda i: (0, i)),
        ],
        out_shape=[
            jax.ShapeDtypeStruct((n, D_MODEL), F32),
            jax.ShapeDtypeStruct((n, D_MODEL), BF16),
            jax.ShapeDtypeStruct((N_EXPERTS, n), F32),
        ],
        compiler_params=_params("parallel"),
        name="outproj_ln_router",
    )(attn, four, gates, h, wa, wf, wo, g.reshape(1, -1), b.reshape(1, -1), wrth, wrtl)


def _topk_kernel(aff_ref, rr_ref, rc_ref, cnt_ref, cntc_ref, *, lead, lp, cap):
    nchunk = lp // LANES
    lane = lax.broadcasted_iota(jnp.int32, (N_EXPERTS, lp), 1)
    bits = jnp.where(lane >= lead, pltpu.bitcast(aff_ref[...], jnp.int32), -1)

    def search(i, t):
        cand = t | (jnp.int32(1) << (30 - i))
        cnt = jnp.sum((bits >= cand).astype(jnp.int32), axis=-1, keepdims=True)
        return jnp.where(cnt >= cap, cand, t)

    thr = lax.fori_loop(0, 31, search, jnp.zeros((N_EXPERTS, 1), jnp.int32))
    gt = bits > thr
    eq = bits == thr
    need = cap - jnp.sum(gt.astype(jnp.int32), axis=-1, keepdims=True)

    r = lax.broadcasted_iota(jnp.int32, (LANES, LANES), 0)
    c = lax.broadcasted_iota(jnp.int32, (LANES, LANES), 1)
    upper = (r <= c).astype(BF16)
    lower = (c <= r).astype(BF16)
    ident = (c == r).astype(BF16)

    chunks = [slice(k * LANES, (k + 1) * LANES) for k in range(nchunk)]
    eqb = eq.astype(BF16)
    needf = need.astype(F32)
    pre_eq = [jnp.dot(eqb[:, sl], upper, preferred_element_type=F32) for sl in chunks]
    off = jnp.zeros((N_EXPERTS, 1), F32)
    sel_chunks = []
    for k, sl in enumerate(chunks):
        sel_chunks.append(gt[:, sl] | (eq[:, sl] & (pre_eq[k] + off <= needf)))
        off = off + pre_eq[k][:, LANES - 1:LANES]

    selb = [sel.astype(BF16) for sel in sel_chunks]
    pre_r = [jnp.dot(s, upper, preferred_element_type=F32) for s in selb]
    pre_c = [lax.dot_general(lower, s, NT_DIMS, preferred_element_type=F32) for s in selb]
    sel_c = [lax.dot_general(ident, s, NT_DIMS, preferred_element_type=F32) for s in selb]
    off_r = jnp.zeros((N_EXPERTS, 1), F32)
    off_c = jnp.zeros((1, N_EXPERTS), F32)
    cnt_lane = lax.broadcasted_iota(jnp.int32, (N_EXPERTS, LANES), 1)
    cnt = jnp.zeros((N_EXPERTS, LANES), F32)
    for k, sl in enumerate(chunks):
        rr_ref[k] = jnp.where(sel_chunks[k], pre_r[k] + (off_r - 1.0), -1.0).astype(jnp.int32)
        off_r = off_r + pre_r[k][:, LANES - 1:LANES]
        cnt = jnp.where(cnt_lane == k + 1, off_r, cnt)
        rc_ref[sl, :] = jnp.where(sel_c[k] > 0.5, pre_c[k] + (off_c - 1.0), -1.0).astype(jnp.int32)
        off_c = off_c + pre_c[k][LANES - 1:LANES, :]
        cntc_ref[k + 1:k + 2, :] = off_c.astype(jnp.int32)
    cnt_ref[...] = cnt.astype(jnp.int32)
    cntc_ref[0:1, :] = jnp.zeros((1, N_EXPERTS), jnp.int32)
    pad_rows = cntc_ref.shape[0] - nchunk - 1
    if pad_rows:
        cntc_ref[nchunk + 1:, :] = jnp.zeros((pad_rows, N_EXPERTS), jnp.int32)


def _topk(aff_r, B, lp, lead, cap):
    n = B * lp
    nchunk = lp // LANES
    assert nchunk + 1 <= LANES
    nb1 = -(-(nchunk + 1) // SUBLANES) * SUBLANES
    return pl.pallas_call(
        functools.partial(_topk_kernel, lead=lead, lp=lp, cap=cap),
        grid=(B,),
        in_specs=[pl.BlockSpec((N_EXPERTS, lp), lambda b: (0, b))],
        out_specs=[
            pl.BlockSpec((None, nchunk, N_EXPERTS, LANES), lambda b: (b, 0, 0, 0)),
            pl.BlockSpec((lp, N_EXPERTS), lambda b: (b, 0)),
            pl.BlockSpec((None, N_EXPERTS, LANES), lambda b: (b, 0, 0)),
            pl.BlockSpec((None, nb1, N_EXPERTS), lambda b: (b, 0, 0)),
        ],
        out_shape=[
            jax.ShapeDtypeStruct((B, nchunk, N_EXPERTS, LANES), jnp.int32),
            jax.ShapeDtypeStruct((n, N_EXPERTS), jnp.int32),
            jax.ShapeDtypeStruct((B, N_EXPERTS, LANES), jnp.int32),
            jax.ShapeDtypeStruct((B, nb1, N_EXPERTS), jnp.int32),
        ],
        compiler_params=_params("parallel"),
        name="topk_select",
    )(aff_r)


SLOT_ALIGN = 16


def _window_constants(win):
    wide = N_EXPERTS * win
    expand = np.zeros((N_EXPERTS, wide), np.float32)
    for e in range(N_EXPERTS):
        expand[e, e * win:(e + 1) * win] = 1.0
    return expand, (np.arange(wide) % win).astype(np.float32)


def _gather_kernel(cnt_ref, cntv_ref, rank_ref, aff_ref, h_ref, o_ref, g_ref,
                   *, slots, ch, win, nchunk):
    bi, c = pl.program_id(0), pl.program_id(1)
    k0 = c * ch
    tt = ch * LANES
    max_start = slots - win

    @pl.when(c == 0)
    def _():
        o_ref[...] = jnp.zeros_like(o_ref)
        g_ref[...] = jnp.zeros_like(g_ref)

    lane = lax.broadcasted_iota(jnp.int32, cntv_ref.shape, 1)
    lo_col = jnp.sum(jnp.where(lane == k0, cntv_ref[...], 0), axis=-1, keepdims=True)
    w_col = jnp.minimum(lo_col & -SLOT_ALIGN, max_start)
    rank = rank_ref[...]
    aff = aff_ref[...]
    rel = rank - w_col
    in_win = (rank >= 0) & (rel >= 0) & (rel < win)
    rel = jnp.where(in_win, rel, -1)
    row = lax.broadcasted_iota(jnp.int32, (win, tt), 0)
    hit_all = jnp.concatenate([rel[e:e + 1, :] == row for e in range(N_EXPERTS)], axis=0)
    rows = h_ref[...]
    xw = jnp.dot(hit_all.astype(BF16), rows, preferred_element_type=F32)

    def add_rows(e, start, hit, vals):
        dst = pl.ds(pl.multiple_of(start, SLOT_ALIGN), win)
        o_ref[e, dst, :] = o_ref[e, dst, :] + vals.astype(BF16)
        g_ref[e, dst, :] += jnp.sum(jnp.where(hit, aff[e:e + 1, :], 0.0), axis=-1, keepdims=True)

    for e in range(N_EXPERTS):
        base = (bi * N_EXPERTS + e) * (nchunk + 1) + k0
        w = jnp.minimum(cnt_ref[base] & -SLOT_ALIGN, max_start)
        add_rows(e, w, hit_all[e * win:(e + 1) * win, :], xw[e * win:(e + 1) * win, :])
        hi = cnt_ref[base + ch]

        @pl.when(hi > w + win)
        def _(e=e, w=w, hi=hi):
            rk = rank[e:e + 1, :]

            def extra(i, carry):
                w2 = w + (i + 1) * win
                w2c = jnp.minimum(w2, max_start)
                hit = ((rk - w2c) == row) & (rk >= w2)
                add_rows(e, w2c, hit, jnp.dot(hit.astype(BF16), rows, preferred_element_type=F32))
                return carry

            lax.fori_loop(0, (hi - w - 1) // win, extra, 0)


def _gather(cnt, cntv, rank_rows, aff_rows, h1b, B, lp, slots, ch, win):
    nchunk = lp // LANES
    tt = ch * LANES
    nt = nchunk // ch
    assert slots % SLOT_ALIGN == 0 and win % SLOT_ALIGN == 0 and win <= slots
    per_seq = lambda bi, c, cnt: (bi, 0, 0, 0)
    return pl.pallas_call(
        functools.partial(_gather_kernel, slots=slots, ch=ch, win=win, nchunk=nchunk),
        grid_spec=pltpu.PrefetchScalarGridSpec(
            num_scalar_prefetch=1,
            grid=(B, nt),
            in_specs=[
                pl.BlockSpec((None, N_EXPERTS, LANES), lambda bi, c, cnt: (bi, 0, 0)),
                pl.BlockSpec((None, N_EXPERTS, tt), lambda bi, c, cnt: (bi, 0, c)),
                pl.BlockSpec((None, N_EXPERTS, tt), lambda bi, c, cnt: (bi, 0, c)),
                pl.BlockSpec((tt, D_MODEL), lambda bi, c, cnt: (bi * nt + c, 0)),
            ],
            out_specs=[pl.BlockSpec((None, N_EXPERTS, slots, D_MODEL), per_seq),
                       pl.BlockSpec((None, N_EXPERTS, slots, 1), per_seq)],
        ),
        out_shape=[jax.ShapeDtypeStruct((B, N_EXPERTS, slots, D_MODEL), BF16),
                   jax.ShapeDtypeStruct((B, N_EXPERTS, slots, 1), F32)],
        compiler_params=_params("parallel", "arbitrary"),
        name="moe_gather",
    )(cnt, cntv, rank_rows, aff_rows, h1b)


def _ffn_up_kernel(x_ref, wg_ref, wu_ref, o_ref, wgb_ref, wub_ref):
    @pl.when(pl.program_id(1) == 0)
    def _():
        wgb_ref[...] = wg_ref[...].astype(BF16)
        wub_ref[...] = wu_ref[...].astype(BF16)

    nseq, slots, width = x_ref.shape
    x = x_ref[...].reshape(nseq * slots, width)
    a = jnp.dot(x, wgb_ref[...], preferred_element_type=F32)
    u = jnp.dot(x, wub_ref[...], preferred_element_type=F32)
    o_ref[...] = (a * jax.nn.sigmoid(a) * u).astype(BF16).reshape(o_ref.shape)


def _ffn_down_kernel(x_ref, g_ref, wd_ref, o_ref, wdb_ref):
    @pl.when(pl.program_id(1) == 0)
    def _():
        wdb_ref[...] = wd_ref[...].astype(BF16)

    nseq, slots, width = x_ref.shape
    y = jnp.dot(x_ref[...].reshape(nseq * slots, width), wdb_ref[...], preferred_element_type=F32)
    y = y * g_ref[...].reshape(nseq * slots, 1)
    o_ref[...] = y.astype(BF16).reshape(o_ref.shape)


def _ffn(xg, gates, wg, wu, wd, layer):
    B, _, slots, _ = xg.shape
    n_up = _pick_tile(B, (2, 1))
    n_down = _pick_tile(B, (4, 2, 1))
    acts = lambda nseq, width: pl.BlockSpec((nseq, None, slots, width), lambda e, b: (b, e, 0, 0))
    weight = lambda rows, cols: pl.BlockSpec((None, None, rows, cols), lambda e, b: (layer, e, 0, 0))
    mid = pl.pallas_call(
        _ffn_up_kernel,
        grid=(N_EXPERTS, B // n_up),
        in_specs=[acts(n_up, D_MODEL), weight(D_MODEL, D_FF_EXPERT), weight(D_MODEL, D_FF_EXPERT)],
        out_specs=acts(n_up, D_FF_EXPERT),
        out_shape=jax.ShapeDtypeStruct((B, N_EXPERTS, slots, D_FF_EXPERT), BF16),
        scratch_shapes=[pltpu.VMEM((D_MODEL, D_FF_EXPERT), BF16), pltpu.VMEM((D_MODEL, D_FF_EXPERT), BF16)],
        compiler_params=_params("arbitrary", "arbitrary"),
        name="moe_ffn_up",
    )(xg, wg, wu)
    return pl.pallas_call(
        _ffn_down_kernel,
        grid=(N_EXPERTS, B // n_down),
        in_specs=[acts(n_down, D_FF_EXPERT), acts(n_down, 1), weight(D_FF_EXPERT, D_MODEL)],
        out_specs=acts(n_down, D_MODEL),
        out_shape=jax.ShapeDtypeStruct(xg.shape, BF16),
        scratch_shapes=[pltpu.VMEM((D_FF_EXPERT, D_MODEL), BF16)],
        compiler_params=_params("arbitrary", "arbitrary"),
        name="moe_ffn_down",
    )(mid, gates, wd)


def _combine_kernel(cnt_ref, cntc_ref, rank_ref, y_ref, h_ref, g_ref, b_ref, expand_ref, rpat_ref,
                    o_ref, acc_ref, *, slots, ch, win, nchunk, alpha):
    bi, c = pl.program_id(0), pl.program_id(1)
    k0 = c * ch
    tt = ch * LANES
    max_start = slots - win
    rank = rank_ref[...]
    lo_row = cntc_ref[pl.ds(k0, 1), :]
    w_row = jnp.minimum(lo_row & -SLOT_ALIGN, max_start)
    rel = rank - w_row
    in_win = (rank >= 0) & (rel >= 0) & (rel < win)
    relb = jnp.where(in_win, rel, -1).astype(F32).astype(BF16)
    rel_wide = jnp.dot(relb, expand_ref[...], preferred_element_type=F32)
    onehot = (rel_wide == rpat_ref[...]).astype(BF16)

    starts, windows = [], []
    for e in range(N_EXPERTS):
        lo = cnt_ref[(bi * N_EXPERTS + e) * (nchunk + 1) + k0]
        w = jnp.minimum(lo & -SLOT_ALIGN, max_start)
        starts.append(w)
        windows.append(y_ref[e, pl.ds(pl.multiple_of(w, SLOT_ALIGN), win), :])
    acc_ref[...] = jnp.dot(onehot, jnp.concatenate(windows, axis=0), preferred_element_type=F32)

    lane = lax.broadcasted_iota(jnp.int32, rank.shape, 1)
    col = lax.broadcasted_iota(jnp.int32, (tt, win), 1)
    for e in range(N_EXPERTS):
        hi = cnt_ref[(bi * N_EXPERTS + e) * (nchunk + 1) + k0 + ch]

        @pl.when(hi > starts[e] + win)
        def _(e=e, hi=hi):
            rk = jnp.sum(jnp.where(lane == e, rank, 0), axis=-1, keepdims=True)

            def extra(i, carry):
                w2 = starts[e] + (i + 1) * win
                w2c = jnp.minimum(w2, max_start)
                rows = y_ref[e, pl.ds(pl.multiple_of(w2c, SLOT_ALIGN), win), :]
                hit = ((rk - w2c) == col) & (rk >= w2)
                acc_ref[...] += jnp.dot(hit.astype(BF16), rows, preferred_element_type=F32)
                return carry

            lax.fori_loop(0, (hi - starts[e] - 1) // win, extra, 0)

    o_ref[...] = _ln(alpha * h_ref[...] + acc_ref[...], g_ref[...], b_ref[...])


def _combine(cnt, cntc, rank_c, y, h1, g, b, B, lp, alpha, ch, win):
    n = B * lp
    slots = y.shape[2]
    nchunk = lp // LANES
    tt = ch * LANES
    nt = nchunk // ch
    assert slots % SLOT_ALIGN == 0 and win % SLOT_ALIGN == 0 and win <= slots
    wide = N_EXPERTS * win
    expand, pos = _window_constants(win)
    tok = lambda bi, c, cnt: (bi * nt + c, 0)
    const = lambda bi, c, cnt: (0, 0)
    return pl.pallas_call(
        functools.partial(_combine_kernel, slots=slots, ch=ch, win=win, nchunk=nchunk, alpha=alpha),
        grid_spec=pltpu.PrefetchScalarGridSpec(
            num_scalar_prefetch=1,
            grid=(B, nt),
            in_specs=[
                pl.BlockSpec((None, cntc.shape[1], N_EXPERTS), lambda bi, c, cnt: (bi, 0, 0)),
                pl.BlockSpec((tt, N_EXPERTS), tok),
                pl.BlockSpec((None, N_EXPERTS, slots, D_MODEL), lambda bi, c, cnt: (bi, 0, 0, 0)),
                pl.BlockSpec((tt, D_MODEL), tok),
                pl.BlockSpec((1, D_MODEL), const),
                pl.BlockSpec((1, D_MODEL), const),
                pl.BlockSpec((N_EXPERTS, wide), const),
                pl.BlockSpec((1, wide), const),
            ],
            out_specs=pl.BlockSpec((tt, D_MODEL), tok),
            scratch_shapes=[pltpu.VMEM((tt, D_MODEL), F32)],
        ),
        out_shape=jax.ShapeDtypeStruct((n, D_MODEL), F32),
        compiler_params=_params("parallel", "arbitrary"),
        name="moe_combine_ln",
    )(cnt, cntc, rank_c, y, h1, g.reshape(1, -1), b.reshape(1, -1),
      jnp.asarray(expand, dtype=BF16), jnp.asarray(pos.reshape(1, wide)))


def kernel(x, meta, ln0_g, ln0_b, w_in, b_gate, sink, w_attn_o, w_four_o, w_out, ln1_g, ln1_b,
           w_router, w_e_gate, w_e_up, w_e_down, ln2_g, ln2_b):
    B, seq, d = x.shape
    depth = w_in.shape[0]
    assert d == D_MODEL and meta.shape == (N_META, D_MODEL)
    assert seq % BLOCK == 0 and N_META % SUBLANES == 0 and N_META <= BLOCK
    L = seq + N_META
    nb = -(-L // BLOCK)
    lp = nb * BLOCK
    lead = lp - L
    cap = CAPACITY_FACTOR * L // N_EXPERTS
    slots = -(-cap // SLOT_ALIGN) * SLOT_ALIGN
    alpha = float((2 * depth) ** 0.25)
    n = B * lp
    nchunk = lp // LANES
    tm = _pick_tile(n, (512, 256, 128))
    tm_in = _pick_tile(n, (1024, 512, 256, 128))
    tf = _pick_tile(lp, (384, 128))
    ch = _pick_tile(nchunk, (3, 2, 1))
    win = min(slots, -(-(ch * LANES * cap // L * 4 // 3 + SLOT_ALIGN) // SLOT_ALIGN) * SLOT_ALIGN)

    gi = np.arange(FOURIER_GROUP)
    ang = 2.0 * np.pi * ((gi[:, None] * gi[None, :]) % FOURIER_GROUP) / FOURIER_GROUP
    cs = jnp.asarray(np.concatenate([np.cos(ang), np.sin(ang)], axis=1) * FOURIER_GROUP ** -0.5, dtype=BF16)
    dft_rows = _fourier_steps(lp, lead, L, tf) * tf
    wdft = _dft_matrix(dft_rows, dft_rows, lead, L)
    bias, variant_of_block = _attn_bias_tables(nb, lead)

    h = _embed(x, meta, ln0_g, ln0_b, nb, lead).reshape(n, D_MODEL)
    for l in range(depth):
        q, kv, pq, gates = _inproj(h, w_in[l].astype(BF16), b_gate[l], cs, tm_in)
        attn = _attention(q.reshape(B, lp, -1), kv.reshape(B, lp, -1), sink[l], bias, variant_of_block, nb)
        four = _fourier(wdft, pq.reshape(B, lp, -1), lead, L, tf)
        h1, h1b, aff_r = _outproj(
            attn.reshape(n, -1), four.reshape(n, -1), gates, h,
            w_attn_o[l].astype(BF16), w_four_o[l].astype(BF16), w_out[l].astype(BF16),
            ln1_g[l], ln1_b[l], w_router[l], alpha, tm)
        rank_r, rank_c, cntv, cntc = _topk(aff_r, B, lp, lead, cap)
        cnt = cntv[:, :, :nchunk + 1].reshape(-1)
        rank_rows = rank_r.transpose(0, 2, 1, 3).reshape(B, N_EXPERTS, lp)
        aff_rows = aff_r.reshape(N_EXPERTS, B, lp).transpose(1, 0, 2)
        xg, gsel = _gather(cnt, cntv, rank_rows, aff_rows, h1b, B, lp, slots, ch, win)
        y = _ffn(xg, gsel, w_e_gate, w_e_up, w_e_down, l)
        h = _combine(cnt, cntc, rank_c, y, h1, ln2_g[l], ln2_b[l], B, lp, alpha, ch, win)
    return h.reshape(B, lp, D_MODEL)[:, lead + N_META:]
```

```python
import functools
import math

import numpy as np
import jax
import jax.numpy as jnp
from jax import lax
from jax.experimental import pallas as pl
from jax.experimental.pallas import tpu as pltpu

D_MODEL = 1024
N_META = 16
N_HEADS = 8
N_KV_HEADS = 2
HEAD_DIM = 64
GQA_GROUP = N_HEADS // N_KV_HEADS
ATTN_WIDTH = N_HEADS * HEAD_DIM
KV_WIDTH = N_KV_HEADS * HEAD_DIM
WINDOW = 128
BLOCK = 128
N_FOURIER_GROUPS = 4
FOURIER_GROUP = 128
FOURIER_WIDTH = N_FOURIER_GROUPS * FOURIER_GROUP
N_BRANCHES = 2
GATE_WIDTH = N_BRANCHES * D_MODEL
N_EXPERTS = 16
CAPACITY_FACTOR = 2
D_FF_EXPERT = 1536
LN_EPS = 1e-5
NEG_INF = -1e30
Q_END = ATTN_WIDTH
K_END = Q_END + KV_WIDTH
V_END = K_END + KV_WIDTH
F_END = V_END + FOURIER_WIDTH
IN_WIDTH = F_END + GATE_WIDTH

LANES = 128
SUBLANES = 8
VMEM_LIMIT_BYTES = 56 * 1024 * 1024

F32 = jnp.float32
BF16 = jnp.bfloat16
NT_DIMS = (((1,), (1,)), ((), ()))


def _pick_tile(n, candidates):
    for c in candidates:
        if n % c == 0:
            return c
    raise ValueError(f"no tile in {candidates} divides {n}")


def _params(*sem):
    return pltpu.CompilerParams(dimension_semantics=sem, vmem_limit_bytes=VMEM_LIMIT_BYTES)


def _ln(x, g, b):
    mu = jnp.mean(x, axis=-1, keepdims=True)
    xc = x - mu
    var = jnp.mean(xc * xc, axis=-1, keepdims=True)
    return xc * lax.rsqrt(var + LN_EPS) * g + b


def _embed_kernel(*refs, lead, group):
    x_refs, (meta_ref, g_ref, b_ref, o_ref) = refs[:group], refs[group:]
    for k in range(1, group):
        o_ref[k * BLOCK:(k + 1) * BLOCK, :] = _ln(x_refs[k][...], g_ref[...], b_ref[...])

    @pl.when(pl.program_id(1) == 0)
    def _():
        o_ref[0:lead, :] = jnp.zeros((lead, D_MODEL), F32)
        o_ref[lead:BLOCK, :] = _ln(meta_ref[...], g_ref[...], b_ref[...])

    @pl.when(pl.program_id(1) > 0)
    def _():
        o_ref[0:BLOCK, :] = _ln(x_refs[0][...], g_ref[...], b_ref[...])


def _embed(x, meta, g, b, nb, lead):
    B = x.shape[0]
    group = _pick_tile(nb, (11, 3, 1))
    x_spec = lambda k: pl.BlockSpec((None, BLOCK, D_MODEL),
                                    lambda bi, j: (bi, jnp.maximum(group * j + k - 1, 0), 0))
    const = lambda bi, j: (0, 0)
    return pl.pallas_call(
        functools.partial(_embed_kernel, lead=lead, group=group),
        grid=(B, nb // group),
        in_specs=[x_spec(k) for k in range(group)] + [
            pl.BlockSpec((N_META, D_MODEL), const),
            pl.BlockSpec((1, D_MODEL), const),
            pl.BlockSpec((1, D_MODEL), const),
        ],
        out_specs=pl.BlockSpec((None, group * BLOCK, D_MODEL), lambda bi, j: (bi, j, 0)),
        out_shape=jax.ShapeDtypeStruct((B, nb * BLOCK, D_MODEL), F32),
        compiler_params=_params("parallel", "arbitrary"),
        name="embed_ln",
    )(*([x] * group), meta, g.reshape(1, -1), b.reshape(1, -1))


def _inproj_kernel(h_ref, w_ref, bg_ref, cs_ref, q_ref, kv_ref, pq_ref, gate_ref):
    hb = h_ref[...].astype(BF16)
    q_ref[...] = jnp.dot(hb, w_ref[:, 0:Q_END], preferred_element_type=F32).astype(BF16)
    kv_ref[...] = jnp.dot(hb, w_ref[:, Q_END:V_END], preferred_element_type=F32).astype(BF16)
    uf = jnp.dot(hb, w_ref[:, V_END:F_END], preferred_element_type=F32).astype(BF16)
    for g in range(N_FOURIER_GROUPS):
        lo = g * FOURIER_GROUP
        pq = jnp.dot(uf[:, lo:lo + FOURIER_GROUP], cs_ref[...], preferred_element_type=F32)
        pq_ref[:, lo:lo + FOURIER_GROUP] = pq[:, 0:FOURIER_GROUP].astype(BF16)
        pq_ref[:, FOURIER_WIDTH + lo:FOURIER_WIDTH + lo + FOURIER_GROUP] = (
            pq[:, FOURIER_GROUP:2 * FOURIER_GROUP].astype(BF16))
    chunk = FOURIER_WIDTH
    for c in range(GATE_WIDTH // chunk):
        lo = c * chunk
        ug = jnp.dot(hb, w_ref[:, F_END + lo:F_END + lo + chunk], preferred_element_type=F32)
        gate_ref[:, lo:lo + chunk] = jax.nn.sigmoid(ug + bg_ref[:, lo:lo + chunk])


def _inproj(h, w_bf16, b_gate, cs, tm):
    n = h.shape[0]
    row = lambda i: (i, 0)
    const = lambda i: (0, 0)
    return pl.pallas_call(
        _inproj_kernel,
        grid=(n // tm,),
        in_specs=[
            pl.BlockSpec((tm, D_MODEL), row),
            pl.BlockSpec((D_MODEL, IN_WIDTH), const),
            pl.BlockSpec((1, GATE_WIDTH), const),
            pl.BlockSpec((FOURIER_GROUP, 2 * FOURIER_GROUP), const),
        ],
        out_specs=[
            pl.BlockSpec((tm, ATTN_WIDTH), row),
            pl.BlockSpec((tm, 2 * KV_WIDTH), row),
            pl.BlockSpec((tm, 2 * FOURIER_WIDTH), row),
            pl.BlockSpec((tm, GATE_WIDTH), row),
        ],
        out_shape=[
            jax.ShapeDtypeStruct((n, ATTN_WIDTH), BF16),
            jax.ShapeDtypeStruct((n, 2 * KV_WIDTH), BF16),
            jax.ShapeDtypeStruct((n, 2 * FOURIER_WIDTH), BF16),
            jax.ShapeDtypeStruct((n, GATE_WIDTH), F32),
        ],
        compiler_params=_params("parallel"),
        name="inproj",
    )(h, w_bf16, b_gate.reshape(1, GATE_WIDTH), cs)


SCORES_AHEAD = 5


def _attn_kernel(sink_ref, q_ref, *refs, group, variant):
    kv_refs, (bias_ref, o_ref) = refs[:group + 2], refs[group + 2:]
    first_block = pl.program_id(1) * group
    kv = jnp.concatenate([r[...] for r in kv_refs], axis=0)
    low_half = lax.broadcasted_iota(jnp.int32, (kv.shape[0], LANES), 1) < HEAD_DIM

    def lane_half_operands(x):
        swapped = jnp.concatenate([x[:, HEAD_DIM:], x[:, :HEAD_DIM]], axis=1)
        zero = jnp.zeros_like(x)
        return {(kvh, half): jnp.where(low_half if half == 0 else ~low_half,
                                       x if kvh == half else swapped, zero)
                for kvh in range(N_KV_HEADS) for half in range(2)}

    k_ops = lane_half_operands(kv[:, 0:KV_WIDTH])
    v_ops = lane_half_operands(kv[:, KV_WIDTH:2 * KV_WIDTH])
    keys = lambda t: slice(t * BLOCK, (t + 3) * BLOCK)

    def scores(t, h):
        pair = h // 2
        qp = q_ref[t * BLOCK:(t + 1) * BLOCK, pair * LANES:(pair + 1) * LANES]
        return lax.dot_general(qp, k_ops[(h // GQA_GROUP, h % 2)][keys(t)], NT_DIMS,
                               preferred_element_type=F32)

    def head_out(t, h, s):
        sink = sink_ref[h]
        logits = s * (HEAD_DIM ** -0.5) + bias_ref[variant(first_block + t), h]
        m = jnp.maximum(jnp.max(logits, axis=-1, keepdims=True), sink)
        p = jnp.exp(logits - m)
        denom = jnp.sum(p, axis=-1, keepdims=True) + jnp.exp(sink - m)
        o = jnp.dot(p.astype(BF16), v_ops[(h // GQA_GROUP, h % 2)][keys(t)], preferred_element_type=F32)
        return o / denom

    work = [(t, h) for t in range(group) for h in range(N_HEADS)]
    ahead = SCORES_AHEAD
    pending = [scores(*w) for w in work[:ahead]]
    for n, (t, h) in enumerate(work):
        s_cur = pending.pop(0)
        if n + ahead < len(work):
            pending.append(scores(*work[n + ahead]))
        o = head_out(t, h, s_cur)
        if h % 2 == 0:
            o_even = o
        else:
            pair = h // 2
            o_ref[t * BLOCK:(t + 1) * BLOCK, pair * LANES:(pair + 1) * LANES] = (o_even + o).astype(BF16)


def _attn_bias_tables(nb, lead):
    qi = np.arange(BLOCK)[:, None]
    si = np.arange(3 * BLOCK)[None, :]
    rel = np.abs(si - BLOCK - qi).astype(np.float32)
    slopes = np.array([2.0 ** (-8.0 * (h + 1) / N_HEADS) for h in range(N_HEADS)], np.float32)
    base = np.where(rel[None] <= WINDOW, -slopes[:, None, None] * rel[None], np.float32(NEG_INF))
    variants, keys, variant_of_block = [], [], []
    for i in range(nb):
        kpos = (i - 1) * BLOCK + np.arange(3 * BLOCK)
        valid = (kpos >= lead) & (kpos < nb * BLOCK)
        key = valid.tobytes()
        if key not in keys:
            keys.append(key)
            variants.append(np.where(valid[None, None, :], base, np.float32(NEG_INF)))
        variant_of_block.append(keys.index(key))
    return jnp.asarray(np.stack(variants).astype(np.float32)), variant_of_block


def _attention(q, kv, sink, bias, variant_of_block, nb):
    B, lp, _ = q.shape
    group = _pick_tile(nb, (3, 1))
    interior = max(set(variant_of_block), key=variant_of_block.count)

    def variant(i):
        v = jnp.int32(interior)
        for blk, var in enumerate(variant_of_block):
            if var != interior:
                v = jnp.where(i == blk, var, v)
        return v

    kv_spec = lambda t: pl.BlockSpec((None, BLOCK, 2 * KV_WIDTH),
                                     lambda b, g: (b, jnp.clip(group * g + t - 1, 0, nb - 1), 0))
    rows = pl.BlockSpec((None, group * BLOCK, ATTN_WIDTH), lambda b, g: (b, g, 0))
    return pl.pallas_call(
        functools.partial(_attn_kernel, group=group, variant=variant),
        grid=(B, nb // group),
        in_specs=[pl.BlockSpec(memory_space=pltpu.SMEM), rows]
                 + [kv_spec(t) for t in range(group + 2)]
                 + [pl.BlockSpec(bias.shape, lambda b, g: (0, 0, 0, 0))],
        out_specs=rows,
        out_shape=jax.ShapeDtypeStruct((B, lp, ATTN_WIDTH), BF16),
        compiler_params=_params("parallel", "arbitrary"),
        name="attention",
    )(sink, q, *([kv] * (group + 2)), bias)


def _fourier_kernel(w_ref, p_ref, q_ref, o_ref, prev_ref, pe_ref, qo_ref, *, lp, lead, seq_len, tf):
    m = pl.program_id(1)
    last = pl.num_programs(1) - 1
    nblk = lp // tf
    shift = lead + 1
    kp = pe_ref.shape[0]
    half = seq_len // 2
    r = lax.broadcasted_iota(jnp.int32, (tf, tf), 0)
    c = lax.broadcasted_iota(jnp.int32, (tf, tf), 1)
    flip = (r + c == tf - 1).astype(BF16)

    @pl.when(m == 0)
    def _():
        row = lax.broadcasted_iota(jnp.int32, (lp, 1), 0)
        dc = jnp.sum(jnp.where(row >= lead, p_ref[...].astype(F32), 0.0), axis=0, keepdims=True)
        prev_ref[...] = jnp.zeros_like(prev_ref)
        prev_ref[tf - 1:tf, :] = dc * np.float32(seq_len ** -0.5)
        def mirror(x_ref, i):
            above = (jnp.dot(flip, x_ref[(nblk - i) * tf:(nblk - i + 1) * tf, :], preferred_element_type=F32)
                     if i > 0 else jnp.zeros((tf, FOURIER_WIDTH), F32))
            below = jnp.dot(flip, x_ref[(nblk - 1 - i) * tf:(nblk - i) * tf, :], preferred_element_type=F32)
            return jnp.concatenate([above[tf - shift:, :], below[:tf - shift, :]], axis=0)

        for i in range(kp // tf):
            rows = slice(i * tf, (i + 1) * tf)
            n = lax.broadcasted_iota(jnp.int32, (tf, 1), 0) + (i * tf - lead)
            paired = (n >= 1) & (n <= half - 1)
            alone = (n == 0) | (n == half)
            p_blk = p_ref[rows, :].astype(F32)
            pe_ref[rows, :] = jnp.where(paired, p_blk + mirror(p_ref, i),
                                        jnp.where(alone, p_blk, 0.0)).astype(BF16)
            qo_ref[rows, :] = jnp.where(paired, q_ref[rows, :].astype(F32) - mirror(q_ref, i), 0.0).astype(BF16)

    t1 = jnp.dot(w_ref[:, 0:kp], pe_ref[...], preferred_element_type=F32)
    t2 = jnp.dot(w_ref[:, kp:2 * kp], qo_ref[...], preferred_element_type=F32)
    direct = t1 + t2
    mirrored = jnp.dot(flip, (t1 - t2).astype(BF16), preferred_element_type=F32)
    shifted = jnp.concatenate([prev_ref[tf - shift:, :], direct[:tf - shift, :]], axis=0)
    prev_ref[...] = direct

    @pl.when(m < last)
    def _():
        o_ref[pl.ds(pl.multiple_of(m * tf, tf), tf), :] = shifted.astype(BF16)
        o_ref[pl.ds(pl.multiple_of((nblk - 1 - m) * tf, tf), tf), :] = mirrored.astype(BF16)

    @pl.when(m == last)
    def _():
        row = lax.broadcasted_iota(jnp.int32, (tf, 1), 0) + (nblk // 2) * tf
        mid = jnp.where(row <= seq_len // 2 + lead, shifted, mirrored)
        o_ref[(nblk // 2) * tf:(nblk // 2 + 1) * tf, :] = mid.astype(BF16)


def _dft_matrix(nrows, ncols, lead, seq_len):
    w = np.float32(2.0 * math.pi / seq_len)
    k = (jnp.arange(nrows, dtype=jnp.int32) + 1)[:, None]
    j = jnp.arange(ncols // LANES, dtype=jnp.int32)[None, :]
    r = jnp.arange(LANES, dtype=jnp.int32)[None, :]
    a = ((k * (LANES * j - lead)) % seq_len).astype(F32) * w
    b = ((k * r) % seq_len).astype(F32) * w
    ca, sa, cb, sb = jnp.cos(a)[:, :, None], jnp.sin(a)[:, :, None], jnp.cos(b)[:, None, :], jnp.sin(b)[:, None, :]
    n = jnp.arange(ncols, dtype=jnp.int32) - lead
    valid = ((n >= 0) & (n <= seq_len // 2)).reshape(1, ncols // LANES, LANES)
    scale = np.float32(seq_len ** -0.5)
    wc = jnp.where(valid, (ca * cb - sa * sb) * scale, 0.0).reshape(nrows, ncols)
    ws = jnp.where(valid, -(sa * cb + ca * sb) * scale, 0.0).reshape(nrows, ncols)
    return jnp.concatenate([wc, ws], axis=1).astype(BF16)


def _fourier_steps(lp, lead, seq_len, tf):
    nblk = lp // tf
    assert seq_len % 2 == 0 and lp % tf == 0 and nblk % 2 == 1 and lead + 1 < tf
    steps = nblk // 2 + 1
    assert steps * tf > seq_len // 2 + lead
    return steps


def _fourier(w, pq, lead, seq_len, tf):
    B, lp, _ = pq.shape
    steps = _fourier_steps(lp, lead, seq_len, tf)
    kp = steps * tf
    return pl.pallas_call(
        functools.partial(_fourier_kernel, lp=lp, lead=lead, seq_len=seq_len, tf=tf),
        grid=(B, steps),
        in_specs=[
            pl.BlockSpec((tf, 2 * kp), lambda b, m: (m, 0)),
            pl.BlockSpec((None, lp, FOURIER_WIDTH), lambda b, m: (b, 0, 0)),
            pl.BlockSpec((None, lp, FOURIER_WIDTH), lambda b, m: (b, 0, 1)),
        ],
        out_specs=pl.BlockSpec((None, lp, FOURIER_WIDTH), lambda b, m: (b, 0, 0)),
        out_shape=jax.ShapeDtypeStruct((B, lp, FOURIER_WIDTH), BF16),
        scratch_shapes=[pltpu.VMEM((tf, FOURIER_WIDTH), F32),
                        pltpu.VMEM((kp, FOURIER_WIDTH), BF16),
                        pltpu.VMEM((kp, FOURIER_WIDTH), BF16)],
        compiler_params=_params("parallel", "arbitrary"),
        name="fourier",
    )(w, pq, pq)


def _split_bf16(x):
    hi = x.astype(BF16)
    lo = (x - hi.astype(F32)).astype(BF16)
    return hi, lo


OUTPROJ_SUB = 128


def _outproj_kernel(attn_ref, four_ref, gate_ref, h_ref, wa_ref, wf_ref, wo_ref, g_ref, b_ref,
                    wrth_ref, wrtl_ref, h1_ref, h1b_ref, affr_ref, *, alpha):
    tm = h_ref.shape[0]
    subs = [slice(i * OUTPROJ_SUB, (i + 1) * OUTPROJ_SUB) for i in range(tm // OUTPROJ_SUB)]
    ya = [jnp.dot(attn_ref[s, :], wa_ref[...], preferred_element_type=F32) for s in subs]
    yf = [jnp.dot(four_ref[s, :], wf_ref[...], preferred_element_type=F32) for s in subs]
    merged = [gate_ref[s, 0:D_MODEL] * a + gate_ref[s, D_MODEL:GATE_WIDTH] * f for s, a, f in zip(subs, ya, yf)]
    mix = [jnp.dot(mg.astype(BF16), wo_ref[...], preferred_element_type=F32) for mg in merged]
    for s, mx in zip(subs, mix):
        h1 = _ln(alpha * h_ref[s, :] + mx, g_ref[...], b_ref[...])
        h1_ref[s, :] = h1
        hi, lo = _split_bf16(h1)
        h1b_ref[s, :] = hi
        lr = (lax.dot_general(wrth_ref[...], hi, NT_DIMS, preferred_element_type=F32)
              + lax.dot_general(wrtl_ref[...], hi, NT_DIMS, preferred_element_type=F32)
              + lax.dot_general(wrth_ref[...], lo, NT_DIMS, preferred_element_type=F32))
        er = jnp.exp(lr - jnp.max(lr, axis=0, keepdims=True))
        affr_ref[:, s] = er / jnp.sum(er, axis=0, keepdims=True)


def _outproj(attn, four, gates, h, wa, wf, wo, g, b, w_router, alpha, tm):
    n = h.shape[0]
    row = lambda i: (i, 0)
    const = lambda i: (0, 0)
    wrth, wrtl = _split_bf16(w_router.T)
    return pl.pallas_call(
        functools.partial(_outproj_kernel, alpha=alpha),
        grid=(n // tm,),
        in_specs=[
            pl.BlockSpec((tm, ATTN_WIDTH), row),
            pl.BlockSpec((tm, FOURIER_WIDTH), row),
            pl.BlockSpec((tm, GATE_WIDTH), row),
            pl.BlockSpec((tm, D_MODEL), row),
            pl.BlockSpec((ATTN_WIDTH, D_MODEL), const),
            pl.BlockSpec((FOURIER_WIDTH, D_MODEL), const),
            pl.BlockSpec((D_MODEL, D_MODEL), const),
            pl.BlockSpec((1, D_MODEL), const),
            pl.BlockSpec((1, D_MODEL), const),
            pl.BlockSpec((N_EXPERTS, D_MODEL), const),
            pl.BlockSpec((N_EXPERTS, D_MODEL), const),
        ],
        out_specs=[
            pl.BlockSpec((tm, D_MODEL), row),
            pl.BlockSpec((tm, D_MODEL), row),
            pl.BlockSpec((N_EXPERTS, tm), lambda i: (0, i)),
        ],
        out_shape=[
            jax.ShapeDtypeStruct((n, D_MODEL), F32),
            jax.ShapeDtypeStruct((n, D_MODEL), BF16),
            jax.ShapeDtypeStruct((N_EXPERTS, n), F32),
        ],
        compiler_params=_params("parallel"),
        name="outproj_ln_router",
    )(attn, four, gates, h, wa, wf, wo, g.reshape(1, -1), b.reshape(1, -1), wrth, wrtl)


def _topk_kernel(aff_ref, rr_ref, rc_ref, cnt_ref, cntc_ref, *, lead, lp, cap):
    nchunk = lp // LANES
    lane = lax.broadcasted_iota(jnp.int32, (N_EXPERTS, lp), 1)
    bits = jnp.where(lane >= lead, pltpu.bitcast(aff_ref[...], jnp.int32), -1)

    def search(i, t):
        cand = t | (jnp.int32(1) << (30 - i))
        cnt = jnp.sum((bits >= cand).astype(jnp.int32), axis=-1, keepdims=True)
        return jnp.where(cnt >= cap, cand, t)

    thr = lax.fori_loop(0, 31, search, jnp.zeros((N_EXPERTS, 1), jnp.int32))
    gt = bits > thr
    eq = bits == thr
    need = cap - jnp.sum(gt.astype(jnp.int32), axis=-1, keepdims=True)

    r = lax.broadcasted_iota(jnp.int32, (LANES, LANES), 0)
    c = lax.broadcasted_iota(jnp.int32, (LANES, LANES), 1)
    upper = (r <= c).astype(BF16)
    lower = (c <= r).astype(BF16)
    ident = (c == r).astype(BF16)

    chunks = [slice(k * LANES, (k + 1) * LANES) for k in range(nchunk)]
    eqb = eq.astype(BF16)
    needf = need.astype(F32)
    pre_eq = [jnp.dot(eqb[:, sl], upper, preferred_element_type=F32) for sl in chunks]
    off = jnp.zeros((N_EXPERTS, 1), F32)
    sel_chunks = []
    for k, sl in enumerate(chunks):
        sel_chunks.append(gt[:, sl] | (eq[:, sl] & (pre_eq[k] + off <= needf)))
        off = off + pre_eq[k][:, LANES - 1:LANES]

    selb = [sel.astype(BF16) for sel in sel_chunks]
    pre_r = [jnp.dot(s, upper, preferred_element_type=F32) for s in selb]
    pre_c = [lax.dot_general(lower, s, NT_DIMS, preferred_element_type=F32) for s in selb]
    sel_c = [lax.dot_general(ident, s, NT_DIMS, preferred_element_type=F32) for s in selb]
    off_r = jnp.zeros((N_EXPERTS, 1), F32)
    off_c = jnp.zeros((1, N_EXPERTS), F32)
    cnt_lane = lax.broadcasted_iota(jnp.int32, (N_EXPERTS, LANES), 1)
    cnt = jnp.zeros((N_EXPERTS, LANES), F32)
    for k, sl in enumerate(chunks):
        rr_ref[k] = jnp.where(sel_chunks[k], pre_r[k] + (off_r - 1.0), -1.0).astype(jnp.int32)
        off_r = off_r + pre_r[k][:, LANES - 1:LANES]
        cnt = jnp.where(cnt_lane == k + 1, off_r, cnt)
        rc_ref[sl, :] = jnp.where(sel_c[k] > 0.5, pre_c[k] + (off_c - 1.0), -1.0).astype(jnp.int32)
        off_c = off_c + pre_c[k][LANES - 1:LANES, :]
        cntc_ref[k + 1:k + 2, :] = off_c.astype(jnp.int32)
    cnt_ref[...] = cnt.astype(jnp.int32)
    cntc_ref[0:1, :] = jnp.zeros((1, N_EXPERTS), jnp.int32)
    pad_rows = cntc_ref.shape[0] - nchunk - 1
    if pad_rows:
        cntc_ref[nchunk + 1:, :] = jnp.zeros((pad_rows, N_EXPERTS), jnp.int32)


def _topk(aff_r, B, lp, lead, cap):
    n = B * lp
    nchunk = lp // LANES
    assert nchunk + 1 <= LANES
    nb1 = -(-(nchunk + 1) // SUBLANES) * SUBLANES
    return pl.pallas_call(
        functools.partial(_topk_kernel, lead=lead, lp=lp, cap=cap),
        grid=(B,),
        in_specs=[pl.BlockSpec((N_EXPERTS, lp), lambda b: (0, b))],
        out_specs=[
            pl.BlockSpec((None, nchunk, N_EXPERTS, LANES), lambda b: (b, 0, 0, 0)),
            pl.BlockSpec((lp, N_EXPERTS), lambda b: (b, 0)),
            pl.BlockSpec((None, N_EXPERTS, LANES), lambda b: (b, 0, 0)),
            pl.BlockSpec((None, nb1, N_EXPERTS), lambda b: (b, 0, 0)),
        ],
        out_shape=[
            jax.ShapeDtypeStruct((B, nchunk, N_EXPERTS, LANES), jnp.int32),
            jax.ShapeDtypeStruct((n, N_EXPERTS), jnp.int32),
            jax.ShapeDtypeStruct((B, N_EXPERTS, LANES), jnp.int32),
            jax.ShapeDtypeStruct((B, nb1, N_EXPERTS), jnp.int32),
        ],
        compiler_params=_params("parallel"),
        name="topk_select",
    )(aff_r)


SLOT_ALIGN = 16


def _window_constants(win):
    wide = N_EXPERTS * win
    expand = np.zeros((N_EXPERTS, wide), np.float32)
    for e in range(N_EXPERTS):
        expand[e, e * win:(e + 1) * win] = 1.0
    return expand, (np.arange(wide) % win).astype(np.float32)


def _gather_kernel(cnt_ref, cntv_ref, rank_ref, aff_ref, h_ref, o_ref, g_ref,
                   *, slots, ch, win, nchunk):
    bi, c = pl.program_id(0), pl.program_id(1)
    k0 = c * ch
    tt = ch * LANES
    max_start = slots - win

    @pl.when(c == 0)
    def _():
        o_ref[...] = jnp.zeros_like(o_ref)
        g_ref[...] = jnp.zeros_like(g_ref)

    lane = lax.broadcasted_iota(jnp.int32, cntv_ref.shape, 1)
    lo_col = jnp.sum(jnp.where(lane == k0, cntv_ref[...], 0), axis=-1, keepdims=True)
    w_col = jnp.minimum(lo_col & -SLOT_ALIGN, max_start)
    rank = rank_ref[...]
    aff = aff_ref[...]
    rel = rank - w_col
    in_win = (rank >= 0) & (rel >= 0) & (rel < win)
    rel = jnp.where(in_win, rel, -1)
    row = lax.broadcasted_iota(jnp.int32, (win, tt), 0)
    hit_all = jnp.concatenate([rel[e:e + 1, :] == row for e in range(N_EXPERTS)], axis=0)
    rows = h_ref[...]
    xw = jnp.dot(hit_all.astype(BF16), rows, preferred_element_type=F32)

    def add_rows(e, start, hit, vals):
        dst = pl.ds(pl.multiple_of(start, SLOT_ALIGN), win)
        o_ref[e, dst, :] = o_ref[e, dst, :] + vals.astype(BF16)
        g_ref[e, dst, :] += jnp.sum(jnp.where(hit, aff[e:e + 1, :], 0.0), axis=-1, keepdims=True)

    for e in range(N_EXPERTS):
        base = (bi * N_EXPERTS + e) * (nchunk + 1) + k0
        w = jnp.minimum(cnt_ref[base] & -SLOT_ALIGN, max_start)
        add_rows(e, w, hit_all[e * win:(e + 1) * win, :], xw[e * win:(e + 1) * win, :])
        hi = cnt_ref[base + ch]

        @pl.when(hi > w + win)
        def _(e=e, w=w, hi=hi):
            rk = rank[e:e + 1, :]

            def extra(i, carry):
                w2 = w + (i + 1) * win
                w2c = jnp.minimum(w2, max_start)
                hit = ((rk - w2c) == row) & (rk >= w2)
                add_rows(e, w2c, hit, jnp.dot(hit.astype(BF16), rows, preferred_element_type=F32))
                return carry

            lax.fori_loop(0, (hi - w - 1) // win, extra, 0)


def _gather(cnt, cntv, rank_rows, aff_rows, h1b, B, lp, slots, ch, win):
    nchunk = lp // LANES
    tt = ch * LANES
    nt = nchunk // ch
    assert slots % SLOT_ALIGN == 0 and win % SLOT_ALIGN == 0 and win <= slots
    per_seq = lambda bi, c, cnt: (bi, 0, 0, 0)
    return pl.pallas_call(
        functools.partial(_gather_kernel, slots=slots, ch=ch, win=win, nchunk=nchunk),
        grid_spec=pltpu.PrefetchScalarGridSpec(
            num_scalar_prefetch=1,
            grid=(B, nt),
            in_specs=[
                pl.BlockSpec((None, N_EXPERTS, LANES), lambda bi, c, cnt: (bi, 0, 0)),
                pl.BlockSpec((None, N_EXPERTS, tt), lambda bi, c, cnt: (bi, 0, c)),
                pl.BlockSpec((None, N_EXPERTS, tt), lambda bi, c, cnt: (bi, 0, c)),
                pl.BlockSpec((tt, D_MODEL), lambda bi, c, cnt: (bi * nt + c, 0)),
            ],
            out_specs=[pl.BlockSpec((None, N_EXPERTS, slots, D_MODEL), per_seq),
                       pl.BlockSpec((None, N_EXPERTS, slots, 1), per_seq)],
        ),
        out_shape=[jax.ShapeDtypeStruct((B, N_EXPERTS, slots, D_MODEL), BF16),
                   jax.ShapeDtypeStruct((B, N_EXPERTS, slots, 1), F32)],
        compiler_params=_params("parallel", "arbitrary"),
        name="moe_gather",
    )(cnt, cntv, rank_rows, aff_rows, h1b)


def _ffn_up_kernel(x_ref, wg_ref, wu_ref, o_ref, wgb_ref, wub_ref):
    @pl.when(pl.program_id(1) == 0)
    def _():
        wgb_ref[...] = wg_ref[...].astype(BF16)
        wub_ref[...] = wu_ref[...].astype(BF16)

    nseq, slots, width = x_ref.shape
    x = x_ref[...].reshape(nseq * slots, width)
    a = jnp.dot(x, wgb_ref[...], preferred_element_type=F32)
    u = jnp.dot(x, wub_ref[...], preferred_element_type=F32)
    o_ref[...] = (a * jax.nn.sigmoid(a) * u).astype(BF16).reshape(o_ref.shape)


def _ffn_down_kernel(x_ref, g_ref, wd_ref, o_ref, wdb_ref):
    @pl.when(pl.program_id(1) == 0)
    def _():
        wdb_ref[...] = wd_ref[...].astype(BF16)

    nseq, slots, width = x_ref.shape
    y = jnp.dot(x_ref[...].reshape(nseq * slots, width), wdb_ref[...], preferred_element_type=F32)
    y = y * g_ref[...].reshape(nseq * slots, 1)
    o_ref[...] = y.astype(BF16).reshape(o_ref.shape)


def _ffn(xg, gates, wg, wu, wd, layer):
    B, _, slots, _ = xg.shape
    n_up = _pick_tile(B, (2, 1))
    n_down = _pick_tile(B, (4, 2, 1))
    acts = lambda nseq, width: pl.BlockSpec((nseq, None, slots, width), lambda e, b: (b, e, 0, 0))
    weight = lambda rows, cols: pl.BlockSpec((None, None, rows, cols), lambda e, b: (layer, e, 0, 0))
    mid = pl.pallas_call(
        _ffn_up_kernel,
        grid=(N_EXPERTS, B // n_up),
        in_specs=[acts(n_up, D_MODEL), weight(D_MODEL, D_FF_EXPERT), weight(D_MODEL, D_FF_EXPERT)],
        out_specs=acts(n_up, D_FF_EXPERT),
        out_shape=jax.ShapeDtypeStruct((B, N_EXPERTS, slots, D_FF_EXPERT), BF16),
        scratch_shapes=[pltpu.VMEM((D_MODEL, D_FF_EXPERT), BF16), pltpu.VMEM((D_MODEL, D_FF_EXPERT), BF16)],
        compiler_params=_params("arbitrary", "arbitrary"),
        name="moe_ffn_up",
    )(xg, wg, wu)
    return pl.pallas_call(
        _ffn_down_kernel,
        grid=(N_EXPERTS, B // n_down),
        in_specs=[acts(n_down, D_FF_EXPERT), acts(n_down, 1), weight(D_FF_EXPERT, D_MODEL)],
        out_specs=acts(n_down, D_MODEL),
        out_shape=jax.ShapeDtypeStruct(xg.shape, BF16),
        scratch_shapes=[pltpu.VMEM((D_FF_EXPERT, D_MODEL), BF16)],
        compiler_params=_params("arbitrary", "arbitrary"),
        name="moe_ffn_down",
    )(mid, gates, wd)


def _combine_kernel(cnt_ref, cntc_ref, rank_ref, y_ref, h_ref, g_ref, b_ref, expand_ref, rpat_ref,
                    o_ref, acc_ref, *, slots, ch, win, nchunk, alpha):
    bi, c = pl.program_id(0), pl.program_id(1)
    k0 = c * ch
    tt = ch * LANES
    max_start = slots - win
    rank = rank_ref[...]
    lo_row = cntc_ref[pl.ds(k0, 1), :]
    w_row = jnp.minimum(lo_row & -SLOT_ALIGN, max_start)
    rel = rank - w_row
    in_win = (rank >= 0) & (rel >= 0) & (rel < win)
    relb = jnp.where(in_win, rel, -1).astype(F32).astype(BF16)
    rel_wide = jnp.dot(relb, expand_ref[...], preferred_element_type=F32)
    onehot = (rel_wide == rpat_ref[...]).astype(BF16)

    starts, windows = [], []
    for e in range(N_EXPERTS):
        lo = cnt_ref[(bi * N_EXPERTS + e) * (nchunk + 1) + k0]
        w = jnp.minimum(lo & -SLOT_ALIGN, max_start)
        starts.append(w)
        windows.append(y_ref[e, pl.ds(pl.multiple_of(w, SLOT_ALIGN), win), :])
    acc_ref[...] = jnp.dot(onehot, jnp.concatenate(windows, axis=0), preferred_element_type=F32)

    lane = lax.broadcasted_iota(jnp.int32, rank.shape, 1)
    col = lax.broadcasted_iota(jnp.int32, (tt, win), 1)
    for e in range(N_EXPERTS):
        hi = cnt_ref[(bi * N_EXPERTS + e) * (nchunk + 1) + k0 + ch]

        @pl.when(hi > starts[e] + win)
        def _(e=e, hi=hi):
            rk = jnp.sum(jnp.where(lane == e, rank, 0), axis=-1, keepdims=True)

            def extra(i, carry):
                w2 = starts[e] + (i + 1) * win
                w2c = jnp.minimum(w2, max_start)
                rows = y_ref[e, pl.ds(pl.multiple_of(w2c, SLOT_ALIGN), win), :]
                hit = ((rk - w2c) == col) & (rk >= w2)
                acc_ref[...] += jnp.dot(hit.astype(BF16), rows, preferred_element_type=F32)
                return carry

            lax.fori_loop(0, (hi - starts[e] - 1) // win, extra, 0)

    o_ref[...] = _ln(alpha * h_ref[...] + acc_ref[...], g_ref[...], b_ref[...])


def _combine(cnt, cntc, rank_c, y, h1, g, b, B, lp, alpha, ch, win):
    n = B * lp
    slots = y.shape[2]
    nchunk = lp // LANES
    tt = ch * LANES
    nt = nchunk // ch
    assert slots % SLOT_ALIGN == 0 and win % SLOT_ALIGN == 0 and win <= slots
    wide = N_EXPERTS * win
    expand, pos = _window_constants(win)
    tok = lambda bi, c, cnt: (bi * nt + c, 0)
    const = lambda bi, c, cnt: (0, 0)
    return pl.pallas_call(
        functools.partial(_combine_kernel, slots=slots, ch=ch, win=win, nchunk=nchunk, alpha=alpha),
        grid_spec=pltpu.PrefetchScalarGridSpec(
            num_scalar_prefetch=1,
            grid=(B, nt),
            in_specs=[
                pl.BlockSpec((None, cntc.shape[1], N_EXPERTS), lambda bi, c, cnt: (bi, 0, 0)),
                pl.BlockSpec((tt, N_EXPERTS), tok),
                pl.BlockSpec((None, N_EXPERTS, slots, D_MODEL), lambda bi, c, cnt: (bi, 0, 0, 0)),
                pl.BlockSpec((tt, D_MODEL), tok),
                pl.BlockSpec((1, D_MODEL), const),
                pl.BlockSpec((1, D_MODEL), const),
                pl.BlockSpec((N_EXPERTS, wide), const),
                pl.BlockSpec((1, wide), const),
            ],
            out_specs=pl.BlockSpec((tt, D_MODEL), tok),
            scratch_shapes=[pltpu.VMEM((tt, D_MODEL), F32)],
        ),
        out_shape=jax.ShapeDtypeStruct((n, D_MODEL), F32),
        compiler_params=_params("parallel", "arbitrary"),
        name="moe_combine_ln",
    )(cnt, cntc, rank_c, y, h1, g.reshape(1, -1), b.reshape(1, -1),
      jnp.asarray(expand, dtype=BF16), jnp.asarray(pos.reshape(1, wide)))


def kernel(x, meta, ln0_g, ln0_b, w_in, b_gate, sink, w_attn_o, w_four_o, w_out, ln1_g, ln1_b,
           w_router, w_e_gate, w_e_up, w_e_down, ln2_g, ln2_b):
    B, seq, d = x.shape
    depth = w_in.shape[0]
    assert d == D_MODEL and meta.shape == (N_META, D_MODEL)
    assert seq % BLOCK == 0 and N_META % SUBLANES == 0 and N_META <= BLOCK
    L = seq + N_META
    nb = -(-L // BLOCK)
    lp = nb * BLOCK
    lead = lp - L
    cap = CAPACITY_FACTOR * L // N_EXPERTS
    slots = -(-cap // SLOT_ALIGN) * SLOT_ALIGN
    alpha = float((2 * depth) ** 0.25)
    n = B * lp
    nchunk = lp // LANES
    tm = _pick_tile(n, (512, 256, 128))
    tm_in = _pick_tile(n, (1024, 512, 256, 128))
    tf = _pick_tile(lp, (384, 128))
    ch = _pick_tile(nchunk, (3, 2, 1))
    win = min(slots, -(-(ch * LANES * cap // L * 4 // 3 + SLOT_ALIGN) // SLOT_ALIGN) * SLOT_ALIGN)

    gi = np.arange(FOURIER_GROUP)
    ang = 2.0 * np.pi * ((gi[:, None] * gi[None, :]) % FOURIER_GROUP) / FOURIER_GROUP
    cs = jnp.asarray(np.concatenate([np.cos(ang), np.sin(ang)], axis=1) * FOURIER_GROUP ** -0.5, dtype=BF16)
    dft_rows = _fourier_steps(lp, lead, L, tf) * tf
    wdft = _dft_matrix(dft_rows, dft_rows, lead, L)
    bias, variant_of_block = _attn_bias_tables(nb, lead)

    h = _embed(x, meta, ln0_g, ln0_b, nb, lead).reshape(n, D_MODEL)
    for l in range(depth):
        q, kv, pq, gates = _inproj(h, w_in[l].astype(BF16), b_gate[l], cs, tm_in)
        attn = _attention(q.reshape(B, lp, -1), kv.reshape(B, lp, -1), sink[l], bias, variant_of_block, nb)
        four = _fourier(wdft, pq.reshape(B, lp, -1), lead, L, tf)
        h1, h1b, aff_r = _outproj(
            attn.reshape(n, -1), four.reshape(n, -1), gates, h,
            w_attn_o[l].astype(BF16), w_four_o[l].astype(BF16), w_out[l].astype(BF16),
            ln1_g[l], ln1_b[l], w_router[l], alpha, tm)
        rank_r, rank_c, cntv, cntc = _topk(aff_r, B, lp, lead, cap)
        cnt = cntv[:, :, :nchunk + 1].reshape(-1)
        rank_rows = rank_r.transpose(0, 2, 1, 3).reshape(B, N_EXPERTS, lp)
        aff_rows = aff_r.reshape(N_EXPERTS, B, lp).transpose(1, 0, 2)
        xg, gsel = _gather(cnt, cntv, rank_rows, aff_rows, h1b, B, lp, slots, ch, win)
        y = _ffn(xg, gsel, w_e_gate, w_e_up, w_e_down, l)
        h = _combine(cnt, cntc, rank_c, y, h1, ln2_g[l], ln2_b[l], B, lp, alpha, ch, win)
    return h.reshape(B, lp, D_MODEL)[:, lead + N_META:]
```

```python
import functools
import math

import numpy as np
import jax
import jax.numpy as jnp
from jax import lax
from jax.experimental import pallas as pl
from jax.experimental.pallas import tpu as pltpu

D_MODEL = 1024
N_META = 16
N_HEADS = 8
N_KV_HEADS = 2
HEAD_DIM = 64
GQA_GROUP = N_HEADS // N_KV_HEADS
ATTN_WIDTH = N_HEADS * HEAD_DIM
KV_WIDTH = N_KV_HEADS * HEAD_DIM
WINDOW = 128
BLOCK = 128
N_FOURIER_GROUPS = 4
FOURIER_GROUP = 128
FOURIER_WIDTH = N_FOURIER_GROUPS * FOURIER_GROUP
N_BRANCHES = 2
GATE_WIDTH = N_BRANCHES * D_MODEL
N_EXPERTS = 16
CAPACITY_FACTOR = 2
D_FF_EXPERT = 1536
LN_EPS = 1e-5
NEG_INF = -1e30
Q_END = ATTN_WIDTH
K_END = Q_END + KV_WIDTH
V_END = K_END + KV_WIDTH
F_END = V_END + FOURIER_WIDTH
IN_WIDTH = F_END + GATE_WIDTH

LANES = 128
SUBLANES = 8
VMEM_LIMIT_BYTES = 56 * 1024 * 1024

F32 = jnp.float32
BF16 = jnp.bfloat16
NT_DIMS = (((1,), (1,)), ((), ()))


def _pick_tile(n, candidates):
    for c in candidates:
        if n % c == 0:
            return c
    raise ValueError(f"no tile in {candidates} divides {n}")


def _params(*sem):
    return pltpu.CompilerParams(dimension_semantics=sem, vmem_limit_bytes=VMEM_LIMIT_BYTES)


def _ln(x, g, b):
    mu = jnp.mean(x, axis=-1, keepdims=True)
    xc = x - mu
    var = jnp.mean(xc * xc, axis=-1, keepdims=True)
    return xc * lax.rsqrt(var + LN_EPS) * g + b


def _embed_kernel(*refs, lead, group):
    x_refs, (meta_ref, g_ref, b_ref, o_ref) = refs[:group], refs[group:]
    for k in range(1, group):
        o_ref[k * BLOCK:(k + 1) * BLOCK, :] = _ln(x_refs[k][...], g_ref[...], b_ref[...])

    @pl.when(pl.program_id(1) == 0)
    def _():
        o_ref[0:lead, :] = jnp.zeros((lead, D_MODEL), F32)
        o_ref[lead:BLOCK, :] = _ln(meta_ref[...], g_ref[...], b_ref[...])

    @pl.when(pl.program_id(1) > 0)
    def _():
        o_ref[0:BLOCK, :] = _ln(x_refs[0][...], g_ref[...], b_ref[...])


def _embed(x, meta, g, b, nb, lead):
    B = x.shape[0]
    group = _pick_tile(nb, (11, 3, 1))
    x_spec = lambda k: pl.BlockSpec((None, BLOCK, D_MODEL),
                                    lambda bi, j: (bi, jnp.maximum(group * j + k - 1, 0), 0))
    const = lambda bi, j: (0, 0)
    return pl.pallas_call(
        functools.partial(_embed_kernel, lead=lead, group=group),
        grid=(B, nb // group),
        in_specs=[x_spec(k) for k in range(group)] + [
            pl.BlockSpec((N_META, D_MODEL), const),
            pl.BlockSpec((1, D_MODEL), const),
            pl.BlockSpec((1, D_MODEL), const),
        ],
        out_specs=pl.BlockSpec((None, group * BLOCK, D_MODEL), lambda bi, j: (bi, j, 0)),
        out_shape=jax.ShapeDtypeStruct((B, nb * BLOCK, D_MODEL), F32),
        compiler_params=_params("parallel", "arbitrary"),
        name="embed_ln",
    )(*([x] * group), meta, g.reshape(1, -1), b.reshape(1, -1))


def _inproj_kernel(h_ref, w_ref, bg_ref, cs_ref, q_ref, kv_ref, pq_ref, gate_ref):
    hb = h_ref[...].astype(BF16)
    q_ref[...] = jnp.dot(hb, w_ref[:, 0:Q_END], preferred_element_type=F32).astype(BF16)
    kv_ref[...] = jnp.dot(hb, w_ref[:, Q_END:V_END], preferred_element_type=F32).astype(BF16)
    uf = jnp.dot(hb, w_ref[:, V_END:F_END], preferred_element_type=F32).astype(BF16)
    for g in range(N_FOURIER_GROUPS):
        lo = g * FOURIER_GROUP
        pq = jnp.dot(uf[:, lo:lo + FOURIER_GROUP], cs_ref[...], preferred_element_type=F32)
        pq_ref[:, lo:lo + FOURIER_GROUP] = pq[:, 0:FOURIER_GROUP].astype(BF16)
        pq_ref[:, FOURIER_WIDTH + lo:FOURIER_WIDTH + lo + FOURIER_GROUP] = (
            pq[:, FOURIER_GROUP:2 * FOURIER_GROUP].astype(BF16))
    chunk = FOURIER_WIDTH
    for c in range(GATE_WIDTH // chunk):
        lo = c * chunk
        ug = jnp.dot(hb, w_ref[:, F_END + lo:F_END + lo + chunk], preferred_element_type=F32)
        gate_ref[:, lo:lo + chunk] = jax.nn.sigmoid(ug + bg_ref[:, lo:lo + chunk])


def _inproj(h, w_bf16, b_gate, cs, tm):
    n = h.shape[0]
    row = lambda i: (i, 0)
    const = lambda i: (0, 0)
    return pl.pallas_call(
        _inproj_kernel,
        grid=(n // tm,),
        in_specs=[
            pl.BlockSpec((tm, D_MODEL), row),
            pl.BlockSpec((D_MODEL, IN_WIDTH), const),
            pl.BlockSpec((1, GATE_WIDTH), const),
            pl.BlockSpec((FOURIER_GROUP, 2 * FOURIER_GROUP), const),
        ],
        out_specs=[
            pl.BlockSpec((tm, ATTN_WIDTH), row),
            pl.BlockSpec((tm, 2 * KV_WIDTH), row),
            pl.BlockSpec((tm, 2 * FOURIER_WIDTH), row),
            pl.BlockSpec((tm, GATE_WIDTH), row),
        ],
        out_shape=[
            jax.ShapeDtypeStruct((n, ATTN_WIDTH), BF16),
            jax.ShapeDtypeStruct((n, 2 * KV_WIDTH), BF16),
            jax.ShapeDtypeStruct((n, 2 * FOURIER_WIDTH), BF16),
            jax.ShapeDtypeStruct((n, GATE_WIDTH), F32),
        ],
        compiler_params=_params("parallel"),
        name="inproj",
    )(h, w_bf16, b_gate.reshape(1, GATE_WIDTH), cs)


SCORES_AHEAD = 5


def _attn_kernel(sink_ref, q_ref, *refs, group, variant):
    kv_refs, (bias_ref, o_ref) = refs[:group + 2], refs[group + 2:]
    first_block = pl.program_id(1) * group
    kv = jnp.concatenate([r[...] for r in kv_refs], axis=0)
    low_half = lax.broadcasted_iota(jnp.int32, (kv.shape[0], LANES), 1) < HEAD_DIM

    def lane_half_operands(x):
        swapped = jnp.concatenate([x[:, HEAD_DIM:], x[:, :HEAD_DIM]], axis=1)
        zero = jnp.zeros_like(x)
        return {(kvh, half): jnp.where(low_half if half == 0 else ~low_half,
                                       x if kvh == half else swapped, zero)
                for kvh in range(N_KV_HEADS) for half in range(2)}

    k_ops = lane_half_operands(kv[:, 0:KV_WIDTH])
    v_ops = lane_half_operands(kv[:, KV_WIDTH:2 * KV_WIDTH])
    keys = lambda t: slice(t * BLOCK, (t + 3) * BLOCK)

    def scores(t, h):
        pair = h // 2
        qp = q_ref[t * BLOCK:(t + 1) * BLOCK, pair * LANES:(pair + 1) * LANES]
        return lax.dot_general(qp, k_ops[(h // GQA_GROUP, h % 2)][keys(t)], NT_DIMS,
                               preferred_element_type=F32)

    def head_out(t, h, s):
        sink = sink_ref[h]
        logits = s * (HEAD_DIM ** -0.5) + bias_ref[variant(first_block + t), h]
        m = jnp.maximum(jnp.max(logits, axis=-1, keepdims=True), sink)
        p = jnp.exp(logits - m)
        denom = jnp.sum(p, axis=-1, keepdims=True) + jnp.exp(sink - m)
        o = jnp.dot(p.astype(BF16), v_ops[(h // GQA_GROUP, h % 2)][keys(t)], preferred_element_type=F32)
        return o / denom

    work = [(t, h) for t in range(group) for h in range(N_HEADS)]
    ahead = SCORES_AHEAD
    pending = [scores(*w) for w in work[:ahead]]
    for n, (t, h) in enumerate(work):
        s_cur = pending.pop(0)
        if n + ahead < len(work):
            pending.append(scores(*work[n + ahead]))
        o = head_out(t, h, s_cur)
        if h % 2 == 0:
            o_even = o
        else:
            pair = h // 2
            o_ref[t * BLOCK:(t + 1) * BLOCK, pair * LANES:(pair + 1) * LANES] = (o_even + o).astype(BF16)


def _attn_bias_tables(nb, lead):
    qi = np.arange(BLOCK)[:, None]
    si = np.arange(3 * BLOCK)[None, :]
    rel = np.abs(si - BLOCK - qi).astype(np.float32)
    slopes = np.array([2.0 ** (-8.0 * (h + 1) / N_HEADS) for h in range(N_HEADS)], np.float32)
    base = np.where(rel[None] <= WINDOW, -slopes[:, None, None] * rel[None], np.float32(NEG_INF))
    variants, keys, variant_of_block = [], [], []
    for i in range(nb):
        kpos = (i - 1) * BLOCK + np.arange(3 * BLOCK)
        valid = (kpos >= lead) & (kpos < nb * BLOCK)
        key = valid.tobytes()
        if key not in keys:
            keys.append(key)
            variants.append(np.where(valid[None, None, :], base, np.float32(NEG_INF)))
        variant_of_block.append(keys.index(key))
    return jnp.asarray(np.stack(variants).astype(np.float32)), variant_of_block


def _attention(q, kv, sink, bias, variant_of_block, nb):
    B, lp, _ = q.shape
    group = _pick_tile(nb, (3, 1))
    interior = max(set(variant_of_block), key=variant_of_block.count)

    def variant(i):
        v = jnp.int32(interior)
        for blk, var in enumerate(variant_of_block):
            if var != interior:
                v = jnp.where(i == blk, var, v)
        return v

    kv_spec = lambda t: pl.BlockSpec((None, BLOCK, 2 * KV_WIDTH),
                                     lambda b, g: (b, jnp.clip(group * g + t - 1, 0, nb - 1), 0))
    rows = pl.BlockSpec((None, group * BLOCK, ATTN_WIDTH), lambda b, g: (b, g, 0))
    return pl.pallas_call(
        functools.partial(_attn_kernel, group=group, variant=variant),
        grid=(B, nb // group),
        in_specs=[pl.BlockSpec(memory_space=pltpu.SMEM), rows]
                 + [kv_spec(t) for t in range(group + 2)]
                 + [pl.BlockSpec(bias.shape, lambda b, g: (0, 0, 0, 0))],
        out_specs=rows,
        out_shape=jax.ShapeDtypeStruct((B, lp, ATTN_WIDTH), BF16),
        compiler_params=_params("parallel", "arbitrary"),
        name="attention",
    )(sink, q, *([kv] * (group + 2)), bias)


def _fourier_kernel(w_ref, p_ref, q_ref, o_ref, prev_ref, pe_ref, qo_ref, *, lp, lead, seq_len, tf):
    m = pl.program_id(1)
    last = pl.num_programs(1) - 1
    nblk = lp // tf
    shift = lead + 1
    kp = pe_ref.shape[0]
    half = seq_len // 2
    r = lax.broadcasted_iota(jnp.int32, (tf, tf), 0)
    c = lax.broadcasted_iota(jnp.int32, (tf, tf), 1)
    flip = (r + c == tf - 1).astype(BF16)

    @pl.when(m == 0)
    def _():
        row = lax.broadcasted_iota(jnp.int32, (lp, 1), 0)
        dc = jnp.sum(jnp.where(row >= lead, p_ref[...].astype(F32), 0.0), axis=0, keepdims=True)
        prev_ref[...] = jnp.zeros_like(prev_ref)
        prev_ref[tf - 1:tf, :] = dc * np.float32(seq_len ** -0.5)
        def mirror(x_ref, i):
            above = (jnp.dot(flip, x_ref[(nblk - i) * tf:(nblk - i + 1) * tf, :], preferred_element_type=F32)
                     if i > 0 else jnp.zeros((tf, FOURIER_WIDTH), F32))
            below = jnp.dot(flip, x_ref[(nblk - 1 - i) * tf:(nblk - i) * tf, :], preferred_element_type=F32)
            return jnp.concatenate([above[tf - shift:, :], below[:tf - shift, :]], axis=0)

        for i in range(kp // tf):
            rows = slice(i * tf, (i + 1) * tf)
            n = lax.broadcasted_iota(jnp.int32, (tf, 1), 0) + (i * tf - lead)
            paired = (n >= 1) & (n <= half - 1)
            alone = (n == 0) | (n == half)
            p_blk = p_ref[rows, :].astype(F32)
            pe_ref[rows, :] = jnp.where(paired, p_blk + mirror(p_ref, i),
                                        jnp.where(alone, p_blk, 0.0)).astype(BF16)
            qo_ref[rows, :] = jnp.where(paired, q_ref[rows, :].astype(F32) - mirror(q_ref, i), 0.0).astype(BF16)

    t1 = jnp.dot(w_ref[:, 0:kp], pe_ref[...], preferred_element_type=F32)
    t2 = jnp.dot(w_ref[:, kp:2 * kp], qo_ref[...], preferred_element_type=F32)
    direct = t1 + t2
    mirrored = jnp.dot(flip, (t1 - t2).astype(BF16), preferred_element_type=F32)
    shifted = jnp.concatenate([prev_ref[tf - shift:, :], direct[:tf - shift, :]], axis=0)
    prev_ref[...] = direct

    @pl.when(m < last)
    def _():
        o_ref[pl.ds(pl.multiple_of(m * tf, tf), tf), :] = shifted.astype(BF16)
        o_ref[pl.ds(pl.multiple_of((nblk - 1 - m) * tf, tf), tf), :] = mirrored.astype(BF16)

    @pl.when(m == last)
    def _():
        row = lax.broadcasted_iota(jnp.int32, (tf, 1), 0) + (nblk // 2) * tf
        mid = jnp.where(row <= seq_len // 2 + lead, shifted, mirrored)
        o_ref[(nblk // 2) * tf:(nblk // 2 + 1) * tf, :] = mid.astype(BF16)


def _dft_matrix(nrows, ncols, lead, seq_len):
    w = np.float32(2.0 * math.pi / seq_len)
    k = (jnp.arange(nrows, dtype=jnp.int32) + 1)[:, None]
    j = jnp.arange(ncols // LANES, dtype=jnp.int32)[None, :]
    r = jnp.arange(LANES, dtype=jnp.int32)[None, :]
    a = ((k * (LANES * j - lead)) % seq_len).astype(F32) * w
    b = ((k * r) % seq_len).astype(F32) * w
    ca, sa, cb, sb = jnp.cos(a)[:, :, None], jnp.sin(a)[:, :, None], jnp.cos(b)[:, None, :], jnp.sin(b)[:, None, :]
    n = jnp.arange(ncols, dtype=jnp.int32) - lead
    valid = ((n >= 0) & (n <= seq_len // 2)).reshape(1, ncols // LANES, LANES)
    scale = np.float32(seq_len ** -0.5)
    wc = jnp.where(valid, (ca * cb - sa * sb) * scale, 0.0).reshape(nrows, ncols)
    ws = jnp.where(valid, -(sa * cb + ca * sb) * scale, 0.0).reshape(nrows, ncols)
    return jnp.concatenate([wc, ws], axis=1).astype(BF16)


def _fourier_steps(lp, lead, seq_len, tf):
    nblk = lp // tf
    assert seq_len % 2 == 0 and lp % tf == 0 and nblk % 2 == 1 and lead + 1 < tf
    steps = nblk // 2 + 1
    assert steps * tf > seq_len // 2 + lead
    return steps


def _fourier(w, pq, lead, seq_len, tf):
    B, lp, _ = pq.shape
    steps = _fourier_steps(lp, lead, seq_len, tf)
    kp = steps * tf
    return pl.pallas_call(
        functools.partial(_fourier_kernel, lp=lp, lead=lead, seq_len=seq_len, tf=tf),
        grid=(B, steps),
        in_specs=[
            pl.BlockSpec((tf, 2 * kp), lambda b, m: (m, 0)),
            pl.BlockSpec((None, lp, FOURIER_WIDTH), lambda b, m: (b, 0, 0)),
            pl.BlockSpec((None, lp, FOURIER_WIDTH), lambda b, m: (b, 0, 1)),
        ],
        out_specs=pl.BlockSpec((None, lp, FOURIER_WIDTH), lambda b, m: (b, 0, 0)),
        out_shape=jax.ShapeDtypeStruct((B, lp, FOURIER_WIDTH), BF16),
        scratch_shapes=[pltpu.VMEM((tf, FOURIER_WIDTH), F32),
                        pltpu.VMEM((kp, FOURIER_WIDTH), BF16),
                        pltpu.VMEM((kp, FOURIER_WIDTH), BF16)],
        compiler_params=_params("parallel", "arbitrary"),
        name="fourier",
    )(w, pq, pq)


def _split_bf16(x):
    hi = x.astype(BF16)
    lo = (x - hi.astype(F32)).astype(BF16)
    return hi, lo


OUTPROJ_SUB = 128


def _outproj_kernel(attn_ref, four_ref, gate_ref, h_ref, wa_ref, wf_ref, wo_ref, g_ref, b_ref,
                    wrth_ref, wrtl_ref, h1_ref, h1b_ref, affr_ref, *, alpha):
    tm = h_ref.shape[0]
    subs = [slice(i * OUTPROJ_SUB, (i + 1) * OUTPROJ_SUB) for i in range(tm // OUTPROJ_SUB)]
    ya = [jnp.dot(attn_ref[s, :], wa_ref[...], preferred_element_type=F32) for s in subs]
    yf = [jnp.dot(four_ref[s, :], wf_ref[...], preferred_element_type=F32) for s in subs]
    merged = [gate_ref[s, 0:D_MODEL] * a + gate_ref[s, D_MODEL:GATE_WIDTH] * f for s, a, f in zip(subs, ya, yf)]
    mix = [jnp.dot(mg.astype(BF16), wo_ref[...], preferred_element_type=F32) for mg in merged]
    for s, mx in zip(subs, mix):
        h1 = _ln(alpha * h_ref[s, :] + mx, g_ref[...], b_ref[...])
        h1_ref[s, :] = h1
        hi, lo = _split_bf16(h1)
        h1b_ref[s, :] = hi
        lr = (lax.dot_general(wrth_ref[...], hi, NT_DIMS, preferred_element_type=F32)
              + lax.dot_general(wrtl_ref[...], hi, NT_DIMS, preferred_element_type=F32)
              + lax.dot_general(wrth_ref[...], lo, NT_DIMS, preferred_element_type=F32))
        er = jnp.exp(lr - jnp.max(lr, axis=0, keepdims=True))
        affr_ref[:, s] = er / jnp.sum(er, axis=0, keepdims=True)


def _outproj(attn, four, gates, h, wa, wf, wo, g, b, w_router, alpha, tm):
    n = h.shape[0]
    row = lambda i: (i, 0)
    const = lambda i: (0, 0)
    wrth, wrtl = _split_bf16(w_router.T)
    return pl.pallas_call(
        functools.partial(_outproj_kernel, alpha=alpha),
        grid=(n // tm,),
        in_specs=[
            pl.BlockSpec((tm, ATTN_WIDTH), row),
            pl.BlockSpec((tm, FOURIER_WIDTH), row),
            pl.BlockSpec((tm, GATE_WIDTH), row),
            pl.BlockSpec((tm, D_MODEL), row),
            pl.BlockSpec((ATTN_WIDTH, D_MODEL), const),
            pl.BlockSpec((FOURIER_WIDTH, D_MODEL), const),
            pl.BlockSpec((D_MODEL, D_MODEL), const),
            pl.BlockSpec((1, D_MODEL), const),
            pl.BlockSpec((1, D_MODEL), const),
            pl.BlockSpec((N_EXPERTS, D_MODEL), const),
            pl.BlockSpec((N_EXPERTS, D_MODEL), const),
        ],
        out_specs=[
            pl.BlockSpec((tm, D_MODEL), row),
            pl.BlockSpec((tm, D_MODEL), row),
            pl.BlockSpec((N_EXPERTS, tm), lambda i: (0, i)),
        ],
        out_shape=[
            jax.ShapeDtypeStruct((n, D_MODEL), F32),
            jax.ShapeDtypeStruct((n, D_MODEL), BF16),
            jax.ShapeDtypeStruct((N_EXPERTS, n), F32),
        ],
        compiler_params=_params("parallel"),
        name="outproj_ln_router",
    )(attn, four, gates, h, wa, wf, wo, g.reshape(1, -1), b.reshape(1, -1), wrth, wrtl)


def _topk_kernel(aff_ref, rr_ref, rc_ref, cnt_ref, cntc_ref, *, lead, lp, cap):
    nchunk = lp // LANES
    lane = lax.broadcasted_iota(jnp.int32, (N_EXPERTS, lp), 1)
    bits = jnp.where(lane >= lead, pltpu.bitcast(aff_ref[...], jnp.int32), -1)

    def search(i, t):
        cand = t | (jnp.int32(1) << (30 - i))
        cnt = jnp.sum((bits >= cand).astype(jnp.int32), axis=-1, keepdims=True)
        return jnp.where(cnt >= cap, cand, t)

    thr = lax.fori_loop(0, 31, search, jnp.zeros((N_EXPERTS, 1), jnp.int32))
    gt = bits > thr
    eq = bits == thr
    need = cap - jnp.sum(gt.astype(jnp.int32), axis=-1, keepdims=True)

    r = lax.broadcasted_iota(jnp.int32, (LANES, LANES), 0)
    c = lax.broadcasted_iota(jnp.int32, (LANES, LANES), 1)
    upper = (r <= c).astype(BF16)
    lower = (c <= r).astype(BF16)
    ident = (c == r).astype(BF16)

    chunks = [slice(k * LANES, (k + 1) * LANES) for k in range(nchunk)]
    eqb = eq.astype(BF16)
    needf = need.astype(F32)
    pre_eq = [jnp.dot(eqb[:, sl], upper, preferred_element_type=F32) for sl in chunks]
    off = jnp.zeros((N_EXPERTS, 1), F32)
    sel_chunks = []
    for k, sl in enumerate(chunks):
        sel_chunks.append(gt[:, sl] | (eq[:, sl] & (pre_eq[k] + off <= needf)))
        off = off + pre_eq[k][:, LANES - 1:LANES]

    selb = [sel.astype(BF16) for sel in sel_chunks]
    pre_r = [jnp.dot(s, upper, preferred_element_type=F32) for s in selb]
    pre_c = [lax.dot_general(lower, s, NT_DIMS, preferred_element_type=F32) for s in selb]
    sel_c = [lax.dot_general(ident, s, NT_DIMS, preferred_element_type=F32) for s in selb]
    off_r = jnp.zeros((N_EXPERTS, 1), F32)
    off_c = jnp.zeros((1, N_EXPERTS), F32)
    cnt_lane = lax.broadcasted_iota(jnp.int32, (N_EXPERTS, LANES), 1)
    cnt = jnp.zeros((N_EXPERTS, LANES), F32)
    for k, sl in enumerate(chunks):
        rr_ref[k] = jnp.where(sel_chunks[k], pre_r[k] + (off_r - 1.0), -1.0).astype(jnp.int32)
        off_r = off_r + pre_r[k][:, LANES - 1:LANES]
        cnt = jnp.where(cnt_lane == k + 1, off_r, cnt)
        rc_ref[sl, :] = jnp.where(sel_c[k] > 0.5, pre_c[k] + (off_c - 1.0), -1.0).astype(jnp.int32)
        off_c = off_c + pre_c[k][LANES - 1:LANES, :]
        cntc_ref[k + 1:k + 2, :] = off_c.astype(jnp.int32)
    cnt_ref[...] = cnt.astype(jnp.int32)
    cntc_ref[0:1, :] = jnp.zeros((1, N_EXPERTS), jnp.int32)
    pad_rows = cntc_ref.shape[0] - nchunk - 1
    if pad_rows:
        cntc_ref[nchunk + 1:, :] = jnp.zeros((pad_rows, N_EXPERTS), jnp.int32)


def _topk(aff_r, B, lp, lead, cap):
    n = B * lp
    nchunk = lp // LANES
    assert nchunk + 1 <= LANES
    nb1 = -(-(nchunk + 1) // SUBLANES) * SUBLANES
    return pl.pallas_call(
        functools.partial(_topk_kernel, lead=lead, lp=lp, cap=cap),
        grid=(B,),
        in_specs=[pl.BlockSpec((N_EXPERTS, lp), lambda b: (0, b))],
        out_specs=[
            pl.BlockSpec((None, nchunk, N_EXPERTS, LANES), lambda b: (b, 0, 0, 0)),
            pl.BlockSpec((lp, N_EXPERTS), lambda b: (b, 0)),
            pl.BlockSpec((None, N_EXPERTS, LANES), lambda b: (b, 0, 0)),
            pl.BlockSpec((None, nb1, N_EXPERTS), lambda b: (b, 0, 0)),
        ],
        out_shape=[
            jax.ShapeDtypeStruct((B, nchunk, N_EXPERTS, LANES), jnp.int32),
            jax.ShapeDtypeStruct((n, N_EXPERTS), jnp.int32),
            jax.ShapeDtypeStruct((B, N_EXPERTS, LANES), jnp.int32),
            jax.ShapeDtypeStruct((B, nb1, N_EXPERTS), jnp.int32),
        ],
        compiler_params=_params("parallel"),
        name="topk_select",
    )(aff_r)


SLOT_ALIGN = 16


def _window_constants(win):
    wide = N_EXPERTS * win
    expand = np.zeros((N_EXPERTS, wide), np.float32)
    for e in range(N_EXPERTS):
        expand[e, e * win:(e + 1) * win] = 1.0
    return expand, (np.arange(wide) % win).astype(np.float32)


def _gather_kernel(cnt_ref, cntv_ref, rank_ref, aff_ref, h_ref, o_ref, g_ref,
                   *, slots, ch, win, nchunk):
    bi, c = pl.program_id(0), pl.program_id(1)
    k0 = c * ch
    tt = ch * LANES
    max_start = slots - win

    @pl.when(c == 0)
    def _():
        o_ref[...] = jnp.zeros_like(o_ref)
        g_ref[...] = jnp.zeros_like(g_ref)

    lane = lax.broadcasted_iota(jnp.int32, cntv_ref.shape, 1)
    lo_col = jnp.sum(jnp.where(lane == k0, cntv_ref[...], 0), axis=-1, keepdims=True)
    w_col = jnp.minimum(lo_col & -SLOT_ALIGN, max_start)
    rank = rank_ref[...]
    aff = aff_ref[...]
    rel = rank - w_col
    in_win = (rank >= 0) & (rel >= 0) & (rel < win)
    rel = jnp.where(in_win, rel, -1)
    row = lax.broadcasted_iota(jnp.int32, (win, tt), 0)
    hit_all = jnp.concatenate([rel[e:e + 1, :] == row for e in range(N_EXPERTS)], axis=0)
    rows = h_ref[...]
    xw = jnp.dot(hit_all.astype(BF16), rows, preferred_element_type=F32)

    def add_rows(e, start, hit, vals):
        dst = pl.ds(pl.multiple_of(start, SLOT_ALIGN), win)
        o_ref[e, dst, :] = o_ref[e, dst, :] + vals.astype(BF16)
        g_ref[e, dst, :] += jnp.sum(jnp.where(hit, aff[e:e + 1, :], 0.0), axis=-1, keepdims=True)

    for e in range(N_EXPERTS):
        base = (bi * N_EXPERTS + e) * (nchunk + 1) + k0
        w = jnp.minimum(cnt_ref[base] & -SLOT_ALIGN, max_start)
        add_rows(e, w, hit_all[e * win:(e + 1) * win, :], xw[e * win:(e + 1) * win, :])
        hi = cnt_ref[base + ch]

        @pl.when(hi > w + win)
        def _(e=e, w=w, hi=hi):
            rk = rank[e:e + 1, :]

            def extra(i, carry):
                w2 = w + (i + 1) * win
                w2c = jnp.minimum(w2, max_start)
                hit = ((rk - w2c) == row) & (rk >= w2)
                add_rows(e, w2c, hit, jnp.dot(hit.astype(BF16), rows, preferred_element_type=F32))
                return carry

            lax.fori_loop(0, (hi - w - 1) // win, extra, 0)


def _gather(cnt, cntv, rank_rows, aff_rows, h1b, B, lp, slots, ch, win):
    nchunk = lp // LANES
    tt = ch * LANES
    nt = nchunk // ch
    assert slots % SLOT_ALIGN == 0 and win % SLOT_ALIGN == 0 and win <= slots
    per_seq = lambda bi, c, cnt: (bi, 0, 0, 0)
    return pl.pallas_call(
        functools.partial(_gather_kernel, slots=slots, ch=ch, win=win, nchunk=nchunk),
        grid_spec=pltpu.PrefetchScalarGridSpec(
            num_scalar_prefetch=1,
            grid=(B, nt),
            in_specs=[
                pl.BlockSpec((None, N_EXPERTS, LANES), lambda bi, c, cnt: (bi, 0, 0)),
                pl.BlockSpec((None, N_EXPERTS, tt), lambda bi, c, cnt: (bi, 0, c)),
                pl.BlockSpec((None, N_EXPERTS, tt), lambda bi, c, cnt: (bi, 0, c)),
                pl.BlockSpec((tt, D_MODEL), lambda bi, c, cnt: (bi * nt + c, 0)),
            ],
            out_specs=[pl.BlockSpec((None, N_EXPERTS, slots, D_MODEL), per_seq),
                       pl.BlockSpec((None, N_EXPERTS, slots, 1), per_seq)],
        ),
        out_shape=[jax.ShapeDtypeStruct((B, N_EXPERTS, slots, D_MODEL), BF16),
                   jax.ShapeDtypeStruct((B, N_EXPERTS, slots, 1), F32)],
        compiler_params=_params("parallel", "arbitrary"),
        name="moe_gather",
    )(cnt, cntv, rank_rows, aff_rows, h1b)


def _ffn_up_kernel(x_ref, wg_ref, wu_ref, o_ref, wgb_ref, wub_ref):
    @pl.when(pl.program_id(1) == 0)
    def _():
        wgb_ref[...] = wg_ref[...].astype(BF16)
        wub_ref[...] = wu_ref[...].astype(BF16)

    nseq, slots, width = x_ref.shape
    x = x_ref[...].reshape(nseq * slots, width)
    a = jnp.dot(x, wgb_ref[...], preferred_element_type=F32)
    u = jnp.dot(x, wub_ref[...], preferred_element_type=F32)
    o_ref[...] = (a * jax.nn.sigmoid(a) * u).astype(BF16).reshape(o_ref.shape)


def _ffn_down_kernel(x_ref, g_ref, wd_ref, o_ref, wdb_ref):
    @pl.when(pl.program_id(1) == 0)
    def _():
        wdb_ref[...] = wd_ref[...].astype(BF16)

    nseq, slots, width = x_ref.shape
    y = jnp.dot(x_ref[...].reshape(nseq * slots, width), wdb_ref[...], preferred_element_type=F32)
    y = y * g_ref[...].reshape(nseq * slots, 1)
    o_ref[...] = y.astype(BF16).reshape(o_ref.shape)


def _ffn(xg, gates, wg, wu, wd, layer):
    B, _, slots, _ = xg.shape
    n_up = _pick_tile(B, (2, 1))
    n_down = _pick_tile(B, (4, 2, 1))
    acts = lambda nseq, width: pl.BlockSpec((nseq, None, slots, width), lambda e, b: (b, e, 0, 0))
    weight = lambda rows, cols: pl.BlockSpec((None, None, rows, cols), lambda e, b: (layer, e, 0, 0))
    mid = pl.pallas_call(
        _ffn_up_kernel,
        grid=(N_EXPERTS, B // n_up),
        in_specs=[acts(n_up, D_MODEL), weight(D_MODEL, D_FF_EXPERT), weight(D_MODEL, D_FF_EXPERT)],
        out_specs=acts(n_up, D_FF_EXPERT),
        out_shape=jax.ShapeDtypeStruct((B, N_EXPERTS, slots, D_FF_EXPERT), BF16),
        scratch_shapes=[pltpu.VMEM((D_MODEL, D_FF_EXPERT), BF16), pltpu.VMEM((D_MODEL, D_FF_EXPERT), BF16)],
        compiler_params=_params("arbitrary", "arbitrary"),
        name="moe_ffn_up",
    )(xg, wg, wu)
    return pl.pallas_call(
        _ffn_down_kernel,
        grid=(N_EXPERTS, B // n_down),
        in_specs=[acts(n_down, D_FF_EXPERT), acts(n_down, 1), weight(D_FF_EXPERT, D_MODEL)],
        out_specs=acts(n_down, D_MODEL),
        out_shape=jax.ShapeDtypeStruct(xg.shape, BF16),
        scratch_shapes=[pltpu.VMEM((D_FF_EXPERT, D_MODEL), BF16)],
        compiler_params=_params("arbitrary", "arbitrary"),
        name="moe_ffn_down",
    )(mid, gates, wd)


def _combine_kernel(cnt_ref, cntc_ref, rank_ref, y_ref, h_ref, g_ref, b_ref, expand_ref, rpat_ref,
                    o_ref, acc_ref, *, slots, ch, win, nchunk, alpha):
    bi, c = pl.program_id(0), pl.program_id(1)
    k0 = c * ch
    tt = ch * LANES
    max_start = slots - win
    rank = rank_ref[...]
    lo_row = cntc_ref[pl.ds(k0, 1), :]
    w_row = jnp.minimum(lo_row & -SLOT_ALIGN, max_start)
    rel = rank - w_row
    in_win = (rank >= 0) & (rel >= 0) & (rel < win)
    relb = jnp.where(in_win, rel, -1).astype(F32).astype(BF16)
    rel_wide = jnp.dot(relb, expand_ref[...], preferred_element_type=F32)
    onehot = (rel_wide == rpat_ref[...]).astype(BF16)

    starts, windows = [], []
    for e in range(N_EXPERTS):
        lo = cnt_ref[(bi * N_EXPERTS + e) * (nchunk + 1) + k0]
        w = jnp.minimum(lo & -SLOT_ALIGN, max_start)
        starts.append(w)
        windows.append(y_ref[e, pl.ds(pl.multiple_of(w, SLOT_ALIGN), win), :])
    acc_ref[...] = jnp.dot(onehot, jnp.concatenate(windows, axis=0), preferred_element_type=F32)

    lane = lax.broadcasted_iota(jnp.int32, rank.shape, 1)
    col = lax.broadcasted_iota(jnp.int32, (tt, win), 1)
    for e in range(N_EXPERTS):
        hi = cnt_ref[(bi * N_EXPERTS + e) * (nchunk + 1) + k0 + ch]

        @pl.when(hi > starts[e] + win)
        def _(e=e, hi=hi):
            rk = jnp.sum(jnp.where(lane == e, rank, 0), axis=-1, keepdims=True)

            def extra(i, carry):
                w2 = starts[e] + (i + 1) * win
                w2c = jnp.minimum(w2, max_start)
                rows = y_ref[e, pl.ds(pl.multiple_of(w2c, SLOT_ALIGN), win), :]
                hit = ((rk - w2c) == col) & (rk >= w2)
                acc_ref[...] += jnp.dot(hit.astype(BF16), rows, preferred_element_type=F32)
                return carry

            lax.fori_loop(0, (hi - starts[e] - 1) // win, extra, 0)

    o_ref[...] = _ln(alpha * h_ref[...] + acc_ref[...], g_ref[...], b_ref[...])


def _combine(cnt, cntc, rank_c, y, h1, g, b, B, lp, alpha, ch, win):
    n = B * lp
    slots = y.shape[2]
    nchunk = lp // LANES
    tt = ch * LANES
    nt = nchunk // ch
    assert slots % SLOT_ALIGN == 0 and win % SLOT_ALIGN == 0 and win <= slots
    wide = N_EXPERTS * win
    expand, pos = _window_constants(win)
    tok = lambda bi, c, cnt: (bi * nt + c, 0)
    const = lambda bi, c, cnt: (0, 0)
    return pl.pallas_call(
        functools.partial(_combine_kernel, slots=slots, ch=ch, win=win, nchunk=nchunk, alpha=alpha),
        grid_spec=pltpu.PrefetchScalarGridSpec(
            num_scalar_prefetch=1,
            grid=(B, nt),
            in_specs=[
                pl.BlockSpec((None, cntc.shape[1], N_EXPERTS), lambda bi, c, cnt: (bi, 0, 0)),
                pl.BlockSpec((tt, N_EXPERTS), tok),
                pl.BlockSpec((None, N_EXPERTS, slots, D_MODEL), lambda bi, c, cnt: (bi, 0, 0, 0)),
                pl.BlockSpec((tt, D_MODEL), tok),
                pl.BlockSpec((1, D_MODEL), const),
                pl.BlockSpec((1, D_MODEL), const),
                pl.BlockSpec((N_EXPERTS, wide), const),
                pl.BlockSpec((1, wide), const),
            ],
            out_specs=pl.BlockSpec((tt, D_MODEL), tok),
            scratch_shapes=[pltpu.VMEM((tt, D_MODEL), F32)],
        ),
        out_shape=jax.ShapeDtypeStruct((n, D_MODEL), F32),
        compiler_params=_params("parallel", "arbitrary"),
        name="moe_combine_ln",
    )(cnt, cntc, rank_c, y, h1, g.reshape(1, -1), b.reshape(1, -1),
      jnp.asarray(expand, dtype=BF16), jnp.asarray(pos.reshape(1, wide)))


def kernel(x, meta, ln0_g, ln0_b, w_in, b_gate, sink, w_attn_o, w_four_o, w_out, ln1_g, ln1_b,
           w_router, w_e_gate, w_e_up, w_e_down, ln2_g, ln2_b):
    B, seq, d = x.shape
    depth = w_in.shape[0]
    assert d == D_MODEL and meta.shape == (N_META, D_MODEL)
    assert seq % BLOCK == 0 and N_META % SUBLANES == 0 and N_META <= BLOCK
    L = seq + N_META
    nb = -(-L // BLOCK)
    lp = nb * BLOCK
    lead = lp - L
    cap = CAPACITY_FACTOR * L // N_EXPERTS
    slots = -(-cap // SLOT_ALIGN) * SLOT_ALIGN
    alpha = float((2 * depth) ** 0.25)
    n = B * lp
    nchunk = lp // LANES
    tm = _pick_tile(n, (1024, 512, 256, 128))
    tf = _pick_tile(lp, (384, 128))
    ch = _pick_tile(nchunk, (3, 2, 1))
    win = min(slots, -(-(ch * LANES * cap // L * 4 // 3 + SLOT_ALIGN) // SLOT_ALIGN) * SLOT_ALIGN)

    gi = np.arange(FOURIER_GROUP)
    ang = 2.0 * np.pi * ((gi[:, None] * gi[None, :]) % FOURIER_GROUP) / FOURIER_GROUP
    cs = jnp.asarray(np.concatenate([np.cos(ang), np.sin(ang)], axis=1) * FOURIER_GROUP ** -0.5, dtype=BF16)
    dft_rows = _fourier_steps(lp, lead, L, tf) * tf
    wdft = _dft_matrix(dft_rows, dft_rows, lead, L)
    bias, variant_of_block = _attn_bias_tables(nb, lead)

    h = _embed(x, meta, ln0_g, ln0_b, nb, lead).reshape(n, D_MODEL)
    for l in range(depth):
        q, kv, pq, gates = _inproj(h, w_in[l].astype(BF16), b_gate[l], cs, tm)
        attn = _attention(q.reshape(B, lp, -1), kv.reshape(B, lp, -1), sink[l], bias, variant_of_block, nb)
        four = _fourier(wdft, pq.reshape(B, lp, -1), lead, L, tf)
        h1, h1b, aff_r = _outproj(
            attn.reshape(n, -1), four.reshape(n, -1), gates, h,
            w_attn_o[l].astype(BF16), w_four_o[l].astype(BF16), w_out[l].astype(BF16),
            ln1_g[l], ln1_b[l], w_router[l], alpha, tm)
        rank_r, rank_c, cntv, cntc = _topk(aff_r, B, lp, lead, cap)
        cnt = cntv[:, :, :nchunk + 1].reshape(-1)
        rank_rows = rank_r.transpose(0, 2, 1, 3).reshape(B, N_EXPERTS, lp)
        aff_rows = aff_r.reshape(N_EXPERTS, B, lp).transpose(1, 0, 2)
        xg, gsel = _gather(cnt, cntv, rank_rows, aff_rows, h1b, B, lp, slots, ch, win)
        y = _ffn(xg, gsel, w_e_gate, w_e_up, w_e_down, l)
        h = _combine(cnt, cntc, rank_c, y, h1, ln2_g[l], ln2_b[l], B, lp, alpha, ch, win)
    return h.reshape(B, lp, D_MODEL)[:, lead + N_META:]
```

```python
import functools
import math

import numpy as np
import jax
import jax.numpy as jnp
from jax import lax
from jax.experimental import pallas as pl
from jax.experimental.pallas import tpu as pltpu

D_MODEL = 1024
N_META = 16
N_HEADS = 8
N_KV_HEADS = 2
HEAD_DIM = 64
GQA_GROUP = N_HEADS // N_KV_HEADS
ATTN_WIDTH = N_HEADS * HEAD_DIM
KV_WIDTH = N_KV_HEADS * HEAD_DIM
WINDOW = 128
BLOCK = 128
N_FOURIER_GROUPS = 4
FOURIER_GROUP = 128
FOURIER_WIDTH = N_FOURIER_GROUPS * FOURIER_GROUP
N_BRANCHES = 2
GATE_WIDTH = N_BRANCHES * D_MODEL
N_EXPERTS = 16
CAPACITY_FACTOR = 2
D_FF_EXPERT = 1536
LN_EPS = 1e-5
NEG_INF = -1e30
Q_END = ATTN_WIDTH
K_END = Q_END + KV_WIDTH
V_END = K_END + KV_WIDTH
F_END = V_END + FOURIER_WIDTH
IN_WIDTH = F_END + GATE_WIDTH

LANES = 128
SUBLANES = 8
VMEM_LIMIT_BYTES = 56 * 1024 * 1024

F32 = jnp.float32
BF16 = jnp.bfloat16
NT_DIMS = (((1,), (1,)), ((), ()))


def _pick_tile(n, candidates):
    for c in candidates:
        if n % c == 0:
            return c
    raise ValueError(f"no tile in {candidates} divides {n}")


def _params(*sem):
    return pltpu.CompilerParams(dimension_semantics=sem, vmem_limit_bytes=VMEM_LIMIT_BYTES)


def _ln(x, g, b):
    mu = jnp.mean(x, axis=-1, keepdims=True)
    xc = x - mu
    var = jnp.mean(xc * xc, axis=-1, keepdims=True)
    return xc * lax.rsqrt(var + LN_EPS) * g + b


def _embed_kernel(*refs, lead, group):
    x_refs, (meta_ref, g_ref, b_ref, o_ref) = refs[:group], refs[group:]
    for k in range(1, group):
        o_ref[k * BLOCK:(k + 1) * BLOCK, :] = _ln(x_refs[k][...], g_ref[...], b_ref[...])

    @pl.when(pl.program_id(1) == 0)
    def _():
        o_ref[0:lead, :] = jnp.zeros((lead, D_MODEL), F32)
        o_ref[lead:BLOCK, :] = _ln(meta_ref[...], g_ref[...], b_ref[...])

    @pl.when(pl.program_id(1) > 0)
    def _():
        o_ref[0:BLOCK, :] = _ln(x_refs[0][...], g_ref[...], b_ref[...])


def _embed(x, meta, g, b, nb, lead):
    B = x.shape[0]
    group = _pick_tile(nb, (11, 3, 1))
    x_spec = lambda k: pl.BlockSpec((None, BLOCK, D_MODEL),
                                    lambda bi, j: (bi, jnp.maximum(group * j + k - 1, 0), 0))
    const = lambda bi, j: (0, 0)
    return pl.pallas_call(
        functools.partial(_embed_kernel, lead=lead, group=group),
        grid=(B, nb // group),
        in_specs=[x_spec(k) for k in range(group)] + [
            pl.BlockSpec((N_META, D_MODEL), const),
            pl.BlockSpec((1, D_MODEL), const),
            pl.BlockSpec((1, D_MODEL), const),
        ],
        out_specs=pl.BlockSpec((None, group * BLOCK, D_MODEL), lambda bi, j: (bi, j, 0)),
        out_shape=jax.ShapeDtypeStruct((B, nb * BLOCK, D_MODEL), F32),
        compiler_params=_params("parallel", "arbitrary"),
        name="embed_ln",
    )(*([x] * group), meta, g.reshape(1, -1), b.reshape(1, -1))


def _inproj_kernel(h_ref, w_ref, bg_ref, cs_ref, q_ref, kv_ref, pq_ref, gate_ref):
    hb = h_ref[...].astype(BF16)
    q_ref[...] = jnp.dot(hb, w_ref[:, 0:Q_END], preferred_element_type=F32).astype(BF16)
    kv_ref[...] = jnp.dot(hb, w_ref[:, Q_END:V_END], preferred_element_type=F32).astype(BF16)
    uf = jnp.dot(hb, w_ref[:, V_END:F_END], preferred_element_type=F32).astype(BF16)
    for g in range(N_FOURIER_GROUPS):
        lo = g * FOURIER_GROUP
        pq = jnp.dot(uf[:, lo:lo + FOURIER_GROUP], cs_ref[...], preferred_element_type=F32)
        pq_ref[:, lo:lo + FOURIER_GROUP] = pq[:, 0:FOURIER_GROUP].astype(BF16)
        pq_ref[:, FOURIER_WIDTH + lo:FOURIER_WIDTH + lo + FOURIER_GROUP] = (
            pq[:, FOURIER_GROUP:2 * FOURIER_GROUP].astype(BF16))
    chunk = FOURIER_WIDTH
    for c in range(GATE_WIDTH // chunk):
        lo = c * chunk
        ug = jnp.dot(hb, w_ref[:, F_END + lo:F_END + lo + chunk], preferred_element_type=F32)
        gate_ref[:, lo:lo + chunk] = jax.nn.sigmoid(ug + bg_ref[:, lo:lo + chunk])


def _inproj(h, w_bf16, b_gate, cs, tm):
    n = h.shape[0]
    row = lambda i: (i, 0)
    const = lambda i: (0, 0)
    return pl.pallas_call(
        _inproj_kernel,
        grid=(n // tm,),
        in_specs=[
            pl.BlockSpec((tm, D_MODEL), row),
            pl.BlockSpec((D_MODEL, IN_WIDTH), const),
            pl.BlockSpec((1, GATE_WIDTH), const),
            pl.BlockSpec((FOURIER_GROUP, 2 * FOURIER_GROUP), const),
        ],
        out_specs=[
            pl.BlockSpec((tm, ATTN_WIDTH), row),
            pl.BlockSpec((tm, 2 * KV_WIDTH), row),
            pl.BlockSpec((tm, 2 * FOURIER_WIDTH), row),
            pl.BlockSpec((tm, GATE_WIDTH), row),
        ],
        out_shape=[
            jax.ShapeDtypeStruct((n, ATTN_WIDTH), BF16),
            jax.ShapeDtypeStruct((n, 2 * KV_WIDTH), BF16),
            jax.ShapeDtypeStruct((n, 2 * FOURIER_WIDTH), BF16),
            jax.ShapeDtypeStruct((n, GATE_WIDTH), F32),
        ],
        compiler_params=_params("parallel"),
        name="inproj",
    )(h, w_bf16, b_gate.reshape(1, GATE_WIDTH), cs)


SCORES_AHEAD = 5


def _attn_kernel(sink_ref, q_ref, *refs, group, variant):
    kv_refs, (bias_ref, o_ref) = refs[:group + 2], refs[group + 2:]
    first_block = pl.program_id(1) * group
    kv = jnp.concatenate([r[...] for r in kv_refs], axis=0)
    low_half = lax.broadcasted_iota(jnp.int32, (kv.shape[0], LANES), 1) < HEAD_DIM

    def lane_half_operands(x):
        swapped = jnp.concatenate([x[:, HEAD_DIM:], x[:, :HEAD_DIM]], axis=1)
        zero = jnp.zeros_like(x)
        return {(kvh, half): jnp.where(low_half if half == 0 else ~low_half,
                                       x if kvh == half else swapped, zero)
                for kvh in range(N_KV_HEADS) for half in range(2)}

    k_ops = lane_half_operands(kv[:, 0:KV_WIDTH])
    v_ops = lane_half_operands(kv[:, KV_WIDTH:2 * KV_WIDTH])
    keys = lambda t: slice(t * BLOCK, (t + 3) * BLOCK)

    def scores(t, h):
        pair = h // 2
        qp = q_ref[t * BLOCK:(t + 1) * BLOCK, pair * LANES:(pair + 1) * LANES]
        return lax.dot_general(qp, k_ops[(h // GQA_GROUP, h % 2)][keys(t)], NT_DIMS,
                               preferred_element_type=F32)

    def head_out(t, h, s):
        sink = sink_ref[h]
        logits = s * (HEAD_DIM ** -0.5) + bias_ref[variant(first_block + t), h]
        m = jnp.maximum(jnp.max(logits, axis=-1, keepdims=True), sink)
        p = jnp.exp(logits - m)
        denom = jnp.sum(p, axis=-1, keepdims=True) + jnp.exp(sink - m)
        o = jnp.dot(p.astype(BF16), v_ops[(h // GQA_GROUP, h % 2)][keys(t)], preferred_element_type=F32)
        return o / denom

    work = [(t, h) for t in range(group) for h in range(N_HEADS)]
    ahead = SCORES_AHEAD
    pending = [scores(*w) for w in work[:ahead]]
    for n, (t, h) in enumerate(work):
        s_cur = pending.pop(0)
        if n + ahead < len(work):
            pending.append(scores(*work[n + ahead]))
        o = head_out(t, h, s_cur)
        if h % 2 == 0:
            o_even = o
        else:
            pair = h // 2
            o_ref[t * BLOCK:(t + 1) * BLOCK, pair * LANES:(pair + 1) * LANES] = (o_even + o).astype(BF16)


def _attn_bias_tables(nb, lead):
    qi = np.arange(BLOCK)[:, None]
    si = np.arange(3 * BLOCK)[None, :]
    rel = np.abs(si - BLOCK - qi).astype(np.float32)
    slopes = np.array([2.0 ** (-8.0 * (h + 1) / N_HEADS) for h in range(N_HEADS)], np.float32)
    base = np.where(rel[None] <= WINDOW, -slopes[:, None, None] * rel[None], np.float32(NEG_INF))
    variants, keys, variant_of_block = [], [], []
    for i in range(nb):
        kpos = (i - 1) * BLOCK + np.arange(3 * BLOCK)
        valid = (kpos >= lead) & (kpos < nb * BLOCK)
        key = valid.tobytes()
        if key not in keys:
            keys.append(key)
            variants.append(np.where(valid[None, None, :], base, np.float32(NEG_INF)))
        variant_of_block.append(keys.index(key))
    return jnp.asarray(np.stack(variants).astype(np.float32)), variant_of_block


def _attention(q, kv, sink, bias, variant_of_block, nb):
    B, lp, _ = q.shape
    group = _pick_tile(nb, (3, 1))
    interior = max(set(variant_of_block), key=variant_of_block.count)

    def variant(i):
        v = jnp.int32(interior)
        for blk, var in enumerate(variant_of_block):
            if var != interior:
                v = jnp.where(i == blk, var, v)
        return v

    kv_spec = lambda t: pl.BlockSpec((None, BLOCK, 2 * KV_WIDTH),
                                     lambda b, g: (b, jnp.clip(group * g + t - 1, 0, nb - 1), 0))
    rows = pl.BlockSpec((None, group * BLOCK, ATTN_WIDTH), lambda b, g: (b, g, 0))
    return pl.pallas_call(
        functools.partial(_attn_kernel, group=group, variant=variant),
        grid=(B, nb // group),
        in_specs=[pl.BlockSpec(memory_space=pltpu.SMEM), rows]
                 + [kv_spec(t) for t in range(group + 2)]
                 + [pl.BlockSpec(bias.shape, lambda b, g: (0, 0, 0, 0))],
        out_specs=rows,
        out_shape=jax.ShapeDtypeStruct((B, lp, ATTN_WIDTH), BF16),
        compiler_params=_params("parallel", "arbitrary"),
        name="attention",
    )(sink, q, *([kv] * (group + 2)), bias)


def _fourier_kernel(w_ref, p_ref, q_ref, o_ref, prev_ref, pe_ref, qo_ref, *, lp, lead, seq_len, tf):
    m = pl.program_id(1)
    last = pl.num_programs(1) - 1
    nblk = lp // tf
    shift = lead + 1
    kp = pe_ref.shape[0]
    half = seq_len // 2
    r = lax.broadcasted_iota(jnp.int32, (tf, tf), 0)
    c = lax.broadcasted_iota(jnp.int32, (tf, tf), 1)
    flip = (r + c == tf - 1).astype(BF16)

    @pl.when(m == 0)
    def _():
        row = lax.broadcasted_iota(jnp.int32, (lp, 1), 0)
        dc = jnp.sum(jnp.where(row >= lead, p_ref[...].astype(F32), 0.0), axis=0, keepdims=True)
        prev_ref[...] = jnp.zeros_like(prev_ref)
        prev_ref[tf - 1:tf, :] = dc * np.float32(seq_len ** -0.5)
        def mirror(x_ref, i):
            above = (jnp.dot(flip, x_ref[(nblk - i) * tf:(nblk - i + 1) * tf, :], preferred_element_type=F32)
                     if i > 0 else jnp.zeros((tf, FOURIER_WIDTH), F32))
            below = jnp.dot(flip, x_ref[(nblk - 1 - i) * tf:(nblk - i) * tf, :], preferred_element_type=F32)
            return jnp.concatenate([above[tf - shift:, :], below[:tf - shift, :]], axis=0)

        for i in range(kp // tf):
            rows = slice(i * tf, (i + 1) * tf)
            n = lax.broadcasted_iota(jnp.int32, (tf, 1), 0) + (i * tf - lead)
            paired = (n >= 1) & (n <= half - 1)
            alone = (n == 0) | (n == half)
            p_blk = p_ref[rows, :].astype(F32)
            pe_ref[rows, :] = jnp.where(paired, p_blk + mirror(p_ref, i),
                                        jnp.where(alone, p_blk, 0.0)).astype(BF16)
            qo_ref[rows, :] = jnp.where(paired, q_ref[rows, :].astype(F32) - mirror(q_ref, i), 0.0).astype(BF16)

    t1 = jnp.dot(w_ref[:, 0:kp], pe_ref[...], preferred_element_type=F32)
    t2 = jnp.dot(w_ref[:, kp:2 * kp], qo_ref[...], preferred_element_type=F32)
    direct = t1 + t2
    mirrored = jnp.dot(flip, (t1 - t2).astype(BF16), preferred_element_type=F32)
    shifted = jnp.concatenate([prev_ref[tf - shift:, :], direct[:tf - shift, :]], axis=0)
    prev_ref[...] = direct

    @pl.when(m < last)
    def _():
        o_ref[pl.ds(pl.multiple_of(m * tf, tf), tf), :] = shifted.astype(BF16)
        o_ref[pl.ds(pl.multiple_of((nblk - 1 - m) * tf, tf), tf), :] = mirrored.astype(BF16)

    @pl.when(m == last)
    def _():
        row = lax.broadcasted_iota(jnp.int32, (tf, 1), 0) + (nblk // 2) * tf
        mid = jnp.where(row <= seq_len // 2 + lead, shifted, mirrored)
        o_ref[(nblk // 2) * tf:(nblk // 2 + 1) * tf, :] = mid.astype(BF16)


def _dft_matrix(nrows, ncols, lead, seq_len):
    w = np.float32(2.0 * math.pi / seq_len)
    k = (jnp.arange(nrows, dtype=jnp.int32) + 1)[:, None]
    j = jnp.arange(ncols // LANES, dtype=jnp.int32)[None, :]
    r = jnp.arange(LANES, dtype=jnp.int32)[None, :]
    a = ((k * (LANES * j - lead)) % seq_len).astype(F32) * w
    b = ((k * r) % seq_len).astype(F32) * w
    ca, sa, cb, sb = jnp.cos(a)[:, :, None], jnp.sin(a)[:, :, None], jnp.cos(b)[:, None, :], jnp.sin(b)[:, None, :]
    n = jnp.arange(ncols, dtype=jnp.int32) - lead
    valid = ((n >= 0) & (n <= seq_len // 2)).reshape(1, ncols // LANES, LANES)
    scale = np.float32(seq_len ** -0.5)
    wc = jnp.where(valid, (ca * cb - sa * sb) * scale, 0.0).reshape(nrows, ncols)
    ws = jnp.where(valid, -(sa * cb + ca * sb) * scale, 0.0).reshape(nrows, ncols)
    return jnp.concatenate([wc, ws], axis=1).astype(BF16)


def _fourier_steps(lp, lead, seq_len, tf):
    nblk = lp // tf
    assert seq_len % 2 == 0 and lp % tf == 0 and nblk % 2 == 1 and lead + 1 < tf
    steps = nblk // 2 + 1
    assert steps * tf > seq_len // 2 + lead
    return steps


def _fourier(w, pq, lead, seq_len, tf):
    B, lp, _ = pq.shape
    steps = _fourier_steps(lp, lead, seq_len, tf)
    kp = steps * tf
    return pl.pallas_call(
        functools.partial(_fourier_kernel, lp=lp, lead=lead, seq_len=seq_len, tf=tf),
        grid=(B, steps),
        in_specs=[
            pl.BlockSpec((tf, 2 * kp), lambda b, m: (m, 0)),
            pl.BlockSpec((None, lp, FOURIER_WIDTH), lambda b, m: (b, 0, 0)),
            pl.BlockSpec((None, lp, FOURIER_WIDTH), lambda b, m: (b, 0, 1)),
        ],
        out_specs=pl.BlockSpec((None, lp, FOURIER_WIDTH), lambda b, m: (b, 0, 0)),
        out_shape=jax.ShapeDtypeStruct((B, lp, FOURIER_WIDTH), BF16),
        scratch_shapes=[pltpu.VMEM((tf, FOURIER_WIDTH), F32),
                        pltpu.VMEM((kp, FOURIER_WIDTH), BF16),
                        pltpu.VMEM((kp, FOURIER_WIDTH), BF16)],
        compiler_params=_params("parallel", "arbitrary"),
        name="fourier",
    )(w, pq, pq)


def _split_bf16(x):
    hi = x.astype(BF16)
    lo = (x - hi.astype(F32)).astype(BF16)
    return hi, lo


OUTPROJ_SUB = 128


def _outproj_kernel(attn_ref, four_ref, gate_ref, h_ref, wa_ref, wf_ref, wo_ref, g_ref, b_ref,
                    wrth_ref, wrtl_ref, h1_ref, h1b_ref, affr_ref, *, alpha):
    tm = h_ref.shape[0]
    subs = [slice(i * OUTPROJ_SUB, (i + 1) * OUTPROJ_SUB) for i in range(tm // OUTPROJ_SUB)]
    ya = [jnp.dot(attn_ref[s, :], wa_ref[...], preferred_element_type=F32) for s in subs]
    yf = [jnp.dot(four_ref[s, :], wf_ref[...], preferred_element_type=F32) for s in subs]
    merged = [gate_ref[s, 0:D_MODEL] * a + gate_ref[s, D_MODEL:GATE_WIDTH] * f for s, a, f in zip(subs, ya, yf)]
    mix = [jnp.dot(mg.astype(BF16), wo_ref[...], preferred_element_type=F32) for mg in merged]
    for s, mx in zip(subs, mix):
        h1 = _ln(alpha * h_ref[s, :] + mx, g_ref[...], b_ref[...])
        h1_ref[s, :] = h1
        hi, lo = _split_bf16(h1)
        h1b_ref[s, :] = hi
        lr = (lax.dot_general(wrth_ref[...], hi, NT_DIMS, preferred_element_type=F32)
              + lax.dot_general(wrtl_ref[...], hi, NT_DIMS, preferred_element_type=F32)
              + lax.dot_general(wrth_ref[...], lo, NT_DIMS, preferred_element_type=F32))
        er = jnp.exp(lr - jnp.max(lr, axis=0, keepdims=True))
        affr_ref[:, s] = er / jnp.sum(er, axis=0, keepdims=True)


def _outproj(attn, four, gates, h, wa, wf, wo, g, b, w_router, alpha, tm):
    n = h.shape[0]
    row = lambda i: (i, 0)
    const = lambda i: (0, 0)
    wrth, wrtl = _split_bf16(w_router.T)
    return pl.pallas_call(
        functools.partial(_outproj_kernel, alpha=alpha),
        grid=(n // tm,),
        in_specs=[
            pl.BlockSpec((tm, ATTN_WIDTH), row),
            pl.BlockSpec((tm, FOURIER_WIDTH), row),
            pl.BlockSpec((tm, GATE_WIDTH), row),
            pl.BlockSpec((tm, D_MODEL), row),
            pl.BlockSpec((ATTN_WIDTH, D_MODEL), const),
            pl.BlockSpec((FOURIER_WIDTH, D_MODEL), const),
            pl.BlockSpec((D_MODEL, D_MODEL), const),
            pl.BlockSpec((1, D_MODEL), const),
            pl.BlockSpec((1, D_MODEL), const),
            pl.BlockSpec((N_EXPERTS, D_MODEL), const),
            pl.BlockSpec((N_EXPERTS, D_MODEL), const),
        ],
        out_specs=[
            pl.BlockSpec((tm, D_MODEL), row),
            pl.BlockSpec((tm, D_MODEL), row),
            pl.BlockSpec((N_EXPERTS, tm), lambda i: (0, i)),
        ],
        out_shape=[
            jax.ShapeDtypeStruct((n, D_MODEL), F32),
            jax.ShapeDtypeStruct((n, D_MODEL), BF16),
            jax.ShapeDtypeStruct((N_EXPERTS, n), F32),
        ],
        compiler_params=_params("parallel"),
        name="outproj_ln_router",
    )(attn, four, gates, h, wa, wf, wo, g.reshape(1, -1), b.reshape(1, -1), wrth, wrtl)


def _topk_kernel(aff_ref, rr_ref, rc_ref, cnt_ref, cntc_ref, *, lead, lp, cap):
    nchunk = lp // LANES
    lane = lax.broadcasted_iota(jnp.int32, (N_EXPERTS, lp), 1)
    bits = jnp.where(lane >= lead, pltpu.bitcast(aff_ref[...], jnp.int32), -1)

    def search(i, t):
        cand = t | (jnp.int32(1) << (30 - i))
        cnt = jnp.sum((bits >= cand).astype(jnp.int32), axis=-1, keepdims=True)
        return jnp.where(cnt >= cap, cand, t)

    thr = lax.fori_loop(0, 31, search, jnp.zeros((N_EXPERTS, 1), jnp.int32))
    gt = bits > thr
    eq = bits == thr
    need = cap - jnp.sum(gt.astype(jnp.int32), axis=-1, keepdims=True)

    r = lax.broadcasted_iota(jnp.int32, (LANES, LANES), 0)
    c = lax.broadcasted_iota(jnp.int32, (LANES, LANES), 1)
    upper = (r <= c).astype(BF16)
    lower = (c <= r).astype(BF16)
    ident = (c == r).astype(BF16)

    chunks = [slice(k * LANES, (k + 1) * LANES) for k in range(nchunk)]
    eqb = eq.astype(BF16)
    needf = need.astype(F32)
    pre_eq = [jnp.dot(eqb[:, sl], upper, preferred_element_type=F32) for sl in chunks]
    off = jnp.zeros((N_EXPERTS, 1), F32)
    sel_chunks = []
    for k, sl in enumerate(chunks):
        sel_chunks.append(gt[:, sl] | (eq[:, sl] & (pre_eq[k] + off <= needf)))
        off = off + pre_eq[k][:, LANES - 1:LANES]

    selb = [sel.astype(BF16) for sel in sel_chunks]
    pre_r = [jnp.dot(s, upper, preferred_element_type=F32) for s in selb]
    pre_c = [lax.dot_general(lower, s, NT_DIMS, preferred_element_type=F32) for s in selb]
    sel_c = [lax.dot_general(ident, s, NT_DIMS, preferred_element_type=F32) for s in selb]
    off_r = jnp.zeros((N_EXPERTS, 1), F32)
    off_c = jnp.zeros((1, N_EXPERTS), F32)
    cnt_lane = lax.broadcasted_iota(jnp.int32, (N_EXPERTS, LANES), 1)
    cnt = jnp.zeros((N_EXPERTS, LANES), F32)
    for k, sl in enumerate(chunks):
        rr_ref[:, sl] = jnp.where(sel_chunks[k], pre_r[k] + (off_r - 1.0), -1.0).astype(jnp.int32)
        off_r = off_r + pre_r[k][:, LANES - 1:LANES]
        cnt = jnp.where(cnt_lane == k + 1, off_r, cnt)
        rc_ref[sl, :] = jnp.where(sel_c[k] > 0.5, pre_c[k] + (off_c - 1.0), -1.0).astype(jnp.int32)
        off_c = off_c + pre_c[k][LANES - 1:LANES, :]
        cntc_ref[k + 1:k + 2, :] = off_c.astype(jnp.int32)
    cnt_ref[...] = cnt.astype(jnp.int32)
    cntc_ref[0:1, :] = jnp.zeros((1, N_EXPERTS), jnp.int32)
    pad_rows = cntc_ref.shape[0] - nchunk - 1
    if pad_rows:
        cntc_ref[nchunk + 1:, :] = jnp.zeros((pad_rows, N_EXPERTS), jnp.int32)


def _topk(aff_r, B, lp, lead, cap):
    n = B * lp
    nchunk = lp // LANES
    assert nchunk + 1 <= LANES
    nb1 = -(-(nchunk + 1) // SUBLANES) * SUBLANES
    return pl.pallas_call(
        functools.partial(_topk_kernel, lead=lead, lp=lp, cap=cap),
        grid=(B,),
        in_specs=[pl.BlockSpec((N_EXPERTS, lp), lambda b: (0, b))],
        out_specs=[
            pl.BlockSpec((N_EXPERTS, lp), lambda b: (0, b)),
            pl.BlockSpec((lp, N_EXPERTS), lambda b: (b, 0)),
            pl.BlockSpec((None, N_EXPERTS, LANES), lambda b: (b, 0, 0)),
            pl.BlockSpec((None, nb1, N_EXPERTS), lambda b: (b, 0, 0)),
        ],
        out_shape=[
            jax.ShapeDtypeStruct((N_EXPERTS, n), jnp.int32),
            jax.ShapeDtypeStruct((n, N_EXPERTS), jnp.int32),
            jax.ShapeDtypeStruct((B, N_EXPERTS, LANES), jnp.int32),
            jax.ShapeDtypeStruct((B, nb1, N_EXPERTS), jnp.int32),
        ],
        compiler_params=_params("parallel"),
        name="topk_select",
    )(aff_r)


SLOT_ALIGN = 16


def _window_constants(win):
    wide = N_EXPERTS * win
    expand = np.zeros((N_EXPERTS, wide), np.float32)
    for e in range(N_EXPERTS):
        expand[e, e * win:(e + 1) * win] = 1.0
    return expand, (np.arange(wide) % win).astype(np.float32)


def _gather_kernel(cnt_ref, cntv_ref, rank_ref, aff_ref, h_ref, o_ref, g_ref,
                   *, slots, ch, win, nchunk):
    bi, c = pl.program_id(0), pl.program_id(1)
    k0 = c * ch
    tt = ch * LANES
    max_start = slots - win

    @pl.when(c == 0)
    def _():
        o_ref[...] = jnp.zeros_like(o_ref)
        g_ref[...] = jnp.zeros_like(g_ref)

    lane = lax.broadcasted_iota(jnp.int32, cntv_ref.shape, 1)
    lo_col = jnp.sum(jnp.where(lane == k0, cntv_ref[...], 0), axis=-1, keepdims=True)
    w_col = jnp.minimum(lo_col & -SLOT_ALIGN, max_start)
    rank = rank_ref[...]
    aff = aff_ref[...]
    rel = rank - w_col
    in_win = (rank >= 0) & (rel >= 0) & (rel < win)
    rel = jnp.where(in_win, rel, -1)
    row = lax.broadcasted_iota(jnp.int32, (win, tt), 0)
    hit_all = jnp.concatenate([rel[e:e + 1, :] == row for e in range(N_EXPERTS)], axis=0)
    rows = h_ref[...]
    xw = jnp.dot(hit_all.astype(BF16), rows, preferred_element_type=F32)

    def add_rows(e, start, hit, vals):
        dst = pl.ds(pl.multiple_of(start, SLOT_ALIGN), win)
        o_ref[e, dst, :] = o_ref[e, dst, :] + vals.astype(BF16)
        g_ref[e, dst, :] += jnp.sum(jnp.where(hit, aff[e:e + 1, :], 0.0), axis=-1, keepdims=True)

    for e in range(N_EXPERTS):
        base = (bi * N_EXPERTS + e) * (nchunk + 1) + k0
        w = jnp.minimum(cnt_ref[base] & -SLOT_ALIGN, max_start)
        add_rows(e, w, hit_all[e * win:(e + 1) * win, :], xw[e * win:(e + 1) * win, :])
        hi = cnt_ref[base + ch]

        @pl.when(hi > w + win)
        def _(e=e, w=w, hi=hi):
            rk = rank[e:e + 1, :]

            def extra(i, carry):
                w2 = w + (i + 1) * win
                w2c = jnp.minimum(w2, max_start)
                hit = ((rk - w2c) == row) & (rk >= w2)
                add_rows(e, w2c, hit, jnp.dot(hit.astype(BF16), rows, preferred_element_type=F32))
                return carry

            lax.fori_loop(0, (hi - w - 1) // win, extra, 0)


def _gather(cnt, cntv, rank_rows, aff_rows, h1b, B, lp, slots, ch, win):
    nchunk = lp // LANES
    tt = ch * LANES
    nt = nchunk // ch
    assert slots % SLOT_ALIGN == 0 and win % SLOT_ALIGN == 0 and win <= slots
    per_seq = lambda bi, c, cnt: (bi, 0, 0, 0)
    return pl.pallas_call(
        functools.partial(_gather_kernel, slots=slots, ch=ch, win=win, nchunk=nchunk),
        grid_spec=pltpu.PrefetchScalarGridSpec(
            num_scalar_prefetch=1,
            grid=(B, nt),
            in_specs=[
                pl.BlockSpec((None, N_EXPERTS, LANES), lambda bi, c, cnt: (bi, 0, 0)),
                pl.BlockSpec((N_EXPERTS, tt), lambda bi, c, cnt: (0, bi * nt + c)),
                pl.BlockSpec((N_EXPERTS, tt), lambda bi, c, cnt: (0, bi * nt + c)),
                pl.BlockSpec((tt, D_MODEL), lambda bi, c, cnt: (bi * nt + c, 0)),
            ],
            out_specs=[pl.BlockSpec((None, N_EXPERTS, slots, D_MODEL), per_seq),
                       pl.BlockSpec((None, N_EXPERTS, slots, 1), per_seq)],
        ),
        out_shape=[jax.ShapeDtypeStruct((B, N_EXPERTS, slots, D_MODEL), BF16),
                   jax.ShapeDtypeStruct((B, N_EXPERTS, slots, 1), F32)],
        compiler_params=_params("parallel", "arbitrary"),
        name="moe_gather",
    )(cnt, cntv, rank_rows, aff_rows, h1b)


def _ffn_up_kernel(x_ref, wg_ref, wu_ref, o_ref, wgb_ref, wub_ref):
    @pl.when(pl.program_id(1) == 0)
    def _():
        wgb_ref[...] = wg_ref[...].astype(BF16)
        wub_ref[...] = wu_ref[...].astype(BF16)

    nseq, slots, width = x_ref.shape
    x = x_ref[...].reshape(nseq * slots, width)
    a = jnp.dot(x, wgb_ref[...], preferred_element_type=F32)
    u = jnp.dot(x, wub_ref[...], preferred_element_type=F32)
    o_ref[...] = (a * jax.nn.sigmoid(a) * u).astype(BF16).reshape(o_ref.shape)


def _ffn_down_kernel(x_ref, g_ref, wd_ref, o_ref, wdb_ref):
    @pl.when(pl.program_id(1) == 0)
    def _():
        wdb_ref[...] = wd_ref[...].astype(BF16)

    nseq, slots, width = x_ref.shape
    y = jnp.dot(x_ref[...].reshape(nseq * slots, width), wdb_ref[...], preferred_element_type=F32)
    y = y * g_ref[...].reshape(nseq * slots, 1)
    o_ref[...] = y.astype(BF16).reshape(o_ref.shape)


def _ffn(xg, gates, wg, wu, wd, layer):
    B, _, slots, _ = xg.shape
    n_up = _pick_tile(B, (2, 1))
    n_down = _pick_tile(B, (4, 2, 1))
    acts = lambda nseq, width: pl.BlockSpec((nseq, None, slots, width), lambda e, b: (b, e, 0, 0))
    weight = lambda rows, cols: pl.BlockSpec((None, None, rows, cols), lambda e, b: (layer, e, 0, 0))
    mid = pl.pallas_call(
        _ffn_up_kernel,
        grid=(N_EXPERTS, B // n_up),
        in_specs=[acts(n_up, D_MODEL), weight(D_MODEL, D_FF_EXPERT), weight(D_MODEL, D_FF_EXPERT)],
        out_specs=acts(n_up, D_FF_EXPERT),
        out_shape=jax.ShapeDtypeStruct((B, N_EXPERTS, slots, D_FF_EXPERT), BF16),
        scratch_shapes=[pltpu.VMEM((D_MODEL, D_FF_EXPERT), BF16), pltpu.VMEM((D_MODEL, D_FF_EXPERT), BF16)],
        compiler_params=_params("arbitrary", "arbitrary"),
        name="moe_ffn_up",
    )(xg, wg, wu)
    return pl.pallas_call(
        _ffn_down_kernel,
        grid=(N_EXPERTS, B // n_down),
        in_specs=[acts(n_down, D_FF_EXPERT), acts(n_down, 1), weight(D_FF_EXPERT, D_MODEL)],
        out_specs=acts(n_down, D_MODEL),
        out_shape=jax.ShapeDtypeStruct(xg.shape, BF16),
        scratch_shapes=[pltpu.VMEM((D_FF_EXPERT, D_MODEL), BF16)],
        compiler_params=_params("arbitrary", "arbitrary"),
        name="moe_ffn_down",
    )(mid, gates, wd)


def _combine_kernel(cnt_ref, cntc_ref, rank_ref, y_ref, h_ref, g_ref, b_ref, expand_ref, rpat_ref,
                    o_ref, acc_ref, *, slots, ch, win, nchunk, alpha):
    bi, c = pl.program_id(0), pl.program_id(1)
    k0 = c * ch
    tt = ch * LANES
    max_start = slots - win
    rank = rank_ref[...]
    lo_row = cntc_ref[pl.ds(k0, 1), :]
    w_row = jnp.minimum(lo_row & -SLOT_ALIGN, max_start)
    rel = rank - w_row
    in_win = (rank >= 0) & (rel >= 0) & (rel < win)
    relb = jnp.where(in_win, rel, -1).astype(F32).astype(BF16)
    rel_wide = jnp.dot(relb, expand_ref[...], preferred_element_type=F32)
    onehot = (rel_wide == rpat_ref[...]).astype(BF16)

    starts, windows = [], []
    for e in range(N_EXPERTS):
        lo = cnt_ref[(bi * N_EXPERTS + e) * (nchunk + 1) + k0]
        w = jnp.minimum(lo & -SLOT_ALIGN, max_start)
        starts.append(w)
        windows.append(y_ref[e, pl.ds(pl.multiple_of(w, SLOT_ALIGN), win), :])
    acc_ref[...] = jnp.dot(onehot, jnp.concatenate(windows, axis=0), preferred_element_type=F32)

    lane = lax.broadcasted_iota(jnp.int32, rank.shape, 1)
    col = lax.broadcasted_iota(jnp.int32, (tt, win), 1)
    for e in range(N_EXPERTS):
        hi = cnt_ref[(bi * N_EXPERTS + e) * (nchunk + 1) + k0 + ch]

        @pl.when(hi > starts[e] + win)
        def _(e=e, hi=hi):
            rk = jnp.sum(jnp.where(lane == e, rank, 0), axis=-1, keepdims=True)

            def extra(i, carry):
                w2 = starts[e] + (i + 1) * win
                w2c = jnp.minimum(w2, max_start)
                rows = y_ref[e, pl.ds(pl.multiple_of(w2c, SLOT_ALIGN), win), :]
                hit = ((rk - w2c) == col) & (rk >= w2)
                acc_ref[...] += jnp.dot(hit.astype(BF16), rows, preferred_element_type=F32)
                return carry

            lax.fori_loop(0, (hi - starts[e] - 1) // win, extra, 0)

    o_ref[...] = _ln(alpha * h_ref[...] + acc_ref[...], g_ref[...], b_ref[...])


def _combine(cnt, cntc, rank_c, y, h1, g, b, B, lp, alpha, ch, win):
    n = B * lp
    slots = y.shape[2]
    nchunk = lp // LANES
    tt = ch * LANES
    nt = nchunk // ch
    assert slots % SLOT_ALIGN == 0 and win % SLOT_ALIGN == 0 and win <= slots
    wide = N_EXPERTS * win
    expand, pos = _window_constants(win)
    tok = lambda bi, c, cnt: (bi * nt + c, 0)
    const = lambda bi, c, cnt: (0, 0)
    return pl.pallas_call(
        functools.partial(_combine_kernel, slots=slots, ch=ch, win=win, nchunk=nchunk, alpha=alpha),
        grid_spec=pltpu.PrefetchScalarGridSpec(
            num_scalar_prefetch=1,
            grid=(B, nt),
            in_specs=[
                pl.BlockSpec((None, cntc.shape[1], N_EXPERTS), lambda bi, c, cnt: (bi, 0, 0)),
                pl.BlockSpec((tt, N_EXPERTS), tok),
                pl.BlockSpec((None, N_EXPERTS, slots, D_MODEL), lambda bi, c, cnt: (bi, 0, 0, 0)),
                pl.BlockSpec((tt, D_MODEL), tok),
                pl.BlockSpec((1, D_MODEL), const),
                pl.BlockSpec((1, D_MODEL), const),
                pl.BlockSpec((N_EXPERTS, wide), const),
                pl.BlockSpec((1, wide), const),
            ],
            out_specs=pl.BlockSpec((tt, D_MODEL), tok),
            scratch_shapes=[pltpu.VMEM((tt, D_MODEL), F32)],
        ),
        out_shape=jax.ShapeDtypeStruct((n, D_MODEL), F32),
        compiler_params=_params("parallel", "arbitrary"),
        name="moe_combine_ln",
    )(cnt, cntc, rank_c, y, h1, g.reshape(1, -1), b.reshape(1, -1),
      jnp.asarray(expand, dtype=BF16), jnp.asarray(pos.reshape(1, wide)))


def kernel(x, meta, ln0_g, ln0_b, w_in, b_gate, sink, w_attn_o, w_four_o, w_out, ln1_g, ln1_b,
           w_router, w_e_gate, w_e_up, w_e_down, ln2_g, ln2_b):
    B, seq, d = x.shape
    depth = w_in.shape[0]
    assert d == D_MODEL and meta.shape == (N_META, D_MODEL)
    assert seq % BLOCK == 0 and N_META % SUBLANES == 0 and N_META <= BLOCK
    L = seq + N_META
    nb = -(-L // BLOCK)
    lp = nb * BLOCK
    lead = lp - L
    cap = CAPACITY_FACTOR * L // N_EXPERTS
    slots = -(-cap // SLOT_ALIGN) * SLOT_ALIGN
    alpha = float((2 * depth) ** 0.25)
    n = B * lp
    nchunk = lp // LANES
    tm = _pick_tile(n, (1024, 512, 256, 128))
    tf = _pick_tile(lp, (384, 128))
    ch = _pick_tile(nchunk, (3, 2, 1))
    win = min(slots, -(-(ch * LANES * cap // L * 4 // 3 + SLOT_ALIGN) // SLOT_ALIGN) * SLOT_ALIGN)

    gi = np.arange(FOURIER_GROUP)
    ang = 2.0 * np.pi * ((gi[:, None] * gi[None, :]) % FOURIER_GROUP) / FOURIER_GROUP
    cs = jnp.asarray(np.concatenate([np.cos(ang), np.sin(ang)], axis=1) * FOURIER_GROUP ** -0.5, dtype=BF16)
    dft_rows = _fourier_steps(lp, lead, L, tf) * tf
    wdft = _dft_matrix(dft_rows, dft_rows, lead, L)
    bias, variant_of_block = _attn_bias_tables(nb, lead)

    h = _embed(x, meta, ln0_g, ln0_b, nb, lead).reshape(n, D_MODEL)
    for l in range(depth):
        q, kv, pq, gates = _inproj(h, w_in[l].astype(BF16), b_gate[l], cs, tm)
        attn = _attention(q.reshape(B, lp, -1), kv.reshape(B, lp, -1), sink[l], bias, variant_of_block, nb)
        four = _fourier(wdft, pq.reshape(B, lp, -1), lead, L, tf)
        h1, h1b, aff_r = _outproj(
            attn.reshape(n, -1), four.reshape(n, -1), gates, h,
            w_attn_o[l].astype(BF16), w_four_o[l].astype(BF16), w_out[l].astype(BF16),
            ln1_g[l], ln1_b[l], w_router[l], alpha, tm)
        rank_r, rank_c, cntv, cntc = _topk(aff_r, B, lp, lead, cap)
        cnt = cntv[:, :, :nchunk + 1].reshape(-1)
        xg, gsel = _gather(cnt, cntv, rank_r, aff_r, h1b, B, lp, slots, ch, win)
        y = _ffn(xg, gsel, w_e_gate, w_e_up, w_e_down, l)
        h = _combine(cnt, cntc, rank_c, y, h1, ln2_g[l], ln2_b[l], B, lp, alpha, ch, win)
    return h.reshape(B, lp, D_MODEL)[:, lead + N_META:]
```

```python
import functools
import math

import numpy as np
import jax
import jax.numpy as jnp
from jax import lax
from jax.experimental import pallas as pl
from jax.experimental.pallas import tpu as pltpu

D_MODEL = 1024
N_META = 16
N_HEADS = 8
N_KV_HEADS = 2
HEAD_DIM = 64
GQA_GROUP = N_HEADS // N_KV_HEADS
ATTN_WIDTH = N_HEADS * HEAD_DIM
KV_WIDTH = N_KV_HEADS * HEAD_DIM
WINDOW = 128
BLOCK = 128
N_FOURIER_GROUPS = 4
FOURIER_GROUP = 128
FOURIER_WIDTH = N_FOURIER_GROUPS * FOURIER_GROUP
N_BRANCHES = 2
GATE_WIDTH = N_BRANCHES * D_MODEL
N_EXPERTS = 16
CAPACITY_FACTOR = 2
D_FF_EXPERT = 1536
LN_EPS = 1e-5
NEG_INF = -1e30
Q_END = ATTN_WIDTH
K_END = Q_END + KV_WIDTH
V_END = K_END + KV_WIDTH
F_END = V_END + FOURIER_WIDTH
IN_WIDTH = F_END + GATE_WIDTH

LANES = 128
SUBLANES = 8
VMEM_LIMIT_BYTES = 56 * 1024 * 1024

F32 = jnp.float32
BF16 = jnp.bfloat16
NT_DIMS = (((1,), (1,)), ((), ()))


def _pick_tile(n, candidates):
    for c in candidates:
        if n % c == 0:
            return c
    raise ValueError(f"no tile in {candidates} divides {n}")


def _params(*sem):
    return pltpu.CompilerParams(dimension_semantics=sem, vmem_limit_bytes=VMEM_LIMIT_BYTES)


def _ln(x, g, b):
    mu = jnp.mean(x, axis=-1, keepdims=True)
    xc = x - mu
    var = jnp.mean(xc * xc, axis=-1, keepdims=True)
    return xc * lax.rsqrt(var + LN_EPS) * g + b


def _embed_kernel(*refs, lead, group):
    x_refs, (meta_ref, g_ref, b_ref, o_ref) = refs[:group], refs[group:]
    for k in range(1, group):
        o_ref[k * BLOCK:(k + 1) * BLOCK, :] = _ln(x_refs[k][...], g_ref[...], b_ref[...])

    @pl.when(pl.program_id(1) == 0)
    def _():
        o_ref[0:lead, :] = jnp.zeros((lead, D_MODEL), F32)
        o_ref[lead:BLOCK, :] = _ln(meta_ref[...], g_ref[...], b_ref[...])

    @pl.when(pl.program_id(1) > 0)
    def _():
        o_ref[0:BLOCK, :] = _ln(x_refs[0][...], g_ref[...], b_ref[...])


def _embed(x, meta, g, b, nb, lead):
    B = x.shape[0]
    group = _pick_tile(nb, (11, 3, 1))
    x_spec = lambda k: pl.BlockSpec((None, BLOCK, D_MODEL),
                                    lambda bi, j: (bi, jnp.maximum(group * j + k - 1, 0), 0))
    const = lambda bi, j: (0, 0)
    return pl.pallas_call(
        functools.partial(_embed_kernel, lead=lead, group=group),
        grid=(B, nb // group),
        in_specs=[x_spec(k) for k in range(group)] + [
            pl.BlockSpec((N_META, D_MODEL), const),
            pl.BlockSpec((1, D_MODEL), const),
            pl.BlockSpec((1, D_MODEL), const),
        ],
        out_specs=pl.BlockSpec((None, group * BLOCK, D_MODEL), lambda bi, j: (bi, j, 0)),
        out_shape=jax.ShapeDtypeStruct((B, nb * BLOCK, D_MODEL), F32),
        compiler_params=_params("parallel", "arbitrary"),
        name="embed_ln",
    )(*([x] * group), meta, g.reshape(1, -1), b.reshape(1, -1))


def _inproj_kernel(h_ref, w_ref, bg_ref, cs_ref, q_ref, kv_ref, pq_ref, gate_ref):
    hb = h_ref[...].astype(BF16)
    q_ref[...] = jnp.dot(hb, w_ref[:, 0:Q_END], preferred_element_type=F32).astype(BF16)
    kv_ref[...] = jnp.dot(hb, w_ref[:, Q_END:V_END], preferred_element_type=F32).astype(BF16)
    uf = jnp.dot(hb, w_ref[:, V_END:F_END], preferred_element_type=F32).astype(BF16)
    for g in range(N_FOURIER_GROUPS):
        lo = g * FOURIER_GROUP
        pq = jnp.dot(uf[:, lo:lo + FOURIER_GROUP], cs_ref[...], preferred_element_type=F32)
        pq_ref[:, lo:lo + FOURIER_GROUP] = pq[:, 0:FOURIER_GROUP].astype(BF16)
        pq_ref[:, FOURIER_WIDTH + lo:FOURIER_WIDTH + lo + FOURIER_GROUP] = (
            pq[:, FOURIER_GROUP:2 * FOURIER_GROUP].astype(BF16))
    chunk = FOURIER_WIDTH
    for c in range(GATE_WIDTH // chunk):
        lo = c * chunk
        ug = jnp.dot(hb, w_ref[:, F_END + lo:F_END + lo + chunk], preferred_element_type=F32)
        gate_ref[:, lo:lo + chunk] = jax.nn.sigmoid(ug + bg_ref[:, lo:lo + chunk])


def _inproj(h, w_bf16, b_gate, cs, tm):
    n = h.shape[0]
    row = lambda i: (i, 0)
    const = lambda i: (0, 0)
    return pl.pallas_call(
        _inproj_kernel,
        grid=(n // tm,),
        in_specs=[
            pl.BlockSpec((tm, D_MODEL), row),
            pl.BlockSpec((D_MODEL, IN_WIDTH), const),
            pl.BlockSpec((1, GATE_WIDTH), const),
            pl.BlockSpec((FOURIER_GROUP, 2 * FOURIER_GROUP), const),
        ],
        out_specs=[
            pl.BlockSpec((tm, ATTN_WIDTH), row),
            pl.BlockSpec((tm, 2 * KV_WIDTH), row),
            pl.BlockSpec((tm, 2 * FOURIER_WIDTH), row),
            pl.BlockSpec((tm, GATE_WIDTH), row),
        ],
        out_shape=[
            jax.ShapeDtypeStruct((n, ATTN_WIDTH), BF16),
            jax.ShapeDtypeStruct((n, 2 * KV_WIDTH), BF16),
            jax.ShapeDtypeStruct((n, 2 * FOURIER_WIDTH), BF16),
            jax.ShapeDtypeStruct((n, GATE_WIDTH), F32),
        ],
        compiler_params=_params("parallel"),
        name="inproj",
    )(h, w_bf16, b_gate.reshape(1, GATE_WIDTH), cs)


SCORES_AHEAD = 5


def _attn_kernel(sink_ref, q_ref, *refs, group, variant):
    kv_refs, (bias_ref, o_ref) = refs[:group + 2], refs[group + 2:]
    first_block = pl.program_id(1) * group
    kv = jnp.concatenate([r[...] for r in kv_refs], axis=0)
    low_half = lax.broadcasted_iota(jnp.int32, (kv.shape[0], LANES), 1) < HEAD_DIM

    def lane_half_operands(x):
        swapped = jnp.concatenate([x[:, HEAD_DIM:], x[:, :HEAD_DIM]], axis=1)
        zero = jnp.zeros_like(x)
        return {(kvh, half): jnp.where(low_half if half == 0 else ~low_half,
                                       x if kvh == half else swapped, zero)
                for kvh in range(N_KV_HEADS) for half in range(2)}

    k_ops = lane_half_operands(kv[:, 0:KV_WIDTH])
    v_ops = lane_half_operands(kv[:, KV_WIDTH:2 * KV_WIDTH])
    keys = lambda t: slice(t * BLOCK, (t + 3) * BLOCK)

    def scores(t, h):
        pair = h // 2
        qp = q_ref[t * BLOCK:(t + 1) * BLOCK, pair * LANES:(pair + 1) * LANES]
        return lax.dot_general(qp, k_ops[(h // GQA_GROUP, h % 2)][keys(t)], NT_DIMS,
                               preferred_element_type=F32)

    def head_out(t, h, s):
        sink = sink_ref[h]
        logits = s * (HEAD_DIM ** -0.5) + bias_ref[variant(first_block + t), h]
        m = jnp.maximum(jnp.max(logits, axis=-1, keepdims=True), sink)
        p = jnp.exp(logits - m)
        denom = jnp.sum(p, axis=-1, keepdims=True) + jnp.exp(sink - m)
        o = jnp.dot(p.astype(BF16), v_ops[(h // GQA_GROUP, h % 2)][keys(t)], preferred_element_type=F32)
        return o / denom

    work = [(t, h) for t in range(group) for h in range(N_HEADS)]
    ahead = SCORES_AHEAD
    pending = [scores(*w) for w in work[:ahead]]
    for n, (t, h) in enumerate(work):
        s_cur = pending.pop(0)
        if n + ahead < len(work):
            pending.append(scores(*work[n + ahead]))
        o = head_out(t, h, s_cur)
        if h % 2 == 0:
            o_even = o
        else:
            pair = h // 2
            o_ref[t * BLOCK:(t + 1) * BLOCK, pair * LANES:(pair + 1) * LANES] = (o_even + o).astype(BF16)


def _attn_bias_tables(nb, lead):
    qi = np.arange(BLOCK)[:, None]
    si = np.arange(3 * BLOCK)[None, :]
    rel = np.abs(si - BLOCK - qi).astype(np.float32)
    slopes = np.array([2.0 ** (-8.0 * (h + 1) / N_HEADS) for h in range(N_HEADS)], np.float32)
    base = np.where(rel[None] <= WINDOW, -slopes[:, None, None] * rel[None], np.float32(NEG_INF))
    variants, keys, variant_of_block = [], [], []
    for i in range(nb):
        kpos = (i - 1) * BLOCK + np.arange(3 * BLOCK)
        valid = (kpos >= lead) & (kpos < nb * BLOCK)
        key = valid.tobytes()
        if key not in keys:
            keys.append(key)
            variants.append(np.where(valid[None, None, :], base, np.float32(NEG_INF)))
        variant_of_block.append(keys.index(key))
    return jnp.asarray(np.stack(variants).astype(np.float32)), variant_of_block


def _attention(q, kv, sink, bias, variant_of_block, nb):
    B, lp, _ = q.shape
    group = _pick_tile(nb, (3, 1))
    interior = max(set(variant_of_block), key=variant_of_block.count)

    def variant(i):
        v = jnp.int32(interior)
        for blk, var in enumerate(variant_of_block):
            if var != interior:
                v = jnp.where(i == blk, var, v)
        return v

    kv_spec = lambda t: pl.BlockSpec((None, BLOCK, 2 * KV_WIDTH),
                                     lambda b, g: (b, jnp.clip(group * g + t - 1, 0, nb - 1), 0))
    rows = pl.BlockSpec((None, group * BLOCK, ATTN_WIDTH), lambda b, g: (b, g, 0))
    return pl.pallas_call(
        functools.partial(_attn_kernel, group=group, variant=variant),
        grid=(B, nb // group),
        in_specs=[pl.BlockSpec(memory_space=pltpu.SMEM), rows]
                 + [kv_spec(t) for t in range(group + 2)]
                 + [pl.BlockSpec(bias.shape, lambda b, g: (0, 0, 0, 0))],
        out_specs=rows,
        out_shape=jax.ShapeDtypeStruct((B, lp, ATTN_WIDTH), BF16),
        compiler_params=_params("parallel", "arbitrary"),
        name="attention",
    )(sink, q, *([kv] * (group + 2)), bias)


def _fourier_kernel(w_ref, p_ref, q_ref, o_ref, prev_ref, pe_ref, qo_ref, *, lp, lead, seq_len, tf):
    m = pl.program_id(1)
    last = pl.num_programs(1) - 1
    nblk = lp // tf
    shift = lead + 1
    kp = pe_ref.shape[0]
    half = seq_len // 2
    r = lax.broadcasted_iota(jnp.int32, (tf, tf), 0)
    c = lax.broadcasted_iota(jnp.int32, (tf, tf), 1)
    flip = (r + c == tf - 1).astype(BF16)

    @pl.when(m == 0)
    def _():
        row = lax.broadcasted_iota(jnp.int32, (lp, 1), 0)
        dc = jnp.sum(jnp.where(row >= lead, p_ref[...].astype(F32), 0.0), axis=0, keepdims=True)
        prev_ref[...] = jnp.zeros_like(prev_ref)
        prev_ref[tf - 1:tf, :] = dc * np.float32(seq_len ** -0.5)
        def mirror(x_ref, i):
            above = (jnp.dot(flip, x_ref[(nblk - i) * tf:(nblk - i + 1) * tf, :], preferred_element_type=F32)
                     if i > 0 else jnp.zeros((tf, FOURIER_WIDTH), F32))
            below = jnp.dot(flip, x_ref[(nblk - 1 - i) * tf:(nblk - i) * tf, :], preferred_element_type=F32)
            return jnp.concatenate([above[tf - shift:, :], below[:tf - shift, :]], axis=0)

        for i in range(kp // tf):
            rows = slice(i * tf, (i + 1) * tf)
            n = lax.broadcasted_iota(jnp.int32, (tf, 1), 0) + (i * tf - lead)
            paired = (n >= 1) & (n <= half - 1)
            alone = (n == 0) | (n == half)
            p_blk = p_ref[rows, :].astype(F32)
            pe_ref[rows, :] = jnp.where(paired, p_blk + mirror(p_ref, i),
                                        jnp.where(alone, p_blk, 0.0)).astype(BF16)
            qo_ref[rows, :] = jnp.where(paired, q_ref[rows, :].astype(F32) - mirror(q_ref, i), 0.0).astype(BF16)

    t1 = jnp.dot(w_ref[:, 0:kp], pe_ref[...], preferred_element_type=F32)
    t2 = jnp.dot(w_ref[:, kp:2 * kp], qo_ref[...], preferred_element_type=F32)
    fsub = w_ref.shape[0] // tf
    tail = prev_ref[tf - shift:, :]
    for t in range(fsub):
        blk = slice(t * tf, (t + 1) * tf)
        s = m * fsub + t
        direct = t1[blk] + t2[blk]
        mirrored = jnp.dot(flip, (t1[blk] - t2[blk]).astype(BF16), preferred_element_type=F32)
        shifted = jnp.concatenate([tail, direct[:tf - shift, :]], axis=0)
        tail = direct[tf - shift:, :]

        def store_pair(s=s, shifted=shifted, mirrored=mirrored):
            o_ref[pl.ds(pl.multiple_of(s * tf, tf), tf), :] = shifted.astype(BF16)
            o_ref[pl.ds(pl.multiple_of((nblk - 1 - s) * tf, tf), tf), :] = mirrored.astype(BF16)

        def store_middle(shifted=shifted, mirrored=mirrored):
            row = lax.broadcasted_iota(jnp.int32, (tf, 1), 0) + (nblk // 2) * tf
            mid = jnp.where(row <= seq_len // 2 + lead, shifted, mirrored)
            o_ref[(nblk // 2) * tf:(nblk // 2 + 1) * tf, :] = mid.astype(BF16)

        if t < fsub - 1:
            store_pair()
        else:
            pl.when(m < last)(store_pair)
            pl.when(m == last)(store_middle)
    prev_ref[tf - shift:, :] = tail


def _dft_matrix(nrows, ncols, lead, seq_len):
    w = np.float32(2.0 * math.pi / seq_len)
    k = (jnp.arange(nrows, dtype=jnp.int32) + 1)[:, None]
    j = jnp.arange(ncols // LANES, dtype=jnp.int32)[None, :]
    r = jnp.arange(LANES, dtype=jnp.int32)[None, :]
    a = ((k * (LANES * j - lead)) % seq_len).astype(F32) * w
    b = ((k * r) % seq_len).astype(F32) * w
    ca, sa, cb, sb = jnp.cos(a)[:, :, None], jnp.sin(a)[:, :, None], jnp.cos(b)[:, None, :], jnp.sin(b)[:, None, :]
    n = jnp.arange(ncols, dtype=jnp.int32) - lead
    valid = ((n >= 0) & (n <= seq_len // 2)).reshape(1, ncols // LANES, LANES)
    scale = np.float32(seq_len ** -0.5)
    wc = jnp.where(valid, (ca * cb - sa * sb) * scale, 0.0).reshape(nrows, ncols)
    ws = jnp.where(valid, -(sa * cb + ca * sb) * scale, 0.0).reshape(nrows, ncols)
    return jnp.concatenate([wc, ws], axis=1).astype(BF16)


def _fourier_steps(lp, lead, seq_len, tf):
    nblk = lp // tf
    assert seq_len % 2 == 0 and lp % tf == 0 and nblk % 2 == 1 and lead + 1 < tf
    steps = nblk // 2 + 1
    assert steps * tf > seq_len // 2 + lead
    return steps


def _fourier(w, pq, lead, seq_len, tf):
    B, lp, _ = pq.shape
    steps = _fourier_steps(lp, lead, seq_len, tf)
    kp = steps * tf
    fsub = _pick_tile(steps, (2, 1))
    return pl.pallas_call(
        functools.partial(_fourier_kernel, lp=lp, lead=lead, seq_len=seq_len, tf=tf),
        grid=(B, steps // fsub),
        in_specs=[
            pl.BlockSpec((fsub * tf, 2 * kp), lambda b, m: (m, 0)),
            pl.BlockSpec((None, lp, FOURIER_WIDTH), lambda b, m: (b, 0, 0)),
            pl.BlockSpec((None, lp, FOURIER_WIDTH), lambda b, m: (b, 0, 1)),
        ],
        out_specs=pl.BlockSpec((None, lp, FOURIER_WIDTH), lambda b, m: (b, 0, 0)),
        out_shape=jax.ShapeDtypeStruct((B, lp, FOURIER_WIDTH), BF16),
        scratch_shapes=[pltpu.VMEM((tf, FOURIER_WIDTH), F32),
                        pltpu.VMEM((kp, FOURIER_WIDTH), BF16),
                        pltpu.VMEM((kp, FOURIER_WIDTH), BF16)],
        compiler_params=_params("parallel", "arbitrary"),
        name="fourier",
    )(w, pq, pq)


def _split_bf16(x):
    hi = x.astype(BF16)
    lo = (x - hi.astype(F32)).astype(BF16)
    return hi, lo


OUTPROJ_SUB = 128


def _outproj_kernel(attn_ref, four_ref, gate_ref, h_ref, wa_ref, wf_ref, wo_ref, g_ref, b_ref,
                    wrth_ref, wrtl_ref, h1_ref, h1b_ref, affr_ref, *, alpha):
    tm = h_ref.shape[0]
    subs = [slice(i * OUTPROJ_SUB, (i + 1) * OUTPROJ_SUB) for i in range(tm // OUTPROJ_SUB)]
    ya = [jnp.dot(attn_ref[s, :], wa_ref[...], preferred_element_type=F32) for s in subs]
    yf = [jnp.dot(four_ref[s, :], wf_ref[...], preferred_element_type=F32) for s in subs]
    merged = [gate_ref[s, 0:D_MODEL] * a + gate_ref[s, D_MODEL:GATE_WIDTH] * f for s, a, f in zip(subs, ya, yf)]
    mix = [jnp.dot(mg.astype(BF16), wo_ref[...], preferred_element_type=F32) for mg in merged]
    for s, mx in zip(subs, mix):
        h1 = _ln(alpha * h_ref[s, :] + mx, g_ref[...], b_ref[...])
        h1_ref[s, :] = h1
        hi, lo = _split_bf16(h1)
        h1b_ref[s, :] = hi
        lr = (lax.dot_general(wrth_ref[...], hi, NT_DIMS, preferred_element_type=F32)
              + lax.dot_general(wrtl_ref[...], hi, NT_DIMS, preferred_element_type=F32)
              + lax.dot_general(wrth_ref[...], lo, NT_DIMS, preferred_element_type=F32))
        er = jnp.exp(lr - jnp.max(lr, axis=0, keepdims=True))
        affr_ref[:, s] = er / jnp.sum(er, axis=0, keepdims=True)


def _outproj(attn, four, gates, h, wa, wf, wo, g, b, w_router, alpha, tm):
    n = h.shape[0]
    row = lambda i: (i, 0)
    const = lambda i: (0, 0)
    wrth, wrtl = _split_bf16(w_router.T)
    return pl.pallas_call(
        functools.partial(_outproj_kernel, alpha=alpha),
        grid=(n // tm,),
        in_specs=[
            pl.BlockSpec((tm, ATTN_WIDTH), row),
            pl.BlockSpec((tm, FOURIER_WIDTH), row),
            pl.BlockSpec((tm, GATE_WIDTH), row),
            pl.BlockSpec((tm, D_MODEL), row),
            pl.BlockSpec((ATTN_WIDTH, D_MODEL), const),
            pl.BlockSpec((FOURIER_WIDTH, D_MODEL), const),
            pl.BlockSpec((D_MODEL, D_MODEL), const),
            pl.BlockSpec((1, D_MODEL), const),
            pl.BlockSpec((1, D_MODEL), const),
            pl.BlockSpec((N_EXPERTS, D_MODEL), const),
            pl.BlockSpec((N_EXPERTS, D_MODEL), const),
        ],
        out_specs=[
            pl.BlockSpec((tm, D_MODEL), row),
            pl.BlockSpec((tm, D_MODEL), row),
            pl.BlockSpec((N_EXPERTS, tm), lambda i: (0, i)),
        ],
        out_shape=[
            jax.ShapeDtypeStruct((n, D_MODEL), F32),
            jax.ShapeDtypeStruct((n, D_MODEL), BF16),
            jax.ShapeDtypeStruct((N_EXPERTS, n), F32),
        ],
        compiler_params=_params("parallel"),
        name="outproj_ln_router",
    )(attn, four, gates, h, wa, wf, wo, g.reshape(1, -1), b.reshape(1, -1), wrth, wrtl)


def _topk_kernel(aff_ref, rr_ref, rc_ref, cnt_ref, cntc_ref, *, lead, lp, cap):
    nchunk = lp // LANES
    lane = lax.broadcasted_iota(jnp.int32, (N_EXPERTS, lp), 1)
    bits = jnp.where(lane >= lead, pltpu.bitcast(aff_ref[...], jnp.int32), -1)

    def search(i, t):
        cand = t | (jnp.int32(1) << (30 - i))
        cnt = jnp.sum((bits >= cand).astype(jnp.int32), axis=-1, keepdims=True)
        return jnp.where(cnt >= cap, cand, t)

    thr = lax.fori_loop(0, 31, search, jnp.zeros((N_EXPERTS, 1), jnp.int32))
    gt = bits > thr
    eq = bits == thr
    need = cap - jnp.sum(gt.astype(jnp.int32), axis=-1, keepdims=True)

    r = lax.broadcasted_iota(jnp.int32, (LANES, LANES), 0)
    c = lax.broadcasted_iota(jnp.int32, (LANES, LANES), 1)
    upper = (r <= c).astype(BF16)
    lower = (c <= r).astype(BF16)
    ident = (c == r).astype(BF16)

    chunks = [slice(k * LANES, (k + 1) * LANES) for k in range(nchunk)]
    eqb = eq.astype(BF16)
    needf = need.astype(F32)
    pre_eq = [jnp.dot(eqb[:, sl], upper, preferred_element_type=F32) for sl in chunks]
    off = jnp.zeros((N_EXPERTS, 1), F32)
    sel_chunks = []
    for k, sl in enumerate(chunks):
        sel_chunks.append(gt[:, sl] | (eq[:, sl] & (pre_eq[k] + off <= needf)))
        off = off + pre_eq[k][:, LANES - 1:LANES]

    selb = [sel.astype(BF16) for sel in sel_chunks]
    pre_r = [jnp.dot(s, upper, preferred_element_type=F32) for s in selb]
    pre_c = [lax.dot_general(lower, s, NT_DIMS, preferred_element_type=F32) for s in selb]
    sel_c = [lax.dot_general(ident, s, NT_DIMS, preferred_element_type=F32) for s in selb]
    off_r = jnp.zeros((N_EXPERTS, 1), F32)
    off_c = jnp.zeros((1, N_EXPERTS), F32)
    cnt_lane = lax.broadcasted_iota(jnp.int32, (N_EXPERTS, LANES), 1)
    cnt = jnp.zeros((N_EXPERTS, LANES), F32)
    for k, sl in enumerate(chunks):
        rr_ref[:, sl] = jnp.where(sel_chunks[k], pre_r[k] + (off_r - 1.0), -1.0).astype(jnp.int32)
        off_r = off_r + pre_r[k][:, LANES - 1:LANES]
        cnt = jnp.where(cnt_lane == k + 1, off_r, cnt)
        rc_ref[sl, :] = jnp.where(sel_c[k] > 0.5, pre_c[k] + (off_c - 1.0), -1.0).astype(jnp.int32)
        off_c = off_c + pre_c[k][LANES - 1:LANES, :]
        cntc_ref[k + 1:k + 2, :] = off_c.astype(jnp.int32)
    cnt_ref[...] = cnt.astype(jnp.int32)
    cntc_ref[0:1, :] = jnp.zeros((1, N_EXPERTS), jnp.int32)
    pad_rows = cntc_ref.shape[0] - nchunk - 1
    if pad_rows:
        cntc_ref[nchunk + 1:, :] = jnp.zeros((pad_rows, N_EXPERTS), jnp.int32)


def _topk(aff_r, B, lp, lead, cap):
    n = B * lp
    nchunk = lp // LANES
    assert nchunk + 1 <= LANES
    nb1 = -(-(nchunk + 1) // SUBLANES) * SUBLANES
    return pl.pallas_call(
        functools.partial(_topk_kernel, lead=lead, lp=lp, cap=cap),
        grid=(B,),
        in_specs=[pl.BlockSpec((N_EXPERTS, lp), lambda b: (0, b))],
        out_specs=[
            pl.BlockSpec((N_EXPERTS, lp), lambda b: (0, b)),
            pl.BlockSpec((lp, N_EXPERTS), lambda b: (b, 0)),
            pl.BlockSpec((None, N_EXPERTS, LANES), lambda b: (b, 0, 0)),
            pl.BlockSpec((None, nb1, N_EXPERTS), lambda b: (b, 0, 0)),
        ],
        out_shape=[
            jax.ShapeDtypeStruct((N_EXPERTS, n), jnp.int32),
            jax.ShapeDtypeStruct((n, N_EXPERTS), jnp.int32),
            jax.ShapeDtypeStruct((B, N_EXPERTS, LANES), jnp.int32),
            jax.ShapeDtypeStruct((B, nb1, N_EXPERTS), jnp.int32),
        ],
        compiler_params=_params("parallel"),
        name="topk_select",
    )(aff_r)


SLOT_ALIGN = 16


def _window_constants(win):
    wide = N_EXPERTS * win
    expand = np.zeros((N_EXPERTS, wide), np.float32)
    for e in range(N_EXPERTS):
        expand[e, e * win:(e + 1) * win] = 1.0
    return expand, (np.arange(wide) % win).astype(np.float32)


def _gather_kernel(cnt_ref, cntv_ref, rank_ref, aff_ref, h_ref, o_ref, g_ref,
                   *, slots, ch, win, nchunk):
    bi, c = pl.program_id(0), pl.program_id(1)
    k0 = c * ch
    tt = ch * LANES
    max_start = slots - win

    @pl.when(c == 0)
    def _():
        o_ref[...] = jnp.zeros_like(o_ref)
        g_ref[...] = jnp.zeros_like(g_ref)

    lane = lax.broadcasted_iota(jnp.int32, cntv_ref.shape, 1)
    lo_col = jnp.sum(jnp.where(lane == k0, cntv_ref[...], 0), axis=-1, keepdims=True)
    w_col = jnp.minimum(lo_col & -SLOT_ALIGN, max_start)
    rank = rank_ref[...]
    aff = aff_ref[...]
    rel = rank - w_col
    in_win = (rank >= 0) & (rel >= 0) & (rel < win)
    rel = jnp.where(in_win, rel, -1)
    row = lax.broadcasted_iota(jnp.int32, (win, tt), 0)
    hit_all = jnp.concatenate([rel[e:e + 1, :] == row for e in range(N_EXPERTS)], axis=0)
    rows = h_ref[...]
    xw = jnp.dot(hit_all.astype(BF16), rows, preferred_element_type=F32)

    def add_rows(e, start, hit, vals):
        dst = pl.ds(pl.multiple_of(start, SLOT_ALIGN), win)
        o_ref[e, dst, :] = o_ref[e, dst, :] + vals.astype(BF16)
        g_ref[e, dst, :] += jnp.sum(jnp.where(hit, aff[e:e + 1, :], 0.0), axis=-1, keepdims=True)

    for e in range(N_EXPERTS):
        base = (bi * N_EXPERTS + e) * (nchunk + 1) + k0
        w = jnp.minimum(cnt_ref[base] & -SLOT_ALIGN, max_start)
        add_rows(e, w, hit_all[e * win:(e + 1) * win, :], xw[e * win:(e + 1) * win, :])
        hi = cnt_ref[base + ch]

        @pl.when(hi > w + win)
        def _(e=e, w=w, hi=hi):
            rk = rank[e:e + 1, :]

            def extra(i, carry):
                w2 = w + (i + 1) * win
                w2c = jnp.minimum(w2, max_start)
                hit = ((rk - w2c) == row) & (rk >= w2)
                add_rows(e, w2c, hit, jnp.dot(hit.astype(BF16), rows, preferred_element_type=F32))
                return carry

            lax.fori_loop(0, (hi - w - 1) // win, extra, 0)


def _gather(cnt, cntv, rank_rows, aff_rows, h1b, B, lp, slots, ch, win):
    nchunk = lp // LANES
    tt = ch * LANES
    nt = nchunk // ch
    assert slots % SLOT_ALIGN == 0 and win % SLOT_ALIGN == 0 and win <= slots
    per_seq = lambda bi, c, cnt: (bi, 0, 0, 0)
    return pl.pallas_call(
        functools.partial(_gather_kernel, slots=slots, ch=ch, win=win, nchunk=nchunk),
        grid_spec=pltpu.PrefetchScalarGridSpec(
            num_scalar_prefetch=1,
            grid=(B, nt),
            in_specs=[
                pl.BlockSpec((None, N_EXPERTS, LANES), lambda bi, c, cnt: (bi, 0, 0)),
                pl.BlockSpec((N_EXPERTS, tt), lambda bi, c, cnt: (0, bi * nt + c)),
                pl.BlockSpec((N_EXPERTS, tt), lambda bi, c, cnt: (0, bi * nt + c)),
                pl.BlockSpec((tt, D_MODEL), lambda bi, c, cnt: (bi * nt + c, 0)),
            ],
            out_specs=[pl.BlockSpec((None, N_EXPERTS, slots, D_MODEL), per_seq),
                       pl.BlockSpec((None, N_EXPERTS, slots, 1), per_seq)],
        ),
        out_shape=[jax.ShapeDtypeStruct((B, N_EXPERTS, slots, D_MODEL), BF16),
                   jax.ShapeDtypeStruct((B, N_EXPERTS, slots, 1), F32)],
        compiler_params=_params("parallel", "arbitrary"),
        name="moe_gather",
    )(cnt, cntv, rank_rows, aff_rows, h1b)


def _ffn_up_kernel(x_ref, wg_ref, wu_ref, o_ref, wgb_ref, wub_ref):
    @pl.when(pl.program_id(1) == 0)
    def _():
        wgb_ref[...] = wg_ref[...].astype(BF16)
        wub_ref[...] = wu_ref[...].astype(BF16)

    nseq, slots, width = x_ref.shape
    x = x_ref[...].reshape(nseq * slots, width)
    a = jnp.dot(x, wgb_ref[...], preferred_element_type=F32)
    u = jnp.dot(x, wub_ref[...], preferred_element_type=F32)
    o_ref[...] = (a * jax.nn.sigmoid(a) * u).astype(BF16).reshape(o_ref.shape)


def _ffn_down_kernel(x_ref, g_ref, wd_ref, o_ref, wdb_ref):
    @pl.when(pl.program_id(1) == 0)
    def _():
        wdb_ref[...] = wd_ref[...].astype(BF16)

    nseq, slots, width = x_ref.shape
    y = jnp.dot(x_ref[...].reshape(nseq * slots, width), wdb_ref[...], preferred_element_type=F32)
    y = y * g_ref[...].reshape(nseq * slots, 1)
    o_ref[...] = y.astype(BF16).reshape(o_ref.shape)


def _ffn(xg, gates, wg, wu, wd, layer):
    B, _, slots, _ = xg.shape
    n_up = _pick_tile(B, (2, 1))
    n_down = _pick_tile(B, (4, 2, 1))
    acts = lambda nseq, width: pl.BlockSpec((nseq, None, slots, width), lambda e, b: (b, e, 0, 0))
    weight = lambda rows, cols: pl.BlockSpec((None, None, rows, cols), lambda e, b: (layer, e, 0, 0))
    mid = pl.pallas_call(
        _ffn_up_kernel,
        grid=(N_EXPERTS, B // n_up),
        in_specs=[acts(n_up, D_MODEL), weight(D_MODEL, D_FF_EXPERT), weight(D_MODEL, D_FF_EXPERT)],
        out_specs=acts(n_up, D_FF_EXPERT),
        out_shape=jax.ShapeDtypeStruct((B, N_EXPERTS, slots, D_FF_EXPERT), BF16),
        scratch_shapes=[pltpu.VMEM((D_MODEL, D_FF_EXPERT), BF16), pltpu.VMEM((D_MODEL, D_FF_EXPERT), BF16)],
        compiler_params=_params("arbitrary", "arbitrary"),
        name="moe_ffn_up",
    )(xg, wg, wu)
    return pl.pallas_call(
        _ffn_down_kernel,
        grid=(N_EXPERTS, B // n_down),
        in_specs=[acts(n_down, D_FF_EXPERT), acts(n_down, 1), weight(D_FF_EXPERT, D_MODEL)],
        out_specs=acts(n_down, D_MODEL),
        out_shape=jax.ShapeDtypeStruct(xg.shape, BF16),
        scratch_shapes=[pltpu.VMEM((D_FF_EXPERT, D_MODEL), BF16)],
        compiler_params=_params("arbitrary", "arbitrary"),
        name="moe_ffn_down",
    )(mid, gates, wd)


def _combine_kernel(cnt_ref, cntc_ref, rank_ref, y_ref, h_ref, g_ref, b_ref, expand_ref, rpat_ref,
                    o_ref, acc_ref, *, slots, ch, win, nchunk, alpha):
    bi, c = pl.program_id(0), pl.program_id(1)
    k0 = c * ch
    tt = ch * LANES
    max_start = slots - win
    rank = rank_ref[...]
    lo_row = cntc_ref[pl.ds(k0, 1), :]
    w_row = jnp.minimum(lo_row & -SLOT_ALIGN, max_start)
    rel = rank - w_row
    in_win = (rank >= 0) & (rel >= 0) & (rel < win)
    relb = jnp.where(in_win, rel, -1).astype(F32).astype(BF16)
    rel_wide = jnp.dot(relb, expand_ref[...], preferred_element_type=F32)
    onehot = (rel_wide == rpat_ref[...]).astype(BF16)

    starts, windows = [], []
    for e in range(N_EXPERTS):
        lo = cnt_ref[(bi * N_EXPERTS + e) * (nchunk + 1) + k0]
        w = jnp.minimum(lo & -SLOT_ALIGN, max_start)
        starts.append(w)
        windows.append(y_ref[e, pl.ds(pl.multiple_of(w, SLOT_ALIGN), win), :])
    acc_ref[...] = jnp.dot(onehot, jnp.concatenate(windows, axis=0), preferred_element_type=F32)

    lane = lax.broadcasted_iota(jnp.int32, rank.shape, 1)
    col = lax.broadcasted_iota(jnp.int32, (tt, win), 1)
    for e in range(N_EXPERTS):
        hi = cnt_ref[(bi * N_EXPERTS + e) * (nchunk + 1) + k0 + ch]

        @pl.when(hi > starts[e] + win)
        def _(e=e, hi=hi):
            rk = jnp.sum(jnp.where(lane == e, rank, 0), axis=-1, keepdims=True)

            def extra(i, carry):
                w2 = starts[e] + (i + 1) * win
                w2c = jnp.minimum(w2, max_start)
                rows = y_ref[e, pl.ds(pl.multiple_of(w2c, SLOT_ALIGN), win), :]
                hit = ((rk - w2c) == col) & (rk >= w2)
                acc_ref[...] += jnp.dot(hit.astype(BF16), rows, preferred_element_type=F32)
                return carry

            lax.fori_loop(0, (hi - starts[e] - 1) // win, extra, 0)

    o_ref[...] = _ln(alpha * h_ref[...] + acc_ref[...], g_ref[...], b_ref[...])


def _combine(cnt, cntc, rank_c, y, h1, g, b, B, lp, alpha, ch, win):
    n = B * lp
    slots = y.shape[2]
    nchunk = lp // LANES
    tt = ch * LANES
    nt = nchunk // ch
    assert slots % SLOT_ALIGN == 0 and win % SLOT_ALIGN == 0 and win <= slots
    wide = N_EXPERTS * win
    expand, pos = _window_constants(win)
    tok = lambda bi, c, cnt: (bi * nt + c, 0)
    const = lambda bi, c, cnt: (0, 0)
    return pl.pallas_call(
        functools.partial(_combine_kernel, slots=slots, ch=ch, win=win, nchunk=nchunk, alpha=alpha),
        grid_spec=pltpu.PrefetchScalarGridSpec(
            num_scalar_prefetch=1,
            grid=(B, nt),
            in_specs=[
                pl.BlockSpec((None, cntc.shape[1], N_EXPERTS), lambda bi, c, cnt: (bi, 0, 0)),
                pl.BlockSpec((tt, N_EXPERTS), tok),
                pl.BlockSpec((None, N_EXPERTS, slots, D_MODEL), lambda bi, c, cnt: (bi, 0, 0, 0)),
                pl.BlockSpec((tt, D_MODEL), tok),
                pl.BlockSpec((1, D_MODEL), const),
                pl.BlockSpec((1, D_MODEL), const),
                pl.BlockSpec((N_EXPERTS, wide), const),
                pl.BlockSpec((1, wide), const),
            ],
            out_specs=pl.BlockSpec((tt, D_MODEL), tok),
            scratch_shapes=[pltpu.VMEM((tt, D_MODEL), F32)],
        ),
        out_shape=jax.ShapeDtypeStruct((n, D_MODEL), F32),
        compiler_params=_params("parallel", "arbitrary"),
        name="moe_combine_ln",
    )(cnt, cntc, rank_c, y, h1, g.reshape(1, -1), b.reshape(1, -1),
      jnp.asarray(expand, dtype=BF16), jnp.asarray(pos.reshape(1, wide)))


def kernel(x, meta, ln0_g, ln0_b, w_in, b_gate, sink, w_attn_o, w_four_o, w_out, ln1_g, ln1_b,
           w_router, w_e_gate, w_e_up, w_e_down, ln2_g, ln2_b):
    B, seq, d = x.shape
    depth = w_in.shape[0]
    assert d == D_MODEL and meta.shape == (N_META, D_MODEL)
    assert seq % BLOCK == 0 and N_META % SUBLANES == 0 and N_META <= BLOCK
    L = seq + N_META
    nb = -(-L // BLOCK)
    lp = nb * BLOCK
    lead = lp - L
    cap = CAPACITY_FACTOR * L // N_EXPERTS
    slots = -(-cap // SLOT_ALIGN) * SLOT_ALIGN
    alpha = float((2 * depth) ** 0.25)
    n = B * lp
    nchunk = lp // LANES
    tm = _pick_tile(n, (1024, 512, 256, 128))
    tf = _pick_tile(lp, (384, 128))
    ch = _pick_tile(nchunk, (3, 2, 1))
    win = min(slots, -(-(ch * LANES * cap // L * 4 // 3 + SLOT_ALIGN) // SLOT_ALIGN) * SLOT_ALIGN)

    gi = np.arange(FOURIER_GROUP)
    ang = 2.0 * np.pi * ((gi[:, None] * gi[None, :]) % FOURIER_GROUP) / FOURIER_GROUP
    cs = jnp.asarray(np.concatenate([np.cos(ang), np.sin(ang)], axis=1) * FOURIER_GROUP ** -0.5, dtype=BF16)
    dft_rows = _fourier_steps(lp, lead, L, tf) * tf
    wdft = _dft_matrix(dft_rows, dft_rows, lead, L)
    bias, variant_of_block = _attn_bias_tables(nb, lead)

    h = _embed(x, meta, ln0_g, ln0_b, nb, lead).reshape(n, D_MODEL)
    for l in range(depth):
        q, kv, pq, gates = _inproj(h, w_in[l].astype(BF16), b_gate[l], cs, tm)
        attn = _attention(q.reshape(B, lp, -1), kv.reshape(B, lp, -1), sink[l], bias, variant_of_block, nb)
        four = _fourier(wdft, pq.reshape(B, lp, -1), lead, L, tf)
        h1, h1b, aff_r = _outproj(
            attn.reshape(n, -1), four.reshape(n, -1), gates, h,
            w_attn_o[l].astype(BF16), w_four_o[l].astype(BF16), w_out[l].astype(BF16),
            ln1_g[l], ln1_b[l], w_router[l], alpha, tm)
        rank_r, rank_c, cntv, cntc = _topk(aff_r, B, lp, lead, cap)
        cnt = cntv[:, :, :nchunk + 1].reshape(-1)
        xg, gsel = _gather(cnt, cntv, rank_r, aff_r, h1b, B, lp, slots, ch, win)
        y = _ffn(xg, gsel, w_e_gate, w_e_up, w_e_down, l)
        h = _combine(cnt, cntc, rank_c, y, h1, ln2_g[l], ln2_b[l], B, lp, alpha, ch, win)
    return h.reshape(B, lp, D_MODEL)[:, lead + N_META:]
```

```python
import functools
import math

import numpy as np
import jax
import jax.numpy as jnp
from jax import lax
from jax.experimental import pallas as pl
from jax.experimental.pallas import tpu as pltpu

D_MODEL = 1024
N_META = 16
N_HEADS = 8
N_KV_HEADS = 2
HEAD_DIM = 64
GQA_GROUP = N_HEADS // N_KV_HEADS
ATTN_WIDTH = N_HEADS * HEAD_DIM
KV_WIDTH = N_KV_HEADS * HEAD_DIM
WINDOW = 128
BLOCK = 128
N_FOURIER_GROUPS = 4
FOURIER_GROUP = 128
FOURIER_WIDTH = N_FOURIER_GROUPS * FOURIER_GROUP
N_BRANCHES = 2
GATE_WIDTH = N_BRANCHES * D_MODEL
N_EXPERTS = 16
CAPACITY_FACTOR = 2
D_FF_EXPERT = 1536
LN_EPS = 1e-5
NEG_INF = -1e30
Q_END = ATTN_WIDTH
K_END = Q_END + KV_WIDTH
V_END = K_END + KV_WIDTH
F_END = V_END + FOURIER_WIDTH
IN_WIDTH = F_END + GATE_WIDTH

LANES = 128
SUBLANES = 8
VMEM_LIMIT_BYTES = 56 * 1024 * 1024

F32 = jnp.float32
BF16 = jnp.bfloat16
NT_DIMS = (((1,), (1,)), ((), ()))


def _pick_tile(n, candidates):
    for c in candidates:
        if n % c == 0:
            return c
    raise ValueError(f"no tile in {candidates} divides {n}")


def _params(*sem):
    return pltpu.CompilerParams(dimension_semantics=sem, vmem_limit_bytes=VMEM_LIMIT_BYTES)


def _ln(x, g, b):
    mu = jnp.mean(x, axis=-1, keepdims=True)
    xc = x - mu
    var = jnp.mean(xc * xc, axis=-1, keepdims=True)
    return xc * lax.rsqrt(var + LN_EPS) * g + b


def _embed_kernel(*refs, lead, group):
    x_refs, (meta_ref, g_ref, b_ref, o_ref) = refs[:group], refs[group:]
    for k in range(1, group):
        o_ref[k * BLOCK:(k + 1) * BLOCK, :] = _ln(x_refs[k][...], g_ref[...], b_ref[...])

    @pl.when(pl.program_id(1) == 0)
    def _():
        o_ref[0:lead, :] = jnp.zeros((lead, D_MODEL), F32)
        o_ref[lead:BLOCK, :] = _ln(meta_ref[...], g_ref[...], b_ref[...])

    @pl.when(pl.program_id(1) > 0)
    def _():
        o_ref[0:BLOCK, :] = _ln(x_refs[0][...], g_ref[...], b_ref[...])


def _embed(x, meta, g, b, nb, lead):
    B = x.shape[0]
    group = _pick_tile(nb, (11, 3, 1))
    x_spec = lambda k: pl.BlockSpec((None, BLOCK, D_MODEL),
                                    lambda bi, j: (bi, jnp.maximum(group * j + k - 1, 0), 0))
    const = lambda bi, j: (0, 0)
    return pl.pallas_call(
        functools.partial(_embed_kernel, lead=lead, group=group),
        grid=(B, nb // group),
        in_specs=[x_spec(k) for k in range(group)] + [
            pl.BlockSpec((N_META, D_MODEL), const),
            pl.BlockSpec((1, D_MODEL), const),
            pl.BlockSpec((1, D_MODEL), const),
        ],
        out_specs=pl.BlockSpec((None, group * BLOCK, D_MODEL), lambda bi, j: (bi, j, 0)),
        out_shape=jax.ShapeDtypeStruct((B, nb * BLOCK, D_MODEL), F32),
        compiler_params=_params("parallel", "arbitrary"),
        name="embed_ln",
    )(*([x] * group), meta, g.reshape(1, -1), b.reshape(1, -1))


def _inproj_kernel(h_ref, w_ref, bg_ref, cs_ref, q_ref, kv_ref, pq_ref, gate_ref):
    hb = h_ref[...].astype(BF16)
    q_ref[...] = jnp.dot(hb, w_ref[:, 0:Q_END], preferred_element_type=F32).astype(BF16)
    kv_ref[...] = jnp.dot(hb, w_ref[:, Q_END:V_END], preferred_element_type=F32).astype(BF16)
    uf = jnp.dot(hb, w_ref[:, V_END:F_END], preferred_element_type=F32).astype(BF16)
    for g in range(N_FOURIER_GROUPS):
        lo = g * FOURIER_GROUP
        pq = jnp.dot(uf[:, lo:lo + FOURIER_GROUP], cs_ref[...], preferred_element_type=F32)
        pq_ref[:, lo:lo + FOURIER_GROUP] = pq[:, 0:FOURIER_GROUP].astype(BF16)
        pq_ref[:, FOURIER_WIDTH + lo:FOURIER_WIDTH + lo + FOURIER_GROUP] = (
            pq[:, FOURIER_GROUP:2 * FOURIER_GROUP].astype(BF16))
    chunk = FOURIER_WIDTH
    for c in range(GATE_WIDTH // chunk):
        lo = c * chunk
        ug = jnp.dot(hb, w_ref[:, F_END + lo:F_END + lo + chunk], preferred_element_type=F32)
        gate_ref[:, lo:lo + chunk] = jax.nn.sigmoid(ug + bg_ref[:, lo:lo + chunk])


def _inproj(h, w_bf16, b_gate, cs, tm):
    n = h.shape[0]
    row = lambda i: (i, 0)
    const = lambda i: (0, 0)
    return pl.pallas_call(
        _inproj_kernel,
        grid=(n // tm,),
        in_specs=[
            pl.BlockSpec((tm, D_MODEL), row),
            pl.BlockSpec((D_MODEL, IN_WIDTH), const),
            pl.BlockSpec((1, GATE_WIDTH), const),
            pl.BlockSpec((FOURIER_GROUP, 2 * FOURIER_GROUP), const),
        ],
        out_specs=[
            pl.BlockSpec((tm, ATTN_WIDTH), row),
            pl.BlockSpec((tm, 2 * KV_WIDTH), row),
            pl.BlockSpec((tm, 2 * FOURIER_WIDTH), row),
            pl.BlockSpec((tm, GATE_WIDTH), row),
        ],
        out_shape=[
            jax.ShapeDtypeStruct((n, ATTN_WIDTH), BF16),
            jax.ShapeDtypeStruct((n, 2 * KV_WIDTH), BF16),
            jax.ShapeDtypeStruct((n, 2 * FOURIER_WIDTH), BF16),
            jax.ShapeDtypeStruct((n, GATE_WIDTH), F32),
        ],
        compiler_params=_params("parallel"),
        name="inproj",
    )(h, w_bf16, b_gate.reshape(1, GATE_WIDTH), cs)


SCORES_AHEAD = 5


def _attn_kernel(sink_ref, q_ref, *refs, group, variant):
    kv_refs, (bias_ref, o_ref) = refs[:group + 2], refs[group + 2:]
    first_block = pl.program_id(1) * group
    kv = jnp.concatenate([r[...] for r in kv_refs], axis=0)
    low_half = lax.broadcasted_iota(jnp.int32, (kv.shape[0], LANES), 1) < HEAD_DIM

    def lane_half_operands(x):
        swapped = jnp.concatenate([x[:, HEAD_DIM:], x[:, :HEAD_DIM]], axis=1)
        zero = jnp.zeros_like(x)
        return {(kvh, half): jnp.where(low_half if half == 0 else ~low_half,
                                       x if kvh == half else swapped, zero)
                for kvh in range(N_KV_HEADS) for half in range(2)}

    k_ops = lane_half_operands(kv[:, 0:KV_WIDTH])
    v_ops = lane_half_operands(kv[:, KV_WIDTH:2 * KV_WIDTH])
    keys = lambda t: slice(t * BLOCK, (t + 3) * BLOCK)

    def scores(t, h):
        pair = h // 2
        qp = q_ref[t * BLOCK:(t + 1) * BLOCK, pair * LANES:(pair + 1) * LANES]
        return lax.dot_general(qp, k_ops[(h // GQA_GROUP, h % 2)][keys(t)], NT_DIMS,
                               preferred_element_type=F32)

    def head_out(t, h, s):
        sink = sink_ref[h]
        logits = s * (HEAD_DIM ** -0.5) + bias_ref[variant(first_block + t), h]
        m = jnp.maximum(jnp.max(logits, axis=-1, keepdims=True), sink)
        p = jnp.exp(logits - m)
        denom = jnp.sum(p, axis=-1, keepdims=True) + jnp.exp(sink - m)
        o = jnp.dot(p.astype(BF16), v_ops[(h // GQA_GROUP, h % 2)][keys(t)], preferred_element_type=F32)
        return o / denom

    work = [(t, h) for t in range(group) for h in range(N_HEADS)]
    ahead = SCORES_AHEAD
    pending = [scores(*w) for w in work[:ahead]]
    for n, (t, h) in enumerate(work):
        s_cur = pending.pop(0)
        if n + ahead < len(work):
            pending.append(scores(*work[n + ahead]))
        o = head_out(t, h, s_cur)
        if h % 2 == 0:
            o_even = o
        else:
            pair = h // 2
            o_ref[t * BLOCK:(t + 1) * BLOCK, pair * LANES:(pair + 1) * LANES] = (o_even + o).astype(BF16)


def _attn_bias_tables(nb, lead):
    qi = np.arange(BLOCK)[:, None]
    si = np.arange(3 * BLOCK)[None, :]
    rel = np.abs(si - BLOCK - qi).astype(np.float32)
    slopes = np.array([2.0 ** (-8.0 * (h + 1) / N_HEADS) for h in range(N_HEADS)], np.float32)
    base = np.where(rel[None] <= WINDOW, -slopes[:, None, None] * rel[None], np.float32(NEG_INF))
    variants, keys, variant_of_block = [], [], []
    for i in range(nb):
        kpos = (i - 1) * BLOCK + np.arange(3 * BLOCK)
        valid = (kpos >= lead) & (kpos < nb * BLOCK)
        key = valid.tobytes()
        if key not in keys:
            keys.append(key)
            variants.append(np.where(valid[None, None, :], base, np.float32(NEG_INF)))
        variant_of_block.append(keys.index(key))
    return jnp.asarray(np.stack(variants).astype(np.float32)), variant_of_block


def _attention(q, kv, sink, bias, variant_of_block, nb):
    B, lp, _ = q.shape
    group = _pick_tile(nb, (11, 3, 1))
    interior = max(set(variant_of_block), key=variant_of_block.count)

    def variant(i):
        v = jnp.int32(interior)
        for blk, var in enumerate(variant_of_block):
            if var != interior:
                v = jnp.where(i == blk, var, v)
        return v

    kv_spec = lambda t: pl.BlockSpec((None, BLOCK, 2 * KV_WIDTH),
                                     lambda b, g: (b, jnp.clip(group * g + t - 1, 0, nb - 1), 0))
    rows = pl.BlockSpec((None, group * BLOCK, ATTN_WIDTH), lambda b, g: (b, g, 0))
    return pl.pallas_call(
        functools.partial(_attn_kernel, group=group, variant=variant),
        grid=(B, nb // group),
        in_specs=[pl.BlockSpec(memory_space=pltpu.SMEM), rows]
                 + [kv_spec(t) for t in range(group + 2)]
                 + [pl.BlockSpec(bias.shape, lambda b, g: (0, 0, 0, 0))],
        out_specs=rows,
        out_shape=jax.ShapeDtypeStruct((B, lp, ATTN_WIDTH), BF16),
        compiler_params=_params("parallel", "arbitrary"),
        name="attention",
    )(sink, q, *([kv] * (group + 2)), bias)


def _fourier_kernel(w_ref, p_ref, q_ref, o_ref, prev_ref, pe_ref, qo_ref, *, lp, lead, seq_len, tf):
    m = pl.program_id(1)
    last = pl.num_programs(1) - 1
    nblk = lp // tf
    shift = lead + 1
    kp = pe_ref.shape[0]
    half = seq_len // 2
    r = lax.broadcasted_iota(jnp.int32, (tf, tf), 0)
    c = lax.broadcasted_iota(jnp.int32, (tf, tf), 1)
    flip = (r + c == tf - 1).astype(BF16)

    @pl.when(m == 0)
    def _():
        row = lax.broadcasted_iota(jnp.int32, (lp, 1), 0)
        dc = jnp.sum(jnp.where(row >= lead, p_ref[...].astype(F32), 0.0), axis=0, keepdims=True)
        prev_ref[...] = jnp.zeros_like(prev_ref)
        prev_ref[tf - 1:tf, :] = dc * np.float32(seq_len ** -0.5)
        def mirror(x_ref, i):
            above = (jnp.dot(flip, x_ref[(nblk - i) * tf:(nblk - i + 1) * tf, :], preferred_element_type=F32)
                     if i > 0 else jnp.zeros((tf, FOURIER_WIDTH), F32))
            below = jnp.dot(flip, x_ref[(nblk - 1 - i) * tf:(nblk - i) * tf, :], preferred_element_type=F32)
            return jnp.concatenate([above[tf - shift:, :], below[:tf - shift, :]], axis=0)

        for i in range(kp // tf):
            rows = slice(i * tf, (i + 1) * tf)
            n = lax.broadcasted_iota(jnp.int32, (tf, 1), 0) + (i * tf - lead)
            paired = (n >= 1) & (n <= half - 1)
            alone = (n == 0) | (n == half)
            p_blk = p_ref[rows, :].astype(F32)
            pe_ref[rows, :] = jnp.where(paired, p_blk + mirror(p_ref, i),
                                        jnp.where(alone, p_blk, 0.0)).astype(BF16)
            qo_ref[rows, :] = jnp.where(paired, q_ref[rows, :].astype(F32) - mirror(q_ref, i), 0.0).astype(BF16)

    t1 = jnp.dot(w_ref[:, 0:kp], pe_ref[...], preferred_element_type=F32)
    t2 = jnp.dot(w_ref[:, kp:2 * kp], qo_ref[...], preferred_element_type=F32)
    fsub = w_ref.shape[0] // tf
    tail = prev_ref[tf - shift:, :]
    for t in range(fsub):
        blk = slice(t * tf, (t + 1) * tf)
        s = m * fsub + t
        direct = t1[blk] + t2[blk]
        mirrored = jnp.dot(flip, (t1[blk] - t2[blk]).astype(BF16), preferred_element_type=F32)
        shifted = jnp.concatenate([tail, direct[:tf - shift, :]], axis=0)
        tail = direct[tf - shift:, :]

        def store_pair(s=s, shifted=shifted, mirrored=mirrored):
            o_ref[pl.ds(pl.multiple_of(s * tf, tf), tf), :] = shifted.astype(BF16)
            o_ref[pl.ds(pl.multiple_of((nblk - 1 - s) * tf, tf), tf), :] = mirrored.astype(BF16)

        def store_middle(shifted=shifted, mirrored=mirrored):
            row = lax.broadcasted_iota(jnp.int32, (tf, 1), 0) + (nblk // 2) * tf
            mid = jnp.where(row <= seq_len // 2 + lead, shifted, mirrored)
            o_ref[(nblk // 2) * tf:(nblk // 2 + 1) * tf, :] = mid.astype(BF16)

        if t < fsub - 1:
            store_pair()
        else:
            pl.when(m < last)(store_pair)
            pl.when(m == last)(store_middle)
    prev_ref[tf - shift:, :] = tail


def _dft_matrix(nrows, ncols, lead, seq_len):
    w = np.float32(2.0 * math.pi / seq_len)
    k = (jnp.arange(nrows, dtype=jnp.int32) + 1)[:, None]
    j = jnp.arange(ncols // LANES, dtype=jnp.int32)[None, :]
    r = jnp.arange(LANES, dtype=jnp.int32)[None, :]
    a = ((k * (LANES * j - lead)) % seq_len).astype(F32) * w
    b = ((k * r) % seq_len).astype(F32) * w
    ca, sa, cb, sb = jnp.cos(a)[:, :, None], jnp.sin(a)[:, :, None], jnp.cos(b)[:, None, :], jnp.sin(b)[:, None, :]
    n = jnp.arange(ncols, dtype=jnp.int32) - lead
    valid = ((n >= 0) & (n <= seq_len // 2)).reshape(1, ncols // LANES, LANES)
    scale = np.float32(seq_len ** -0.5)
    wc = jnp.where(valid, (ca * cb - sa * sb) * scale, 0.0).reshape(nrows, ncols)
    ws = jnp.where(valid, -(sa * cb + ca * sb) * scale, 0.0).reshape(nrows, ncols)
    return jnp.concatenate([wc, ws], axis=1).astype(BF16)


def _fourier_steps(lp, lead, seq_len, tf):
    nblk = lp // tf
    assert seq_len % 2 == 0 and lp % tf == 0 and nblk % 2 == 1 and lead + 1 < tf
    steps = nblk // 2 + 1
    assert steps * tf > seq_len // 2 + lead
    return steps


def _fourier(w, pq, lead, seq_len, tf):
    B, lp, _ = pq.shape
    steps = _fourier_steps(lp, lead, seq_len, tf)
    kp = steps * tf
    fsub = _pick_tile(steps, (2, 1))
    return pl.pallas_call(
        functools.partial(_fourier_kernel, lp=lp, lead=lead, seq_len=seq_len, tf=tf),
        grid=(B, steps // fsub),
        in_specs=[
            pl.BlockSpec((fsub * tf, 2 * kp), lambda b, m: (m, 0)),
            pl.BlockSpec((None, lp, FOURIER_WIDTH), lambda b, m: (b, 0, 0)),
            pl.BlockSpec((None, lp, FOURIER_WIDTH), lambda b, m: (b, 0, 1)),
        ],
        out_specs=pl.BlockSpec((None, lp, FOURIER_WIDTH), lambda b, m: (b, 0, 0)),
        out_shape=jax.ShapeDtypeStruct((B, lp, FOURIER_WIDTH), BF16),
        scratch_shapes=[pltpu.VMEM((tf, FOURIER_WIDTH), F32),
                        pltpu.VMEM((kp, FOURIER_WIDTH), BF16),
                        pltpu.VMEM((kp, FOURIER_WIDTH), BF16)],
        compiler_params=_params("parallel", "arbitrary"),
        name="fourier",
    )(w, pq, pq)


def _split_bf16(x):
    hi = x.astype(BF16)
    lo = (x - hi.astype(F32)).astype(BF16)
    return hi, lo


OUTPROJ_SUB = 128


def _outproj_kernel(attn_ref, four_ref, gate_ref, h_ref, wa_ref, wf_ref, wo_ref, g_ref, b_ref,
                    wrth_ref, wrtl_ref, h1_ref, h1b_ref, affr_ref, *, alpha):
    tm = h_ref.shape[0]
    subs = [slice(i * OUTPROJ_SUB, (i + 1) * OUTPROJ_SUB) for i in range(tm // OUTPROJ_SUB)]
    ya = [jnp.dot(attn_ref[s, :], wa_ref[...], preferred_element_type=F32) for s in subs]
    yf = [jnp.dot(four_ref[s, :], wf_ref[...], preferred_element_type=F32) for s in subs]
    merged = [gate_ref[s, 0:D_MODEL] * a + gate_ref[s, D_MODEL:GATE_WIDTH] * f for s, a, f in zip(subs, ya, yf)]
    mix = [jnp.dot(mg.astype(BF16), wo_ref[...], preferred_element_type=F32) for mg in merged]
    for s, mx in zip(subs, mix):
        h1 = _ln(alpha * h_ref[s, :] + mx, g_ref[...], b_ref[...])
        h1_ref[s, :] = h1
        hi, lo = _split_bf16(h1)
        h1b_ref[s, :] = hi
        lr = (lax.dot_general(wrth_ref[...], hi, NT_DIMS, preferred_element_type=F32)
              + lax.dot_general(wrtl_ref[...], hi, NT_DIMS, preferred_element_type=F32)
              + lax.dot_general(wrth_ref[...], lo, NT_DIMS, preferred_element_type=F32))
        er = jnp.exp(lr - jnp.max(lr, axis=0, keepdims=True))
        affr_ref[:, s] = er / jnp.sum(er, axis=0, keepdims=True)


def _outproj(attn, four, gates, h, wa, wf, wo, g, b, w_router, alpha, tm):
    n = h.shape[0]
    row = lambda i: (i, 0)
    const = lambda i: (0, 0)
    wrth, wrtl = _split_bf16(w_router.T)
    return pl.pallas_call(
        functools.partial(_outproj_kernel, alpha=alpha),
        grid=(n // tm,),
        in_specs=[
            pl.BlockSpec((tm, ATTN_WIDTH), row),
            pl.BlockSpec((tm, FOURIER_WIDTH), row),
            pl.BlockSpec((tm, GATE_WIDTH), row),
            pl.BlockSpec((tm, D_MODEL), row),
            pl.BlockSpec((ATTN_WIDTH, D_MODEL), const),
            pl.BlockSpec((FOURIER_WIDTH, D_MODEL), const),
            pl.BlockSpec((D_MODEL, D_MODEL), const),
            pl.BlockSpec((1, D_MODEL), const),
            pl.BlockSpec((1, D_MODEL), const),
            pl.BlockSpec((N_EXPERTS, D_MODEL), const),
            pl.BlockSpec((N_EXPERTS, D_MODEL), const),
        ],
        out_specs=[
            pl.BlockSpec((tm, D_MODEL), row),
            pl.BlockSpec((tm, D_MODEL), row),
            pl.BlockSpec((N_EXPERTS, tm), lambda i: (0, i)),
        ],
        out_shape=[
            jax.ShapeDtypeStruct((n, D_MODEL), F32),
            jax.ShapeDtypeStruct((n, D_MODEL), BF16),
            jax.ShapeDtypeStruct((N_EXPERTS, n), F32),
        ],
        compiler_params=_params("parallel"),
        name="outproj_ln_router",
    )(attn, four, gates, h, wa, wf, wo, g.reshape(1, -1), b.reshape(1, -1), wrth, wrtl)


def _topk_kernel(aff_ref, rr_ref, rc_ref, cnt_ref, cntc_ref, *, lead, lp, cap):
    nchunk = lp // LANES
    lane = lax.broadcasted_iota(jnp.int32, (N_EXPERTS, lp), 1)
    bits = jnp.where(lane >= lead, pltpu.bitcast(aff_ref[...], jnp.int32), -1)

    def search(i, t):
        cand = t | (jnp.int32(1) << (30 - i))
        cnt = jnp.sum((bits >= cand).astype(jnp.int32), axis=-1, keepdims=True)
        return jnp.where(cnt >= cap, cand, t)

    thr = lax.fori_loop(0, 31, search, jnp.zeros((N_EXPERTS, 1), jnp.int32))
    gt = bits > thr
    eq = bits == thr
    need = cap - jnp.sum(gt.astype(jnp.int32), axis=-1, keepdims=True)

    r = lax.broadcasted_iota(jnp.int32, (LANES, LANES), 0)
    c = lax.broadcasted_iota(jnp.int32, (LANES, LANES), 1)
    upper = (r <= c).astype(BF16)
    lower = (c <= r).astype(BF16)
    ident = (c == r).astype(BF16)

    chunks = [slice(k * LANES, (k + 1) * LANES) for k in range(nchunk)]
    eqb = eq.astype(BF16)
    needf = need.astype(F32)
    pre_eq = [jnp.dot(eqb[:, sl], upper, preferred_element_type=F32) for sl in chunks]
    off = jnp.zeros((N_EXPERTS, 1), F32)
    sel_chunks = []
    for k, sl in enumerate(chunks):
        sel_chunks.append(gt[:, sl] | (eq[:, sl] & (pre_eq[k] + off <= needf)))
        off = off + pre_eq[k][:, LANES - 1:LANES]

    selb = [sel.astype(BF16) for sel in sel_chunks]
    pre_r = [jnp.dot(s, upper, preferred_element_type=F32) for s in selb]
    pre_c = [lax.dot_general(lower, s, NT_DIMS, preferred_element_type=F32) for s in selb]
    sel_c = [lax.dot_general(ident, s, NT_DIMS, preferred_element_type=F32) for s in selb]
    off_r = jnp.zeros((N_EXPERTS, 1), F32)
    off_c = jnp.zeros((1, N_EXPERTS), F32)
    cnt_lane = lax.broadcasted_iota(jnp.int32, (N_EXPERTS, LANES), 1)
    cnt = jnp.zeros((N_EXPERTS, LANES), F32)
    for k, sl in enumerate(chunks):
        rr_ref[:, sl] = jnp.where(sel_chunks[k], pre_r[k] + (off_r - 1.0), -1.0).astype(jnp.int32)
        off_r = off_r + pre_r[k][:, LANES - 1:LANES]
        cnt = jnp.where(cnt_lane == k + 1, off_r, cnt)
        rc_ref[sl, :] = jnp.where(sel_c[k] > 0.5, pre_c[k] + (off_c - 1.0), -1.0).astype(jnp.int32)
        off_c = off_c + pre_c[k][LANES - 1:LANES, :]
        cntc_ref[k + 1:k + 2, :] = off_c.astype(jnp.int32)
    cnt_ref[...] = cnt.astype(jnp.int32)
    cntc_ref[0:1, :] = jnp.zeros((1, N_EXPERTS), jnp.int32)
    pad_rows = cntc_ref.shape[0] - nchunk - 1
    if pad_rows:
        cntc_ref[nchunk + 1:, :] = jnp.zeros((pad_rows, N_EXPERTS), jnp.int32)


def _topk(aff_r, B, lp, lead, cap):
    n = B * lp
    nchunk = lp // LANES
    assert nchunk + 1 <= LANES
    nb1 = -(-(nchunk + 1) // SUBLANES) * SUBLANES
    return pl.pallas_call(
        functools.partial(_topk_kernel, lead=lead, lp=lp, cap=cap),
        grid=(B,),
        in_specs=[pl.BlockSpec((N_EXPERTS, lp), lambda b: (0, b))],
        out_specs=[
            pl.BlockSpec((N_EXPERTS, lp), lambda b: (0, b)),
            pl.BlockSpec((lp, N_EXPERTS), lambda b: (b, 0)),
            pl.BlockSpec((None, N_EXPERTS, LANES), lambda b: (b, 0, 0)),
            pl.BlockSpec((None, nb1, N_EXPERTS), lambda b: (b, 0, 0)),
        ],
        out_shape=[
            jax.ShapeDtypeStruct((N_EXPERTS, n), jnp.int32),
            jax.ShapeDtypeStruct((n, N_EXPERTS), jnp.int32),
            jax.ShapeDtypeStruct((B, N_EXPERTS, LANES), jnp.int32),
            jax.ShapeDtypeStruct((B, nb1, N_EXPERTS), jnp.int32),
        ],
        compiler_params=_params("parallel"),
        name="topk_select",
    )(aff_r)


SLOT_ALIGN = 16


def _window_constants(win):
    wide = N_EXPERTS * win
    expand = np.zeros((N_EXPERTS, wide), np.float32)
    for e in range(N_EXPERTS):
        expand[e, e * win:(e + 1) * win] = 1.0
    return expand, (np.arange(wide) % win).astype(np.float32)


def _gather_kernel(cnt_ref, cntv_ref, rank_ref, aff_ref, h_ref, o_ref, g_ref,
                   *, slots, ch, win, nchunk):
    bi, c = pl.program_id(0), pl.program_id(1)
    k0 = c * ch
    tt = ch * LANES
    max_start = slots - win

    @pl.when(c == 0)
    def _():
        o_ref[...] = jnp.zeros_like(o_ref)
        g_ref[...] = jnp.zeros_like(g_ref)

    lane = lax.broadcasted_iota(jnp.int32, cntv_ref.shape, 1)
    lo_col = jnp.sum(jnp.where(lane == k0, cntv_ref[...], 0), axis=-1, keepdims=True)
    w_col = jnp.minimum(lo_col & -SLOT_ALIGN, max_start)
    rank = rank_ref[...]
    aff = aff_ref[...]
    rel = rank - w_col
    in_win = (rank >= 0) & (rel >= 0) & (rel < win)
    rel = jnp.where(in_win, rel, -1)
    row = lax.broadcasted_iota(jnp.int32, (win, tt), 0)
    hit_all = jnp.concatenate([rel[e:e + 1, :] == row for e in range(N_EXPERTS)], axis=0)
    rows = h_ref[...]
    xw = jnp.dot(hit_all.astype(BF16), rows, preferred_element_type=F32)

    def add_rows(e, start, hit, vals):
        dst = pl.ds(pl.multiple_of(start, SLOT_ALIGN), win)
        o_ref[e, dst, :] = o_ref[e, dst, :] + vals.astype(BF16)
        g_ref[e, dst, :] += jnp.sum(jnp.where(hit, aff[e:e + 1, :], 0.0), axis=-1, keepdims=True)

    for e in range(N_EXPERTS):
        base = (bi * N_EXPERTS + e) * (nchunk + 1) + k0
        w = jnp.minimum(cnt_ref[base] & -SLOT_ALIGN, max_start)
        add_rows(e, w, hit_all[e * win:(e + 1) * win, :], xw[e * win:(e + 1) * win, :])
        hi = cnt_ref[base + ch]

        @pl.when(hi > w + win)
        def _(e=e, w=w, hi=hi):
            rk = rank[e:e + 1, :]

            def extra(i, carry):
                w2 = w + (i + 1) * win
                w2c = jnp.minimum(w2, max_start)
                hit = ((rk - w2c) == row) & (rk >= w2)
                add_rows(e, w2c, hit, jnp.dot(hit.astype(BF16), rows, preferred_element_type=F32))
                return carry

            lax.fori_loop(0, (hi - w - 1) // win, extra, 0)


def _gather(cnt, cntv, rank_rows, aff_rows, h1b, B, lp, slots, ch, win):
    nchunk = lp // LANES
    tt = ch * LANES
    nt = nchunk // ch
    assert slots % SLOT_ALIGN == 0 and win % SLOT_ALIGN == 0 and win <= slots
    per_seq = lambda bi, c, cnt: (bi, 0, 0, 0)
    return pl.pallas_call(
        functools.partial(_gather_kernel, slots=slots, ch=ch, win=win, nchunk=nchunk),
        grid_spec=pltpu.PrefetchScalarGridSpec(
            num_scalar_prefetch=1,
            grid=(B, nt),
            in_specs=[
                pl.BlockSpec((None, N_EXPERTS, LANES), lambda bi, c, cnt: (bi, 0, 0)),
                pl.BlockSpec((N_EXPERTS, tt), lambda bi, c, cnt: (0, bi * nt + c)),
                pl.BlockSpec((N_EXPERTS, tt), lambda bi, c, cnt: (0, bi * nt + c)),
                pl.BlockSpec((tt, D_MODEL), lambda bi, c, cnt: (bi * nt + c, 0)),
            ],
            out_specs=[pl.BlockSpec((None, N_EXPERTS, slots, D_MODEL), per_seq),
                       pl.BlockSpec((None, N_EXPERTS, slots, 1), per_seq)],
        ),
        out_shape=[jax.ShapeDtypeStruct((B, N_EXPERTS, slots, D_MODEL), BF16),
                   jax.ShapeDtypeStruct((B, N_EXPERTS, slots, 1), F32)],
        compiler_params=_params("parallel", "arbitrary"),
        name="moe_gather",
    )(cnt, cntv, rank_rows, aff_rows, h1b)


def _ffn_up_kernel(x_ref, wg_ref, wu_ref, o_ref, wgb_ref, wub_ref):
    @pl.when(pl.program_id(1) == 0)
    def _():
        wgb_ref[...] = wg_ref[...].astype(BF16)
        wub_ref[...] = wu_ref[...].astype(BF16)

    nseq, slots, width = x_ref.shape
    x = x_ref[...].reshape(nseq * slots, width)
    a = jnp.dot(x, wgb_ref[...], preferred_element_type=F32)
    u = jnp.dot(x, wub_ref[...], preferred_element_type=F32)
    o_ref[...] = (a * jax.nn.sigmoid(a) * u).astype(BF16).reshape(o_ref.shape)


def _ffn_down_kernel(x_ref, g_ref, wd_ref, o_ref, wdb_ref):
    @pl.when(pl.program_id(1) == 0)
    def _():
        wdb_ref[...] = wd_ref[...].astype(BF16)

    nseq, slots, width = x_ref.shape
    y = jnp.dot(x_ref[...].reshape(nseq * slots, width), wdb_ref[...], preferred_element_type=F32)
    y = y * g_ref[...].reshape(nseq * slots, 1)
    o_ref[...] = y.astype(BF16).reshape(o_ref.shape)


def _ffn(xg, gates, wg, wu, wd, layer):
    B, _, slots, _ = xg.shape
    n_up = _pick_tile(B, (2, 1))
    n_down = _pick_tile(B, (4, 2, 1))
    acts = lambda nseq, width: pl.BlockSpec((nseq, None, slots, width), lambda e, b: (b, e, 0, 0))
    weight = lambda rows, cols: pl.BlockSpec((None, None, rows, cols), lambda e, b: (layer, e, 0, 0))
    mid = pl.pallas_call(
        _ffn_up_kernel,
        grid=(N_EXPERTS, B // n_up),
        in_specs=[acts(n_up, D_MODEL), weight(D_MODEL, D_FF_EXPERT), weight(D_MODEL, D_FF_EXPERT)],
        out_specs=acts(n_up, D_FF_EXPERT),
        out_shape=jax.ShapeDtypeStruct((B, N_EXPERTS, slots, D_FF_EXPERT), BF16),
        scratch_shapes=[pltpu.VMEM((D_MODEL, D_FF_EXPERT), BF16), pltpu.VMEM((D_MODEL, D_FF_EXPERT), BF16)],
        compiler_params=_params("arbitrary", "arbitrary"),
        name="moe_ffn_up",
    )(xg, wg, wu)
    return pl.pallas_call(
        _ffn_down_kernel,
        grid=(N_EXPERTS, B // n_down),
        in_specs=[acts(n_down, D_FF_EXPERT), acts(n_down, 1), weight(D_FF_EXPERT, D_MODEL)],
        out_specs=acts(n_down, D_MODEL),
        out_shape=jax.ShapeDtypeStruct(xg.shape, BF16),
        scratch_shapes=[pltpu.VMEM((D_FF_EXPERT, D_MODEL), BF16)],
        compiler_params=_params("arbitrary", "arbitrary"),
        name="moe_ffn_down",
    )(mid, gates, wd)


def _combine_kernel(cnt_ref, cntc_ref, rank_ref, y_ref, h_ref, g_ref, b_ref, expand_ref, rpat_ref,
                    o_ref, acc_ref, *, slots, ch, win, nchunk, alpha):
    bi, c = pl.program_id(0), pl.program_id(1)
    k0 = c * ch
    tt = ch * LANES
    max_start = slots - win
    rank = rank_ref[...]
    lo_row = cntc_ref[pl.ds(k0, 1), :]
    w_row = jnp.minimum(lo_row & -SLOT_ALIGN, max_start)
    rel = rank - w_row
    in_win = (rank >= 0) & (rel >= 0) & (rel < win)
    relb = jnp.where(in_win, rel, -1).astype(F32).astype(BF16)
    rel_wide = jnp.dot(relb, expand_ref[...], preferred_element_type=F32)
    onehot = (rel_wide == rpat_ref[...]).astype(BF16)

    starts, windows = [], []
    for e in range(N_EXPERTS):
        lo = cnt_ref[(bi * N_EXPERTS + e) * (nchunk + 1) + k0]
        w = jnp.minimum(lo & -SLOT_ALIGN, max_start)
        starts.append(w)
        windows.append(y_ref[e, pl.ds(pl.multiple_of(w, SLOT_ALIGN), win), :])
    acc_ref[...] = jnp.dot(onehot, jnp.concatenate(windows, axis=0), preferred_element_type=F32)

    lane = lax.broadcasted_iota(jnp.int32, rank.shape, 1)
    col = lax.broadcasted_iota(jnp.int32, (tt, win), 1)
    for e in range(N_EXPERTS):
        hi = cnt_ref[(bi * N_EXPERTS + e) * (nchunk + 1) + k0 + ch]

        @pl.when(hi > starts[e] + win)
        def _(e=e, hi=hi):
            rk = jnp.sum(jnp.where(lane == e, rank, 0), axis=-1, keepdims=True)

            def extra(i, carry):
                w2 = starts[e] + (i + 1) * win
                w2c = jnp.minimum(w2, max_start)
                rows = y_ref[e, pl.ds(pl.multiple_of(w2c, SLOT_ALIGN), win), :]
                hit = ((rk - w2c) == col) & (rk >= w2)
                acc_ref[...] += jnp.dot(hit.astype(BF16), rows, preferred_element_type=F32)
                return carry

            lax.fori_loop(0, (hi - starts[e] - 1) // win, extra, 0)

    o_ref[...] = _ln(alpha * h_ref[...] + acc_ref[...], g_ref[...], b_ref[...])


def _combine(cnt, cntc, rank_c, y, h1, g, b, B, lp, alpha, ch, win):
    n = B * lp
    slots = y.shape[2]
    nchunk = lp // LANES
    tt = ch * LANES
    nt = nchunk // ch
    assert slots % SLOT_ALIGN == 0 and win % SLOT_ALIGN == 0 and win <= slots
    wide = N_EXPERTS * win
    expand, pos = _window_constants(win)
    tok = lambda bi, c, cnt: (bi * nt + c, 0)
    const = lambda bi, c, cnt: (0, 0)
    return pl.pallas_call(
        functools.partial(_combine_kernel, slots=slots, ch=ch, win=win, nchunk=nchunk, alpha=alpha),
        grid_spec=pltpu.PrefetchScalarGridSpec(
            num_scalar_prefetch=1,
            grid=(B, nt),
            in_specs=[
                pl.BlockSpec((None, cntc.shape[1], N_EXPERTS), lambda bi, c, cnt: (bi, 0, 0)),
                pl.BlockSpec((tt, N_EXPERTS), tok),
                pl.BlockSpec((None, N_EXPERTS, slots, D_MODEL), lambda bi, c, cnt: (bi, 0, 0, 0)),
                pl.BlockSpec((tt, D_MODEL), tok),
                pl.BlockSpec((1, D_MODEL), const),
                pl.BlockSpec((1, D_MODEL), const),
                pl.BlockSpec((N_EXPERTS, wide), const),
                pl.BlockSpec((1, wide), const),
            ],
            out_specs=pl.BlockSpec((tt, D_MODEL), tok),
            scratch_shapes=[pltpu.VMEM((tt, D_MODEL), F32)],
        ),
        out_shape=jax.ShapeDtypeStruct((n, D_MODEL), F32),
        compiler_params=_params("parallel", "arbitrary"),
        name="moe_combine_ln",
    )(cnt, cntc, rank_c, y, h1, g.reshape(1, -1), b.reshape(1, -1),
      jnp.asarray(expand, dtype=BF16), jnp.asarray(pos.reshape(1, wide)))


def kernel(x, meta, ln0_g, ln0_b, w_in, b_gate, sink, w_attn_o, w_four_o, w_out, ln1_g, ln1_b,
           w_router, w_e_gate, w_e_up, w_e_down, ln2_g, ln2_b):
    B, seq, d = x.shape
    depth = w_in.shape[0]
    assert d == D_MODEL and meta.shape == (N_META, D_MODEL)
    assert seq % BLOCK == 0 and N_META % SUBLANES == 0 and N_META <= BLOCK
    L = seq + N_META
    nb = -(-L // BLOCK)
    lp = nb * BLOCK
    lead = lp - L
    cap = CAPACITY_FACTOR * L // N_EXPERTS
    slots = -(-cap // SLOT_ALIGN) * SLOT_ALIGN
    alpha = float((2 * depth) ** 0.25)
    n = B * lp
    nchunk = lp // LANES
    tm = _pick_tile(n, (1024, 512, 256, 128))
    tf = _pick_tile(lp, (384, 128))
    ch = _pick_tile(nchunk, (3, 2, 1))
    win = min(slots, -(-(ch * LANES * cap // L * 4 // 3 + SLOT_ALIGN) // SLOT_ALIGN) * SLOT_ALIGN)

    gi = np.arange(FOURIER_GROUP)
    ang = 2.0 * np.pi * ((gi[:, None] * gi[None, :]) % FOURIER_GROUP) / FOURIER_GROUP
    cs = jnp.asarray(np.concatenate([np.cos(ang), np.sin(ang)], axis=1) * FOURIER_GROUP ** -0.5, dtype=BF16)
    dft_rows = _fourier_steps(lp, lead, L, tf) * tf
    wdft = _dft_matrix(dft_rows, dft_rows, lead, L)
    bias, variant_of_block = _attn_bias_tables(nb, lead)

    h = _embed(x, meta, ln0_g, ln0_b, nb, lead).reshape(n, D_MODEL)
    for l in range(depth):
        q, kv, pq, gates = _inproj(h, w_in[l].astype(BF16), b_gate[l], cs, tm)
        attn = _attention(q.reshape(B, lp, -1), kv.reshape(B, lp, -1), sink[l], bias, variant_of_block, nb)
        four = _fourier(wdft, pq.reshape(B, lp, -1), lead, L, tf)
        h1, h1b, aff_r = _outproj(
            attn.reshape(n, -1), four.reshape(n, -1), gates, h,
            w_attn_o[l].astype(BF16), w_four_o[l].astype(BF16), w_out[l].astype(BF16),
            ln1_g[l], ln1_b[l], w_router[l], alpha, tm)
        rank_r, rank_c, cntv, cntc = _topk(aff_r, B, lp, lead, cap)
        cnt = cntv[:, :, :nchunk + 1].reshape(-1)
        xg, gsel = _gather(cnt, cntv, rank_r, aff_r, h1b, B, lp, slots, ch, win)
        y = _ffn(xg, gsel, w_e_gate, w_e_up, w_e_down, l)
        h = _combine(cnt, cntc, rank_c, y, h1, ln2_g[l], ln2_b[l], B, lp, alpha, ch, win)
    return h.reshape(B, lp, D_MODEL)[:, lead + N_META:]
```

```python
import functools
import math

import numpy as np
import jax
import jax.numpy as jnp
from jax import lax
from jax.experimental import pallas as pl
from jax.experimental.pallas import tpu as pltpu

D_MODEL = 1024
N_META = 16
N_HEADS = 8
N_KV_HEADS = 2
HEAD_DIM = 64
GQA_GROUP = N_HEADS // N_KV_HEADS
ATTN_WIDTH = N_HEADS * HEAD_DIM
KV_WIDTH = N_KV_HEADS * HEAD_DIM
WINDOW = 128
BLOCK = 128
N_FOURIER_GROUPS = 4
FOURIER_GROUP = 128
FOURIER_WIDTH = N_FOURIER_GROUPS * FOURIER_GROUP
N_BRANCHES = 2
GATE_WIDTH = N_BRANCHES * D_MODEL
N_EXPERTS = 16
CAPACITY_FACTOR = 2
D_FF_EXPERT = 1536
LN_EPS = 1e-5
NEG_INF = -1e30
Q_END = ATTN_WIDTH
K_END = Q_END + KV_WIDTH
V_END = K_END + KV_WIDTH
F_END = V_END + FOURIER_WIDTH
IN_WIDTH = F_END + GATE_WIDTH

LANES = 128
SUBLANES = 8
VMEM_LIMIT_BYTES = 56 * 1024 * 1024

F32 = jnp.float32
BF16 = jnp.bfloat16
NT_DIMS = (((1,), (1,)), ((), ()))


def _pick_tile(n, candidates):
    for c in candidates:
        if n % c == 0:
            return c
    raise ValueError(f"no tile in {candidates} divides {n}")


def _params(*sem):
    return pltpu.CompilerParams(dimension_semantics=sem, vmem_limit_bytes=VMEM_LIMIT_BYTES)


def _ln(x, g, b):
    mu = jnp.mean(x, axis=-1, keepdims=True)
    xc = x - mu
    var = jnp.mean(xc * xc, axis=-1, keepdims=True)
    return xc * lax.rsqrt(var + LN_EPS) * g + b


def _embed_kernel(*refs, lead, group):
    x_refs, (meta_ref, g_ref, b_ref, o_ref) = refs[:group], refs[group:]
    for k in range(1, group):
        o_ref[k * BLOCK:(k + 1) * BLOCK, :] = _ln(x_refs[k][...], g_ref[...], b_ref[...])

    @pl.when(pl.program_id(1) == 0)
    def _():
        o_ref[0:lead, :] = jnp.zeros((lead, D_MODEL), F32)
        o_ref[lead:BLOCK, :] = _ln(meta_ref[...], g_ref[...], b_ref[...])

    @pl.when(pl.program_id(1) > 0)
    def _():
        o_ref[0:BLOCK, :] = _ln(x_refs[0][...], g_ref[...], b_ref[...])


def _embed(x, meta, g, b, nb, lead):
    B = x.shape[0]
    group = _pick_tile(nb, (11, 3, 1))
    x_spec = lambda k: pl.BlockSpec((None, BLOCK, D_MODEL),
                                    lambda bi, j: (bi, jnp.maximum(group * j + k - 1, 0), 0))
    const = lambda bi, j: (0, 0)
    return pl.pallas_call(
        functools.partial(_embed_kernel, lead=lead, group=group),
        grid=(B, nb // group),
        in_specs=[x_spec(k) for k in range(group)] + [
            pl.BlockSpec((N_META, D_MODEL), const),
            pl.BlockSpec((1, D_MODEL), const),
            pl.BlockSpec((1, D_MODEL), const),
        ],
        out_specs=pl.BlockSpec((None, group * BLOCK, D_MODEL), lambda bi, j: (bi, j, 0)),
        out_shape=jax.ShapeDtypeStruct((B, nb * BLOCK, D_MODEL), F32),
        compiler_params=_params("parallel", "arbitrary"),
        name="embed_ln",
    )(*([x] * group), meta, g.reshape(1, -1), b.reshape(1, -1))


def _inproj_kernel(h_ref, w_ref, bg_ref, cs_ref, q_ref, kv_ref, pq_ref, gate_ref):
    hb = h_ref[...].astype(BF16)
    q_ref[...] = jnp.dot(hb, w_ref[:, 0:Q_END], preferred_element_type=F32).astype(BF16)
    kv_ref[...] = jnp.dot(hb, w_ref[:, Q_END:V_END], preferred_element_type=F32).astype(BF16)
    uf = jnp.dot(hb, w_ref[:, V_END:F_END], preferred_element_type=F32).astype(BF16)
    for g in range(N_FOURIER_GROUPS):
        lo = g * FOURIER_GROUP
        pq = jnp.dot(uf[:, lo:lo + FOURIER_GROUP], cs_ref[...], preferred_element_type=F32)
        pq_ref[:, lo:lo + FOURIER_GROUP] = pq[:, 0:FOURIER_GROUP].astype(BF16)
        pq_ref[:, FOURIER_WIDTH + lo:FOURIER_WIDTH + lo + FOURIER_GROUP] = (
            pq[:, FOURIER_GROUP:2 * FOURIER_GROUP].astype(BF16))
    chunk = FOURIER_WIDTH
    for c in range(GATE_WIDTH // chunk):
        lo = c * chunk
        ug = jnp.dot(hb, w_ref[:, F_END + lo:F_END + lo + chunk], preferred_element_type=F32)
        gate_ref[:, lo:lo + chunk] = jax.nn.sigmoid(ug + bg_ref[:, lo:lo + chunk])


def _inproj(h, w_bf16, b_gate, cs, tm):
    n = h.shape[0]
    row = lambda i: (i, 0)
    const = lambda i: (0, 0)
    return pl.pallas_call(
        _inproj_kernel,
        grid=(n // tm,),
        in_specs=[
            pl.BlockSpec((tm, D_MODEL), row),
            pl.BlockSpec((D_MODEL, IN_WIDTH), const),
            pl.BlockSpec((1, GATE_WIDTH), const),
            pl.BlockSpec((FOURIER_GROUP, 2 * FOURIER_GROUP), const),
        ],
        out_specs=[
            pl.BlockSpec((tm, ATTN_WIDTH), row),
            pl.BlockSpec((tm, 2 * KV_WIDTH), row),
            pl.BlockSpec((tm, 2 * FOURIER_WIDTH), row),
            pl.BlockSpec((tm, GATE_WIDTH), row),
        ],
        out_shape=[
            jax.ShapeDtypeStruct((n, ATTN_WIDTH), BF16),
            jax.ShapeDtypeStruct((n, 2 * KV_WIDTH), BF16),
            jax.ShapeDtypeStruct((n, 2 * FOURIER_WIDTH), BF16),
            jax.ShapeDtypeStruct((n, GATE_WIDTH), F32),
        ],
        compiler_params=_params("parallel"),
        name="inproj",
    )(h, w_bf16, b_gate.reshape(1, GATE_WIDTH), cs)


LOG2_E = 1.4426950408889634
SCORES_AHEAD = 5


def _attn_kernel(sink_ref, q_ref, *refs, group, variant):
    kv_refs, (bias_ref, o_ref) = refs[:group + 2], refs[group + 2:]
    first_block = pl.program_id(1) * group
    kv = jnp.concatenate([r[...] for r in kv_refs], axis=0)
    low_half = lax.broadcasted_iota(jnp.int32, (kv.shape[0], LANES), 1) < HEAD_DIM

    def lane_half_operands(x):
        swapped = jnp.concatenate([x[:, HEAD_DIM:], x[:, :HEAD_DIM]], axis=1)
        zero = jnp.zeros_like(x)
        return {(kvh, half): jnp.where(low_half if half == 0 else ~low_half,
                                       x if kvh == half else swapped, zero)
                for kvh in range(N_KV_HEADS) for half in range(2)}

    k_ops = lane_half_operands(kv[:, 0:KV_WIDTH])
    v_ops = lane_half_operands(kv[:, KV_WIDTH:2 * KV_WIDTH])
    keys = lambda t: slice(t * BLOCK, (t + 3) * BLOCK)

    def scores(t, h):
        pair = h // 2
        qp = q_ref[t * BLOCK:(t + 1) * BLOCK, pair * LANES:(pair + 1) * LANES]
        return lax.dot_general(qp, k_ops[(h // GQA_GROUP, h % 2)][keys(t)], NT_DIMS,
                               preferred_element_type=F32)

    def head_out(t, h, s):
        sink = sink_ref[h] * LOG2_E
        logits = s * (HEAD_DIM ** -0.5 * LOG2_E) + bias_ref[variant(first_block + t), h]
        m = jnp.maximum(jnp.max(logits, axis=-1, keepdims=True), sink)
        p = jnp.exp2(logits - m)
        denom = jnp.sum(p, axis=-1, keepdims=True) + jnp.exp2(sink - m)
        o = jnp.dot(p.astype(BF16), v_ops[(h // GQA_GROUP, h % 2)][keys(t)], preferred_element_type=F32)
        return o / denom

    work = [(t, h) for t in range(group) for h in range(N_HEADS)]
    ahead = SCORES_AHEAD
    pending = [scores(*w) for w in work[:ahead]]
    for n, (t, h) in enumerate(work):
        s_cur = pending.pop(0)
        if n + ahead < len(work):
            pending.append(scores(*work[n + ahead]))
        o = head_out(t, h, s_cur)
        if h % 2 == 0:
            o_even = o
        else:
            pair = h // 2
            o_ref[t * BLOCK:(t + 1) * BLOCK, pair * LANES:(pair + 1) * LANES] = (o_even + o).astype(BF16)


def _attn_bias_tables(nb, lead):
    qi = np.arange(BLOCK)[:, None]
    si = np.arange(3 * BLOCK)[None, :]
    rel = np.abs(si - BLOCK - qi).astype(np.float32)
    slopes = np.array([2.0 ** (-8.0 * (h + 1) / N_HEADS) for h in range(N_HEADS)], np.float32)
    base = np.where(rel[None] <= WINDOW, -slopes[:, None, None] * rel[None] * np.float32(LOG2_E),
                    np.float32(NEG_INF))
    variants, keys, variant_of_block = [], [], []
    for i in range(nb):
        kpos = (i - 1) * BLOCK + np.arange(3 * BLOCK)
        valid = (kpos >= lead) & (kpos < nb * BLOCK)
        key = valid.tobytes()
        if key not in keys:
            keys.append(key)
            variants.append(np.where(valid[None, None, :], base, np.float32(NEG_INF)))
        variant_of_block.append(keys.index(key))
    return jnp.asarray(np.stack(variants).astype(np.float32)), variant_of_block


def _attention(q, kv, sink, bias, variant_of_block, nb):
    B, lp, _ = q.shape
    group = _pick_tile(nb, (11, 3, 1))
    interior = max(set(variant_of_block), key=variant_of_block.count)

    def variant(i):
        v = jnp.int32(interior)
        for blk, var in enumerate(variant_of_block):
            if var != interior:
                v = jnp.where(i == blk, var, v)
        return v

    kv_spec = lambda t: pl.BlockSpec((None, BLOCK, 2 * KV_WIDTH),
                                     lambda b, g: (b, jnp.clip(group * g + t - 1, 0, nb - 1), 0))
    rows = pl.BlockSpec((None, group * BLOCK, ATTN_WIDTH), lambda b, g: (b, g, 0))
    return pl.pallas_call(
        functools.partial(_attn_kernel, group=group, variant=variant),
        grid=(B, nb // group),
        in_specs=[pl.BlockSpec(memory_space=pltpu.SMEM), rows]
                 + [kv_spec(t) for t in range(group + 2)]
                 + [pl.BlockSpec(bias.shape, lambda b, g: (0, 0, 0, 0))],
        out_specs=rows,
        out_shape=jax.ShapeDtypeStruct((B, lp, ATTN_WIDTH), BF16),
        compiler_params=_params("parallel", "arbitrary"),
        name="attention",
    )(sink, q, *([kv] * (group + 2)), bias)


def _fourier_kernel(w_ref, p_ref, q_ref, o_ref, prev_ref, pe_ref, qo_ref, *, lp, lead, seq_len, tf):
    m = pl.program_id(1)
    last = pl.num_programs(1) - 1
    nblk = lp // tf
    shift = lead + 1
    kp = pe_ref.shape[0]
    half = seq_len // 2
    r = lax.broadcasted_iota(jnp.int32, (tf, tf), 0)
    c = lax.broadcasted_iota(jnp.int32, (tf, tf), 1)
    flip = (r + c == tf - 1).astype(BF16)

    @pl.when(m == 0)
    def _():
        row = lax.broadcasted_iota(jnp.int32, (lp, 1), 0)
        dc = jnp.sum(jnp.where(row >= lead, p_ref[...].astype(F32), 0.0), axis=0, keepdims=True)
        prev_ref[...] = jnp.zeros_like(prev_ref)
        prev_ref[tf - 1:tf, :] = dc * np.float32(seq_len ** -0.5)
        def mirror(x_ref, i):
            above = (jnp.dot(flip, x_ref[(nblk - i) * tf:(nblk - i + 1) * tf, :], preferred_element_type=F32)
                     if i > 0 else jnp.zeros((tf, FOURIER_WIDTH), F32))
            below = jnp.dot(flip, x_ref[(nblk - 1 - i) * tf:(nblk - i) * tf, :], preferred_element_type=F32)
            return jnp.concatenate([above[tf - shift:, :], below[:tf - shift, :]], axis=0)

        for i in range(kp // tf):
            rows = slice(i * tf, (i + 1) * tf)
            n = lax.broadcasted_iota(jnp.int32, (tf, 1), 0) + (i * tf - lead)
            paired = (n >= 1) & (n <= half - 1)
            alone = (n == 0) | (n == half)
            p_blk = p_ref[rows, :].astype(F32)
            pe_ref[rows, :] = jnp.where(paired, p_blk + mirror(p_ref, i),
                                        jnp.where(alone, p_blk, 0.0)).astype(BF16)
            qo_ref[rows, :] = jnp.where(paired, q_ref[rows, :].astype(F32) - mirror(q_ref, i), 0.0).astype(BF16)

    t1 = jnp.dot(w_ref[:, 0:kp], pe_ref[...], preferred_element_type=F32)
    t2 = jnp.dot(w_ref[:, kp:2 * kp], qo_ref[...], preferred_element_type=F32)
    fsub = w_ref.shape[0] // tf
    tail = prev_ref[tf - shift:, :]
    for t in range(fsub):
        blk = slice(t * tf, (t + 1) * tf)
        s = m * fsub + t
        direct = t1[blk] + t2[blk]
        mirrored = jnp.dot(flip, (t1[blk] - t2[blk]).astype(BF16), preferred_element_type=F32)
        shifted = jnp.concatenate([tail, direct[:tf - shift, :]], axis=0)
        tail = direct[tf - shift:, :]

        def store_pair(s=s, shifted=shifted, mirrored=mirrored):
            o_ref[pl.ds(pl.multiple_of(s * tf, tf), tf), :] = shifted.astype(BF16)
            o_ref[pl.ds(pl.multiple_of((nblk - 1 - s) * tf, tf), tf), :] = mirrored.astype(BF16)

        def store_middle(shifted=shifted, mirrored=mirrored):
            row = lax.broadcasted_iota(jnp.int32, (tf, 1), 0) + (nblk // 2) * tf
            mid = jnp.where(row <= seq_len // 2 + lead, shifted, mirrored)
            o_ref[(nblk // 2) * tf:(nblk // 2 + 1) * tf, :] = mid.astype(BF16)

        if t < fsub - 1:
            store_pair()
        else:
            pl.when(m < last)(store_pair)
            pl.when(m == last)(store_middle)
    prev_ref[tf - shift:, :] = tail


def _dft_matrix(nrows, ncols, lead, seq_len):
    w = np.float32(2.0 * math.pi / seq_len)
    k = (jnp.arange(nrows, dtype=jnp.int32) + 1)[:, None]
    j = jnp.arange(ncols // LANES, dtype=jnp.int32)[None, :]
    r = jnp.arange(LANES, dtype=jnp.int32)[None, :]
    a = ((k * (LANES * j - lead)) % seq_len).astype(F32) * w
    b = ((k * r) % seq_len).astype(F32) * w
    ca, sa, cb, sb = jnp.cos(a)[:, :, None], jnp.sin(a)[:, :, None], jnp.cos(b)[:, None, :], jnp.sin(b)[:, None, :]
    n = jnp.arange(ncols, dtype=jnp.int32) - lead
    valid = ((n >= 0) & (n <= seq_len // 2)).reshape(1, ncols // LANES, LANES)
    scale = np.float32(seq_len ** -0.5)
    wc = jnp.where(valid, (ca * cb - sa * sb) * scale, 0.0).reshape(nrows, ncols)
    ws = jnp.where(valid, -(sa * cb + ca * sb) * scale, 0.0).reshape(nrows, ncols)
    return jnp.concatenate([wc, ws], axis=1).astype(BF16)


def _fourier_steps(lp, lead, seq_len, tf):
    nblk = lp // tf
    assert seq_len % 2 == 0 and lp % tf == 0 and nblk % 2 == 1 and lead + 1 < tf
    steps = nblk // 2 + 1
    assert steps * tf > seq_len // 2 + lead
    return steps


def _fourier(w, pq, lead, seq_len, tf):
    B, lp, _ = pq.shape
    steps = _fourier_steps(lp, lead, seq_len, tf)
    kp = steps * tf
    fsub = _pick_tile(steps, (2, 1))
    return pl.pallas_call(
        functools.partial(_fourier_kernel, lp=lp, lead=lead, seq_len=seq_len, tf=tf),
        grid=(B, steps // fsub),
        in_specs=[
            pl.BlockSpec((fsub * tf, 2 * kp), lambda b, m: (m, 0)),
            pl.BlockSpec((None, lp, FOURIER_WIDTH), lambda b, m: (b, 0, 0)),
            pl.BlockSpec((None, lp, FOURIER_WIDTH), lambda b, m: (b, 0, 1)),
        ],
        out_specs=pl.BlockSpec((None, lp, FOURIER_WIDTH), lambda b, m: (b, 0, 0)),
        out_shape=jax.ShapeDtypeStruct((B, lp, FOURIER_WIDTH), BF16),
        scratch_shapes=[pltpu.VMEM((tf, FOURIER_WIDTH), F32),
                        pltpu.VMEM((kp, FOURIER_WIDTH), BF16),
                        pltpu.VMEM((kp, FOURIER_WIDTH), BF16)],
        compiler_params=_params("parallel", "arbitrary"),
        name="fourier",
    )(w, pq, pq)


def _split_bf16(x):
    hi = x.astype(BF16)
    lo = (x - hi.astype(F32)).astype(BF16)
    return hi, lo


OUTPROJ_SUB = 128


def _outproj_kernel(attn_ref, four_ref, gate_ref, h_ref, wa_ref, wf_ref, wo_ref, g_ref, b_ref,
                    wrth_ref, wrtl_ref, h1_ref, h1b_ref, affr_ref, *, alpha):
    tm = h_ref.shape[0]
    subs = [slice(i * OUTPROJ_SUB, (i + 1) * OUTPROJ_SUB) for i in range(tm // OUTPROJ_SUB)]
    ya = [jnp.dot(attn_ref[s, :], wa_ref[...], preferred_element_type=F32) for s in subs]
    yf = [jnp.dot(four_ref[s, :], wf_ref[...], preferred_element_type=F32) for s in subs]
    merged = [gate_ref[s, 0:D_MODEL] * a + gate_ref[s, D_MODEL:GATE_WIDTH] * f for s, a, f in zip(subs, ya, yf)]
    mix = [jnp.dot(mg.astype(BF16), wo_ref[...], preferred_element_type=F32) for mg in merged]
    for s, mx in zip(subs, mix):
        h1 = _ln(alpha * h_ref[s, :] + mx, g_ref[...], b_ref[...])
        h1_ref[s, :] = h1
        hi, lo = _split_bf16(h1)
        h1b_ref[s, :] = hi
        lr = (lax.dot_general(wrth_ref[...], hi, NT_DIMS, preferred_element_type=F32)
              + lax.dot_general(wrtl_ref[...], hi, NT_DIMS, preferred_element_type=F32)
              + lax.dot_general(wrth_ref[...], lo, NT_DIMS, preferred_element_type=F32))
        er = jnp.exp(lr - jnp.max(lr, axis=0, keepdims=True))
        affr_ref[:, s] = er / jnp.sum(er, axis=0, keepdims=True)


def _outproj(attn, four, gates, h, wa, wf, wo, g, b, w_router, alpha, tm):
    n = h.shape[0]
    row = lambda i: (i, 0)
    const = lambda i: (0, 0)
    wrth, wrtl = _split_bf16(w_router.T)
    return pl.pallas_call(
        functools.partial(_outproj_kernel, alpha=alpha),
        grid=(n // tm,),
        in_specs=[
            pl.BlockSpec((tm, ATTN_WIDTH), row),
            pl.BlockSpec((tm, FOURIER_WIDTH), row),
            pl.BlockSpec((tm, GATE_WIDTH), row),
            pl.BlockSpec((tm, D_MODEL), row),
            pl.BlockSpec((ATTN_WIDTH, D_MODEL), const),
            pl.BlockSpec((FOURIER_WIDTH, D_MODEL), const),
            pl.BlockSpec((D_MODEL, D_MODEL), const),
            pl.BlockSpec((1, D_MODEL), const),
            pl.BlockSpec((1, D_MODEL), const),
            pl.BlockSpec((N_EXPERTS, D_MODEL), const),
            pl.BlockSpec((N_EXPERTS, D_MODEL), const),
        ],
        out_specs=[
            pl.BlockSpec((tm, D_MODEL), row),
            pl.BlockSpec((tm, D_MODEL), row),
            pl.BlockSpec((N_EXPERTS, tm), lambda i: (0, i)),
        ],
        out_shape=[
            jax.ShapeDtypeStruct((n, D_MODEL), F32),
            jax.ShapeDtypeStruct((n, D_MODEL), BF16),
            jax.ShapeDtypeStruct((N_EXPERTS, n), F32),
        ],
        compiler_params=_params("parallel"),
        name="outproj_ln_router",
    )(attn, four, gates, h, wa, wf, wo, g.reshape(1, -1), b.reshape(1, -1), wrth, wrtl)


def _topk_kernel(aff_ref, rr_ref, rc_ref, cnt_ref, cntc_ref, *, lead, lp, cap):
    nchunk = lp // LANES
    lane = lax.broadcasted_iota(jnp.int32, (N_EXPERTS, lp), 1)
    bits = jnp.where(lane >= lead, pltpu.bitcast(aff_ref[...], jnp.int32), -1)

    def search(i, t):
        cand = t | (jnp.int32(1) << (30 - i))
        cnt = jnp.sum((bits >= cand).astype(jnp.int32), axis=-1, keepdims=True)
        return jnp.where(cnt >= cap, cand, t)

    thr = lax.fori_loop(0, 31, search, jnp.zeros((N_EXPERTS, 1), jnp.int32))
    gt = bits > thr
    eq = bits == thr
    need = cap - jnp.sum(gt.astype(jnp.int32), axis=-1, keepdims=True)

    r = lax.broadcasted_iota(jnp.int32, (LANES, LANES), 0)
    c = lax.broadcasted_iota(jnp.int32, (LANES, LANES), 1)
    upper = (r <= c).astype(BF16)
    lower = (c <= r).astype(BF16)
    ident = (c == r).astype(BF16)

    chunks = [slice(k * LANES, (k + 1) * LANES) for k in range(nchunk)]
    eqb = eq.astype(BF16)
    needf = need.astype(F32)
    pre_eq = [jnp.dot(eqb[:, sl], upper, preferred_element_type=F32) for sl in chunks]
    off = jnp.zeros((N_EXPERTS, 1), F32)
    sel_chunks = []
    for k, sl in enumerate(chunks):
        sel_chunks.append(gt[:, sl] | (eq[:, sl] & (pre_eq[k] + off <= needf)))
        off = off + pre_eq[k][:, LANES - 1:LANES]

    selb = [sel.astype(BF16) for sel in sel_chunks]
    pre_r = [jnp.dot(s, upper, preferred_element_type=F32) for s in selb]
    pre_c = [lax.dot_general(lower, s, NT_DIMS, preferred_element_type=F32) for s in selb]
    sel_c = [lax.dot_general(ident, s, NT_DIMS, preferred_element_type=F32) for s in selb]
    off_r = jnp.zeros((N_EXPERTS, 1), F32)
    off_c = jnp.zeros((1, N_EXPERTS), F32)
    cnt_lane = lax.broadcasted_iota(jnp.int32, (N_EXPERTS, LANES), 1)
    cnt = jnp.zeros((N_EXPERTS, LANES), F32)
    for k, sl in enumerate(chunks):
        rr_ref[:, sl] = jnp.where(sel_chunks[k], pre_r[k] + (off_r - 1.0), -1.0).astype(jnp.int32)
        off_r = off_r + pre_r[k][:, LANES - 1:LANES]
        cnt = jnp.where(cnt_lane == k + 1, off_r, cnt)
        rc_ref[sl, :] = jnp.where(sel_c[k] > 0.5, pre_c[k] + (off_c - 1.0), -1.0).astype(jnp.int32)
        off_c = off_c + pre_c[k][LANES - 1:LANES, :]
        cntc_ref[k + 1:k + 2, :] = off_c.astype(jnp.int32)
    cnt_ref[...] = cnt.astype(jnp.int32)
    cntc_ref[0:1, :] = jnp.zeros((1, N_EXPERTS), jnp.int32)
    pad_rows = cntc_ref.shape[0] - nchunk - 1
    if pad_rows:
        cntc_ref[nchunk + 1:, :] = jnp.zeros((pad_rows, N_EXPERTS), jnp.int32)


def _topk(aff_r, B, lp, lead, cap):
    n = B * lp
    nchunk = lp // LANES
    assert nchunk + 1 <= LANES
    nb1 = -(-(nchunk + 1) // SUBLANES) * SUBLANES
    return pl.pallas_call(
        functools.partial(_topk_kernel, lead=lead, lp=lp, cap=cap),
        grid=(B,),
        in_specs=[pl.BlockSpec((N_EXPERTS, lp), lambda b: (0, b))],
        out_specs=[
            pl.BlockSpec((N_EXPERTS, lp), lambda b: (0, b)),
            pl.BlockSpec((lp, N_EXPERTS), lambda b: (b, 0)),
            pl.BlockSpec((None, N_EXPERTS, LANES), lambda b: (b, 0, 0)),
            pl.BlockSpec((None, nb1, N_EXPERTS), lambda b: (b, 0, 0)),
        ],
        out_shape=[
            jax.ShapeDtypeStruct((N_EXPERTS, n), jnp.int32),
            jax.ShapeDtypeStruct((n, N_EXPERTS), jnp.int32),
            jax.ShapeDtypeStruct((B, N_EXPERTS, LANES), jnp.int32),
            jax.ShapeDtypeStruct((B, nb1, N_EXPERTS), jnp.int32),
        ],
        compiler_params=_params("parallel"),
        name="topk_select",
    )(aff_r)


SLOT_ALIGN = 16


def _window_constants(win):
    wide = N_EXPERTS * win
    expand = np.zeros((N_EXPERTS, wide), np.float32)
    for e in range(N_EXPERTS):
        expand[e, e * win:(e + 1) * win] = 1.0
    return expand, (np.arange(wide) % win).astype(np.float32)


def _gather_kernel(cnt_ref, cntv_ref, rank_ref, aff_ref, h_ref, o_ref, g_ref,
                   *, slots, ch, win, nchunk):
    bi, c = pl.program_id(0), pl.program_id(1)
    k0 = c * ch
    tt = ch * LANES
    max_start = slots - win

    @pl.when(c == 0)
    def _():
        o_ref[...] = jnp.zeros_like(o_ref)
        g_ref[...] = jnp.zeros_like(g_ref)

    lane = lax.broadcasted_iota(jnp.int32, cntv_ref.shape, 1)
    lo_col = jnp.sum(jnp.where(lane == k0, cntv_ref[...], 0), axis=-1, keepdims=True)
    w_col = jnp.minimum(lo_col & -SLOT_ALIGN, max_start)
    rank = rank_ref[...]
    aff = aff_ref[...]
    rel = rank - w_col
    in_win = (rank >= 0) & (rel >= 0) & (rel < win)
    rel = jnp.where(in_win, rel, -1)
    row = lax.broadcasted_iota(jnp.int32, (win, tt), 0)
    hit_all = jnp.concatenate([rel[e:e + 1, :] == row for e in range(N_EXPERTS)], axis=0)
    rows = h_ref[...]
    xw = jnp.dot(hit_all.astype(BF16), rows, preferred_element_type=F32)

    def add_rows(e, start, hit, vals):
        dst = pl.ds(pl.multiple_of(start, SLOT_ALIGN), win)
        o_ref[e, dst, :] = o_ref[e, dst, :] + vals.astype(BF16)
        g_ref[e, dst, :] += jnp.sum(jnp.where(hit, aff[e:e + 1, :], 0.0), axis=-1, keepdims=True)

    for e in range(N_EXPERTS):
        base = (bi * N_EXPERTS + e) * (nchunk + 1) + k0
        w = jnp.minimum(cnt_ref[base] & -SLOT_ALIGN, max_start)
        add_rows(e, w, hit_all[e * win:(e + 1) * win, :], xw[e * win:(e + 1) * win, :])
        hi = cnt_ref[base + ch]

        @pl.when(hi > w + win)
        def _(e=e, w=w, hi=hi):
            rk = rank[e:e + 1, :]

            def extra(i, carry):
                w2 = w + (i + 1) * win
                w2c = jnp.minimum(w2, max_start)
                hit = ((rk - w2c) == row) & (rk >= w2)
                add_rows(e, w2c, hit, jnp.dot(hit.astype(BF16), rows, preferred_element_type=F32))
                return carry

            lax.fori_loop(0, (hi - w - 1) // win, extra, 0)


def _gather(cnt, cntv, rank_rows, aff_rows, h1b, B, lp, slots, ch, win):
    nchunk = lp // LANES
    tt = ch * LANES
    nt = nchunk // ch
    assert slots % SLOT_ALIGN == 0 and win % SLOT_ALIGN == 0 and win <= slots
    per_seq = lambda bi, c, cnt: (bi, 0, 0, 0)
    return pl.pallas_call(
        functools.partial(_gather_kernel, slots=slots, ch=ch, win=win, nchunk=nchunk),
        grid_spec=pltpu.PrefetchScalarGridSpec(
            num_scalar_prefetch=1,
            grid=(B, nt),
            in_specs=[
                pl.BlockSpec((None, N_EXPERTS, LANES), lambda bi, c, cnt: (bi, 0, 0)),
                pl.BlockSpec((N_EXPERTS, tt), lambda bi, c, cnt: (0, bi * nt + c)),
                pl.BlockSpec((N_EXPERTS, tt), lambda bi, c, cnt: (0, bi * nt + c)),
                pl.BlockSpec((tt, D_MODEL), lambda bi, c, cnt: (bi * nt + c, 0)),
            ],
            out_specs=[pl.BlockSpec((None, N_EXPERTS, slots, D_MODEL), per_seq),
                       pl.BlockSpec((None, N_EXPERTS, slots, 1), per_seq)],
        ),
        out_shape=[jax.ShapeDtypeStruct((B, N_EXPERTS, slots, D_MODEL), BF16),
                   jax.ShapeDtypeStruct((B, N_EXPERTS, slots, 1), F32)],
        compiler_params=_params("parallel", "arbitrary"),
        name="moe_gather",
    )(cnt, cntv, rank_rows, aff_rows, h1b)


def _ffn_up_kernel(x_ref, wg_ref, wu_ref, o_ref, wgb_ref, wub_ref):
    @pl.when(pl.program_id(1) == 0)
    def _():
        wgb_ref[...] = wg_ref[...].astype(BF16)
        wub_ref[...] = wu_ref[...].astype(BF16)

    nseq, slots, width = x_ref.shape
    x = x_ref[...].reshape(nseq * slots, width)
    a = jnp.dot(x, wgb_ref[...], preferred_element_type=F32)
    u = jnp.dot(x, wub_ref[...], preferred_element_type=F32)
    o_ref[...] = (a * jax.nn.sigmoid(a) * u).astype(BF16).reshape(o_ref.shape)


def _ffn_down_kernel(x_ref, g_ref, wd_ref, o_ref, wdb_ref):
    @pl.when(pl.program_id(1) == 0)
    def _():
        wdb_ref[...] = wd_ref[...].astype(BF16)

    nseq, slots, width = x_ref.shape
    y = jnp.dot(x_ref[...].reshape(nseq * slots, width), wdb_ref[...], preferred_element_type=F32)
    y = y * g_ref[...].reshape(nseq * slots, 1)
    o_ref[...] = y.astype(BF16).reshape(o_ref.shape)


def _ffn(xg, gates, wg, wu, wd, layer):
    B, _, slots, _ = xg.shape
    n_up = _pick_tile(B, (2, 1))
    n_down = _pick_tile(B, (4, 2, 1))
    acts = lambda nseq, width: pl.BlockSpec((nseq, None, slots, width), lambda e, b: (b, e, 0, 0))
    weight = lambda rows, cols: pl.BlockSpec((None, None, rows, cols), lambda e, b: (layer, e, 0, 0))
    mid = pl.pallas_call(
        _ffn_up_kernel,
        grid=(N_EXPERTS, B // n_up),
        in_specs=[acts(n_up, D_MODEL), weight(D_MODEL, D_FF_EXPERT), weight(D_MODEL, D_FF_EXPERT)],
        out_specs=acts(n_up, D_FF_EXPERT),
        out_shape=jax.ShapeDtypeStruct((B, N_EXPERTS, slots, D_FF_EXPERT), BF16),
        scratch_shapes=[pltpu.VMEM((D_MODEL, D_FF_EXPERT), BF16), pltpu.VMEM((D_MODEL, D_FF_EXPERT), BF16)],
        compiler_params=_params("arbitrary", "arbitrary"),
        name="moe_ffn_up",
    )(xg, wg, wu)
    return pl.pallas_call(
        _ffn_down_kernel,
        grid=(N_EXPERTS, B // n_down),
        in_specs=[acts(n_down, D_FF_EXPERT), acts(n_down, 1), weight(D_FF_EXPERT, D_MODEL)],
        out_specs=acts(n_down, D_MODEL),
        out_shape=jax.ShapeDtypeStruct(xg.shape, BF16),
        scratch_shapes=[pltpu.VMEM((D_FF_EXPERT, D_MODEL), BF16)],
        compiler_params=_params("arbitrary", "arbitrary"),
        name="moe_ffn_down",
    )(mid, gates, wd)


def _combine_kernel(cnt_ref, cntc_ref, rank_ref, y_ref, h_ref, g_ref, b_ref, expand_ref, rpat_ref,
                    o_ref, acc_ref, *, slots, ch, win, nchunk, alpha):
    bi, c = pl.program_id(0), pl.program_id(1)
    k0 = c * ch
    tt = ch * LANES
    max_start = slots - win
    rank = rank_ref[...]
    lo_row = cntc_ref[pl.ds(k0, 1), :]
    w_row = jnp.minimum(lo_row & -SLOT_ALIGN, max_start)
    rel = rank - w_row
    in_win = (rank >= 0) & (rel >= 0) & (rel < win)
    relb = jnp.where(in_win, rel, -1).astype(F32).astype(BF16)
    rel_wide = jnp.dot(relb, expand_ref[...], preferred_element_type=F32)
    onehot = (rel_wide == rpat_ref[...]).astype(BF16)

    starts, windows = [], []
    for e in range(N_EXPERTS):
        lo = cnt_ref[(bi * N_EXPERTS + e) * (nchunk + 1) + k0]
        w = jnp.minimum(lo & -SLOT_ALIGN, max_start)
        starts.append(w)
        windows.append(y_ref[e, pl.ds(pl.multiple_of(w, SLOT_ALIGN), win), :])
    acc_ref[...] = jnp.dot(onehot, jnp.concatenate(windows, axis=0), preferred_element_type=F32)

    lane = lax.broadcasted_iota(jnp.int32, rank.shape, 1)
    col = lax.broadcasted_iota(jnp.int32, (tt, win), 1)
    for e in range(N_EXPERTS):
        hi = cnt_ref[(bi * N_EXPERTS + e) * (nchunk + 1) + k0 + ch]

        @pl.when(hi > starts[e] + win)
        def _(e=e, hi=hi):
            rk = jnp.sum(jnp.where(lane == e, rank, 0), axis=-1, keepdims=True)

            def extra(i, carry):
                w2 = starts[e] + (i + 1) * win
                w2c = jnp.minimum(w2, max_start)
                rows = y_ref[e, pl.ds(pl.multiple_of(w2c, SLOT_ALIGN), win), :]
                hit = ((rk - w2c) == col) & (rk >= w2)
                acc_ref[...] += jnp.dot(hit.astype(BF16), rows, preferred_element_type=F32)
                return carry

            lax.fori_loop(0, (hi - starts[e] - 1) // win, extra, 0)

    o_ref[...] = _ln(alpha * h_ref[...] + acc_ref[...], g_ref[...], b_ref[...])


def _combine(cnt, cntc, rank_c, y, h1, g, b, B, lp, alpha, ch, win):
    n = B * lp
    slots = y.shape[2]
    nchunk = lp // LANES
    tt = ch * LANES
    nt = nchunk // ch
    assert slots % SLOT_ALIGN == 0 and win % SLOT_ALIGN == 0 and win <= slots
    wide = N_EXPERTS * win
    expand, pos = _window_constants(win)
    tok = lambda bi, c, cnt: (bi * nt + c, 0)
    const = lambda bi, c, cnt: (0, 0)
    return pl.pallas_call(
        functools.partial(_combine_kernel, slots=slots, ch=ch, win=win, nchunk=nchunk, alpha=alpha),
        grid_spec=pltpu.PrefetchScalarGridSpec(
            num_scalar_prefetch=1,
            grid=(B, nt),
            in_specs=[
                pl.BlockSpec((None, cntc.shape[1], N_EXPERTS), lambda bi, c, cnt: (bi, 0, 0)),
                pl.BlockSpec((tt, N_EXPERTS), tok),
                pl.BlockSpec((None, N_EXPERTS, slots, D_MODEL), lambda bi, c, cnt: (bi, 0, 0, 0)),
                pl.BlockSpec((tt, D_MODEL), tok),
                pl.BlockSpec((1, D_MODEL), const),
                pl.BlockSpec((1, D_MODEL), const),
                pl.BlockSpec((N_EXPERTS, wide), const),
                pl.BlockSpec((1, wide), const),
            ],
            out_specs=pl.BlockSpec((tt, D_MODEL), tok),
            scratch_shapes=[pltpu.VMEM((tt, D_MODEL), F32)],
        ),
        out_shape=jax.ShapeDtypeStruct((n, D_MODEL), F32),
        compiler_params=_params("parallel", "arbitrary"),
        name="moe_combine_ln",
    )(cnt, cntc, rank_c, y, h1, g.reshape(1, -1), b.reshape(1, -1),
      jnp.asarray(expand, dtype=BF16), jnp.asarray(pos.reshape(1, wide)))


def kernel(x, meta, ln0_g, ln0_b, w_in, b_gate, sink, w_attn_o, w_four_o, w_out, ln1_g, ln1_b,
           w_router, w_e_gate, w_e_up, w_e_down, ln2_g, ln2_b):
    B, seq, d = x.shape
    depth = w_in.shape[0]
    assert d == D_MODEL and meta.shape == (N_META, D_MODEL)
    assert seq % BLOCK == 0 and N_META % SUBLANES == 0 and N_META <= BLOCK
    L = seq + N_META
    nb = -(-L // BLOCK)
    lp = nb * BLOCK
    lead = lp - L
    cap = CAPACITY_FACTOR * L // N_EXPERTS
    slots = -(-cap // SLOT_ALIGN) * SLOT_ALIGN
    alpha = float((2 * depth) ** 0.25)
    n = B * lp
    nchunk = lp // LANES
    tm = _pick_tile(n, (1024, 512, 256, 128))
    tf = _pick_tile(lp, (384, 128))
    ch = _pick_tile(nchunk, (3, 2, 1))
    win = min(slots, -(-(ch * LANES * cap // L * 4 // 3 + SLOT_ALIGN) // SLOT_ALIGN) * SLOT_ALIGN)

    gi = np.arange(FOURIER_GROUP)
    ang = 2.0 * np.pi * ((gi[:, None] * gi[None, :]) % FOURIER_GROUP) / FOURIER_GROUP
    cs = jnp.asarray(np.concatenate([np.cos(ang), np.sin(ang)], axis=1) * FOURIER_GROUP ** -0.5, dtype=BF16)
    dft_rows = _fourier_steps(lp, lead, L, tf) * tf
    wdft = _dft_matrix(dft_rows, dft_rows, lead, L)
    bias, variant_of_block = _attn_bias_tables(nb, lead)

    h = _embed(x, meta, ln0_g, ln0_b, nb, lead).reshape(n, D_MODEL)
    for l in range(depth):
        q, kv, pq, gates = _inproj(h, w_in[l].astype(BF16), b_gate[l], cs, tm)
        attn = _attention(q.reshape(B, lp, -1), kv.reshape(B, lp, -1), sink[l], bias, variant_of_block, nb)
        four = _fourier(wdft, pq.reshape(B, lp, -1), lead, L, tf)
        h1, h1b, aff_r = _outproj(
            attn.reshape(n, -1), four.reshape(n, -1), gates, h,
            w_attn_o[l].astype(BF16), w_four_o[l].astype(BF16), w_out[l].astype(BF16),
            ln1_g[l], ln1_b[l], w_router[l], alpha, tm)
        rank_r, rank_c, cntv, cntc = _topk(aff_r, B, lp, lead, cap)
        cnt = cntv[:, :, :nchunk + 1].reshape(-1)
        xg, gsel = _gather(cnt, cntv, rank_r, aff_r, h1b, B, lp, slots, ch, win)
        y = _ffn(xg, gsel, w_e_gate, w_e_up, w_e_down, l)
        h = _combine(cnt, cntc, rank_c, y, h1, ln2_g[l], ln2_b[l], B, lp, alpha, ch, win)
    return h.reshape(B, lp, D_MODEL)[:, lead + N_META:]
```

```python
import functools
import math

import numpy as np
import jax
import jax.numpy as jnp
from jax import lax
from jax.experimental import pallas as pl
from jax.experimental.pallas import tpu as pltpu

D_MODEL = 1024
N_META = 16
N_HEADS = 8
N_KV_HEADS = 2
HEAD_DIM = 64
GQA_GROUP = N_HEADS // N_KV_HEADS
ATTN_WIDTH = N_HEADS * HEAD_DIM
KV_WIDTH = N_KV_HEADS * HEAD_DIM
WINDOW = 128
BLOCK = 128
N_FOURIER_GROUPS = 4
FOURIER_GROUP = 128
FOURIER_WIDTH = N_FOURIER_GROUPS * FOURIER_GROUP
N_BRANCHES = 2
GATE_WIDTH = N_BRANCHES * D_MODEL
N_EXPERTS = 16
CAPACITY_FACTOR = 2
D_FF_EXPERT = 1536
LN_EPS = 1e-5
NEG_INF = -1e30
Q_END = ATTN_WIDTH
K_END = Q_END + KV_WIDTH
V_END = K_END + KV_WIDTH
F_END = V_END + FOURIER_WIDTH
IN_WIDTH = F_END + GATE_WIDTH

LANES = 128
SUBLANES = 8
VMEM_LIMIT_BYTES = 56 * 1024 * 1024

F32 = jnp.float32
BF16 = jnp.bfloat16
NT_DIMS = (((1,), (1,)), ((), ()))


def _pick_tile(n, candidates):
    for c in candidates:
        if n % c == 0:
            return c
    raise ValueError(f"no tile in {candidates} divides {n}")


def _params(*sem):
    return pltpu.CompilerParams(dimension_semantics=sem, vmem_limit_bytes=VMEM_LIMIT_BYTES)


def _ln(x, g, b):
    mu = jnp.mean(x, axis=-1, keepdims=True)
    xc = x - mu
    var = jnp.mean(xc * xc, axis=-1, keepdims=True)
    return xc * lax.rsqrt(var + LN_EPS) * g + b


def _embed_kernel(*refs, lead, group):
    x_refs, (meta_ref, g_ref, b_ref, o_ref) = refs[:group], refs[group:]
    for k in range(1, group):
        o_ref[k * BLOCK:(k + 1) * BLOCK, :] = _ln(x_refs[k][...], g_ref[...], b_ref[...])

    @pl.when(pl.program_id(1) == 0)
    def _():
        o_ref[0:lead, :] = jnp.zeros((lead, D_MODEL), F32)
        o_ref[lead:BLOCK, :] = _ln(meta_ref[...], g_ref[...], b_ref[...])

    @pl.when(pl.program_id(1) > 0)
    def _():
        o_ref[0:BLOCK, :] = _ln(x_refs[0][...], g_ref[...], b_ref[...])


def _embed(x, meta, g, b, nb, lead):
    B = x.shape[0]
    group = _pick_tile(nb, (11, 3, 1))
    x_spec = lambda k: pl.BlockSpec((None, BLOCK, D_MODEL),
                                    lambda bi, j: (bi, jnp.maximum(group * j + k - 1, 0), 0))
    const = lambda bi, j: (0, 0)
    return pl.pallas_call(
        functools.partial(_embed_kernel, lead=lead, group=group),
        grid=(B, nb // group),
        in_specs=[x_spec(k) for k in range(group)] + [
            pl.BlockSpec((N_META, D_MODEL), const),
            pl.BlockSpec((1, D_MODEL), const),
            pl.BlockSpec((1, D_MODEL), const),
        ],
        out_specs=pl.BlockSpec((None, group * BLOCK, D_MODEL), lambda bi, j: (bi, j, 0)),
        out_shape=jax.ShapeDtypeStruct((B, nb * BLOCK, D_MODEL), F32),
        compiler_params=_params("parallel", "arbitrary"),
        name="embed_ln",
    )(*([x] * group), meta, g.reshape(1, -1), b.reshape(1, -1))


def _inproj_kernel(h_ref, w_ref, bg_ref, cs_ref, q_ref, kv_ref, pq_ref, gate_ref):
    hb = h_ref[...].astype(BF16)
    q_ref[...] = jnp.dot(hb, w_ref[:, 0:Q_END], preferred_element_type=F32).astype(BF16)
    kv_ref[...] = jnp.dot(hb, w_ref[:, Q_END:V_END], preferred_element_type=F32).astype(BF16)
    uf = jnp.dot(hb, w_ref[:, V_END:F_END], preferred_element_type=F32).astype(BF16)
    for g in range(N_FOURIER_GROUPS):
        lo = g * FOURIER_GROUP
        pq = jnp.dot(uf[:, lo:lo + FOURIER_GROUP], cs_ref[...], preferred_element_type=F32)
        pq_ref[:, lo:lo + FOURIER_GROUP] = pq[:, 0:FOURIER_GROUP].astype(BF16)
        pq_ref[:, FOURIER_WIDTH + lo:FOURIER_WIDTH + lo + FOURIER_GROUP] = (
            pq[:, FOURIER_GROUP:2 * FOURIER_GROUP].astype(BF16))
    chunk = FOURIER_WIDTH
    for c in range(GATE_WIDTH // chunk):
        lo = c * chunk
        ug = jnp.dot(hb, w_ref[:, F_END + lo:F_END + lo + chunk], preferred_element_type=F32)
        gate_ref[:, lo:lo + chunk] = jax.nn.sigmoid(ug + bg_ref[:, lo:lo + chunk])


def _inproj(h, w_bf16, b_gate, cs, tm):
    n = h.shape[0]
    row = lambda i: (i, 0)
    const = lambda i: (0, 0)
    return pl.pallas_call(
        _inproj_kernel,
        grid=(n // tm,),
        in_specs=[
            pl.BlockSpec((tm, D_MODEL), row),
            pl.BlockSpec((D_MODEL, IN_WIDTH), const),
            pl.BlockSpec((1, GATE_WIDTH), const),
            pl.BlockSpec((FOURIER_GROUP, 2 * FOURIER_GROUP), const),
        ],
        out_specs=[
            pl.BlockSpec((tm, ATTN_WIDTH), row),
            pl.BlockSpec((tm, 2 * KV_WIDTH), row),
            pl.BlockSpec((tm, 2 * FOURIER_WIDTH), row),
            pl.BlockSpec((tm, GATE_WIDTH), row),
        ],
        out_shape=[
            jax.ShapeDtypeStruct((n, ATTN_WIDTH), BF16),
            jax.ShapeDtypeStruct((n, 2 * KV_WIDTH), BF16),
            jax.ShapeDtypeStruct((n, 2 * FOURIER_WIDTH), BF16),
            jax.ShapeDtypeStruct((n, GATE_WIDTH), F32),
        ],
        compiler_params=_params("parallel"),
        name="inproj",
    )(h, w_bf16, b_gate.reshape(1, GATE_WIDTH), cs)


LOG2_E = 1.4426950408889634
SCORES_AHEAD = 3


def _attn_kernel(sink_ref, q_ref, *refs, group, variant):
    kv_refs, (bias_ref, o_ref) = refs[:group + 2], refs[group + 2:]
    first_block = pl.program_id(1) * group
    kv = jnp.concatenate([r[...] for r in kv_refs], axis=0)
    low_half = lax.broadcasted_iota(jnp.int32, (kv.shape[0], LANES), 1) < HEAD_DIM

    def lane_half_operands(x):
        swapped = jnp.concatenate([x[:, HEAD_DIM:], x[:, :HEAD_DIM]], axis=1)
        zero = jnp.zeros_like(x)
        return {(kvh, half): jnp.where(low_half if half == 0 else ~low_half,
                                       x if kvh == half else swapped, zero)
                for kvh in range(N_KV_HEADS) for half in range(2)}

    k_ops = lane_half_operands(kv[:, 0:KV_WIDTH])
    v_ops = lane_half_operands(kv[:, KV_WIDTH:2 * KV_WIDTH])
    keys = lambda t: slice(t * BLOCK, (t + 3) * BLOCK)

    def scores(t, h):
        pair = h // 2
        qp = q_ref[t * BLOCK:(t + 1) * BLOCK, pair * LANES:(pair + 1) * LANES]
        return lax.dot_general(qp, k_ops[(h // GQA_GROUP, h % 2)][keys(t)], NT_DIMS,
                               preferred_element_type=F32)

    def head_out(t, h, s):
        sink = sink_ref[h] * LOG2_E
        logits = s * (HEAD_DIM ** -0.5 * LOG2_E) + bias_ref[variant(first_block + t), h]
        m = jnp.maximum(jnp.max(logits, axis=-1, keepdims=True), sink)
        p = jnp.exp2(logits - m)
        denom = jnp.sum(p, axis=-1, keepdims=True) + jnp.exp2(sink - m)
        o = jnp.dot(p.astype(BF16), v_ops[(h // GQA_GROUP, h % 2)][keys(t)], preferred_element_type=F32)
        return o / denom

    work = [(t, h) for t in range(group) for h in range(N_HEADS)]
    ahead = SCORES_AHEAD
    pending = [scores(*w) for w in work[:ahead]]
    for n, (t, h) in enumerate(work):
        s_cur = pending.pop(0)
        if n + ahead < len(work):
            pending.append(scores(*work[n + ahead]))
        o = head_out(t, h, s_cur)
        if h % 2 == 0:
            o_even = o
        else:
            pair = h // 2
            o_ref[t * BLOCK:(t + 1) * BLOCK, pair * LANES:(pair + 1) * LANES] = (o_even + o).astype(BF16)


def _attn_bias_tables(nb, lead):
    qi = np.arange(BLOCK)[:, None]
    si = np.arange(3 * BLOCK)[None, :]
    rel = np.abs(si - BLOCK - qi).astype(np.float32)
    slopes = np.array([2.0 ** (-8.0 * (h + 1) / N_HEADS) for h in range(N_HEADS)], np.float32)
    base = np.where(rel[None] <= WINDOW, -slopes[:, None, None] * rel[None] * np.float32(LOG2_E),
                    np.float32(NEG_INF))
    variants, keys, variant_of_block = [], [], []
    for i in range(nb):
        kpos = (i - 1) * BLOCK + np.arange(3 * BLOCK)
        valid = (kpos >= lead) & (kpos < nb * BLOCK)
        key = valid.tobytes()
        if key not in keys:
            keys.append(key)
            variants.append(np.where(valid[None, None, :], base, np.float32(NEG_INF)))
        variant_of_block.append(keys.index(key))
    return jnp.asarray(np.stack(variants).astype(np.float32)), variant_of_block


def _attention(q, kv, sink, bias, variant_of_block, nb):
    B, lp, _ = q.shape
    group = _pick_tile(nb, (11, 3, 1))
    interior = max(set(variant_of_block), key=variant_of_block.count)

    def variant(i):
        v = jnp.int32(interior)
        for blk, var in enumerate(variant_of_block):
            if var != interior:
                v = jnp.where(i == blk, var, v)
        return v

    kv_spec = lambda t: pl.BlockSpec((None, BLOCK, 2 * KV_WIDTH),
                                     lambda b, g: (b, jnp.clip(group * g + t - 1, 0, nb - 1), 0))
    rows = pl.BlockSpec((None, group * BLOCK, ATTN_WIDTH), lambda b, g: (b, g, 0))
    return pl.pallas_call(
        functools.partial(_attn_kernel, group=group, variant=variant),
        grid=(B, nb // group),
        in_specs=[pl.BlockSpec(memory_space=pltpu.SMEM), rows]
                 + [kv_spec(t) for t in range(group + 2)]
                 + [pl.BlockSpec(bias.shape, lambda b, g: (0, 0, 0, 0))],
        out_specs=rows,
        out_shape=jax.ShapeDtypeStruct((B, lp, ATTN_WIDTH), BF16),
        compiler_params=_params("parallel", "arbitrary"),
        name="attention",
    )(sink, q, *([kv] * (group + 2)), bias)


def _fourier_kernel(w_ref, p_ref, q_ref, o_ref, prev_ref, pe_ref, qo_ref, *, lp, lead, seq_len, tf):
    m = pl.program_id(1)
    last = pl.num_programs(1) - 1
    nblk = lp // tf
    shift = lead + 1
    kp = pe_ref.shape[0]
    half = seq_len // 2
    r = lax.broadcasted_iota(jnp.int32, (tf, tf), 0)
    c = lax.broadcasted_iota(jnp.int32, (tf, tf), 1)
    flip = (r + c == tf - 1).astype(BF16)

    @pl.when(m == 0)
    def _():
        row = lax.broadcasted_iota(jnp.int32, (lp, 1), 0)
        dc = jnp.sum(jnp.where(row >= lead, p_ref[...].astype(F32), 0.0), axis=0, keepdims=True)
        prev_ref[...] = jnp.zeros_like(prev_ref)
        prev_ref[tf - 1:tf, :] = dc * np.float32(seq_len ** -0.5)
        def mirror(x_ref, i):
            above = (jnp.dot(flip, x_ref[(nblk - i) * tf:(nblk - i + 1) * tf, :], preferred_element_type=F32)
                     if i > 0 else jnp.zeros((tf, FOURIER_WIDTH), F32))
            below = jnp.dot(flip, x_ref[(nblk - 1 - i) * tf:(nblk - i) * tf, :], preferred_element_type=F32)
            return jnp.concatenate([above[tf - shift:, :], below[:tf - shift, :]], axis=0)

        for i in range(kp // tf):
            rows = slice(i * tf, (i + 1) * tf)
            n = lax.broadcasted_iota(jnp.int32, (tf, 1), 0) + (i * tf - lead)
            paired = (n >= 1) & (n <= half - 1)
            alone = (n == 0) | (n == half)
            p_blk = p_ref[rows, :].astype(F32)
            pe_ref[rows, :] = jnp.where(paired, p_blk + mirror(p_ref, i),
                                        jnp.where(alone, p_blk, 0.0)).astype(BF16)
            qo_ref[rows, :] = jnp.where(paired, q_ref[rows, :].astype(F32) - mirror(q_ref, i), 0.0).astype(BF16)

    t1 = jnp.dot(w_ref[:, 0:kp], pe_ref[...], preferred_element_type=F32)
    t2 = jnp.dot(w_ref[:, kp:2 * kp], qo_ref[...], preferred_element_type=F32)
    fsub = w_ref.shape[0] // tf
    tail = prev_ref[tf - shift:, :]
    for t in range(fsub):
        blk = slice(t * tf, (t + 1) * tf)
        s = m * fsub + t
        direct = t1[blk] + t2[blk]
        mirrored = jnp.dot(flip, (t1[blk] - t2[blk]).astype(BF16), preferred_element_type=F32)
        shifted = jnp.concatenate([tail, direct[:tf - shift, :]], axis=0)
        tail = direct[tf - shift:, :]

        def store_pair(s=s, shifted=shifted, mirrored=mirrored):
            o_ref[pl.ds(pl.multiple_of(s * tf, tf), tf), :] = shifted.astype(BF16)
            o_ref[pl.ds(pl.multiple_of((nblk - 1 - s) * tf, tf), tf), :] = mirrored.astype(BF16)

        def store_middle(shifted=shifted, mirrored=mirrored):
            row = lax.broadcasted_iota(jnp.int32, (tf, 1), 0) + (nblk // 2) * tf
            mid = jnp.where(row <= seq_len // 2 + lead, shifted, mirrored)
            o_ref[(nblk // 2) * tf:(nblk // 2 + 1) * tf, :] = mid.astype(BF16)

        if t < fsub - 1:
            store_pair()
        else:
            pl.when(m < last)(store_pair)
            pl.when(m == last)(store_middle)
    prev_ref[tf - shift:, :] = tail


def _dft_matrix(nrows, ncols, lead, seq_len):
    w = np.float32(2.0 * math.pi / seq_len)
    k = (jnp.arange(nrows, dtype=jnp.int32) + 1)[:, None]
    j = jnp.arange(ncols // LANES, dtype=jnp.int32)[None, :]
    r = jnp.arange(LANES, dtype=jnp.int32)[None, :]
    a = ((k * (LANES * j - lead)) % seq_len).astype(F32) * w
    b = ((k * r) % seq_len).astype(F32) * w
    ca, sa, cb, sb = jnp.cos(a)[:, :, None], jnp.sin(a)[:, :, None], jnp.cos(b)[:, None, :], jnp.sin(b)[:, None, :]
    n = jnp.arange(ncols, dtype=jnp.int32) - lead
    valid = ((n >= 0) & (n <= seq_len // 2)).reshape(1, ncols // LANES, LANES)
    scale = np.float32(seq_len ** -0.5)
    wc = jnp.where(valid, (ca * cb - sa * sb) * scale, 0.0).reshape(nrows, ncols)
    ws = jnp.where(valid, -(sa * cb + ca * sb) * scale, 0.0).reshape(nrows, ncols)
    return jnp.concatenate([wc, ws], axis=1).astype(BF16)


def _fourier_steps(lp, lead, seq_len, tf):
    nblk = lp // tf
    assert seq_len % 2 == 0 and lp % tf == 0 and nblk % 2 == 1 and lead + 1 < tf
    steps = nblk // 2 + 1
    assert steps * tf > seq_len // 2 + lead
    return steps


def _fourier(w, pq, lead, seq_len, tf):
    B, lp, _ = pq.shape
    steps = _fourier_steps(lp, lead, seq_len, tf)
    kp = steps * tf
    fsub = _pick_tile(steps, (2, 1))
    return pl.pallas_call(
        functools.partial(_fourier_kernel, lp=lp, lead=lead, seq_len=seq_len, tf=tf),
        grid=(B, steps // fsub),
        in_specs=[
            pl.BlockSpec((fsub * tf, 2 * kp), lambda b, m: (m, 0)),
            pl.BlockSpec((None, lp, FOURIER_WIDTH), lambda b, m: (b, 0, 0)),
            pl.BlockSpec((None, lp, FOURIER_WIDTH), lambda b, m: (b, 0, 1)),
        ],
        out_specs=pl.BlockSpec((None, lp, FOURIER_WIDTH), lambda b, m: (b, 0, 0)),
        out_shape=jax.ShapeDtypeStruct((B, lp, FOURIER_WIDTH), BF16),
        scratch_shapes=[pltpu.VMEM((tf, FOURIER_WIDTH), F32),
                        pltpu.VMEM((kp, FOURIER_WIDTH), BF16),
                        pltpu.VMEM((kp, FOURIER_WIDTH), BF16)],
        compiler_params=_params("parallel", "arbitrary"),
        name="fourier",
    )(w, pq, pq)


def _split_bf16(x):
    hi = x.astype(BF16)
    lo = (x - hi.astype(F32)).astype(BF16)
    return hi, lo


OUTPROJ_SUB = 128


def _outproj_kernel(attn_ref, four_ref, gate_ref, h_ref, wa_ref, wf_ref, wo_ref, g_ref, b_ref,
                    wrth_ref, wrtl_ref, h1_ref, h1b_ref, affr_ref, *, alpha):
    tm = h_ref.shape[0]
    subs = [slice(i * OUTPROJ_SUB, (i + 1) * OUTPROJ_SUB) for i in range(tm // OUTPROJ_SUB)]
    ya = [jnp.dot(attn_ref[s, :], wa_ref[...], preferred_element_type=F32) for s in subs]
    yf = [jnp.dot(four_ref[s, :], wf_ref[...], preferred_element_type=F32) for s in subs]
    merged = [gate_ref[s, 0:D_MODEL] * a + gate_ref[s, D_MODEL:GATE_WIDTH] * f for s, a, f in zip(subs, ya, yf)]
    mix = [jnp.dot(mg.astype(BF16), wo_ref[...], preferred_element_type=F32) for mg in merged]
    for s, mx in zip(subs, mix):
        h1 = _ln(alpha * h_ref[s, :] + mx, g_ref[...], b_ref[...])
        h1_ref[s, :] = h1
        hi, lo = _split_bf16(h1)
        h1b_ref[s, :] = hi
        lr = (lax.dot_general(wrth_ref[...], hi, NT_DIMS, preferred_element_type=F32)
              + lax.dot_general(wrtl_ref[...], hi, NT_DIMS, preferred_element_type=F32)
              + lax.dot_general(wrth_ref[...], lo, NT_DIMS, preferred_element_type=F32))
        er = jnp.exp(lr - jnp.max(lr, axis=0, keepdims=True))
        affr_ref[:, s] = er / jnp.sum(er, axis=0, keepdims=True)


def _outproj(attn, four, gates, h, wa, wf, wo, g, b, w_router, alpha, tm):
    n = h.shape[0]
    row = lambda i: (i, 0)
    const = lambda i: (0, 0)
    wrth, wrtl = _split_bf16(w_router.T)
    return pl.pallas_call(
        functools.partial(_outproj_kernel, alpha=alpha),
        grid=(n // tm,),
        in_specs=[
            pl.BlockSpec((tm, ATTN_WIDTH), row),
            pl.BlockSpec((tm, FOURIER_WIDTH), row),
            pl.BlockSpec((tm, GATE_WIDTH), row),
            pl.BlockSpec((tm, D_MODEL), row),
            pl.BlockSpec((ATTN_WIDTH, D_MODEL), const),
            pl.BlockSpec((FOURIER_WIDTH, D_MODEL), const),
            pl.BlockSpec((D_MODEL, D_MODEL), const),
            pl.BlockSpec((1, D_MODEL), const),
            pl.BlockSpec((1, D_MODEL), const),
            pl.BlockSpec((N_EXPERTS, D_MODEL), const),
            pl.BlockSpec((N_EXPERTS, D_MODEL), const),
        ],
        out_specs=[
            pl.BlockSpec((tm, D_MODEL), row),
            pl.BlockSpec((tm, D_MODEL), row),
            pl.BlockSpec((N_EXPERTS, tm), lambda i: (0, i)),
        ],
        out_shape=[
            jax.ShapeDtypeStruct((n, D_MODEL), F32),
            jax.ShapeDtypeStruct((n, D_MODEL), BF16),
            jax.ShapeDtypeStruct((N_EXPERTS, n), F32),
        ],
        compiler_params=_params("parallel"),
        name="outproj_ln_router",
    )(attn, four, gates, h, wa, wf, wo, g.reshape(1, -1), b.reshape(1, -1), wrth, wrtl)


def _topk_kernel(aff_ref, rr_ref, rc_ref, cnt_ref, cntc_ref, *, lead, lp, cap):
    nchunk = lp // LANES
    lane = lax.broadcasted_iota(jnp.int32, (N_EXPERTS, lp), 1)
    bits = jnp.where(lane >= lead, pltpu.bitcast(aff_ref[...], jnp.int32), -1)

    def search(i, t):
        cand = t | (jnp.int32(1) << (30 - i))
        cnt = jnp.sum((bits >= cand).astype(jnp.int32), axis=-1, keepdims=True)
        return jnp.where(cnt >= cap, cand, t)

    thr = lax.fori_loop(0, 31, search, jnp.zeros((N_EXPERTS, 1), jnp.int32))
    gt = bits > thr
    eq = bits == thr
    need = cap - jnp.sum(gt.astype(jnp.int32), axis=-1, keepdims=True)

    r = lax.broadcasted_iota(jnp.int32, (LANES, LANES), 0)
    c = lax.broadcasted_iota(jnp.int32, (LANES, LANES), 1)
    upper = (r <= c).astype(BF16)
    lower = (c <= r).astype(BF16)
    ident = (c == r).astype(BF16)

    chunks = [slice(k * LANES, (k + 1) * LANES) for k in range(nchunk)]
    eqb = eq.astype(BF16)
    needf = need.astype(F32)
    pre_eq = [jnp.dot(eqb[:, sl], upper, preferred_element_type=F32) for sl in chunks]
    off = jnp.zeros((N_EXPERTS, 1), F32)
    sel_chunks = []
    for k, sl in enumerate(chunks):
        sel_chunks.append(gt[:, sl] | (eq[:, sl] & (pre_eq[k] + off <= needf)))
        off = off + pre_eq[k][:, LANES - 1:LANES]

    selb = [sel.astype(BF16) for sel in sel_chunks]
    pre_r = [jnp.dot(s, upper, preferred_element_type=F32) for s in selb]
    pre_c = [lax.dot_general(lower, s, NT_DIMS, preferred_element_type=F32) for s in selb]
    sel_c = [lax.dot_general(ident, s, NT_DIMS, preferred_element_type=F32) for s in selb]
    off_r = jnp.zeros((N_EXPERTS, 1), F32)
    off_c = jnp.zeros((1, N_EXPERTS), F32)
    cnt_lane = lax.broadcasted_iota(jnp.int32, (N_EXPERTS, LANES), 1)
    cnt = jnp.zeros((N_EXPERTS, LANES), F32)
    for k, sl in enumerate(chunks):
        rr_ref[:, sl] = jnp.where(sel_chunks[k], pre_r[k] + (off_r - 1.0), -1.0).astype(jnp.int32)
        off_r = off_r + pre_r[k][:, LANES - 1:LANES]
        cnt = jnp.where(cnt_lane == k + 1, off_r, cnt)
        rc_ref[sl, :] = jnp.where(sel_c[k] > 0.5, pre_c[k] + (off_c - 1.0), -1.0).astype(jnp.int32)
        off_c = off_c + pre_c[k][LANES - 1:LANES, :]
        cntc_ref[k + 1:k + 2, :] = off_c.astype(jnp.int32)
    cnt_ref[...] = cnt.astype(jnp.int32)
    cntc_ref[0:1, :] = jnp.zeros((1, N_EXPERTS), jnp.int32)
    pad_rows = cntc_ref.shape[0] - nchunk - 1
    if pad_rows:
        cntc_ref[nchunk + 1:, :] = jnp.zeros((pad_rows, N_EXPERTS), jnp.int32)


def _topk(aff_r, B, lp, lead, cap):
    n = B * lp
    nchunk = lp // LANES
    assert nchunk + 1 <= LANES
    nb1 = -(-(nchunk + 1) // SUBLANES) * SUBLANES
    return pl.pallas_call(
        functools.partial(_topk_kernel, lead=lead, lp=lp, cap=cap),
        grid=(B,),
        in_specs=[pl.BlockSpec((N_EXPERTS, lp), lambda b: (0, b))],
        out_specs=[
            pl.BlockSpec((N_EXPERTS, lp), lambda b: (0, b)),
            pl.BlockSpec((lp, N_EXPERTS), lambda b: (b, 0)),
            pl.BlockSpec((None, N_EXPERTS, LANES), lambda b: (b, 0, 0)),
            pl.BlockSpec((None, nb1, N_EXPERTS), lambda b: (b, 0, 0)),
        ],
        out_shape=[
            jax.ShapeDtypeStruct((N_EXPERTS, n), jnp.int32),
            jax.ShapeDtypeStruct((n, N_EXPERTS), jnp.int32),
            jax.ShapeDtypeStruct((B, N_EXPERTS, LANES), jnp.int32),
            jax.ShapeDtypeStruct((B, nb1, N_EXPERTS), jnp.int32),
        ],
        compiler_params=_params("parallel"),
        name="topk_select",
    )(aff_r)


SLOT_ALIGN = 16


def _window_constants(win):
    wide = N_EXPERTS * win
    expand = np.zeros((N_EXPERTS, wide), np.float32)
    for e in range(N_EXPERTS):
        expand[e, e * win:(e + 1) * win] = 1.0
    return expand, (np.arange(wide) % win).astype(np.float32)


def _gather_kernel(cnt_ref, cntv_ref, rank_ref, aff_ref, h_ref, o_ref, g_ref,
                   *, slots, ch, win, nchunk):
    bi, c = pl.program_id(0), pl.program_id(1)
    k0 = c * ch
    tt = ch * LANES
    max_start = slots - win

    @pl.when(c == 0)
    def _():
        o_ref[...] = jnp.zeros_like(o_ref)
        g_ref[...] = jnp.zeros_like(g_ref)

    lane = lax.broadcasted_iota(jnp.int32, cntv_ref.shape, 1)
    lo_col = jnp.sum(jnp.where(lane == k0, cntv_ref[...], 0), axis=-1, keepdims=True)
    w_col = jnp.minimum(lo_col & -SLOT_ALIGN, max_start)
    rank = rank_ref[...]
    aff = aff_ref[...]
    rel = rank - w_col
    in_win = (rank >= 0) & (rel >= 0) & (rel < win)
    rel = jnp.where(in_win, rel, -1)
    row = lax.broadcasted_iota(jnp.int32, (win, tt), 0)
    hit_all = jnp.concatenate([rel[e:e + 1, :] == row for e in range(N_EXPERTS)], axis=0)
    rows = h_ref[...]
    xw = jnp.dot(hit_all.astype(BF16), rows, preferred_element_type=F32)

    def add_rows(e, start, hit, vals):
        dst = pl.ds(pl.multiple_of(start, SLOT_ALIGN), win)
        o_ref[e, dst, :] = o_ref[e, dst, :] + vals.astype(BF16)
        g_ref[e, dst, :] += jnp.sum(jnp.where(hit, aff[e:e + 1, :], 0.0), axis=-1, keepdims=True)

    for e in range(N_EXPERTS):
        base = (bi * N_EXPERTS + e) * (nchunk + 1) + k0
        w = jnp.minimum(cnt_ref[base] & -SLOT_ALIGN, max_start)
        add_rows(e, w, hit_all[e * win:(e + 1) * win, :], xw[e * win:(e + 1) * win, :])
        hi = cnt_ref[base + ch]

        @pl.when(hi > w + win)
        def _(e=e, w=w, hi=hi):
            rk = rank[e:e + 1, :]

            def extra(i, carry):
                w2 = w + (i + 1) * win
                w2c = jnp.minimum(w2, max_start)
                hit = ((rk - w2c) == row) & (rk >= w2)
                add_rows(e, w2c, hit, jnp.dot(hit.astype(BF16), rows, preferred_element_type=F32))
                return carry

            lax.fori_loop(0, (hi - w - 1) // win, extra, 0)


def _gather(cnt, cntv, rank_rows, aff_rows, h1b, B, lp, slots, ch, win):
    nchunk = lp // LANES
    tt = ch * LANES
    nt = nchunk // ch
    assert slots % SLOT_ALIGN == 0 and win % SLOT_ALIGN == 0 and win <= slots
    per_seq = lambda bi, c, cnt: (bi, 0, 0, 0)
    return pl.pallas_call(
        functools.partial(_gather_kernel, slots=slots, ch=ch, win=win, nchunk=nchunk),
        grid_spec=pltpu.PrefetchScalarGridSpec(
            num_scalar_prefetch=1,
            grid=(B, nt),
            in_specs=[
                pl.BlockSpec((None, N_EXPERTS, LANES), lambda bi, c, cnt: (bi, 0, 0)),
                pl.BlockSpec((N_EXPERTS, tt), lambda bi, c, cnt: (0, bi * nt + c)),
                pl.BlockSpec((N_EXPERTS, tt), lambda bi, c, cnt: (0, bi * nt + c)),
                pl.BlockSpec((tt, D_MODEL), lambda bi, c, cnt: (bi * nt + c, 0)),
            ],
            out_specs=[pl.BlockSpec((None, N_EXPERTS, slots, D_MODEL), per_seq),
                       pl.BlockSpec((None, N_EXPERTS, slots, 1), per_seq)],
        ),
        out_shape=[jax.ShapeDtypeStruct((B, N_EXPERTS, slots, D_MODEL), BF16),
                   jax.ShapeDtypeStruct((B, N_EXPERTS, slots, 1), F32)],
        compiler_params=_params("parallel", "arbitrary"),
        name="moe_gather",
    )(cnt, cntv, rank_rows, aff_rows, h1b)


def _ffn_up_kernel(x_ref, wg_ref, wu_ref, o_ref, wgb_ref, wub_ref):
    @pl.when(pl.program_id(1) == 0)
    def _():
        wgb_ref[...] = wg_ref[...].astype(BF16)
        wub_ref[...] = wu_ref[...].astype(BF16)

    nseq, slots, width = x_ref.shape
    x = x_ref[...].reshape(nseq * slots, width)
    a = jnp.dot(x, wgb_ref[...], preferred_element_type=F32)
    u = jnp.dot(x, wub_ref[...], preferred_element_type=F32)
    o_ref[...] = (a * jax.nn.sigmoid(a) * u).astype(BF16).reshape(o_ref.shape)


def _ffn_down_kernel(x_ref, g_ref, wd_ref, o_ref, wdb_ref):
    @pl.when(pl.program_id(1) == 0)
    def _():
        wdb_ref[...] = wd_ref[...].astype(BF16)

    nseq, slots, width = x_ref.shape
    y = jnp.dot(x_ref[...].reshape(nseq * slots, width), wdb_ref[...], preferred_element_type=F32)
    y = y * g_ref[...].reshape(nseq * slots, 1)
    o_ref[...] = y.astype(BF16).reshape(o_ref.shape)


def _ffn(xg, gates, wg, wu, wd, layer):
    B, _, slots, _ = xg.shape
    n_up = _pick_tile(B, (2, 1))
    n_down = _pick_tile(B, (4, 2, 1))
    acts = lambda nseq, width: pl.BlockSpec((nseq, None, slots, width), lambda e, b: (b, e, 0, 0))
    weight = lambda rows, cols: pl.BlockSpec((None, None, rows, cols), lambda e, b: (layer, e, 0, 0))
    mid = pl.pallas_call(
        _ffn_up_kernel,
        grid=(N_EXPERTS, B // n_up),
        in_specs=[acts(n_up, D_MODEL), weight(D_MODEL, D_FF_EXPERT), weight(D_MODEL, D_FF_EXPERT)],
        out_specs=acts(n_up, D_FF_EXPERT),
        out_shape=jax.ShapeDtypeStruct((B, N_EXPERTS, slots, D_FF_EXPERT), BF16),
        scratch_shapes=[pltpu.VMEM((D_MODEL, D_FF_EXPERT), BF16), pltpu.VMEM((D_MODEL, D_FF_EXPERT), BF16)],
        compiler_params=_params("arbitrary", "arbitrary"),
        name="moe_ffn_up",
    )(xg, wg, wu)
    return pl.pallas_call(
        _ffn_down_kernel,
        grid=(N_EXPERTS, B // n_down),
        in_specs=[acts(n_down, D_FF_EXPERT), acts(n_down, 1), weight(D_FF_EXPERT, D_MODEL)],
        out_specs=acts(n_down, D_MODEL),
        out_shape=jax.ShapeDtypeStruct(xg.shape, BF16),
        scratch_shapes=[pltpu.VMEM((D_FF_EXPERT, D_MODEL), BF16)],
        compiler_params=_params("arbitrary", "arbitrary"),
        name="moe_ffn_down",
    )(mid, gates, wd)


def _combine_kernel(cnt_ref, cntc_ref, rank_ref, y_ref, h_ref, g_ref, b_ref, expand_ref, rpat_ref,
                    o_ref, acc_ref, *, slots, ch, win, nchunk, alpha):
    bi, c = pl.program_id(0), pl.program_id(1)
    k0 = c * ch
    tt = ch * LANES
    max_start = slots - win
    rank = rank_ref[...]
    lo_row = cntc_ref[pl.ds(k0, 1), :]
    w_row = jnp.minimum(lo_row & -SLOT_ALIGN, max_start)
    rel = rank - w_row
    in_win = (rank >= 0) & (rel >= 0) & (rel < win)
    relb = jnp.where(in_win, rel, -1).astype(F32).astype(BF16)
    rel_wide = jnp.dot(relb, expand_ref[...], preferred_element_type=F32)
    onehot = (rel_wide == rpat_ref[...]).astype(BF16)

    starts, windows = [], []
    for e in range(N_EXPERTS):
        lo = cnt_ref[(bi * N_EXPERTS + e) * (nchunk + 1) + k0]
        w = jnp.minimum(lo & -SLOT_ALIGN, max_start)
        starts.append(w)
        windows.append(y_ref[e, pl.ds(pl.multiple_of(w, SLOT_ALIGN), win), :])
    acc_ref[...] = jnp.dot(onehot, jnp.concatenate(windows, axis=0), preferred_element_type=F32)

    lane = lax.broadcasted_iota(jnp.int32, rank.shape, 1)
    col = lax.broadcasted_iota(jnp.int32, (tt, win), 1)
    for e in range(N_EXPERTS):
        hi = cnt_ref[(bi * N_EXPERTS + e) * (nchunk + 1) + k0 + ch]

        @pl.when(hi > starts[e] + win)
        def _(e=e, hi=hi):
            rk = jnp.sum(jnp.where(lane == e, rank, 0), axis=-1, keepdims=True)

            def extra(i, carry):
                w2 = starts[e] + (i + 1) * win
                w2c = jnp.minimum(w2, max_start)
                rows = y_ref[e, pl.ds(pl.multiple_of(w2c, SLOT_ALIGN), win), :]
                hit = ((rk - w2c) == col) & (rk >= w2)
                acc_ref[...] += jnp.dot(hit.astype(BF16), rows, preferred_element_type=F32)
                return carry

            lax.fori_loop(0, (hi - starts[e] - 1) // win, extra, 0)

    o_ref[...] = _ln(alpha * h_ref[...] + acc_ref[...], g_ref[...], b_ref[...])


def _combine(cnt, cntc, rank_c, y, h1, g, b, B, lp, alpha, ch, win):
    n = B * lp
    slots = y.shape[2]
    nchunk = lp // LANES
    tt = ch * LANES
    nt = nchunk // ch
    assert slots % SLOT_ALIGN == 0 and win % SLOT_ALIGN == 0 and win <= slots
    wide = N_EXPERTS * win
    expand, pos = _window_constants(win)
    tok = lambda bi, c, cnt: (bi * nt + c, 0)
    const = lambda bi, c, cnt: (0, 0)
    return pl.pallas_call(
        functools.partial(_combine_kernel, slots=slots, ch=ch, win=win, nchunk=nchunk, alpha=alpha),
        grid_spec=pltpu.PrefetchScalarGridSpec(
            num_scalar_prefetch=1,
            grid=(B, nt),
            in_specs=[
                pl.BlockSpec((None, cntc.shape[1], N_EXPERTS), lambda bi, c, cnt: (bi, 0, 0)),
                pl.BlockSpec((tt, N_EXPERTS), tok),
                pl.BlockSpec((None, N_EXPERTS, slots, D_MODEL), lambda bi, c, cnt: (bi, 0, 0, 0)),
                pl.BlockSpec((tt, D_MODEL), tok),
                pl.BlockSpec((1, D_MODEL), const),
                pl.BlockSpec((1, D_MODEL), const),
                pl.BlockSpec((N_EXPERTS, wide), const),
                pl.BlockSpec((1, wide), const),
            ],
            out_specs=pl.BlockSpec((tt, D_MODEL), tok),
            scratch_shapes=[pltpu.VMEM((tt, D_MODEL), F32)],
        ),
        out_shape=jax.ShapeDtypeStruct((n, D_MODEL), F32),
        compiler_params=_params("parallel", "arbitrary"),
        name="moe_combine_ln",
    )(cnt, cntc, rank_c, y, h1, g.reshape(1, -1), b.reshape(1, -1),
      jnp.asarray(expand, dtype=BF16), jnp.asarray(pos.reshape(1, wide)))


def kernel(x, meta, ln0_g, ln0_b, w_in, b_gate, sink, w_attn_o, w_four_o, w_out, ln1_g, ln1_b,
           w_router, w_e_gate, w_e_up, w_e_down, ln2_g, ln2_b):
    B, seq, d = x.shape
    depth = w_in.shape[0]
    assert d == D_MODEL and meta.shape == (N_META, D_MODEL)
    assert seq % BLOCK == 0 and N_META % SUBLANES == 0 and N_META <= BLOCK
    L = seq + N_META
    nb = -(-L // BLOCK)
    lp = nb * BLOCK
    lead = lp - L
    cap = CAPACITY_FACTOR * L // N_EXPERTS
    slots = -(-cap // SLOT_ALIGN) * SLOT_ALIGN
    alpha = float((2 * depth) ** 0.25)
    n = B * lp
    nchunk = lp // LANES
    tm = _pick_tile(n, (1024, 512, 256, 128))
    tf = _pick_tile(lp, (384, 128))
    ch = _pick_tile(nchunk, (3, 2, 1))
    win = min(slots, -(-(ch * LANES * cap // L * 4 // 3 + SLOT_ALIGN) // SLOT_ALIGN) * SLOT_ALIGN)

    gi = np.arange(FOURIER_GROUP)
    ang = 2.0 * np.pi * ((gi[:, None] * gi[None, :]) % FOURIER_GROUP) / FOURIER_GROUP
    cs = jnp.asarray(np.concatenate([np.cos(ang), np.sin(ang)], axis=1) * FOURIER_GROUP ** -0.5, dtype=BF16)
    dft_rows = _fourier_steps(lp, lead, L, tf) * tf
    wdft = _dft_matrix(dft_rows, dft_rows, lead, L)
    bias, variant_of_block = _attn_bias_tables(nb, lead)

    h = _embed(x, meta, ln0_g, ln0_b, nb, lead).reshape(n, D_MODEL)
    for l in range(depth):
        q, kv, pq, gates = _inproj(h, w_in[l].astype(BF16), b_gate[l], cs, tm)
        attn = _attention(q.reshape(B, lp, -1), kv.reshape(B, lp, -1), sink[l], bias, variant_of_block, nb)
        four = _fourier(wdft, pq.reshape(B, lp, -1), lead, L, tf)
        h1, h1b, aff_r = _outproj(
            attn.reshape(n, -1), four.reshape(n, -1), gates, h,
            w_attn_o[l].astype(BF16), w_four_o[l].astype(BF16), w_out[l].astype(BF16),
            ln1_g[l], ln1_b[l], w_router[l], alpha, tm)
        rank_r, rank_c, cntv, cntc = _topk(aff_r, B, lp, lead, cap)
        cnt = cntv[:, :, :nchunk + 1].reshape(-1)
        xg, gsel = _gather(cnt, cntv, rank_r, aff_r, h1b, B, lp, slots, ch, win)
        y = _ffn(xg, gsel, w_e_gate, w_e_up, w_e_down, l)
        h = _combine(cnt, cntc, rank_c, y, h1, ln2_g[l], ln2_b[l], B, lp, alpha, ch, win)
    return h.reshape(B, lp, D_MODEL)[:, lead + N_META:]
```

```python
import functools
import math

import numpy as np
import jax
import jax.numpy as jnp
from jax import lax
from jax.experimental import pallas as pl
from jax.experimental.pallas import tpu as pltpu

D_MODEL = 1024
N_META = 16
N_HEADS = 8
N_KV_HEADS = 2
HEAD_DIM = 64
GQA_GROUP = N_HEADS // N_KV_HEADS
ATTN_WIDTH = N_HEADS * HEAD_DIM
KV_WIDTH = N_KV_HEADS * HEAD_DIM
WINDOW = 128
BLOCK = 128
N_FOURIER_GROUPS = 4
FOURIER_GROUP = 128
FOURIER_WIDTH = N_FOURIER_GROUPS * FOURIER_GROUP
N_BRANCHES = 2
GATE_WIDTH = N_BRANCHES * D_MODEL
N_EXPERTS = 16
CAPACITY_FACTOR = 2
D_FF_EXPERT = 1536
LN_EPS = 1e-5
NEG_INF = -1e30
Q_END = ATTN_WIDTH
K_END = Q_END + KV_WIDTH
V_END = K_END + KV_WIDTH
F_END = V_END + FOURIER_WIDTH
IN_WIDTH = F_END + GATE_WIDTH

LANES = 128
SUBLANES = 8
VMEM_LIMIT_BYTES = 56 * 1024 * 1024

F32 = jnp.float32
BF16 = jnp.bfloat16
NT_DIMS = (((1,), (1,)), ((), ()))


def _pick_tile(n, candidates):
    for c in candidates:
        if n % c == 0:
            return c
    raise ValueError(f"no tile in {candidates} divides {n}")


def _params(*sem):
    return pltpu.CompilerParams(dimension_semantics=sem, vmem_limit_bytes=VMEM_LIMIT_BYTES)


def _ln(x, g, b):
    mu = jnp.mean(x, axis=-1, keepdims=True)
    xc = x - mu
    var = jnp.mean(xc * xc, axis=-1, keepdims=True)
    return xc * lax.rsqrt(var + LN_EPS) * g + b


def _embed_kernel(*refs, lead, group):
    x_refs, (meta_ref, g_ref, b_ref, o_ref) = refs[:group], refs[group:]
    for k in range(1, group):
        o_ref[k * BLOCK:(k + 1) * BLOCK, :] = _ln(x_refs[k][...], g_ref[...], b_ref[...])

    @pl.when(pl.program_id(1) == 0)
    def _():
        o_ref[0:lead, :] = jnp.zeros((lead, D_MODEL), F32)
        o_ref[lead:BLOCK, :] = _ln(meta_ref[...], g_ref[...], b_ref[...])

    @pl.when(pl.program_id(1) > 0)
    def _():
        o_ref[0:BLOCK, :] = _ln(x_refs[0][...], g_ref[...], b_ref[...])


def _embed(x, meta, g, b, nb, lead):
    B = x.shape[0]
    group = _pick_tile(nb, (11, 3, 1))
    x_spec = lambda k: pl.BlockSpec((None, BLOCK, D_MODEL),
                                    lambda bi, j: (bi, jnp.maximum(group * j + k - 1, 0), 0))
    const = lambda bi, j: (0, 0)
    return pl.pallas_call(
        functools.partial(_embed_kernel, lead=lead, group=group),
        grid=(B, nb // group),
        in_specs=[x_spec(k) for k in range(group)] + [
            pl.BlockSpec((N_META, D_MODEL), const),
            pl.BlockSpec((1, D_MODEL), const),
            pl.BlockSpec((1, D_MODEL), const),
        ],
        out_specs=pl.BlockSpec((None, group * BLOCK, D_MODEL), lambda bi, j: (bi, j, 0)),
        out_shape=jax.ShapeDtypeStruct((B, nb * BLOCK, D_MODEL), F32),
        compiler_params=_params("parallel", "arbitrary"),
        name="embed_ln",
    )(*([x] * group), meta, g.reshape(1, -1), b.reshape(1, -1))


def _inproj_kernel(h_ref, w_ref, bg_ref, cs_ref, q_ref, kv_ref, pq_ref, gate_ref):
    hb = h_ref[...].astype(BF16)
    q_ref[...] = jnp.dot(hb, w_ref[:, 0:Q_END], preferred_element_type=F32).astype(BF16)
    kv_ref[...] = jnp.dot(hb, w_ref[:, Q_END:V_END], preferred_element_type=F32).astype(BF16)
    uf = jnp.dot(hb, w_ref[:, V_END:F_END], preferred_element_type=F32).astype(BF16)
    for g in range(N_FOURIER_GROUPS):
        lo = g * FOURIER_GROUP
        pq = jnp.dot(uf[:, lo:lo + FOURIER_GROUP], cs_ref[...], preferred_element_type=F32)
        pq_ref[:, lo:lo + FOURIER_GROUP] = pq[:, 0:FOURIER_GROUP].astype(BF16)
        pq_ref[:, FOURIER_WIDTH + lo:FOURIER_WIDTH + lo + FOURIER_GROUP] = (
            pq[:, FOURIER_GROUP:2 * FOURIER_GROUP].astype(BF16))
    chunk = FOURIER_WIDTH
    for c in range(GATE_WIDTH // chunk):
        lo = c * chunk
        ug = jnp.dot(hb, w_ref[:, F_END + lo:F_END + lo + chunk], preferred_element_type=F32)
        gate_ref[:, lo:lo + chunk] = jax.nn.sigmoid(ug + bg_ref[:, lo:lo + chunk])


def _inproj(h, w_bf16, b_gate, cs, tm):
    n = h.shape[0]
    row = lambda i: (i, 0)
    const = lambda i: (0, 0)
    return pl.pallas_call(
        _inproj_kernel,
        grid=(n // tm,),
        in_specs=[
            pl.BlockSpec((tm, D_MODEL), row),
            pl.BlockSpec((D_MODEL, IN_WIDTH), const),
            pl.BlockSpec((1, GATE_WIDTH), const),
            pl.BlockSpec((FOURIER_GROUP, 2 * FOURIER_GROUP), const),
        ],
        out_specs=[
            pl.BlockSpec((tm, ATTN_WIDTH), row),
            pl.BlockSpec((tm, 2 * KV_WIDTH), row),
            pl.BlockSpec((tm, 2 * FOURIER_WIDTH), row),
            pl.BlockSpec((tm, GATE_WIDTH), row),
        ],
        out_shape=[
            jax.ShapeDtypeStruct((n, ATTN_WIDTH), BF16),
            jax.ShapeDtypeStruct((n, 2 * KV_WIDTH), BF16),
            jax.ShapeDtypeStruct((n, 2 * FOURIER_WIDTH), BF16),
            jax.ShapeDtypeStruct((n, GATE_WIDTH), F32),
        ],
        compiler_params=_params("parallel"),
        name="inproj",
    )(h, w_bf16, b_gate.reshape(1, GATE_WIDTH), cs)


LOG2_E = 1.4426950408889634
SCORES_AHEAD = 5


def _attn_kernel(sink_ref, q_ref, *refs, group, variant):
    kv_refs, (bias_ref, o_ref) = refs[:group + 2], refs[group + 2:]
    first_block = pl.program_id(1) * group
    kv = jnp.concatenate([r[...] for r in kv_refs], axis=0)
    low_half = lax.broadcasted_iota(jnp.int32, (kv.shape[0], LANES), 1) < HEAD_DIM

    def lane_half_operands(x):
        swapped = jnp.concatenate([x[:, HEAD_DIM:], x[:, :HEAD_DIM]], axis=1)
        zero = jnp.zeros_like(x)
        return {(kvh, half): jnp.where(low_half if half == 0 else ~low_half,
                                       x if kvh == half else swapped, zero)
                for kvh in range(N_KV_HEADS) for half in range(2)}

    k_ops = lane_half_operands(kv[:, 0:KV_WIDTH])
    v_ops = lane_half_operands(kv[:, KV_WIDTH:2 * KV_WIDTH])
    keys = lambda t: slice(t * BLOCK, (t + 3) * BLOCK)

    def scores(t, h):
        pair = h // 2
        qp = q_ref[t * BLOCK:(t + 1) * BLOCK, pair * LANES:(pair + 1) * LANES]
        return lax.dot_general(qp, k_ops[(h // GQA_GROUP, h % 2)][keys(t)], NT_DIMS,
                               preferred_element_type=F32)

    def head_out(t, h, s):
        sink = sink_ref[h] * LOG2_E
        logits = s * (HEAD_DIM ** -0.5 * LOG2_E) + bias_ref[variant(first_block + t), h]
        m = jnp.maximum(jnp.max(logits, axis=-1, keepdims=True), sink)
        p = jnp.exp2(logits - m)
        denom = jnp.sum(p, axis=-1, keepdims=True) + jnp.exp2(sink - m)
        o = jnp.dot(p.astype(BF16), v_ops[(h // GQA_GROUP, h % 2)][keys(t)], preferred_element_type=F32)
        return o / denom

    work = [(t, h) for t in range(group) for h in range(N_HEADS)]
    ahead = SCORES_AHEAD
    pending = [scores(*w) for w in work[:ahead]]
    for n, (t, h) in enumerate(work):
        s_cur = pending.pop(0)
        if n + ahead < len(work):
            pending.append(scores(*work[n + ahead]))
        o = head_out(t, h, s_cur)
        if h % 2 == 0:
            o_even = o
        else:
            pair = h // 2
            o_ref[t * BLOCK:(t + 1) * BLOCK, pair * LANES:(pair + 1) * LANES] = (o_even + o).astype(BF16)


def _attn_bias_tables(nb, lead):
    qi = np.arange(BLOCK)[:, None]
    si = np.arange(3 * BLOCK)[None, :]
    rel = np.abs(si - BLOCK - qi).astype(np.float32)
    slopes = np.array([2.0 ** (-8.0 * (h + 1) / N_HEADS) for h in range(N_HEADS)], np.float32)
    base = np.where(rel[None] <= WINDOW, -slopes[:, None, None] * rel[None] * np.float32(LOG2_E),
                    np.float32(NEG_INF))
    variants, keys, variant_of_block = [], [], []
    for i in range(nb):
        kpos = (i - 1) * BLOCK + np.arange(3 * BLOCK)
        valid = (kpos >= lead) & (kpos < nb * BLOCK)
        key = valid.tobytes()
        if key not in keys:
            keys.append(key)
            variants.append(np.where(valid[None, None, :], base, np.float32(NEG_INF)))
        variant_of_block.append(keys.index(key))
    return jnp.asarray(np.stack(variants).astype(np.float32)), variant_of_block


def _attention(q, kv, sink, bias, variant_of_block, nb):
    B, lp, _ = q.shape
    group = _pick_tile(nb, (11, 3, 1))
    interior = max(set(variant_of_block), key=variant_of_block.count)

    def variant(i):
        v = jnp.int32(interior)
        for blk, var in enumerate(variant_of_block):
            if var != interior:
                v = jnp.where(i == blk, var, v)
        return v

    kv_spec = lambda t: pl.BlockSpec((None, BLOCK, 2 * KV_WIDTH),
                                     lambda b, g: (b, jnp.clip(group * g + t - 1, 0, nb - 1), 0))
    rows = pl.BlockSpec((None, group * BLOCK, ATTN_WIDTH), lambda b, g: (b, g, 0))
    return pl.pallas_call(
        functools.partial(_attn_kernel, group=group, variant=variant),
        grid=(B, nb // group),
        in_specs=[pl.BlockSpec(memory_space=pltpu.SMEM), rows]
                 + [kv_spec(t) for t in range(group + 2)]
                 + [pl.BlockSpec(bias.shape, lambda b, g: (0, 0, 0, 0))],
        out_specs=rows,
        out_shape=jax.ShapeDtypeStruct((B, lp, ATTN_WIDTH), BF16),
        compiler_params=_params("parallel", "arbitrary"),
        name="attention",
    )(sink, q, *([kv] * (group + 2)), bias)


def _fourier_kernel(w_ref, p_ref, q_ref, o_ref, prev_ref, pe_ref, qo_ref, *, lp, lead, seq_len, tf):
    m = pl.program_id(1)
    last = pl.num_programs(1) - 1
    nblk = lp // tf
    shift = lead + 1
    kp = pe_ref.shape[0]
    half = seq_len // 2
    r = lax.broadcasted_iota(jnp.int32, (tf, tf), 0)
    c = lax.broadcasted_iota(jnp.int32, (tf, tf), 1)
    flip = (r + c == tf - 1).astype(BF16)

    @pl.when(m == 0)
    def _():
        row = lax.broadcasted_iota(jnp.int32, (lp, 1), 0)
        dc = jnp.sum(jnp.where(row >= lead, p_ref[...].astype(F32), 0.0), axis=0, keepdims=True)
        prev_ref[...] = jnp.zeros_like(prev_ref)
        prev_ref[tf - 1:tf, :] = dc * np.float32(seq_len ** -0.5)
        def mirror(x_ref, i):
            above = (jnp.dot(flip, x_ref[(nblk - i) * tf:(nblk - i + 1) * tf, :], preferred_element_type=F32)
                     if i > 0 else jnp.zeros((tf, FOURIER_WIDTH), F32))
            below = jnp.dot(flip, x_ref[(nblk - 1 - i) * tf:(nblk - i) * tf, :], preferred_element_type=F32)
            return jnp.concatenate([above[tf - shift:, :], below[:tf - shift, :]], axis=0)

        for i in range(kp // tf):
            rows = slice(i * tf, (i + 1) * tf)
            n = lax.broadcasted_iota(jnp.int32, (tf, 1), 0) + (i * tf - lead)
            paired = (n >= 1) & (n <= half - 1)
            alone = (n == 0) | (n == half)
            p_blk = p_ref[rows, :].astype(F32)
            pe_ref[rows, :] = jnp.where(paired, p_blk + mirror(p_ref, i),
                                        jnp.where(alone, p_blk, 0.0)).astype(BF16)
            qo_ref[rows, :] = jnp.where(paired, q_ref[rows, :].astype(F32) - mirror(q_ref, i), 0.0).astype(BF16)

    t1 = jnp.dot(w_ref[:, 0:kp], pe_ref[...], preferred_element_type=F32)
    t2 = jnp.dot(w_ref[:, kp:2 * kp], qo_ref[...], preferred_element_type=F32)
    fsub = w_ref.shape[0] // tf
    tail = prev_ref[tf - shift:, :]
    for t in range(fsub):
        blk = slice(t * tf, (t + 1) * tf)
        s = m * fsub + t
        direct = t1[blk] + t2[blk]
        mirrored = jnp.dot(flip, (t1[blk] - t2[blk]).astype(BF16), preferred_element_type=F32)
        shifted = jnp.concatenate([tail, direct[:tf - shift, :]], axis=0)
        tail = direct[tf - shift:, :]

        def store_pair(s=s, shifted=shifted, mirrored=mirrored):
            o_ref[pl.ds(pl.multiple_of(s * tf, tf), tf), :] = shifted.astype(BF16)
            o_ref[pl.ds(pl.multiple_of((nblk - 1 - s) * tf, tf), tf), :] = mirrored.astype(BF16)

        def store_middle(shifted=shifted, mirrored=mirrored):
            row = lax.broadcasted_iota(jnp.int32, (tf, 1), 0) + (nblk // 2) * tf
            mid = jnp.where(row <= seq_len // 2 + lead, shifted, mirrored)
            o_ref[(nblk // 2) * tf:(nblk // 2 + 1) * tf, :] = mid.astype(BF16)

        if t < fsub - 1:
            store_pair()
        else:
            pl.when(m < last)(store_pair)
            pl.when(m == last)(store_middle)
    prev_ref[tf - shift:, :] = tail


def _dft_matrix(nrows, ncols, lead, seq_len):
    w = np.float32(2.0 * math.pi / seq_len)
    k = (jnp.arange(nrows, dtype=jnp.int32) + 1)[:, None]
    j = jnp.arange(ncols // LANES, dtype=jnp.int32)[None, :]
    r = jnp.arange(LANES, dtype=jnp.int32)[None, :]
    a = ((k * (LANES * j - lead)) % seq_len).astype(F32) * w
    b = ((k * r) % seq_len).astype(F32) * w
    ca, sa, cb, sb = jnp.cos(a)[:, :, None], jnp.sin(a)[:, :, None], jnp.cos(b)[:, None, :], jnp.sin(b)[:, None, :]
    n = jnp.arange(ncols, dtype=jnp.int32) - lead
    valid = ((n >= 0) & (n <= seq_len // 2)).reshape(1, ncols // LANES, LANES)
    scale = np.float32(seq_len ** -0.5)
    wc = jnp.where(valid, (ca * cb - sa * sb) * scale, 0.0).reshape(nrows, ncols)
    ws = jnp.where(valid, -(sa * cb + ca * sb) * scale, 0.0).reshape(nrows, ncols)
    return jnp.concatenate([wc, ws], axis=1).astype(BF16)


def _fourier_steps(lp, lead, seq_len, tf):
    nblk = lp // tf
    assert seq_len % 2 == 0 and lp % tf == 0 and nblk % 2 == 1 and lead + 1 < tf
    steps = nblk // 2 + 1
    assert steps * tf > seq_len // 2 + lead
    return steps


def _fourier(w, pq, lead, seq_len, tf):
    B, lp, _ = pq.shape
    steps = _fourier_steps(lp, lead, seq_len, tf)
    kp = steps * tf
    fsub = _pick_tile(steps, (2, 1))
    return pl.pallas_call(
        functools.partial(_fourier_kernel, lp=lp, lead=lead, seq_len=seq_len, tf=tf),
        grid=(B, steps // fsub),
        in_specs=[
            pl.BlockSpec((fsub * tf, 2 * kp), lambda b, m: (m, 0)),
            pl.BlockSpec((None, lp, FOURIER_WIDTH), lambda b, m: (b, 0, 0)),
            pl.BlockSpec((None, lp, FOURIER_WIDTH), lambda b, m: (b, 0, 1)),
        ],
        out_specs=pl.BlockSpec((None, lp, FOURIER_WIDTH), lambda b, m: (b, 0, 0)),
        out_shape=jax.ShapeDtypeStruct((B, lp, FOURIER_WIDTH), BF16),
        scratch_shapes=[pltpu.VMEM((tf, FOURIER_WIDTH), F32),
                        pltpu.VMEM((kp, FOURIER_WIDTH), BF16),
                        pltpu.VMEM((kp, FOURIER_WIDTH), BF16)],
        compiler_params=_params("parallel", "arbitrary"),
        name="fourier",
    )(w, pq, pq)


def _split_bf16(x):
    hi = x.astype(BF16)
    lo = (x - hi.astype(F32)).astype(BF16)
    return hi, lo


OUTPROJ_SUB = 256


def _outproj_kernel(attn_ref, four_ref, gate_ref, h_ref, wa_ref, wf_ref, wo_ref, g_ref, b_ref,
                    wrth_ref, wrtl_ref, h1_ref, h1b_ref, affr_ref, *, alpha):
    tm = h_ref.shape[0]
    subs = [slice(i * OUTPROJ_SUB, (i + 1) * OUTPROJ_SUB) for i in range(tm // OUTPROJ_SUB)]
    ya = [jnp.dot(attn_ref[s, :], wa_ref[...], preferred_element_type=F32) for s in subs]
    yf = [jnp.dot(four_ref[s, :], wf_ref[...], preferred_element_type=F32) for s in subs]
    merged = [gate_ref[s, 0:D_MODEL] * a + gate_ref[s, D_MODEL:GATE_WIDTH] * f for s, a, f in zip(subs, ya, yf)]
    mix = [jnp.dot(mg.astype(BF16), wo_ref[...], preferred_element_type=F32) for mg in merged]
    for s, mx in zip(subs, mix):
        h1 = _ln(alpha * h_ref[s, :] + mx, g_ref[...], b_ref[...])
        h1_ref[s, :] = h1
        hi, lo = _split_bf16(h1)
        h1b_ref[s, :] = hi
        lr = (lax.dot_general(wrth_ref[...], hi, NT_DIMS, preferred_element_type=F32)
              + lax.dot_general(wrtl_ref[...], hi, NT_DIMS, preferred_element_type=F32)
              + lax.dot_general(wrth_ref[...], lo, NT_DIMS, preferred_element_type=F32))
        er = jnp.exp(lr - jnp.max(lr, axis=0, keepdims=True))
        affr_ref[:, s] = er / jnp.sum(er, axis=0, keepdims=True)


def _outproj(attn, four, gates, h, wa, wf, wo, g, b, w_router, alpha, tm):
    n = h.shape[0]
    row = lambda i: (i, 0)
    const = lambda i: (0, 0)
    wrth, wrtl = _split_bf16(w_router.T)
    return pl.pallas_call(
        functools.partial(_outproj_kernel, alpha=alpha),
        grid=(n // tm,),
        in_specs=[
            pl.BlockSpec((tm, ATTN_WIDTH), row),
            pl.BlockSpec((tm, FOURIER_WIDTH), row),
            pl.BlockSpec((tm, GATE_WIDTH), row),
            pl.BlockSpec((tm, D_MODEL), row),
            pl.BlockSpec((ATTN_WIDTH, D_MODEL), const),
            pl.BlockSpec((FOURIER_WIDTH, D_MODEL), const),
            pl.BlockSpec((D_MODEL, D_MODEL), const),
            pl.BlockSpec((1, D_MODEL), const),
            pl.BlockSpec((1, D_MODEL), const),
            pl.BlockSpec((N_EXPERTS, D_MODEL), const),
            pl.BlockSpec((N_EXPERTS, D_MODEL), const),
        ],
        out_specs=[
            pl.BlockSpec((tm, D_MODEL), row),
            pl.BlockSpec((tm, D_MODEL), row),
            pl.BlockSpec((N_EXPERTS, tm), lambda i: (0, i)),
        ],
        out_shape=[
            jax.ShapeDtypeStruct((n, D_MODEL), F32),
            jax.ShapeDtypeStruct((n, D_MODEL), BF16),
            jax.ShapeDtypeStruct((N_EXPERTS, n), F32),
        ],
        compiler_params=_params("parallel"),
        name="outproj_ln_router",
    )(attn, four, gates, h, wa, wf, wo, g.reshape(1, -1), b.reshape(1, -1), wrth, wrtl)


def _topk_kernel(aff_ref, rr_ref, rc_ref, cnt_ref, cntc_ref, *, lead, lp, cap):
    nchunk = lp // LANES
    lane = lax.broadcasted_iota(jnp.int32, (N_EXPERTS, lp), 1)
    bits = jnp.where(lane >= lead, pltpu.bitcast(aff_ref[...], jnp.int32), -1)

    def search(i, t):
        cand = t | (jnp.int32(1) << (30 - i))
        cnt = jnp.sum((bits >= cand).astype(jnp.int32), axis=-1, keepdims=True)
        return jnp.where(cnt >= cap, cand, t)

    thr = lax.fori_loop(0, 31, search, jnp.zeros((N_EXPERTS, 1), jnp.int32))
    gt = bits > thr
    eq = bits == thr
    need = cap - jnp.sum(gt.astype(jnp.int32), axis=-1, keepdims=True)

    r = lax.broadcasted_iota(jnp.int32, (LANES, LANES), 0)
    c = lax.broadcasted_iota(jnp.int32, (LANES, LANES), 1)
    upper = (r <= c).astype(BF16)
    lower = (c <= r).astype(BF16)
    ident = (c == r).astype(BF16)

    chunks = [slice(k * LANES, (k + 1) * LANES) for k in range(nchunk)]
    eqb = eq.astype(BF16)
    needf = need.astype(F32)
    pre_eq = [jnp.dot(eqb[:, sl], upper, preferred_element_type=F32) for sl in chunks]
    off = jnp.zeros((N_EXPERTS, 1), F32)
    sel_chunks = []
    for k, sl in enumerate(chunks):
        sel_chunks.append(gt[:, sl] | (eq[:, sl] & (pre_eq[k] + off <= needf)))
        off = off + pre_eq[k][:, LANES - 1:LANES]

    selb = [sel.astype(BF16) for sel in sel_chunks]
    pre_r = [jnp.dot(s, upper, preferred_element_type=F32) for s in selb]
    pre_c = [lax.dot_general(lower, s, NT_DIMS, preferred_element_type=F32) for s in selb]
    sel_c = [lax.dot_general(ident, s, NT_DIMS, preferred_element_type=F32) for s in selb]
    off_r = jnp.zeros((N_EXPERTS, 1), F32)
    off_c = jnp.zeros((1, N_EXPERTS), F32)
    cnt_lane = lax.broadcasted_iota(jnp.int32, (N_EXPERTS, LANES), 1)
    cnt = jnp.zeros((N_EXPERTS, LANES), F32)
    for k, sl in enumerate(chunks):
        rr_ref[:, sl] = jnp.where(sel_chunks[k], pre_r[k] + (off_r - 1.0), -1.0).astype(jnp.int32)
        off_r = off_r + pre_r[k][:, LANES - 1:LANES]
        cnt = jnp.where(cnt_lane == k + 1, off_r, cnt)
        rc_ref[sl, :] = jnp.where(sel_c[k] > 0.5, pre_c[k] + (off_c - 1.0), -1.0).astype(jnp.int32)
        off_c = off_c + pre_c[k][LANES - 1:LANES, :]
        cntc_ref[k + 1:k + 2, :] = off_c.astype(jnp.int32)
    cnt_ref[...] = cnt.astype(jnp.int32)
    cntc_ref[0:1, :] = jnp.zeros((1, N_EXPERTS), jnp.int32)
    pad_rows = cntc_ref.shape[0] - nchunk - 1
    if pad_rows:
        cntc_ref[nchunk + 1:, :] = jnp.zeros((pad_rows, N_EXPERTS), jnp.int32)


def _topk(aff_r, B, lp, lead, cap):
    n = B * lp
    nchunk = lp // LANES
    assert nchunk + 1 <= LANES
    nb1 = -(-(nchunk + 1) // SUBLANES) * SUBLANES
    return pl.pallas_call(
        functools.partial(_topk_kernel, lead=lead, lp=lp, cap=cap),
        grid=(B,),
        in_specs=[pl.BlockSpec((N_EXPERTS, lp), lambda b: (0, b))],
        out_specs=[
            pl.BlockSpec((N_EXPERTS, lp), lambda b: (0, b)),
            pl.BlockSpec((lp, N_EXPERTS), lambda b: (b, 0)),
            pl.BlockSpec((None, N_EXPERTS, LANES), lambda b: (b, 0, 0)),
            pl.BlockSpec((None, nb1, N_EXPERTS), lambda b: (b, 0, 0)),
        ],
        out_shape=[
            jax.ShapeDtypeStruct((N_EXPERTS, n), jnp.int32),
            jax.ShapeDtypeStruct((n, N_EXPERTS), jnp.int32),
            jax.ShapeDtypeStruct((B, N_EXPERTS, LANES), jnp.int32),
            jax.ShapeDtypeStruct((B, nb1, N_EXPERTS), jnp.int32),
        ],
        compiler_params=_params("parallel"),
        name="topk_select",
    )(aff_r)


SLOT_ALIGN = 16


def _window_constants(win):
    wide = N_EXPERTS * win
    expand = np.zeros((N_EXPERTS, wide), np.float32)
    for e in range(N_EXPERTS):
        expand[e, e * win:(e + 1) * win] = 1.0
    return expand, (np.arange(wide) % win).astype(np.float32)


def _gather_kernel(cnt_ref, cntv_ref, rank_ref, aff_ref, h_ref, o_ref, g_ref,
                   *, slots, ch, win, nchunk):
    bi, c = pl.program_id(0), pl.program_id(1)
    k0 = c * ch
    tt = ch * LANES
    max_start = slots - win

    @pl.when(c == 0)
    def _():
        o_ref[...] = jnp.zeros_like(o_ref)
        g_ref[...] = jnp.zeros_like(g_ref)

    lane = lax.broadcasted_iota(jnp.int32, cntv_ref.shape, 1)
    lo_col = jnp.sum(jnp.where(lane == k0, cntv_ref[...], 0), axis=-1, keepdims=True)
    w_col = jnp.minimum(lo_col & -SLOT_ALIGN, max_start)
    rank = rank_ref[...]
    aff = aff_ref[...]
    rel = rank - w_col
    in_win = (rank >= 0) & (rel >= 0) & (rel < win)
    rel = jnp.where(in_win, rel, -1)
    row = lax.broadcasted_iota(jnp.int32, (win, tt), 0)
    hit_all = jnp.concatenate([rel[e:e + 1, :] == row for e in range(N_EXPERTS)], axis=0)
    rows = h_ref[...]
    xw = jnp.dot(hit_all.astype(BF16), rows, preferred_element_type=F32)

    def add_rows(e, start, hit, vals):
        dst = pl.ds(pl.multiple_of(start, SLOT_ALIGN), win)
        o_ref[e, dst, :] = o_ref[e, dst, :] + vals.astype(BF16)
        g_ref[e, dst, :] += jnp.sum(jnp.where(hit, aff[e:e + 1, :], 0.0), axis=-1, keepdims=True)

    for e in range(N_EXPERTS):
        base = (bi * N_EXPERTS + e) * (nchunk + 1) + k0
        w = jnp.minimum(cnt_ref[base] & -SLOT_ALIGN, max_start)
        add_rows(e, w, hit_all[e * win:(e + 1) * win, :], xw[e * win:(e + 1) * win, :])
        hi = cnt_ref[base + ch]

        @pl.when(hi > w + win)
        def _(e=e, w=w, hi=hi):
            rk = rank[e:e + 1, :]

            def extra(i, carry):
                w2 = w + (i + 1) * win
                w2c = jnp.minimum(w2, max_start)
                hit = ((rk - w2c) == row) & (rk >= w2)
                add_rows(e, w2c, hit, jnp.dot(hit.astype(BF16), rows, preferred_element_type=F32))
                return carry

            lax.fori_loop(0, (hi - w - 1) // win, extra, 0)


def _gather(cnt, cntv, rank_rows, aff_rows, h1b, B, lp, slots, ch, win):
    nchunk = lp // LANES
    tt = ch * LANES
    nt = nchunk // ch
    assert slots % SLOT_ALIGN == 0 and win % SLOT_ALIGN == 0 and win <= slots
    per_seq = lambda bi, c, cnt: (bi, 0, 0, 0)
    return pl.pallas_call(
        functools.partial(_gather_kernel, slots=slots, ch=ch, win=win, nchunk=nchunk),
        grid_spec=pltpu.PrefetchScalarGridSpec(
            num_scalar_prefetch=1,
            grid=(B, nt),
            in_specs=[
                pl.BlockSpec((None, N_EXPERTS, LANES), lambda bi, c, cnt: (bi, 0, 0)),
                pl.BlockSpec((N_EXPERTS, tt), lambda bi, c, cnt: (0, bi * nt + c)),
                pl.BlockSpec((N_EXPERTS, tt), lambda bi, c, cnt: (0, bi * nt + c)),
                pl.BlockSpec((tt, D_MODEL), lambda bi, c, cnt: (bi * nt + c, 0)),
            ],
            out_specs=[pl.BlockSpec((None, N_EXPERTS, slots, D_MODEL), per_seq),
                       pl.BlockSpec((None, N_EXPERTS, slots, 1), per_seq)],
        ),
        out_shape=[jax.ShapeDtypeStruct((B, N_EXPERTS, slots, D_MODEL), BF16),
                   jax.ShapeDtypeStruct((B, N_EXPERTS, slots, 1), F32)],
        compiler_params=_params("parallel", "arbitrary"),
        name="moe_gather",
    )(cnt, cntv, rank_rows, aff_rows, h1b)


def _ffn_up_kernel(x_ref, wg_ref, wu_ref, o_ref, wgb_ref, wub_ref):
    @pl.when(pl.program_id(1) == 0)
    def _():
        wgb_ref[...] = wg_ref[...].astype(BF16)
        wub_ref[...] = wu_ref[...].astype(BF16)

    nseq, slots, width = x_ref.shape
    x = x_ref[...].reshape(nseq * slots, width)
    a = jnp.dot(x, wgb_ref[...], preferred_element_type=F32)
    u = jnp.dot(x, wub_ref[...], preferred_element_type=F32)
    o_ref[...] = (a * jax.nn.sigmoid(a) * u).astype(BF16).reshape(o_ref.shape)


def _ffn_down_kernel(x_ref, g_ref, wd_ref, o_ref, wdb_ref):
    @pl.when(pl.program_id(1) == 0)
    def _():
        wdb_ref[...] = wd_ref[...].astype(BF16)

    nseq, slots, width = x_ref.shape
    y = jnp.dot(x_ref[...].reshape(nseq * slots, width), wdb_ref[...], preferred_element_type=F32)
    y = y * g_ref[...].reshape(nseq * slots, 1)
    o_ref[...] = y.astype(BF16).reshape(o_ref.shape)


def _ffn(xg, gates, wg, wu, wd, layer):
    B, _, slots, _ = xg.shape
    n_up = _pick_tile(B, (2, 1))
    n_down = _pick_tile(B, (4, 2, 1))
    acts = lambda nseq, width: pl.BlockSpec((nseq, None, slots, width), lambda e, b: (b, e, 0, 0))
    weight = lambda rows, cols: pl.BlockSpec((None, None, rows, cols), lambda e, b: (layer, e, 0, 0))
    mid = pl.pallas_call(
        _ffn_up_kernel,
        grid=(N_EXPERTS, B // n_up),
        in_specs=[acts(n_up, D_MODEL), weight(D_MODEL, D_FF_EXPERT), weight(D_MODEL, D_FF_EXPERT)],
        out_specs=acts(n_up, D_FF_EXPERT),
        out_shape=jax.ShapeDtypeStruct((B, N_EXPERTS, slots, D_FF_EXPERT), BF16),
        scratch_shapes=[pltpu.VMEM((D_MODEL, D_FF_EXPERT), BF16), pltpu.VMEM((D_MODEL, D_FF_EXPERT), BF16)],
        compiler_params=_params("arbitrary", "arbitrary"),
        name="moe_ffn_up",
    )(xg, wg, wu)
    return pl.pallas_call(
        _ffn_down_kernel,
        grid=(N_EXPERTS, B // n_down),
        in_specs=[acts(n_down, D_FF_EXPERT), acts(n_down, 1), weight(D_FF_EXPERT, D_MODEL)],
        out_specs=acts(n_down, D_MODEL),
        out_shape=jax.ShapeDtypeStruct(xg.shape, BF16),
        scratch_shapes=[pltpu.VMEM((D_FF_EXPERT, D_MODEL), BF16)],
        compiler_params=_params("arbitrary", "arbitrary"),
        name="moe_ffn_down",
    )(mid, gates, wd)


def _combine_kernel(cnt_ref, cntc_ref, rank_ref, y_ref, h_ref, g_ref, b_ref, expand_ref, rpat_ref,
                    o_ref, acc_ref, *, slots, ch, win, nchunk, alpha):
    bi, c = pl.program_id(0), pl.program_id(1)
    k0 = c * ch
    tt = ch * LANES
    max_start = slots - win
    rank = rank_ref[...]
    lo_row = cntc_ref[pl.ds(k0, 1), :]
    w_row = jnp.minimum(lo_row & -SLOT_ALIGN, max_start)
    rel = rank - w_row
    in_win = (rank >= 0) & (rel >= 0) & (rel < win)
    relb = jnp.where(in_win, rel, -1).astype(F32).astype(BF16)
    rel_wide = jnp.dot(relb, expand_ref[...], preferred_element_type=F32)
    onehot = (rel_wide == rpat_ref[...]).astype(BF16)

    starts, windows = [], []
    for e in range(N_EXPERTS):
        lo = cnt_ref[(bi * N_EXPERTS + e) * (nchunk + 1) + k0]
        w = jnp.minimum(lo & -SLOT_ALIGN, max_start)
        starts.append(w)
        windows.append(y_ref[e, pl.ds(pl.multiple_of(w, SLOT_ALIGN), win), :])
    acc_ref[...] = jnp.dot(onehot, jnp.concatenate(windows, axis=0), preferred_element_type=F32)

    lane = lax.broadcasted_iota(jnp.int32, rank.shape, 1)
    col = lax.broadcasted_iota(jnp.int32, (tt, win), 1)
    for e in range(N_EXPERTS):
        hi = cnt_ref[(bi * N_EXPERTS + e) * (nchunk + 1) + k0 + ch]

        @pl.when(hi > starts[e] + win)
        def _(e=e, hi=hi):
            rk = jnp.sum(jnp.where(lane == e, rank, 0), axis=-1, keepdims=True)

            def extra(i, carry):
                w2 = starts[e] + (i + 1) * win
                w2c = jnp.minimum(w2, max_start)
                rows = y_ref[e, pl.ds(pl.multiple_of(w2c, SLOT_ALIGN), win), :]
                hit = ((rk - w2c) == col) & (rk >= w2)
                acc_ref[...] += jnp.dot(hit.astype(BF16), rows, preferred_element_type=F32)
                return carry

            lax.fori_loop(0, (hi - starts[e] - 1) // win, extra, 0)

    o_ref[...] = _ln(alpha * h_ref[...] + acc_ref[...], g_ref[...], b_ref[...])


def _combine(cnt, cntc, rank_c, y, h1, g, b, B, lp, alpha, ch, win):
    n = B * lp
    slots = y.shape[2]
    nchunk = lp // LANES
    tt = ch * LANES
    nt = nchunk // ch
    assert slots % SLOT_ALIGN == 0 and win % SLOT_ALIGN == 0 and win <= slots
    wide = N_EXPERTS * win
    expand, pos = _window_constants(win)
    tok = lambda bi, c, cnt: (bi * nt + c, 0)
    const = lambda bi, c, cnt: (0, 0)
    return pl.pallas_call(
        functools.partial(_combine_kernel, slots=slots, ch=ch, win=win, nchunk=nchunk, alpha=alpha),
        grid_spec=pltpu.PrefetchScalarGridSpec(
            num_scalar_prefetch=1,
            grid=(B, nt),
            in_specs=[
                pl.BlockSpec((None, cntc.shape[1], N_EXPERTS), lambda bi, c, cnt: (bi, 0, 0)),
                pl.BlockSpec((tt, N_EXPERTS), tok),
                pl.BlockSpec((None, N_EXPERTS, slots, D_MODEL), lambda bi, c, cnt: (bi, 0, 0, 0)),
                pl.BlockSpec((tt, D_MODEL), tok),
                pl.BlockSpec((1, D_MODEL), const),
                pl.BlockSpec((1, D_MODEL), const),
                pl.BlockSpec((N_EXPERTS, wide), const),
                pl.BlockSpec((1, wide), const),
            ],
            out_specs=pl.BlockSpec((tt, D_MODEL), tok),
            scratch_shapes=[pltpu.VMEM((tt, D_MODEL), F32)],
        ),
        out_shape=jax.ShapeDtypeStruct((n, D_MODEL), F32),
        compiler_params=_params("parallel", "arbitrary"),
        name="moe_combine_ln",
    )(cnt, cntc, rank_c, y, h1, g.reshape(1, -1), b.reshape(1, -1),
      jnp.asarray(expand, dtype=BF16), jnp.asarray(pos.reshape(1, wide)))


def kernel(x, meta, ln0_g, ln0_b, w_in, b_gate, sink, w_attn_o, w_four_o, w_out, ln1_g, ln1_b,
           w_router, w_e_gate, w_e_up, w_e_down, ln2_g, ln2_b):
    B, seq, d = x.shape
    depth = w_in.shape[0]
    assert d == D_MODEL and meta.shape == (N_META, D_MODEL)
    assert seq % BLOCK == 0 and N_META % SUBLANES == 0 and N_META <= BLOCK
    L = seq + N_META
    nb = -(-L // BLOCK)
    lp = nb * BLOCK
    lead = lp - L
    cap = CAPACITY_FACTOR * L // N_EXPERTS
    slots = -(-cap // SLOT_ALIGN) * SLOT_ALIGN
    alpha = float((2 * depth) ** 0.25)
    n = B * lp
    nchunk = lp // LANES
    tm = _pick_tile(n, (1024, 512, 256, 128))
    tf = _pick_tile(lp, (384, 128))
    ch = _pick_tile(nchunk, (3, 2, 1))
    win = min(slots, -(-(ch * LANES * cap // L * 4 // 3 + SLOT_ALIGN) // SLOT_ALIGN) * SLOT_ALIGN)

    gi = np.arange(FOURIER_GROUP)
    ang = 2.0 * np.pi * ((gi[:, None] * gi[None, :]) % FOURIER_GROUP) / FOURIER_GROUP
    cs = jnp.asarray(np.concatenate([np.cos(ang), np.sin(ang)], axis=1) * FOURIER_GROUP ** -0.5, dtype=BF16)
    dft_rows = _fourier_steps(lp, lead, L, tf) * tf
    wdft = _dft_matrix(dft_rows, dft_rows, lead, L)
    bias, variant_of_block = _attn_bias_tables(nb, lead)

    h = _embed(x, meta, ln0_g, ln0_b, nb, lead).reshape(n, D_MODEL)
    for l in range(depth):
        q, kv, pq, gates = _inproj(h, w_in[l].astype(BF16), b_gate[l], cs, tm)
        attn = _attention(q.reshape(B, lp, -1), kv.reshape(B, lp, -1), sink[l], bias, variant_of_block, nb)
        four = _fourier(wdft, pq.reshape(B, lp, -1), lead, L, tf)
        h1, h1b, aff_r = _outproj(
            attn.reshape(n, -1), four.reshape(n, -1), gates, h,
            w_attn_o[l].astype(BF16), w_four_o[l].astype(BF16), w_out[l].astype(BF16),
            ln1_g[l], ln1_b[l], w_router[l], alpha, tm)
        rank_r, rank_c, cntv, cntc = _topk(aff_r, B, lp, lead, cap)
        cnt = cntv[:, :, :nchunk + 1].reshape(-1)
        xg, gsel = _gather(cnt, cntv, rank_r, aff_r, h1b, B, lp, slots, ch, win)
        y = _ffn(xg, gsel, w_e_gate, w_e_up, w_e_down, l)
        h = _combine(cnt, cntc, rank_c, y, h1, ln2_g[l], ln2_b[l], B, lp, alpha, ch, win)
    return h.reshape(B, lp, D_MODEL)[:, lead + N_META:]
```

```python
import functools
import math

import numpy as np
import jax
import jax.numpy as jnp
from jax import lax
from jax.experimental import pallas as pl
from jax.experimental.pallas import tpu as pltpu

D_MODEL = 1024
N_META = 16
N_HEADS = 8
N_KV_HEADS = 2
HEAD_DIM = 64
GQA_GROUP = N_HEADS // N_KV_HEADS
ATTN_WIDTH = N_HEADS * HEAD_DIM
KV_WIDTH = N_KV_HEADS * HEAD_DIM
WINDOW = 128
BLOCK = 128
N_FOURIER_GROUPS = 4
FOURIER_GROUP = 128
FOURIER_WIDTH = N_FOURIER_GROUPS * FOURIER_GROUP
N_BRANCHES = 2
GATE_WIDTH = N_BRANCHES * D_MODEL
N_EXPERTS = 16
CAPACITY_FACTOR = 2
D_FF_EXPERT = 1536
LN_EPS = 1e-5
NEG_INF = -1e30
Q_END = ATTN_WIDTH
K_END = Q_END + KV_WIDTH
V_END = K_END + KV_WIDTH
F_END = V_END + FOURIER_WIDTH
IN_WIDTH = F_END + GATE_WIDTH

LANES = 128
SUBLANES = 8
VMEM_LIMIT_BYTES = 56 * 1024 * 1024

F32 = jnp.float32
BF16 = jnp.bfloat16
NT_DIMS = (((1,), (1,)), ((), ()))


def _pick_tile(n, candidates):
    for c in candidates:
        if n % c == 0:
            return c
    raise ValueError(f"no tile in {candidates} divides {n}")


def _params(*sem):
    return pltpu.CompilerParams(dimension_semantics=sem, vmem_limit_bytes=VMEM_LIMIT_BYTES)


def _ln(x, g, b):
    mu = jnp.mean(x, axis=-1, keepdims=True)
    xc = x - mu
    var = jnp.mean(xc * xc, axis=-1, keepdims=True)
    return xc * lax.rsqrt(var + LN_EPS) * g + b


def _embed_kernel(*refs, lead, group):
    x_refs, (meta_ref, g_ref, b_ref, o_ref) = refs[:group], refs[group:]
    for k in range(1, group):
        o_ref[k * BLOCK:(k + 1) * BLOCK, :] = _ln(x_refs[k][...], g_ref[...], b_ref[...])

    @pl.when(pl.program_id(1) == 0)
    def _():
        o_ref[0:lead, :] = jnp.zeros((lead, D_MODEL), F32)
        o_ref[lead:BLOCK, :] = _ln(meta_ref[...], g_ref[...], b_ref[...])

    @pl.when(pl.program_id(1) > 0)
    def _():
        o_ref[0:BLOCK, :] = _ln(x_refs[0][...], g_ref[...], b_ref[...])


def _embed(x, meta, g, b, nb, lead):
    B = x.shape[0]
    group = _pick_tile(nb, (11, 3, 1))
    x_spec = lambda k: pl.BlockSpec((None, BLOCK, D_MODEL),
                                    lambda bi, j: (bi, jnp.maximum(group * j + k - 1, 0), 0))
    const = lambda bi, j: (0, 0)
    return pl.pallas_call(
        functools.partial(_embed_kernel, lead=lead, group=group),
        grid=(B, nb // group),
        in_specs=[x_spec(k) for k in range(group)] + [
            pl.BlockSpec((N_META, D_MODEL), const),
            pl.BlockSpec((1, D_MODEL), const),
            pl.BlockSpec((1, D_MODEL), const),
        ],
        out_specs=pl.BlockSpec((None, group * BLOCK, D_MODEL), lambda bi, j: (bi, j, 0)),
        out_shape=jax.ShapeDtypeStruct((B, nb * BLOCK, D_MODEL), F32),
        compiler_params=_params("parallel", "arbitrary"),
        name="embed_ln",
    )(*([x] * group), meta, g.reshape(1, -1), b.reshape(1, -1))


def _inproj_kernel(h_ref, w_ref, bg_ref, cs_ref, q_ref, kv_ref, pq_ref, gate_ref):
    hb = h_ref[...].astype(BF16)
    q_ref[...] = jnp.dot(hb, w_ref[:, 0:Q_END], preferred_element_type=F32).astype(BF16)
    kv_ref[...] = jnp.dot(hb, w_ref[:, Q_END:V_END], preferred_element_type=F32).astype(BF16)
    uf = jnp.dot(hb, w_ref[:, V_END:F_END], preferred_element_type=F32).astype(BF16)
    for g in range(N_FOURIER_GROUPS):
        lo = g * FOURIER_GROUP
        pq = jnp.dot(uf[:, lo:lo + FOURIER_GROUP], cs_ref[...], preferred_element_type=F32)
        pq_ref[:, lo:lo + FOURIER_GROUP] = pq[:, 0:FOURIER_GROUP].astype(BF16)
        pq_ref[:, FOURIER_WIDTH + lo:FOURIER_WIDTH + lo + FOURIER_GROUP] = (
            pq[:, FOURIER_GROUP:2 * FOURIER_GROUP].astype(BF16))
    chunk = FOURIER_WIDTH
    for c in range(GATE_WIDTH // chunk):
        lo = c * chunk
        ug = jnp.dot(hb, w_ref[:, F_END + lo:F_END + lo + chunk], preferred_element_type=F32)
        gate_ref[:, lo:lo + chunk] = jax.nn.sigmoid(ug + bg_ref[:, lo:lo + chunk])


def _inproj(h, w_bf16, b_gate, cs, tm):
    n = h.shape[0]
    row = lambda i: (i, 0)
    const = lambda i: (0, 0)
    return pl.pallas_call(
        _inproj_kernel,
        grid=(n // tm,),
        in_specs=[
            pl.BlockSpec((tm, D_MODEL), row),
            pl.BlockSpec((D_MODEL, IN_WIDTH), const),
            pl.BlockSpec((1, GATE_WIDTH), const),
            pl.BlockSpec((FOURIER_GROUP, 2 * FOURIER_GROUP), const),
        ],
        out_specs=[
            pl.BlockSpec((tm, ATTN_WIDTH), row),
            pl.BlockSpec((tm, 2 * KV_WIDTH), row),
            pl.BlockSpec((tm, 2 * FOURIER_WIDTH), row),
            pl.BlockSpec((tm, GATE_WIDTH), row),
        ],
        out_shape=[
            jax.ShapeDtypeStruct((n, ATTN_WIDTH), BF16),
            jax.ShapeDtypeStruct((n, 2 * KV_WIDTH), BF16),
            jax.ShapeDtypeStruct((n, 2 * FOURIER_WIDTH), BF16),
            jax.ShapeDtypeStruct((n, GATE_WIDTH), F32),
        ],
        compiler_params=_params("parallel"),
        name="inproj",
    )(h, w_bf16, b_gate.reshape(1, GATE_WIDTH), cs)


LOG2_E = 1.4426950408889634
SCORES_AHEAD = 5


def _attn_kernel(sink_ref, q_ref, *refs, group, variant):
    kv_refs, (bias_ref, o_ref) = refs[:group + 2], refs[group + 2:]
    first_block = pl.program_id(1) * group
    kv = jnp.concatenate([r[...] for r in kv_refs], axis=0)
    low_half = lax.broadcasted_iota(jnp.int32, (kv.shape[0], LANES), 1) < HEAD_DIM

    def lane_half_operands(x):
        swapped = jnp.concatenate([x[:, HEAD_DIM:], x[:, :HEAD_DIM]], axis=1)
        zero = jnp.zeros_like(x)
        return {(kvh, half): jnp.where(low_half if half == 0 else ~low_half,
                                       x if kvh == half else swapped, zero)
                for kvh in range(N_KV_HEADS) for half in range(2)}

    k_ops = lane_half_operands(kv[:, 0:KV_WIDTH])
    v_ops = lane_half_operands(kv[:, KV_WIDTH:2 * KV_WIDTH])
    keys = lambda t: slice(t * BLOCK, (t + 3) * BLOCK)

    def scores(t, h):
        pair = h // 2
        qp = q_ref[t * BLOCK:(t + 1) * BLOCK, pair * LANES:(pair + 1) * LANES]
        return lax.dot_general(qp, k_ops[(h // GQA_GROUP, h % 2)][keys(t)], NT_DIMS,
                               preferred_element_type=F32)

    def head_out(t, h, s):
        sink = sink_ref[h] * LOG2_E
        logits = s * (HEAD_DIM ** -0.5 * LOG2_E) + bias_ref[variant(first_block + t), h]
        m = jnp.maximum(jnp.max(logits, axis=-1, keepdims=True), sink)
        p = jnp.exp2(logits - m)
        denom = jnp.sum(p, axis=-1, keepdims=True) + jnp.exp2(sink - m)
        o = jnp.dot(p.astype(BF16), v_ops[(h // GQA_GROUP, h % 2)][keys(t)], preferred_element_type=F32)
        return o / denom

    work = [(t, h) for t in range(group) for h in range(N_HEADS)]
    ahead = SCORES_AHEAD
    pending = [scores(*w) for w in work[:ahead]]
    for n, (t, h) in enumerate(work):
        s_cur = pending.pop(0)
        if n + ahead < len(work):
            pending.append(scores(*work[n + ahead]))
        o = head_out(t, h, s_cur)
        if h % 2 == 0:
            o_even = o
        else:
            pair = h // 2
            o_ref[t * BLOCK:(t + 1) * BLOCK, pair * LANES:(pair + 1) * LANES] = (o_even + o).astype(BF16)


def _attn_bias_tables(nb, lead):
    qi = np.arange(BLOCK)[:, None]
    si = np.arange(3 * BLOCK)[None, :]
    rel = np.abs(si - BLOCK - qi).astype(np.float32)
    slopes = np.array([2.0 ** (-8.0 * (h + 1) / N_HEADS) for h in range(N_HEADS)], np.float32)
    base = np.where(rel[None] <= WINDOW, -slopes[:, None, None] * rel[None] * np.float32(LOG2_E),
                    np.float32(NEG_INF))
    variants, keys, variant_of_block = [], [], []
    for i in range(nb):
        kpos = (i - 1) * BLOCK + np.arange(3 * BLOCK)
        valid = (kpos >= lead) & (kpos < nb * BLOCK)
        key = valid.tobytes()
        if key not in keys:
            keys.append(key)
            variants.append(np.where(valid[None, None, :], base, np.float32(NEG_INF)))
        variant_of_block.append(keys.index(key))
    return jnp.asarray(np.stack(variants).astype(np.float32)), variant_of_block


def _attention(q, kv, sink, bias, variant_of_block, nb):
    B, lp, _ = q.shape
    group = _pick_tile(nb, (11, 3, 1))
    interior = max(set(variant_of_block), key=variant_of_block.count)

    def variant(i):
        v = jnp.int32(interior)
        for blk, var in enumerate(variant_of_block):
            if var != interior:
                v = jnp.where(i == blk, var, v)
        return v

    kv_spec = lambda t: pl.BlockSpec((None, BLOCK, 2 * KV_WIDTH),
                                     lambda b, g: (b, jnp.clip(group * g + t - 1, 0, nb - 1), 0))
    rows = pl.BlockSpec((None, group * BLOCK, ATTN_WIDTH), lambda b, g: (b, g, 0))
    return pl.pallas_call(
        functools.partial(_attn_kernel, group=group, variant=variant),
        grid=(B, nb // group),
        in_specs=[pl.BlockSpec(memory_space=pltpu.SMEM), rows]
                 + [kv_spec(t) for t in range(group + 2)]
                 + [pl.BlockSpec(bias.shape, lambda b, g: (0, 0, 0, 0))],
        out_specs=rows,
        out_shape=jax.ShapeDtypeStruct((B, lp, ATTN_WIDTH), BF16),
        compiler_params=_params("parallel", "arbitrary"),
        name="attention",
    )(sink, q, *([kv] * (group + 2)), bias)


def _fourier_kernel(w_ref, p_ref, q_ref, o_ref, prev_ref, pe_ref, qo_ref, *, lp, lead, seq_len, tf):
    m = pl.program_id(1)
    last = pl.num_programs(1) - 1
    nblk = lp // tf
    shift = lead + 1
    kp = pe_ref.shape[0]
    half = seq_len // 2
    r = lax.broadcasted_iota(jnp.int32, (tf, tf), 0)
    c = lax.broadcasted_iota(jnp.int32, (tf, tf), 1)
    flip = (r + c == tf - 1).astype(BF16)

    @pl.when(m == 0)
    def _():
        row = lax.broadcasted_iota(jnp.int32, (lp, 1), 0)
        dc = jnp.sum(jnp.where(row >= lead, p_ref[...].astype(F32), 0.0), axis=0, keepdims=True)
        prev_ref[...] = jnp.zeros_like(prev_ref)
        prev_ref[tf - 1:tf, :] = dc * np.float32(seq_len ** -0.5)
        def mirror(x_ref, i):
            above = (jnp.dot(flip, x_ref[(nblk - i) * tf:(nblk - i + 1) * tf, :], preferred_element_type=F32)
                     if i > 0 else jnp.zeros((tf, FOURIER_WIDTH), F32))
            below = jnp.dot(flip, x_ref[(nblk - 1 - i) * tf:(nblk - i) * tf, :], preferred_element_type=F32)
            return jnp.concatenate([above[tf - shift:, :], below[:tf - shift, :]], axis=0)

        for i in range(kp // tf):
            rows = slice(i * tf, (i + 1) * tf)
            n = lax.broadcasted_iota(jnp.int32, (tf, 1), 0) + (i * tf - lead)
            paired = (n >= 1) & (n <= half - 1)
            alone = (n == 0) | (n == half)
            p_blk = p_ref[rows, :].astype(F32)
            pe_ref[rows, :] = jnp.where(paired, p_blk + mirror(p_ref, i),
                                        jnp.where(alone, p_blk, 0.0)).astype(BF16)
            qo_ref[rows, :] = jnp.where(paired, q_ref[rows, :].astype(F32) - mirror(q_ref, i), 0.0).astype(BF16)

    t1 = jnp.dot(w_ref[:, 0:kp], pe_ref[...], preferred_element_type=F32)
    t2 = jnp.dot(w_ref[:, kp:2 * kp], qo_ref[...], preferred_element_type=F32)
    fsub = w_ref.shape[0] // tf
    tail = prev_ref[tf - shift:, :]
    for t in range(fsub):
        blk = slice(t * tf, (t + 1) * tf)
        s = m * fsub + t
        direct = t1[blk] + t2[blk]
        mirrored = jnp.dot(flip, (t1[blk] - t2[blk]).astype(BF16), preferred_element_type=F32)
        shifted = jnp.concatenate([tail, direct[:tf - shift, :]], axis=0)
        tail = direct[tf - shift:, :]

        def store_pair(s=s, shifted=shifted, mirrored=mirrored):
            o_ref[pl.ds(pl.multiple_of(s * tf, tf), tf), :] = shifted.astype(BF16)
            o_ref[pl.ds(pl.multiple_of((nblk - 1 - s) * tf, tf), tf), :] = mirrored.astype(BF16)

        def store_middle(shifted=shifted, mirrored=mirrored):
            row = lax.broadcasted_iota(jnp.int32, (tf, 1), 0) + (nblk // 2) * tf
            mid = jnp.where(row <= seq_len // 2 + lead, shifted, mirrored)
            o_ref[(nblk // 2) * tf:(nblk // 2 + 1) * tf, :] = mid.astype(BF16)

        if t < fsub - 1:
            store_pair()
        else:
            pl.when(m < last)(store_pair)
            pl.when(m == last)(store_middle)
    prev_ref[tf - shift:, :] = tail


def _dft_matrix(nrows, ncols, lead, seq_len):
    w = np.float32(2.0 * math.pi / seq_len)
    k = (jnp.arange(nrows, dtype=jnp.int32) + 1)[:, None]
    j = jnp.arange(ncols // LANES, dtype=jnp.int32)[None, :]
    r = jnp.arange(LANES, dtype=jnp.int32)[None, :]
    a = ((k * (LANES * j - lead)) % seq_len).astype(F32) * w
    b = ((k * r) % seq_len).astype(F32) * w
    ca, sa, cb, sb = jnp.cos(a)[:, :, None], jnp.sin(a)[:, :, None], jnp.cos(b)[:, None, :], jnp.sin(b)[:, None, :]
    n = jnp.arange(ncols, dtype=jnp.int32) - lead
    valid = ((n >= 0) & (n <= seq_len // 2)).reshape(1, ncols // LANES, LANES)
    scale = np.float32(seq_len ** -0.5)
    wc = jnp.where(valid, (ca * cb - sa * sb) * scale, 0.0).reshape(nrows, ncols)
    ws = jnp.where(valid, -(sa * cb + ca * sb) * scale, 0.0).reshape(nrows, ncols)
    return jnp.concatenate([wc, ws], axis=1).astype(BF16)


def _fourier_steps(lp, lead, seq_len, tf):
    nblk = lp // tf
    assert seq_len % 2 == 0 and lp % tf == 0 and nblk % 2 == 1 and lead + 1 < tf
    steps = nblk // 2 + 1
    assert steps * tf > seq_len // 2 + lead
    return steps


def _fourier(w, pq, lead, seq_len, tf):
    B, lp, _ = pq.shape
    steps = _fourier_steps(lp, lead, seq_len, tf)
    kp = steps * tf
    fsub = _pick_tile(steps, (2, 1))
    return pl.pallas_call(
        functools.partial(_fourier_kernel, lp=lp, lead=lead, seq_len=seq_len, tf=tf),
        grid=(B, steps // fsub),
        in_specs=[
            pl.BlockSpec((fsub * tf, 2 * kp), lambda b, m: (m, 0)),
            pl.BlockSpec((None, lp, FOURIER_WIDTH), lambda b, m: (b, 0, 0)),
            pl.BlockSpec((None, lp, FOURIER_WIDTH), lambda b, m: (b, 0, 1)),
        ],
        out_specs=pl.BlockSpec((None, lp, FOURIER_WIDTH), lambda b, m: (b, 0, 0)),
        out_shape=jax.ShapeDtypeStruct((B, lp, FOURIER_WIDTH), BF16),
        scratch_shapes=[pltpu.VMEM((tf, FOURIER_WIDTH), F32),
                        pltpu.VMEM((kp, FOURIER_WIDTH), BF16),
                        pltpu.VMEM((kp, FOURIER_WIDTH), BF16)],
        compiler_params=_params("parallel", "arbitrary"),
        name="fourier",
    )(w, pq, pq)


def _split_bf16(x):
    hi = x.astype(BF16)
    lo = (x - hi.astype(F32)).astype(BF16)
    return hi, lo


OUTPROJ_SUB = 256


def _outproj_kernel(attn_ref, four_ref, gate_ref, h_ref, wa_ref, wf_ref, wo_ref, g_ref, b_ref,
                    wrth_ref, wrtl_ref, h1_ref, h1b_ref, affr_ref, *, alpha):
    tm = h_ref.shape[0]
    subs = [slice(i * OUTPROJ_SUB, (i + 1) * OUTPROJ_SUB) for i in range(tm // OUTPROJ_SUB)]
    ya = [jnp.dot(attn_ref[s, :], wa_ref[...], preferred_element_type=F32) for s in subs]
    yf = [jnp.dot(four_ref[s, :], wf_ref[...], preferred_element_type=F32) for s in subs]
    merged = [gate_ref[s, 0:D_MODEL] * a + gate_ref[s, D_MODEL:GATE_WIDTH] * f for s, a, f in zip(subs, ya, yf)]
    mix = [jnp.dot(mg.astype(BF16), wo_ref[...], preferred_element_type=F32) for mg in merged]
    for s, mx in zip(subs, mix):
        h1 = _ln(alpha * h_ref[s, :] + mx, g_ref[...], b_ref[...])
        h1_ref[s, :] = h1
        hi, lo = _split_bf16(h1)
        h1b_ref[s, :] = hi
        lr = (lax.dot_general(wrth_ref[...], hi, NT_DIMS, preferred_element_type=F32)
              + lax.dot_general(wrtl_ref[...], hi, NT_DIMS, preferred_element_type=F32)
              + lax.dot_general(wrth_ref[...], lo, NT_DIMS, preferred_element_type=F32))
        er = jnp.exp(lr - jnp.max(lr, axis=0, keepdims=True))
        affr_ref[:, s] = er / jnp.sum(er, axis=0, keepdims=True)


def _outproj(attn, four, gates, h, wa, wf, wo, g, b, w_router, alpha, tm):
    n = h.shape[0]
    row = lambda i: (i, 0)
    const = lambda i: (0, 0)
    wrth, wrtl = _split_bf16(w_router.T)
    tm = _pick_tile(n, (512, 256, 128))
    deep = lambda width: pl.BlockSpec((tm, width), row, pipeline_mode=pl.Buffered(3))
    in_specs = [
        deep(ATTN_WIDTH), deep(FOURIER_WIDTH), deep(GATE_WIDTH), deep(D_MODEL),
        pl.BlockSpec((ATTN_WIDTH, D_MODEL), const),
        pl.BlockSpec((FOURIER_WIDTH, D_MODEL), const),
        pl.BlockSpec((D_MODEL, D_MODEL), const),
        pl.BlockSpec((1, D_MODEL), const),
        pl.BlockSpec((1, D_MODEL), const),
        pl.BlockSpec((N_EXPERTS, D_MODEL), const),
        pl.BlockSpec((N_EXPERTS, D_MODEL), const),
    ]
    out_specs = [
        pl.BlockSpec((tm, D_MODEL), row),
        pl.BlockSpec((tm, D_MODEL), row),
        pl.BlockSpec((N_EXPERTS, tm), lambda i: (0, i)),
    ]

    def outer(*refs):
        pltpu.emit_pipeline(functools.partial(_outproj_kernel, alpha=alpha), grid=(n // tm,),
                            in_specs=in_specs, out_specs=out_specs)(*refs)

    anywhere = pl.BlockSpec(memory_space=pl.ANY)
    return pl.pallas_call(
        outer,
        in_specs=[anywhere] * len(in_specs),
        out_specs=[anywhere] * len(out_specs),
        out_shape=[
            jax.ShapeDtypeStruct((n, D_MODEL), F32),
            jax.ShapeDtypeStruct((n, D_MODEL), BF16),
            jax.ShapeDtypeStruct((N_EXPERTS, n), F32),
        ],
        compiler_params=pltpu.CompilerParams(vmem_limit_bytes=VMEM_LIMIT_BYTES),
        name="outproj_ln_router",
    )(attn, four, gates, h, wa, wf, wo, g.reshape(1, -1), b.reshape(1, -1), wrth, wrtl)


def _topk_kernel(aff_ref, rr_ref, rc_ref, cnt_ref, cntc_ref, *, lead, lp, cap):
    nchunk = lp // LANES
    lane = lax.broadcasted_iota(jnp.int32, (N_EXPERTS, lp), 1)
    bits = jnp.where(lane >= lead, pltpu.bitcast(aff_ref[...], jnp.int32), -1)

    def search(i, t):
        cand = t | (jnp.int32(1) << (30 - i))
        cnt = jnp.sum((bits >= cand).astype(jnp.int32), axis=-1, keepdims=True)
        return jnp.where(cnt >= cap, cand, t)

    thr = lax.fori_loop(0, 31, search, jnp.zeros((N_EXPERTS, 1), jnp.int32))
    gt = bits > thr
    eq = bits == thr
    need = cap - jnp.sum(gt.astype(jnp.int32), axis=-1, keepdims=True)

    r = lax.broadcasted_iota(jnp.int32, (LANES, LANES), 0)
    c = lax.broadcasted_iota(jnp.int32, (LANES, LANES), 1)
    upper = (r <= c).astype(BF16)
    lower = (c <= r).astype(BF16)
    ident = (c == r).astype(BF16)

    chunks = [slice(k * LANES, (k + 1) * LANES) for k in range(nchunk)]
    eqb = eq.astype(BF16)
    needf = need.astype(F32)
    pre_eq = [jnp.dot(eqb[:, sl], upper, preferred_element_type=F32) for sl in chunks]
    off = jnp.zeros((N_EXPERTS, 1), F32)
    sel_chunks = []
    for k, sl in enumerate(chunks):
        sel_chunks.append(gt[:, sl] | (eq[:, sl] & (pre_eq[k] + off <= needf)))
        off = off + pre_eq[k][:, LANES - 1:LANES]

    selb = [sel.astype(BF16) for sel in sel_chunks]
    pre_r = [jnp.dot(s, upper, preferred_element_type=F32) for s in selb]
    pre_c = [lax.dot_general(lower, s, NT_DIMS, preferred_element_type=F32) for s in selb]
    sel_c = [lax.dot_general(ident, s, NT_DIMS, preferred_element_type=F32) for s in selb]
    off_r = jnp.zeros((N_EXPERTS, 1), F32)
    off_c = jnp.zeros((1, N_EXPERTS), F32)
    cnt_lane = lax.broadcasted_iota(jnp.int32, (N_EXPERTS, LANES), 1)
    cnt = jnp.zeros((N_EXPERTS, LANES), F32)
    for k, sl in enumerate(chunks):
        rr_ref[:, sl] = jnp.where(sel_chunks[k], pre_r[k] + (off_r - 1.0), -1.0).astype(jnp.int32)
        off_r = off_r + pre_r[k][:, LANES - 1:LANES]
        cnt = jnp.where(cnt_lane == k + 1, off_r, cnt)
        rc_ref[sl, :] = jnp.where(sel_c[k] > 0.5, pre_c[k] + (off_c - 1.0), -1.0).astype(jnp.int32)
        off_c = off_c + pre_c[k][LANES - 1:LANES, :]
        cntc_ref[k + 1:k + 2, :] = off_c.astype(jnp.int32)
    cnt_ref[...] = cnt.astype(jnp.int32)
    cntc_ref[0:1, :] = jnp.zeros((1, N_EXPERTS), jnp.int32)
    pad_rows = cntc_ref.shape[0] - nchunk - 1
    if pad_rows:
        cntc_ref[nchunk + 1:, :] = jnp.zeros((pad_rows, N_EXPERTS), jnp.int32)


def _topk(aff_r, B, lp, lead, cap):
    n = B * lp
    nchunk = lp // LANES
    assert nchunk + 1 <= LANES
    nb1 = -(-(nchunk + 1) // SUBLANES) * SUBLANES
    return pl.pallas_call(
        functools.partial(_topk_kernel, lead=lead, lp=lp, cap=cap),
        grid=(B,),
        in_specs=[pl.BlockSpec((N_EXPERTS, lp), lambda b: (0, b))],
        out_specs=[
            pl.BlockSpec((N_EXPERTS, lp), lambda b: (0, b)),
            pl.BlockSpec((lp, N_EXPERTS), lambda b: (b, 0)),
            pl.BlockSpec((None, N_EXPERTS, LANES), lambda b: (b, 0, 0)),
            pl.BlockSpec((None, nb1, N_EXPERTS), lambda b: (b, 0, 0)),
        ],
        out_shape=[
            jax.ShapeDtypeStruct((N_EXPERTS, n), jnp.int32),
            jax.ShapeDtypeStruct((n, N_EXPERTS), jnp.int32),
            jax.ShapeDtypeStruct((B, N_EXPERTS, LANES), jnp.int32),
            jax.ShapeDtypeStruct((B, nb1, N_EXPERTS), jnp.int32),
        ],
        compiler_params=_params("parallel"),
        name="topk_select",
    )(aff_r)


SLOT_ALIGN = 16


def _window_constants(win):
    wide = N_EXPERTS * win
    expand = np.zeros((N_EXPERTS, wide), np.float32)
    for e in range(N_EXPERTS):
        expand[e, e * win:(e + 1) * win] = 1.0
    return expand, (np.arange(wide) % win).astype(np.float32)


def _gather_kernel(cnt_ref, cntv_ref, rank_ref, aff_ref, h_ref, o_ref, g_ref,
                   *, slots, ch, win, nchunk):
    bi, c = pl.program_id(0), pl.program_id(1)
    k0 = c * ch
    tt = ch * LANES
    max_start = slots - win

    @pl.when(c == 0)
    def _():
        o_ref[...] = jnp.zeros_like(o_ref)
        g_ref[...] = jnp.zeros_like(g_ref)

    lane = lax.broadcasted_iota(jnp.int32, cntv_ref.shape, 1)
    lo_col = jnp.sum(jnp.where(lane == k0, cntv_ref[...], 0), axis=-1, keepdims=True)
    w_col = jnp.minimum(lo_col & -SLOT_ALIGN, max_start)
    rank = rank_ref[...]
    aff = aff_ref[...]
    rel = rank - w_col
    in_win = (rank >= 0) & (rel >= 0) & (rel < win)
    rel = jnp.where(in_win, rel, -1)
    row = lax.broadcasted_iota(jnp.int32, (win, tt), 0)
    hit_all = jnp.concatenate([rel[e:e + 1, :] == row for e in range(N_EXPERTS)], axis=0)
    rows = h_ref[...]
    xw = jnp.dot(hit_all.astype(BF16), rows, preferred_element_type=F32)

    def add_rows(e, start, hit, vals):
        dst = pl.ds(pl.multiple_of(start, SLOT_ALIGN), win)
        o_ref[e, dst, :] = o_ref[e, dst, :] + vals.astype(BF16)
        g_ref[e, dst, :] += jnp.sum(jnp.where(hit, aff[e:e + 1, :], 0.0), axis=-1, keepdims=True)

    for e in range(N_EXPERTS):
        base = (bi * N_EXPERTS + e) * (nchunk + 1) + k0
        w = jnp.minimum(cnt_ref[base] & -SLOT_ALIGN, max_start)
        add_rows(e, w, hit_all[e * win:(e + 1) * win, :], xw[e * win:(e + 1) * win, :])
        hi = cnt_ref[base + ch]

        @pl.when(hi > w + win)
        def _(e=e, w=w, hi=hi):
            rk = rank[e:e + 1, :]

            def extra(i, carry):
                w2 = w + (i + 1) * win
                w2c = jnp.minimum(w2, max_start)
                hit = ((rk - w2c) == row) & (rk >= w2)
                add_rows(e, w2c, hit, jnp.dot(hit.astype(BF16), rows, preferred_element_type=F32))
                return carry

            lax.fori_loop(0, (hi - w - 1) // win, extra, 0)


def _gather(cnt, cntv, rank_rows, aff_rows, h1b, B, lp, slots, ch, win):
    nchunk = lp // LANES
    tt = ch * LANES
    nt = nchunk // ch
    assert slots % SLOT_ALIGN == 0 and win % SLOT_ALIGN == 0 and win <= slots
    per_seq = lambda bi, c, cnt: (bi, 0, 0, 0)
    return pl.pallas_call(
        functools.partial(_gather_kernel, slots=slots, ch=ch, win=win, nchunk=nchunk),
        grid_spec=pltpu.PrefetchScalarGridSpec(
            num_scalar_prefetch=1,
            grid=(B, nt),
            in_specs=[
                pl.BlockSpec((None, N_EXPERTS, LANES), lambda bi, c, cnt: (bi, 0, 0)),
                pl.BlockSpec((N_EXPERTS, tt), lambda bi, c, cnt: (0, bi * nt + c)),
                pl.BlockSpec((N_EXPERTS, tt), lambda bi, c, cnt: (0, bi * nt + c)),
                pl.BlockSpec((tt, D_MODEL), lambda bi, c, cnt: (bi * nt + c, 0)),
            ],
            out_specs=[pl.BlockSpec((None, N_EXPERTS, slots, D_MODEL), per_seq),
                       pl.BlockSpec((None, N_EXPERTS, slots, 1), per_seq)],
        ),
        out_shape=[jax.ShapeDtypeStruct((B, N_EXPERTS, slots, D_MODEL), BF16),
                   jax.ShapeDtypeStruct((B, N_EXPERTS, slots, 1), F32)],
        compiler_params=_params("parallel", "arbitrary"),
        name="moe_gather",
    )(cnt, cntv, rank_rows, aff_rows, h1b)


def _ffn_up_kernel(x_ref, wg_ref, wu_ref, o_ref, wgb_ref, wub_ref):
    @pl.when(pl.program_id(1) == 0)
    def _():
        wgb_ref[...] = wg_ref[...].astype(BF16)
        wub_ref[...] = wu_ref[...].astype(BF16)

    nseq, slots, width = x_ref.shape
    x = x_ref[...].reshape(nseq * slots, width)
    a = jnp.dot(x, wgb_ref[...], preferred_element_type=F32)
    u = jnp.dot(x, wub_ref[...], preferred_element_type=F32)
    o_ref[...] = (a * jax.nn.sigmoid(a) * u).astype(BF16).reshape(o_ref.shape)


def _ffn_down_kernel(x_ref, g_ref, wd_ref, o_ref, wdb_ref):
    @pl.when(pl.program_id(1) == 0)
    def _():
        wdb_ref[...] = wd_ref[...].astype(BF16)

    nseq, slots, width = x_ref.shape
    y = jnp.dot(x_ref[...].reshape(nseq * slots, width), wdb_ref[...], preferred_element_type=F32)
    y = y * g_ref[...].reshape(nseq * slots, 1)
    o_ref[...] = y.astype(BF16).reshape(o_ref.shape)


def _ffn(xg, gates, wg, wu, wd, layer):
    B, _, slots, _ = xg.shape
    n_up = _pick_tile(B, (2, 1))
    n_down = _pick_tile(B, (4, 2, 1))
    acts = lambda nseq, width: pl.BlockSpec((nseq, None, slots, width), lambda e, b: (b, e, 0, 0))
    weight = lambda rows, cols: pl.BlockSpec((None, None, rows, cols), lambda e, b: (layer, e, 0, 0))
    mid = pl.pallas_call(
        _ffn_up_kernel,
        grid=(N_EXPERTS, B // n_up),
        in_specs=[acts(n_up, D_MODEL), weight(D_MODEL, D_FF_EXPERT), weight(D_MODEL, D_FF_EXPERT)],
        out_specs=acts(n_up, D_FF_EXPERT),
        out_shape=jax.ShapeDtypeStruct((B, N_EXPERTS, slots, D_FF_EXPERT), BF16),
        scratch_shapes=[pltpu.VMEM((D_MODEL, D_FF_EXPERT), BF16), pltpu.VMEM((D_MODEL, D_FF_EXPERT), BF16)],
        compiler_params=_params("arbitrary", "arbitrary"),
        name="moe_ffn_up",
    )(xg, wg, wu)
    return pl.pallas_call(
        _ffn_down_kernel,
        grid=(N_EXPERTS, B // n_down),
        in_specs=[acts(n_down, D_FF_EXPERT), acts(n_down, 1), weight(D_FF_EXPERT, D_MODEL)],
        out_specs=acts(n_down, D_MODEL),
        out_shape=jax.ShapeDtypeStruct(xg.shape, BF16),
        scratch_shapes=[pltpu.VMEM((D_FF_EXPERT, D_MODEL), BF16)],
        compiler_params=_params("arbitrary", "arbitrary"),
        name="moe_ffn_down",
    )(mid, gates, wd)


def _combine_kernel(cnt_ref, cntc_ref, rank_ref, y_ref, h_ref, g_ref, b_ref, expand_ref, rpat_ref,
                    o_ref, acc_ref, *, slots, ch, win, nchunk, alpha):
    bi, c = pl.program_id(0), pl.program_id(1)
    k0 = c * ch
    tt = ch * LANES
    max_start = slots - win
    rank = rank_ref[...]
    lo_row = cntc_ref[pl.ds(k0, 1), :]
    w_row = jnp.minimum(lo_row & -SLOT_ALIGN, max_start)
    rel = rank - w_row
    in_win = (rank >= 0) & (rel >= 0) & (rel < win)
    relb = jnp.where(in_win, rel, -1).astype(F32).astype(BF16)
    rel_wide = jnp.dot(relb, expand_ref[...], preferred_element_type=F32)
    onehot = (rel_wide == rpat_ref[...]).astype(BF16)

    starts, windows = [], []
    for e in range(N_EXPERTS):
        lo = cnt_ref[(bi * N_EXPERTS + e) * (nchunk + 1) + k0]
        w = jnp.minimum(lo & -SLOT_ALIGN, max_start)
        starts.append(w)
        windows.append(y_ref[e, pl.ds(pl.multiple_of(w, SLOT_ALIGN), win), :])
    acc_ref[...] = jnp.dot(onehot, jnp.concatenate(windows, axis=0), preferred_element_type=F32)

    lane = lax.broadcasted_iota(jnp.int32, rank.shape, 1)
    col = lax.broadcasted_iota(jnp.int32, (tt, win), 1)
    for e in range(N_EXPERTS):
        hi = cnt_ref[(bi * N_EXPERTS + e) * (nchunk + 1) + k0 + ch]

        @pl.when(hi > starts[e] + win)
        def _(e=e, hi=hi):
            rk = jnp.sum(jnp.where(lane == e, rank, 0), axis=-1, keepdims=True)

            def extra(i, carry):
                w2 = starts[e] + (i + 1) * win
                w2c = jnp.minimum(w2, max_start)
                rows = y_ref[e, pl.ds(pl.multiple_of(w2c, SLOT_ALIGN), win), :]
                hit = ((rk - w2c) == col) & (rk >= w2)
                acc_ref[...] += jnp.dot(hit.astype(BF16), rows, preferred_element_type=F32)
                return carry

            lax.fori_loop(0, (hi - starts[e] - 1) // win, extra, 0)

    o_ref[...] = _ln(alpha * h_ref[...] + acc_ref[...], g_ref[...], b_ref[...])


def _combine(cnt, cntc, rank_c, y, h1, g, b, B, lp, alpha, ch, win):
    n = B * lp
    slots = y.shape[2]
    nchunk = lp // LANES
    tt = ch * LANES
    nt = nchunk // ch
    assert slots % SLOT_ALIGN == 0 and win % SLOT_ALIGN == 0 and win <= slots
    wide = N_EXPERTS * win
    expand, pos = _window_constants(win)
    tok = lambda bi, c, cnt: (bi * nt + c, 0)
    const = lambda bi, c, cnt: (0, 0)
    return pl.pallas_call(
        functools.partial(_combine_kernel, slots=slots, ch=ch, win=win, nchunk=nchunk, alpha=alpha),
        grid_spec=pltpu.PrefetchScalarGridSpec(
            num_scalar_prefetch=1,
            grid=(B, nt),
            in_specs=[
                pl.BlockSpec((None, cntc.shape[1], N_EXPERTS), lambda bi, c, cnt: (bi, 0, 0)),
                pl.BlockSpec((tt, N_EXPERTS), tok),
                pl.BlockSpec((None, N_EXPERTS, slots, D_MODEL), lambda bi, c, cnt: (bi, 0, 0, 0)),
                pl.BlockSpec((tt, D_MODEL), tok),
                pl.BlockSpec((1, D_MODEL), const),
                pl.BlockSpec((1, D_MODEL), const),
                pl.BlockSpec((N_EXPERTS, wide), const),
                pl.BlockSpec((1, wide), const),
            ],
            out_specs=pl.BlockSpec((tt, D_MODEL), tok),
            scratch_shapes=[pltpu.VMEM((tt, D_MODEL), F32)],
        ),
        out_shape=jax.ShapeDtypeStruct((n, D_MODEL), F32),
        compiler_params=_params("parallel", "arbitrary"),
        name="moe_combine_ln",
    )(cnt, cntc, rank_c, y, h1, g.reshape(1, -1), b.reshape(1, -1),
      jnp.asarray(expand, dtype=BF16), jnp.asarray(pos.reshape(1, wide)))


def kernel(x, meta, ln0_g, ln0_b, w_in, b_gate, sink, w_attn_o, w_four_o, w_out, ln1_g, ln1_b,
           w_router, w_e_gate, w_e_up, w_e_down, ln2_g, ln2_b):
    B, seq, d = x.shape
    depth = w_in.shape[0]
    assert d == D_MODEL and meta.shape == (N_META, D_MODEL)
    assert seq % BLOCK == 0 and N_META % SUBLANES == 0 and N_META <= BLOCK
    L = seq + N_META
    nb = -(-L // BLOCK)
    lp = nb * BLOCK
    lead = lp - L
    cap = CAPACITY_FACTOR * L // N_EXPERTS
    slots = -(-cap // SLOT_ALIGN) * SLOT_ALIGN
    alpha = float((2 * depth) ** 0.25)
    n = B * lp
    nchunk = lp // LANES
    tm = _pick_tile(n, (1024, 512, 256, 128))
    tf = _pick_tile(lp, (384, 128))
    ch = _pick_tile(nchunk, (3, 2, 1))
    win = min(slots, -(-(ch * LANES * cap // L * 4 // 3 + SLOT_ALIGN) // SLOT_ALIGN) * SLOT_ALIGN)

    gi = np.arange(FOURIER_GROUP)
    ang = 2.0 * np.pi * ((gi[:, None] * gi[None, :]) % FOURIER_GROUP) / FOURIER_GROUP
    cs = jnp.asarray(np.concatenate([np.cos(ang), np.sin(ang)], axis=1) * FOURIER_GROUP ** -0.5, dtype=BF16)
    dft_rows = _fourier_steps(lp, lead, L, tf) * tf
    wdft = _dft_matrix(dft_rows, dft_rows, lead, L)
    bias, variant_of_block = _attn_bias_tables(nb, lead)

    h = _embed(x, meta, ln0_g, ln0_b, nb, lead).reshape(n, D_MODEL)
    for l in range(depth):
        q, kv, pq, gates = _inproj(h, w_in[l].astype(BF16), b_gate[l], cs, tm)
        attn = _attention(q.reshape(B, lp, -1), kv.reshape(B, lp, -1), sink[l], bias, variant_of_block, nb)
        four = _fourier(wdft, pq.reshape(B, lp, -1), lead, L, tf)
        h1, h1b, aff_r = _outproj(
            attn.reshape(n, -1), four.reshape(n, -1), gates, h,
            w_attn_o[l].astype(BF16), w_four_o[l].astype(BF16), w_out[l].astype(BF16),
            ln1_g[l], ln1_b[l], w_router[l], alpha, tm)
        rank_r, rank_c, cntv, cntc = _topk(aff_r, B, lp, lead, cap)
        cnt = cntv[:, :, :nchunk + 1].reshape(-1)
        xg, gsel = _gather(cnt, cntv, rank_r, aff_r, h1b, B, lp, slots, ch, win)
        y = _ffn(xg, gsel, w_e_gate, w_e_up, w_e_down, l)
        h = _combine(cnt, cntc, rank_c, y, h1, ln2_g[l], ln2_b[l], B, lp, alpha, ch, win)
    return h.reshape(B, lp, D_MODEL)[:, lead + N_META:]
```
